```python
import jax, jax.numpy as jnp
from jax import lax
import numpy as np

D_MODEL = 1024
BATCH = 8
SEQ = 4096
DEPTH = 4

N_MIXERS = 3
BLOCK = 128
ROPE_THETA = 500000.0
ROPE_FRAC = 4
EPS = 1e-6
NEG_INF = -1e30

A_HEADS = 8
A_QK_DIM = 128
A_ROPE_DIM = A_QK_DIM // ROPE_FRAC
A_NOPE_DIM = A_QK_DIM - A_ROPE_DIM
A_V_DIM = 128
A_KV_RANK = 256
IDX_HEADS = 16
IDX_DIM = 64
TOPK_MAX = 256

B_HEADS = 16
B_KV_HEADS = 4
B_HEAD_DIM = 64
B_WINDOW = 128

C_HEADS = 8
C_HEAD_DIM = 128
C_BRANCHES = ((128, 1), (512, 4), (2048, 16))

MEM_LEN = 256
MEM_HEADS = 4
MEM_HEAD_DIM = 64
MEM_WIDTH = MEM_HEADS * MEM_HEAD_DIM

D_FF = 2816
CONV_WIDTH = 3

MIX_WIDTH = 1024
OUT_IN = MIX_WIDTH + MEM_WIDTH
A_SPLITS = (A_HEADS * A_QK_DIM, A_KV_RANK, A_ROPE_DIM, IDX_HEADS * IDX_DIM, IDX_DIM, IDX_HEADS, MEM_WIDTH)
B_SPLITS = (B_HEADS * B_HEAD_DIM, B_KV_HEADS * B_HEAD_DIM, B_KV_HEADS * B_HEAD_DIM, MEM_WIDTH)
C_SPLITS = (C_HEADS * C_HEAD_DIM, C_HEADS * C_HEAD_DIM, C_HEADS * C_HEAD_DIM, MEM_WIDTH)
N_A = len(range(0, DEPTH, N_MIXERS))
N_B = len(range(1, DEPTH, N_MIXERS))
N_C = len(range(2, DEPTH, N_MIXERS))

kernel_name = 'hybrid_dsa_swa_dilated_memory_convffn'


def rmsnorm(x, g):
    xf = x.astype(jnp.float32)
    y = xf * lax.rsqrt(jnp.mean(xf * xf, axis=-1, keepdims=True) + EPS)
    return (y * g.astype(jnp.float32)).astype(x.dtype)


def split_cols(t, sizes):
    return jnp.split(t, np.cumsum(sizes)[:-1].tolist(), axis=-1)


def rope_tables(positions, rot_dim):
    inv = ROPE_THETA ** (-jnp.arange(0, rot_dim, 2, dtype=jnp.float32) / rot_dim)
    ang = positions.astype(jnp.float32)[..., None] * inv
    return jnp.cos(ang), jnp.sin(ang)


def apply_partial_rope(x, cos, sin):
    half = cos.shape[-1]
    rot = 2 * half
    xf = x[..., :rot].astype(jnp.float32)
    x1, x2 = xf[..., :half], xf[..., half:]
    c, s = cos[:, :, None, :], sin[:, :, None, :]
    xr = jnp.concatenate([x1 * c - x2 * s, x2 * c + x1 * s], axis=-1).astype(x.dtype)
    return jnp.concatenate([xr, x[..., rot:]], axis=-1)


def banded_window_attention(q, k, v, max_dist, scale, sink=None):
    B, n = q.shape[0], q.shape[1]
    nb = -(-n // BLOCK)
    pad = nb * BLOCK - n
    if pad:
        q = jnp.pad(q, ((0, 0), (0, pad), (0, 0), (0, 0), (0, 0)))
        k = jnp.pad(k, ((0, 0), (0, pad), (0, 0), (0, 0)))
        v = jnp.pad(v, ((0, 0), (0, pad), (0, 0), (0, 0)))
    qb = q.reshape(B, nb, BLOCK, *q.shape[2:])
    kb = k.reshape(B, nb, BLOCK, *k.shape[2:])
    vb = v.reshape(B, nb, BLOCK, *v.shape[2:])
    prev = ((0, 0), (1, 0), (0, 0), (0, 0), (0, 0))
    kk = jnp.concatenate([jnp.pad(kb[:, :-1], prev), kb], axis=2)
    vv = jnp.concatenate([jnp.pad(vb[:, :-1], prev), vb], axis=2)
    s = jnp.einsum('bnqhgd,bnkhd->bnhgqk', qb, kk).astype(jnp.float32) * scale
    qi = jnp.arange(BLOCK)[:, None]
    kj = jnp.arange(2 * BLOCK)[None, :]
    dist = BLOCK + qi - kj
    band = (dist >= 0) & (dist <= max_dist)
    blk = jnp.arange(nb)[:, None, None]
    mask = band[None] & ((blk > 0) | (kj >= BLOCK)[None])
    s = jnp.where(mask[None, :, None, None], s, NEG_INF)
    m = jnp.max(s, axis=-1)
    if sink is not None:
        sk = sink.astype(jnp.float32)[None, None, :, :, None]
        m = jnp.maximum(m, sk)
    p = jnp.exp(s - m[..., None])
    l = jnp.sum(p, axis=-1)
    if sink is not None:
        l = l + jnp.exp(sk - m)
    p = p / l[..., None]
    o = jnp.einsum('bnhgqk,bnkhd->bnqhgd', p, vv.astype(jnp.float32))
    o = o.reshape(B, nb * BLOCK, *o.shape[3:])[:, :n].astype(q.dtype)
    lse = (m + jnp.log(l)).transpose(0, 1, 4, 2, 3)
    lse = lse.reshape(B, nb * BLOCK, *lse.shape[3:])[:, :n]
    return o, lse


def dsa_attention(q_nope, q_rope, c_kv, k_rope, q_idx, k_idx, w_idx, w_uk, w_uv):
    B, S = q_nope.shape[0], q_nope.shape[1]
    topk = min(TOPK_MAX, S // 4)
    nb = S // BLOCK
    scale = A_QK_DIM ** -0.5
    q_lat = jnp.einsum('bshn,rhn->bshr', q_nope, w_uk)
    key_pos = jnp.arange(S)

    def to_blocks(t):
        return t.reshape(B, nb, BLOCK, *t.shape[2:]).swapaxes(0, 1)

    def one_block(args):
        ql, qr, qi, wi, t = args
        dots = jnp.einsum('bqhd,bsd->bqhs', qi, k_idx).astype(jnp.float32)
        score = jnp.einsum('bqhs,bqh->bqs', jax.nn.relu(dots), wi.astype(jnp.float32))
        causal = key_pos[None, :] <= t[:, None]
        score = jnp.where(causal[None], score, -jnp.inf)
        _, idx = lax.top_k(score, topk)
        c_sel = jax.vmap(lambda c, i: c[i])(c_kv, idx)
        kr_sel = jax.vmap(lambda c, i: c[i])(k_rope, idx)
        s = (jnp.einsum('bqhr,bqkr->bqhk', ql, c_sel)
             + jnp.einsum('bqhe,bqke->bqhk', qr, kr_sel)).astype(jnp.float32) * scale
        valid = idx <= t[None, :, None]
        s = jnp.where(valid[:, :, None, :], s, NEG_INF)
        p = jax.nn.softmax(s, axis=-1)
        return jnp.einsum('bqhk,bqkr->bqhr', p, c_sel.astype(jnp.float32)).astype(ql.dtype)

    t_ids = jnp.arange(S).reshape(nb, BLOCK)
    o_lat = lax.map(one_block, (to_blocks(q_lat), to_blocks(q_rope), to_blocks(q_idx), to_blocks(w_idx), t_ids))
    o_lat = o_lat.swapaxes(0, 1).reshape(B, S, A_HEADS, A_KV_RANK)
    return jnp.einsum('bshr,rhv->bshv', o_lat, w_uv)


def mixer_a(h, cos32, sin32, cos16, sin16, w_in, kv_norm, w_uk, w_uv):
    B, S, _ = h.shape
    q, c_kv, k_rope, q_idx, k_idx, w_idx, q_mem = split_cols(h @ w_in, A_SPLITS)
    q = apply_partial_rope(q.reshape(B, S, A_HEADS, A_QK_DIM), cos32, sin32)
    q_rope, q_nope = q[..., :A_ROPE_DIM], q[..., A_ROPE_DIM:]
    k_rope = apply_partial_rope(k_rope[:, :, None, :], cos32, sin32)[:, :, 0]
    c_kv = rmsnorm(c_kv, kv_norm)
    q_idx = apply_partial_rope(q_idx.reshape(B, S, IDX_HEADS, IDX_DIM), cos16, sin16)
    k_idx = apply_partial_rope(k_idx[:, :, None, :], cos16, sin16)[:, :, 0]
    w_idx = w_idx * (IDX_HEADS * IDX_DIM) ** -0.5
    o = dsa_attention(q_nope, q_rope, c_kv, k_rope, q_idx, k_idx, w_idx, w_uk, w_uv)
    return o.reshape(B, S, A_HEADS * A_V_DIM), q_mem


def mixer_b(h, cos16, sin16, w_in, sinks):
    B, S, _ = h.shape
    q, k, v, q_mem = split_cols(h @ w_in, B_SPLITS)
    group = B_HEADS // B_KV_HEADS
    q = apply_partial_rope(q.reshape(B, S, B_HEADS, B_HEAD_DIM), cos16, sin16)
    k = apply_partial_rope(k.reshape(B, S, B_KV_HEADS, B_HEAD_DIM), cos16, sin16)
    v = v.reshape(B, S, B_KV_HEADS, B_HEAD_DIM)
    q = q.reshape(B, S, B_KV_HEADS, group, B_HEAD_DIM)
    o, _ = banded_window_attention(q, k, v, B_WINDOW - 1, B_HEAD_DIM ** -0.5,
                                   sink=sinks.reshape(B_KV_HEADS, group))
    return o.reshape(B, S, B_HEADS * B_HEAD_DIM), q_mem


def dilate(t, d):
    B, S = t.shape[0], t.shape[1]
    return t.reshape(B, S // d, d, *t.shape[2:]).swapaxes(1, 2).reshape(B * d, S // d, *t.shape[2:])


def undilate(t, d, B):
    n = t.shape[1]
    return t.reshape(B, d, n, *t.shape[2:]).swapaxes(1, 2).reshape(B, n * d, *t.shape[2:])


def mixer_c(h, cos32, sin32, w_in):
    B, S, _ = h.shape
    q, k, v, q_mem = split_cols(h @ w_in, C_SPLITS)
    q = apply_partial_rope(q.reshape(B, S, C_HEADS, C_HEAD_DIM), cos32, sin32)
    k = apply_partial_rope(k.reshape(B, S, C_HEADS, C_HEAD_DIM), cos32, sin32)
    v = v.reshape(B, S, C_HEADS, C_HEAD_DIM)
    outs, lses = [], []
    for window, d in C_BRANCHES:
        o_g, lse_g = banded_window_attention(dilate(q, d)[:, :, :, None], dilate(k, d), dilate(v, d),
                                             window // d, C_HEAD_DIM ** -0.5)
        outs.append(undilate(o_g[:, :, :, 0], d, B))
        lses.append(undilate(lse_g[..., 0], d, B))
    alpha = jax.nn.softmax(jnp.stack(lses), axis=0)
    o = jnp.sum(alpha[..., None] * jnp.stack(outs).astype(jnp.float32), axis=0).astype(h.dtype)
    return o.reshape(B, S, C_HEADS * C_HEAD_DIM), q_mem


def memory_cross_attention(q_mem, mem_n, w_mem_kv):
    B, S, _ = q_mem.shape
    k, v = jnp.split(mem_n @ w_mem_kv, 2, axis=-1)
    k = k.reshape(B, -1, MEM_HEADS, MEM_HEAD_DIM)
    v = v.reshape(B, -1, MEM_HEADS, MEM_HEAD_DIM)
    q = q_mem.reshape(B, S, MEM_HEADS, MEM_HEAD_DIM)
    s = jnp.einsum('bshd,bmhd->bhsm', q, k).astype(jnp.float32) * MEM_HEAD_DIM ** -0.5
    p = jax.nn.softmax(s, axis=-1)
    o = jnp.einsum('bhsm,bmhd->bshd', p, v.astype(jnp.float32)).astype(q_mem.dtype)
    return o.reshape(B, S, MEM_WIDTH)


def conv_glu_ffn(h, w_up, conv_w, conv_b, w_down):
    S = h.shape[1]
    a, b = jnp.split(h @ w_up, 2, axis=-1)
    a_pad = jnp.pad(a, ((0, 0), (CONV_WIDTH - 1, 0), (0, 0)))
    a = sum(conv_w[j] * a_pad[:, j:j + S] for j in range(CONV_WIDTH)) + conv_b
    return (jax.nn.silu(a) * b) @ w_down


def setup_inputs(seed: int = 0) -> dict:
    key = jax.random.key(seed)
    ks = iter(jax.random.split(key, 40))

    def w(shape, fan_in):
        return jax.random.normal(next(ks), shape, jnp.float32) * fan_in ** -0.5

    def gain(shape):
        return 1.0 + 0.02 * jax.random.normal(next(ks), shape, jnp.float32)

    x = jax.random.normal(next(ks), (BATCH, SEQ, D_MODEL), jnp.float32)
    mem = jax.random.normal(next(ks), (BATCH, MEM_LEN, D_MODEL), jnp.float32)
    offset = jax.random.randint(next(ks), (BATCH, 1), 0, 4096, dtype=jnp.int32)
    positions = offset + jnp.arange(SEQ, dtype=jnp.int32)[None, :]
    return {
        'x': x, 'mem': mem, 'positions': positions,
        'g_mix': gain((DEPTH, D_MODEL)), 'g_ffn': gain((DEPTH, D_MODEL)),
        'g_mem': gain((D_MODEL,)), 'g_final': gain((D_MODEL,)),
        'w_mem_kv': w((DEPTH, D_MODEL, 2 * MEM_WIDTH), D_MODEL),
        'a_w_in': w((N_A, D_MODEL, sum(A_SPLITS)), D_MODEL),
        'a_kv_norm': gain((N_A, A_KV_RANK)),
        'a_w_uk': w((N_A, A_KV_RANK, A_HEADS, A_NOPE_DIM), A_KV_RANK),
        'a_w_uv': w((N_A, A_KV_RANK, A_HEADS, A_V_DIM), A_KV_RANK),
        'a_w_out': w((N_A, OUT_IN, D_MODEL), OUT_IN),
        'b_w_in': w((N_B, D_MODEL, sum(B_SPLITS)), D_MODEL),
        'b_sinks': 0.5 * jax.random.normal(next(ks), (N_B, B_HEADS), jnp.float32),
        'b_w_out': w((N_B, OUT_IN, D_MODEL), OUT_IN),
        'c_w_in': w((N_C, D_MODEL, sum(C_SPLITS)), D_MODEL),
        'c_w_out': w((N_C, OUT_IN, D_MODEL), OUT_IN),
        'f_w_up': w((DEPTH, D_MODEL, 2 * D_FF), D_MODEL),
        'f_conv_w': w((DEPTH, CONV_WIDTH, D_FF), CONV_WIDTH),
        'f_conv_b': 0.02 * jax.random.normal(next(ks), (DEPTH, D_FF), jnp.float32),
        'f_w_down': w((DEPTH, D_FF, D_MODEL), D_FF),
    }


def reference(x, mem, positions, g_mix, g_ffn, g_mem, g_final, w_mem_kv,
              a_w_in, a_kv_norm, a_w_uk, a_w_uv, a_w_out,
              b_w_in, b_sinks, b_w_out, c_w_in, c_w_out,
              f_w_up, f_conv_w, f_conv_b, f_w_down):
    cos32, sin32 = rope_tables(positions, A_QK_DIM // ROPE_FRAC)
    cos16, sin16 = rope_tables(positions, B_HEAD_DIM // ROPE_FRAC)
    mem_n = rmsnorm(mem, g_mem)
    for i in range(DEPTH):
        kind, j = i % N_MIXERS, i // N_MIXERS
        h = rmsnorm(x, g_mix[i])
        if kind == 0:
            mix, q_mem = mixer_a(h, cos32, sin32, cos16, sin16, a_w_in[j], a_kv_norm[j], a_w_uk[j], a_w_uv[j])
            w_out = a_w_out[j]
        elif kind == 1:
            mix, q_mem = mixer_b(h, cos16, sin16, b_w_in[j], b_sinks[j])
            w_out = b_w_out[j]
        else:
            mix, q_mem = mixer_c(h, cos32, sin32, c_w_in[j])
            w_out = c_w_out[j]
        mem_o = memory_cross_attention(q_mem, mem_n, w_mem_kv[i])
        x = x + jnp.concatenate([mix, mem_o], axis=-1) @ w_out
        x = x + conv_glu_ffn(rmsnorm(x, g_ffn[i]), f_w_up[i], f_conv_w[i], f_conv_b[i], f_w_down[i])
    return rmsnorm(x, g_final)
```

```python
import functools

import jax
import jax.numpy as jnp
from jax import lax
from jax.experimental import pallas as pl
from jax.experimental.pallas import tpu as pltpu

F32 = jnp.float32
BF16 = jnp.bfloat16
I32 = jnp.int32

LANES = 128
BLOCK = 128
ROPE_THETA = 500000.0
EPS = 1e-6
NEG_INF = -1e30
INT_MIN = -(2**31)

A_HEADS = 8
A_QK_DIM = 128
A_ROPE_DIM = 32
A_KV_RANK = 256
IDX_HEADS = 16
IDX_DIM = 64
TOPK_MAX = 256
B_HEADS = 16
B_KV_HEADS = 4
B_HEAD_DIM = 64
B_WINDOW = 128
C_HEADS = 8
C_HEAD_DIM = 128
C_BRANCHES = ((128, 1), (512, 4), (2048, 16))
MEM_HEADS = 4
MEM_HEAD_DIM = 64
MEM_WIDTH = MEM_HEADS * MEM_HEAD_DIM
CONV_WIDTH = 3
KEY_CHUNK = 256

VMEM_LIMIT = 56 * 1024 * 1024

_NT = (((1,), (1,)), ((), ()))


def _cparams(sem):
    return pltpu.CompilerParams(dimension_semantics=sem, vmem_limit_bytes=VMEM_LIMIT)


def _rms(x, g):
    return x * lax.rsqrt(jnp.mean(x * x, axis=-1, keepdims=True) + EPS) * g


def _rope_tile(t, tab, half):
    c, sa, sb = tab[:, 0:LANES], tab[:, LANES:2 * LANES], tab[:, 2 * LANES:3 * LANES]
    return t * c + pltpu.roll(t, half, 1) * sa + pltpu.roll(t, LANES - half, 1) * sb


def _rope_table(positions, head_dim, rot):
    half = rot // 2
    inv = ROPE_THETA ** (-jnp.arange(0, rot, 2, dtype=F32) / rot)
    ang = positions.astype(F32)[..., None] * inv
    cos, sin = jnp.cos(ang), jnp.sin(ang)
    rest = head_dim - rot
    lead = cos.shape[:-1]
    c = jnp.concatenate([cos, cos, jnp.ones(lead + (rest,), F32)], axis=-1)
    sa = jnp.concatenate([jnp.zeros_like(sin), sin, jnp.zeros(lead + (rest,), F32)], axis=-1)
    sb = jnp.concatenate([-sin, jnp.zeros_like(sin), jnp.zeros(lead + (rest,), F32)], axis=-1)
    reps = LANES // head_dim
    return jnp.concatenate([jnp.tile(t, (1, 1, reps)) for t in (c, sa, sb)], axis=-1)


def _proj_a_kernel(x_ref, g_ref, w_ref, kvn_ref, t32_ref, t16_ref,
                   q_ref, kx_ref, ct_ref, qi_ref, ki_ref, wi_ref, qm_ref, *, tm):
    hb = _rms(x_ref[...], g_ref[...]).astype(BF16)
    t32 = t32_ref[...]
    t16 = t16_ref[...]

    def mm(a, b):
        return jnp.dot(hb, w_ref[:, a:b], preferred_element_type=F32)

    scale = A_QK_DIM ** -0.5
    for j in range(0, A_HEADS * A_QK_DIM, 2 * LANES):
        y = mm(j, j + 2 * LANES)
        for u in range(2):
            t = _rope_tile(y[:, u * LANES:(u + 1) * LANES], t32, A_ROPE_DIM // 2)
            q_ref[:, j + u * LANES:j + (u + 1) * LANES] = (t * scale).astype(BF16)
    o = A_HEADS * A_QK_DIM
    c = _rms(mm(o, o + A_KV_RANK), kvn_ref[...])
    kx_ref[:, 0:A_KV_RANK] = c.astype(BF16)
    for u in range(tm // KEY_CHUNK):
        ct_ref[u] = c[u * KEY_CHUNK:(u + 1) * KEY_CHUNK, :].T.astype(BF16)
    o += A_KV_RANK
    kx_ref[:, A_KV_RANK:A_KV_RANK + LANES] = _rope_tile(mm(o, o + LANES), t32, A_ROPE_DIM // 2).astype(BF16)
    o += LANES
    for j in range(0, IDX_HEADS * IDX_DIM, 2 * LANES):
        y = mm(o + j, o + j + 2 * LANES)
        for u in range(2):
            t = _rope_tile(y[:, u * LANES:(u + 1) * LANES], t16, IDX_DIM // 8)
            qi_ref[:, j + u * LANES:j + (u + 1) * LANES] = t.astype(BF16)
    o += IDX_HEADS * IDX_DIM
    y = mm(o, o + 2 * LANES)
    for u in range(2):
        ki_ref[:, u * LANES:(u + 1) * LANES] = _rope_tile(
            y[:, u * LANES:(u + 1) * LANES], t16, IDX_DIM // 8).astype(BF16)
    o += 2 * LANES
    wi_ref[...] = mm(o, o + LANES) * (IDX_HEADS * IDX_DIM) ** -0.5
    o += LANES
    qm_ref[...] = (mm(o, o + MEM_WIDTH) * MEM_HEAD_DIM ** -0.5).astype(BF16)


def _prep_a_w_in(w):
    d = w.shape[0]
    sizes = (A_HEADS * A_QK_DIM, A_KV_RANK, A_ROPE_DIM, IDX_HEADS * IDX_DIM, IDX_DIM, IDX_HEADS, MEM_WIDTH)
    offs = [0]
    for s in sizes:
        offs.append(offs[-1] + s)
    q, ckv, kr, qi, ki, wi, qm = [w[:, offs[i]:offs[i + 1]] for i in range(len(sizes))]
    z = lambda n: jnp.zeros((d, n), w.dtype)
    return jnp.concatenate([
        q, ckv, kr, z(LANES - A_ROPE_DIM), qi,
        ki, z(LANES - IDX_DIM), z(LANES - IDX_DIM), ki,
        wi, z(LANES - IDX_HEADS), qm], axis=1).astype(BF16)


def _proj_a(x, g, w, kvn, t32, t16, tm):
    B, S, D = x.shape
    n = w.shape[1]
    row = lambda width: pl.BlockSpec((None, tm, width), lambda b, i: (b, i, 0))
    const = lambda shape: pl.BlockSpec(shape, lambda b, i: (0,) * len(shape))
    out_shape = (
        jax.ShapeDtypeStruct((B, S, A_HEADS * A_QK_DIM), BF16),
        jax.ShapeDtypeStruct((B, S, A_KV_RANK + LANES), BF16),
        jax.ShapeDtypeStruct((B, S // KEY_CHUNK, A_KV_RANK, KEY_CHUNK), BF16),
        jax.ShapeDtypeStruct((B, S, IDX_HEADS * IDX_DIM), BF16),
        jax.ShapeDtypeStruct((B, S, 2 * LANES), BF16),
        jax.ShapeDtypeStruct((B, S, LANES), F32),
        jax.ShapeDtypeStruct((B, S, MEM_WIDTH), BF16),
    )
    out_specs = (
        row(A_HEADS * A_QK_DIM), row(A_KV_RANK + LANES),
        pl.BlockSpec((None, tm // KEY_CHUNK, A_KV_RANK, KEY_CHUNK), lambda b, i: (b, i, 0, 0)),
        row(IDX_HEADS * IDX_DIM), row(2 * LANES), row(LANES), row(MEM_WIDTH),
    )
    return pl.pallas_call(
        functools.partial(_proj_a_kernel, tm=tm),
        grid=(B, S // tm),
        in_specs=[row(D), const((1, D)), const((D, n)), const((1, A_KV_RANK)), row(3 * LANES), row(3 * LANES)],
        out_specs=out_specs,
        out_shape=out_shape,
        compiler_params=_cparams(("parallel", "parallel")),
        name="proj_a",
    )(x, g, w, kvn, t32, t16)


def _proj_qkv_kernel(x_ref, g_ref, w_ref, tab_ref, q_ref, k_ref, v_ref, qm_ref, *, head_dim, rot, nq, nk, nv):
    hb = _rms(x_ref[...], g_ref[...]).astype(BF16)
    tab = tab_ref[...]

    def mm(a, b):
        return jnp.dot(hb, w_ref[:, a:b], preferred_element_type=F32)

    scale = head_dim ** -0.5
    for j in range(0, nq, 2 * LANES):
        y = mm(j, j + 2 * LANES)
        for u in range(2):
            t = _rope_tile(y[:, u * LANES:(u + 1) * LANES], tab, rot // 2)
            q_ref[:, j + u * LANES:j + (u + 1) * LANES] = (t * scale).astype(BF16)
    for j in range(0, nk, 2 * LANES):
        y = mm(nq + j, nq + j + 2 * LANES)
        for u in range(2):
            t = _rope_tile(y[:, u * LANES:(u + 1) * LANES], tab, rot // 2)
            k_ref[:, j + u * LANES:j + (u + 1) * LANES] = t.astype(BF16)
    for j in range(0, nv, 2 * LANES):
        v_ref[:, j:j + 2 * LANES] = mm(nq + nk + j, nq + nk + j + 2 * LANES).astype(BF16)
    o = nq + nk + nv
    qm_ref[...] = (mm(o, o + MEM_WIDTH) * MEM_HEAD_DIM ** -0.5).astype(BF16)


def _proj_qkv(x, g, w, tab, tm, head_dim, rot, nq, nk, nv):
    B, S, D = x.shape
    n = w.shape[1]
    row = lambda width: pl.BlockSpec((None, tm, width), lambda b, i: (b, i, 0))
    const = lambda shape: pl.BlockSpec(shape, lambda b, i: (0,) * len(shape))
    return pl.pallas_call(
        functools.partial(_proj_qkv_kernel, head_dim=head_dim, rot=rot, nq=nq, nk=nk, nv=nv),
        grid=(B, S // tm),
        in_specs=[row(D), const((1, D)), const((D, n)), row(3 * LANES)],
        out_specs=(row(nq), row(nk), row(nv), row(MEM_WIDTH)),
        out_shape=(jax.ShapeDtypeStruct((B, S, nq), BF16), jax.ShapeDtypeStruct((B, S, nk), BF16),
                   jax.ShapeDtypeStruct((B, S, nv), BF16), jax.ShapeDtypeStruct((B, S, MEM_WIDTH), BF16)),
        compiler_params=_cparams(("parallel", "parallel")),
        name="proj_qkv",
    )(x, g, w, tab)


def _prep_b_w_in(w):
    d = w.shape[0]
    nq, nkv = B_HEADS * B_HEAD_DIM, B_KV_HEADS * B_HEAD_DIM
    q, k, v, qm = w[:, :nq], w[:, nq:nq + nkv], w[:, nq + nkv:nq + 2 * nkv], w[:, nq + 2 * nkv:]
    z = jnp.zeros((d, B_HEAD_DIM), w.dtype)

    def spread(t):
        cols = []
        for h in range(B_KV_HEADS):
            th = t[:, h * B_HEAD_DIM:(h + 1) * B_HEAD_DIM]
            cols += [th, z, z, th]
        return jnp.concatenate(cols, axis=1)

    return jnp.concatenate([q, spread(k), spread(v), qm], axis=1).astype(BF16)


def _dsa_kernel(q_ref, qi_ref, wi_ref, kx_ref, ct_ref, ki_ref, wuk_ref, wuv_ref, o_ref,
                keys_ref, qext_ref, acc_ref, m_ref, l_ref, *, topk):
    kc = KEY_CHUNK
    i = pl.program_id(1)
    nch = (i * BLOCK + BLOCK + kc - 1) // kc
    lane = lax.broadcasted_iota(I32, (BLOCK, LANES), 1)

    for h in range(A_HEADS):
        qh = q_ref[:, h * LANES:(h + 1) * LANES]
        qext_ref[h, :, 0:A_KV_RANK] = jnp.dot(qh, wuk_ref[h], preferred_element_type=F32).astype(BF16)
        qext_ref[h, :, A_KV_RANK:A_KV_RANK + LANES] = jnp.where(
            lane < A_ROPE_DIM, qh.astype(F32), 0.0).astype(BF16)
    w_t = wi_ref[...].T

    qpos = i * BLOCK + lax.broadcasted_iota(I32, (kc, BLOCK), 1)
    krow = lax.broadcasted_iota(I32, (kc, BLOCK), 0)

    def score_body(c, carry):
        off = pl.multiple_of(c * kc, kc)
        kk = ki_ref[pl.ds(off, kc), :]
        klo, khi = kk[:, 0:LANES], kk[:, LANES:2 * LANES]
        acc = jnp.zeros((kc, BLOCK), F32)
        for j in range(IDX_HEADS // 2):
            qt = qi_ref[:, j * LANES:(j + 1) * LANES]
            dlo = lax.dot_general(klo, qt, _NT, preferred_element_type=F32)
            dhi = lax.dot_general(khi, qt, _NT, preferred_element_type=F32)
            acc = acc + jnp.maximum(dlo, 0.0) * w_t[2 * j:2 * j + 1, :]
            acc = acc + jnp.maximum(dhi, 0.0) * w_t[2 * j + 1:2 * j + 2, :]
        bits = pltpu.bitcast(acc, I32)
        key = jnp.where(bits < 0, bits ^ 0x7FFFFFFF, bits)
        keys_ref[pl.ds(off, kc), :] = jnp.where(off + krow <= qpos, key, INT_MIN)
        return carry

    lax.fori_loop(0, nch, score_body, 0)

    def bit_body(p, res):
        trial = res | lax.shift_left(jnp.int32(1), 31 - p)
        thr = trial ^ INT_MIN

        def cnt_body(c, cnt):
            off = pl.multiple_of(c * kc, kc)
            ge = (keys_ref[pl.ds(off, kc), :] >= thr).astype(I32)
            return cnt + jnp.sum(ge.reshape(kc // 8, 8, BLOCK), axis=0)

        cnt = lax.fori_loop(0, nch, cnt_body, jnp.zeros((8, BLOCK), I32))
        return jnp.where(jnp.sum(cnt, axis=0, keepdims=True) >= topk, trial, res)

    res = lax.fori_loop(0, 32, bit_body, jnp.zeros((1, BLOCK), I32))
    thr = jnp.maximum(res ^ INT_MIN, INT_MIN + 1)

    m_ref[...] = jnp.full(m_ref.shape, NEG_INF, F32)
    l_ref[...] = jnp.zeros(l_ref.shape, F32)
    acc_ref[...] = jnp.zeros(acc_ref.shape, F32)

    def att_body(c, carry):
        off = pl.multiple_of(c * kc, kc)
        kx = kx_ref[pl.ds(off, kc), :]
        ct = ct_ref[c]
        mask = keys_ref[pl.ds(off, kc), :] >= thr
        for h in range(A_HEADS):
            s = lax.dot_general(kx, qext_ref[h], _NT, preferred_element_type=F32)
            s = jnp.where(mask, s, NEG_INF)
            m_prev = m_ref[h:h + 1, :]
            m_new = jnp.maximum(m_prev, jnp.max(s, axis=0, keepdims=True))
            alpha = jnp.exp(m_prev - m_new)
            p = jnp.exp(s - m_new)
            l_ref[h:h + 1, :] = alpha * l_ref[h:h + 1, :] + jnp.sum(p, axis=0, keepdims=True)
            m_ref[h:h + 1, :] = m_new
            acc_ref[h] = alpha * acc_ref[h] + jnp.dot(ct, p.astype(BF16), preferred_element_type=F32)
        return carry

    lax.fori_loop(0, nch, att_body, 0)

    for h in range(A_HEADS):
        o_lat = (acc_ref[h] * (1.0 / l_ref[h:h + 1, :])).T.astype(BF16)
        o_ref[:, h * LANES:(h + 1) * LANES] = jnp.dot(
            o_lat, wuv_ref[h], preferred_element_type=F32).astype(BF16)


def _dsa(q, qi, wi, kx, ct, ki, wuk, wuv):
    B, S, _ = q.shape
    topk = min(TOPK_MAX, S // 4)
    blk = lambda width: pl.BlockSpec((None, BLOCK, width), lambda b, i: (b, i, 0))
    seq = lambda width: pl.BlockSpec((None, S, width), lambda b, i: (b, 0, 0))
    const3 = lambda shape: pl.BlockSpec(shape, lambda b, i: (0, 0, 0))
    return pl.pallas_call(
        functools.partial(_dsa_kernel, topk=topk),
        grid=(B, S // BLOCK),
        in_specs=[blk(A_HEADS * A_QK_DIM), blk(IDX_HEADS * IDX_DIM), blk(LANES),
                  seq(A_KV_RANK + LANES),
                  pl.BlockSpec((None, S // KEY_CHUNK, A_KV_RANK, KEY_CHUNK), lambda b, i: (b, 0, 0, 0)),
                  seq(2 * LANES), const3(wuk.shape), const3(wuv.shape)],
        out_specs=blk(A_HEADS * LANES),
        out_shape=jax.ShapeDtypeStruct((B, S, A_HEADS * LANES), BF16),
        scratch_shapes=[
            pltpu.VMEM((S, BLOCK), I32),
            pltpu.VMEM((A_HEADS, BLOCK, A_KV_RANK + LANES), BF16),
            pltpu.VMEM((A_HEADS, A_KV_RANK, BLOCK), F32),
            pltpu.VMEM((A_HEADS, BLOCK), F32),
            pltpu.VMEM((A_HEADS, BLOCK), F32),
        ],
        compiler_params=_cparams(("parallel", "arbitrary")),
        name="dsa",
    )(q, qi, wi, kx, ct, ki, wuk, wuv)


def _banded_kernel(*refs, subheads, max_dist, has_sink, has_prev):
    refs = list(refs)
    sink_ref = refs.pop(0) if has_sink else None
    q_ref, kp_ref, kc_ref, vp_ref, vc_ref = refs[:5]
    refs = refs[5:]
    if has_prev:
        oprev_ref, lprev_ref = refs[:2]
        refs = refs[2:]
    o_ref, lse_ref = refs

    n = pl.program_id(2)
    qi = lax.broadcasted_iota(I32, (BLOCK, 2 * BLOCK), 0)
    kj = lax.broadcasted_iota(I32, (BLOCK, 2 * BLOCK), 1)
    dist = BLOCK + qi - kj
    mask = (dist >= 0) & (dist <= max_dist) & ((kj >= BLOCK) | (n > 0))
    lane = lax.broadcasted_iota(I32, (BLOCK, LANES), 1)
    lse_tile = jnp.zeros((BLOCK, LANES), F32)

    for j, subs in enumerate(subheads):
        qt = q_ref[:, j * LANES:(j + 1) * LANES]
        o_t = None
        lse_t = None
        for kt, vt, hidx in subs:
            kk = jnp.concatenate([kp_ref[:, kt * LANES:(kt + 1) * LANES],
                                  kc_ref[:, kt * LANES:(kt + 1) * LANES]], axis=0)
            vv = jnp.concatenate([vp_ref[:, vt * LANES:(vt + 1) * LANES],
                                  vc_ref[:, vt * LANES:(vt + 1) * LANES]], axis=0)
            s = lax.dot_general(qt, kk, _NT, preferred_element_type=F32)
            s = jnp.where(mask, s, NEG_INF)
            m = jnp.max(s, axis=-1, keepdims=True)
            if has_sink:
                sk = sink_ref[hidx]
                m = jnp.maximum(m, sk)
            p = jnp.exp(s - m)
            l = jnp.sum(p, axis=-1, keepdims=True)
            if has_sink:
                l = l + jnp.exp(sk - m)
            o_s = jnp.dot(p.astype(BF16), vv, preferred_element_type=F32) / l
            o_t = o_s if o_t is None else o_t + o_s
            lse_t = m + jnp.log(l)
        if has_prev:
            lp = lprev_ref[:, j:j + 1]
            op = oprev_ref[:, j * LANES:(j + 1) * LANES].astype(F32)
            mx = jnp.maximum(lp, lse_t)
            wp, wn = jnp.exp(lp - mx), jnp.exp(lse_t - mx)
            den = wp + wn
            o_t = (wp * op + wn * o_t) / den
            lse_t = mx + jnp.log(den)
        o_ref[:, j * LANES:(j + 1) * LANES] = o_t.astype(o_ref.dtype)
        lse_tile = jnp.where(lane == j, lse_t, lse_tile)
    lse_ref[...] = lse_tile


def _banded(q, k, v, *, dil, subheads, max_dist, sinks=None, prev=None):
    B, S, wq = q.shape
    wk, wv = k.shape[-1], v.shape[-1]
    n = S // dil
    view = lambda t: t.reshape(B, n, dil * t.shape[-1])
    cur = lambda width: pl.BlockSpec((None, BLOCK, width), lambda b, r, i: (b, i, r))
    prv = lambda width: pl.BlockSpec((None, BLOCK, width), lambda b, r, i: (b, jnp.maximum(i - 1, 0), r))
    in_specs = [cur(wq), prv(wk), cur(wk), prv(wv), cur(wv)]
    args = [view(q), view(k), view(k), view(v), view(v)]
    if sinks is not None:
        in_specs.insert(0, pl.BlockSpec(memory_space=pltpu.SMEM))
        args.insert(0, sinks)
    if prev is not None:
        in_specs += [cur(wq), cur(LANES)]
        args += [view(prev[0]), view(prev[1])]
    o, lse = pl.pallas_call(
        functools.partial(_banded_kernel, subheads=subheads, max_dist=max_dist,
                          has_sink=sinks is not None, has_prev=prev is not None),
        grid=(B, dil, n // BLOCK),
        in_specs=in_specs,
        out_specs=(cur(wq), cur(LANES)),
        out_shape=(jax.ShapeDtypeStruct((B, n, dil * wq), BF16), jax.ShapeDtypeStruct((B, n, dil * LANES), F32)),
        compiler_params=_cparams(("parallel", "parallel", "arbitrary")),
        name="banded",
    )(*args)
    return o.reshape(B, S, wq), lse.reshape(B, S, LANES)


def _memkv_kernel(mem_ref, g_ref, w_ref, k_ref, v_ref):
    hb = _rms(mem_ref[...], g_ref[...]).astype(BF16)
    y = jnp.dot(hb, w_ref[...], preferred_element_type=F32)
    lane = lax.broadcasted_iota(I32, (y.shape[0], LANES), 1)
    for t in range(MEM_WIDTH // LANES):
        for out_ref, base in ((k_ref, 0), (v_ref, MEM_WIDTH)):
            tile = y[:, base + t * LANES:base + (t + 1) * LANES]
            out_ref[:, (2 * t) * LANES:(2 * t + 1) * LANES] = jnp.where(lane < MEM_HEAD_DIM, tile, 0.0).astype(BF16)
            out_ref[:, (2 * t + 1) * LANES:(2 * t + 2) * LANES] = jnp.where(lane >= MEM_HEAD_DIM, tile, 0.0).astype(BF16)


def _memkv(mem, g_mem, w):
    B, M, D = mem.shape
    L = w.shape[0]
    out = jax.ShapeDtypeStruct((L, B, M, 2 * MEM_WIDTH), BF16)
    ospec = pl.BlockSpec((None, None, M, 2 * MEM_WIDTH), lambda l, b: (l, b, 0, 0))
    return pl.pallas_call(
        _memkv_kernel,
        grid=(L, B),
        in_specs=[pl.BlockSpec((None, M, D), lambda l, b: (b, 0, 0)),
                  pl.BlockSpec((1, D), lambda l, b: (0, 0)),
                  pl.BlockSpec((None, D, 2 * MEM_WIDTH), lambda l, b: (l, 0, 0))],
        out_specs=(ospec, ospec),
        out_shape=(out, out),
        compiler_params=_cparams(("parallel", "parallel")),
        name="memkv",
    )(mem, g_mem, w)


def _out_kernel(x_ref, mix_ref, qm_ref, mk_ref, mv_ref, w_ref, o_ref):
    mw = mix_ref.shape[-1]
    y = x_ref[...] + jnp.dot(mix_ref[...], w_ref[0:mw, :], preferred_element_type=F32)
    for t in range(MEM_WIDTH // LANES):
        qt = qm_ref[:, t * LANES:(t + 1) * LANES]
        o_t = None
        for u in range(2):
            sl = slice((2 * t + u) * LANES, (2 * t + u + 1) * LANES)
            s = lax.dot_general(qt, mk_ref[:, sl], _NT, preferred_element_type=F32)
            p = jnp.exp(s - jnp.max(s, axis=-1, keepdims=True))
            l = jnp.sum(p, axis=-1, keepdims=True)
            o_s = jnp.dot(p.astype(BF16), mv_ref[:, sl], preferred_element_type=F32) / l
            o_t = o_s if o_t is None else o_t + o_s
        y = y + jnp.dot(o_t.astype(BF16), w_ref[mw + t * LANES:mw + (t + 1) * LANES, :],
                        preferred_element_type=F32)
    o_ref[...] = y


def _out_proj(x, mix, qm, mk, mv, w, tm):
    B, S, D = x.shape
    M = mk.shape[1]
    row = lambda width: pl.BlockSpec((None, tm, width), lambda b, i: (b, i, 0))
    mem = pl.BlockSpec((None, M, 2 * MEM_WIDTH), lambda b, i: (b, 0, 0))
    return pl.pallas_call(
        _out_kernel,
        grid=(B, S // tm),
        in_specs=[row(D), row(mix.shape[-1]), row(MEM_WIDTH), mem, mem,
                  pl.BlockSpec(w.shape, lambda b, i: (0, 0))],
        out_specs=row(D),
        out_shape=jax.ShapeDtypeStruct((B, S, D), F32),
        compiler_params=_cparams(("parallel", "parallel")),
        name="out_proj",
    )(x, mix, qm, mk, mv, w)


def _ffn_kernel(x_ref, g_ref, wup_ref, cw_ref, cb_ref, wdn_ref, gf_ref, o_ref, carry_ref, *, tm, cw, final):
    dff = wdn_ref.shape[0]
    first = pl.program_id(1) == 0
    x = x_ref[...]
    hb = _rms(x, g_ref[...]).astype(BF16)
    row = lax.broadcasted_iota(I32, (tm, cw), 0)
    acc = x
    for c0 in range(0, dff, cw):
        a = jnp.dot(hb, wup_ref[:, c0:c0 + cw], preferred_element_type=F32)
        b = jnp.dot(hb, wup_ref[:, dff + c0:dff + c0 + cw], preferred_element_type=F32)
        prev = jnp.where(first, 0.0, carry_ref[:, c0:c0 + cw])
        p1, p2 = prev[7:8, :], prev[6:7, :]
        a1 = jnp.where(row == 0, p1, pltpu.roll(a, 1, 0))
        a2 = jnp.where(row == 0, p2, jnp.where(row == 1, p1, pltpu.roll(a, 2, 0)))
        carry_ref[:, c0:c0 + cw] = a[tm - 8:tm, :]
        w = cw_ref[:, c0:c0 + cw]
        conv = w[0:1, :] * a2 + w[1:2, :] * a1 + w[2:3, :] * a + cb_ref[:, c0:c0 + cw]
        gated = conv / (1.0 + jnp.exp(-conv)) * b
        acc = acc + jnp.dot(gated.astype(BF16), wdn_ref[c0:c0 + cw, :], preferred_element_type=F32)
    if final:
        acc = _rms(acc, gf_ref[...])
    o_ref[...] = acc


def _ffn(x, g, wup, cw, cb, wdn, gf, tm, final):
    B, S, D = x.shape
    dff = wdn.shape[0]
    row = pl.BlockSpec((None, tm, D), lambda b, i: (b, i, 0))
    const = lambda shape: pl.BlockSpec(shape, lambda b, i: (0, 0), pipeline_mode=pl.Buffered(1))
    return pl.pallas_call(
        functools.partial(_ffn_kernel, tm=tm, cw=2 * LANES, final=final),
        grid=(B, S // tm),
        in_specs=[row, const((1, D)), const(wup.shape), const(cw.shape), const((1, dff)), const(wdn.shape),
                  const((1, D))],
        out_specs=row,
        out_shape=jax.ShapeDtypeStruct((B, S, D), F32),
        scratch_shapes=[pltpu.VMEM((8, dff), F32)],
        compiler_params=_cparams(("arbitrary", "arbitrary")),
        name="ffn",
    )(x, g, wup, cw, cb, wdn, gf)


def _b_subheads():
    return tuple(((2 * (j // 2), 2 * (j // 2), 2 * j), (2 * (j // 2) + 1, 2 * (j // 2) + 1, 2 * j + 1))
                 for j in range(B_HEADS // 2))


def _c_subheads():
    return tuple(((j, j, j),) for j in range(C_HEADS))


def kernel(x, mem, positions, g_mix, g_ffn, g_mem, g_final, w_mem_kv, a_w_in, a_kv_norm, a_w_uk, a_w_uv, a_w_out,
           b_w_in, b_sinks, b_w_out, c_w_in, c_w_out, f_w_up, f_conv_w, f_conv_b, f_w_down):
    B, S, D = x.shape
    depth = g_mix.shape[0]
    tm = min(512, S)
    tab32 = _rope_table(positions, A_QK_DIM, A_ROPE_DIM)
    tab16 = _rope_table(positions, B_HEAD_DIM, B_HEAD_DIM // 4)
    mem_k, mem_v = _memkv(mem, g_mem.reshape(1, D), w_mem_kv.astype(BF16))
    conv_w = jnp.pad(f_conv_w, ((0, 0), (0, 8 - CONV_WIDTH), (0, 0)))
    for i in range(depth):
        kind, j = i % 3, i // 3
        g = g_mix[i].reshape(1, D)
        if kind == 0:
            q, kx, ct, qi, ki, wi, qm = _proj_a(x, g, _prep_a_w_in(a_w_in[j]), a_kv_norm[j].reshape(1, -1),
                                                tab32, tab16, tm)
            wuk = jnp.pad(jnp.transpose(a_w_uk[j], (1, 2, 0)), ((0, 0), (A_ROPE_DIM, 0), (0, 0))).astype(BF16)
            wuv = jnp.transpose(a_w_uv[j], (1, 0, 2)).astype(BF16)
            mix = _dsa(q, qi, wi, kx, ct, ki, wuk, wuv)
            w_out = a_w_out[j]
        elif kind == 1:
            nq, nkv = B_HEADS * B_HEAD_DIM, 4 * B_KV_HEADS * B_HEAD_DIM
            q, k, v, qm = _proj_qkv(x, g, _prep_b_w_in(b_w_in[j]), tab16, tm, B_HEAD_DIM, B_HEAD_DIM // 4,
                                    nq, nkv, nkv)
            mix, _ = _banded(q, k, v, dil=1, subheads=_b_subheads(), max_dist=B_WINDOW - 1, sinks=b_sinks[j])
            w_out = b_w_out[j]
        else:
            nq = C_HEADS * C_HEAD_DIM
            q, k, v, qm = _proj_qkv(x, g, c_w_in[j].astype(BF16), tab32, tm, C_HEAD_DIM, C_HEAD_DIM // 4,
                                    nq, nq, nq)
            prev = None
            for window, dil in C_BRANCHES:
                prev = _banded(q, k, v, dil=dil, subheads=_c_subheads(), max_dist=window // dil, prev=prev)
            mix = prev[0]
            w_out = c_w_out[j]
        x = _out_proj(x, mix, qm, mem_k[i], mem_v[i], w_out.astype(BF16), tm)
        x = _ffn(x, g_ffn[i].reshape(1, D), f_w_up[i].astype(BF16), conv_w[i], f_conv_b[i].reshape(1, -1),
                 f_w_down[i].astype(BF16), g_final.reshape(1, D), min(256, S), i == depth - 1)
    return x
```

```python
import functools

import jax
import jax.numpy as jnp
from jax import lax
from jax.experimental import pallas as pl
from jax.experimental.pallas import tpu as pltpu

F32 = jnp.float32
BF16 = jnp.bfloat16
I32 = jnp.int32

LANES = 128
BLOCK = 128
ROPE_THETA = 500000.0
EPS = 1e-6
NEG_INF = -1e30
INT_MIN = -(2**31)

A_HEADS = 8
A_QK_DIM = 128
A_ROPE_DIM = 32
A_KV_RANK = 256
IDX_HEADS = 16
IDX_DIM = 64
TOPK_MAX = 256
B_HEADS = 16
B_KV_HEADS = 4
B_HEAD_DIM = 64
B_WINDOW = 128
C_HEADS = 8
C_HEAD_DIM = 128
C_BRANCHES = ((128, 1), (512, 4), (2048, 16))
MEM_HEADS = 4
MEM_HEAD_DIM = 64
MEM_WIDTH = MEM_HEADS * MEM_HEAD_DIM
CONV_WIDTH = 3
KEY_CHUNK = 256

VMEM_LIMIT = 56 * 1024 * 1024

_NT = (((1,), (1,)), ((), ()))


def _cparams(sem):
    return pltpu.CompilerParams(dimension_semantics=sem, vmem_limit_bytes=VMEM_LIMIT)


def _rms(x, g):
    return x * lax.rsqrt(jnp.mean(x * x, axis=-1, keepdims=True) + EPS) * g


def _rope_tile(t, tab, half):
    c, sa, sb = tab[:, 0:LANES], tab[:, LANES:2 * LANES], tab[:, 2 * LANES:3 * LANES]
    return t * c + pltpu.roll(t, half, 1) * sa + pltpu.roll(t, LANES - half, 1) * sb


def _rope_table(positions, head_dim, rot):
    half = rot // 2
    inv = ROPE_THETA ** (-jnp.arange(0, rot, 2, dtype=F32) / rot)
    ang = positions.astype(F32)[..., None] * inv
    cos, sin = jnp.cos(ang), jnp.sin(ang)
    rest = head_dim - rot
    lead = cos.shape[:-1]
    c = jnp.concatenate([cos, cos, jnp.ones(lead + (rest,), F32)], axis=-1)
    sa = jnp.concatenate([jnp.zeros_like(sin), sin, jnp.zeros(lead + (rest,), F32)], axis=-1)
    sb = jnp.concatenate([-sin, jnp.zeros_like(sin), jnp.zeros(lead + (rest,), F32)], axis=-1)
    reps = LANES // head_dim
    return jnp.concatenate([jnp.tile(t, (1, 1, reps)) for t in (c, sa, sb)], axis=-1)


def _proj_a_kernel(x_ref, g_ref, w_ref, kvn_ref, t32_ref, t16_ref,
                   q_ref, kx_ref, ct_ref, qi_ref, ki_ref, wi_ref, qm_ref, *, tm):
    hb = _rms(x_ref[...], g_ref[...]).astype(BF16)
    t32 = t32_ref[...]
    t16 = t16_ref[...]

    def mm(a, b):
        return jnp.dot(hb, w_ref[:, a:b], preferred_element_type=F32)

    scale = A_QK_DIM ** -0.5
    for j in range(0, A_HEADS * A_QK_DIM, 2 * LANES):
        y = mm(j, j + 2 * LANES)
        for u in range(2):
            t = _rope_tile(y[:, u * LANES:(u + 1) * LANES], t32, A_ROPE_DIM // 2)
            q_ref[:, j + u * LANES:j + (u + 1) * LANES] = (t * scale).astype(BF16)
    o = A_HEADS * A_QK_DIM
    c = _rms(mm(o, o + A_KV_RANK), kvn_ref[...])
    kx_ref[:, 0:A_KV_RANK] = c.astype(BF16)
    for u in range(tm // KEY_CHUNK):
        ct_ref[u] = c[u * KEY_CHUNK:(u + 1) * KEY_CHUNK, :].T.astype(BF16)
    o += A_KV_RANK
    kx_ref[:, A_KV_RANK:A_KV_RANK + LANES] = _rope_tile(mm(o, o + LANES), t32, A_ROPE_DIM // 2).astype(BF16)
    o += LANES
    for j in range(0, IDX_HEADS * IDX_DIM, 2 * LANES):
        y = mm(o + j, o + j + 2 * LANES)
        for u in range(2):
            t = _rope_tile(y[:, u * LANES:(u + 1) * LANES], t16, IDX_DIM // 8)
            qi_ref[:, j + u * LANES:j + (u + 1) * LANES] = t.astype(BF16)
    o += IDX_HEADS * IDX_DIM
    y = mm(o, o + 2 * LANES)
    for u in range(2):
        ki_ref[:, u * LANES:(u + 1) * LANES] = _rope_tile(
            y[:, u * LANES:(u + 1) * LANES], t16, IDX_DIM // 8).astype(BF16)
    o += 2 * LANES
    wi_ref[...] = mm(o, o + LANES) * (IDX_HEADS * IDX_DIM) ** -0.5
    o += LANES
    qm_ref[...] = (mm(o, o + MEM_WIDTH) * MEM_HEAD_DIM ** -0.5).astype(BF16)


def _prep_a_w_in(w):
    d = w.shape[0]
    sizes = (A_HEADS * A_QK_DIM, A_KV_RANK, A_ROPE_DIM, IDX_HEADS * IDX_DIM, IDX_DIM, IDX_HEADS, MEM_WIDTH)
    offs = [0]
    for s in sizes:
        offs.append(offs[-1] + s)
    q, ckv, kr, qi, ki, wi, qm = [w[:, offs[i]:offs[i + 1]] for i in range(len(sizes))]
    z = lambda n: jnp.zeros((d, n), w.dtype)
    return jnp.concatenate([
        q, ckv, kr, z(LANES - A_ROPE_DIM), qi,
        ki, z(LANES - IDX_DIM), z(LANES - IDX_DIM), ki,
        wi, z(LANES - IDX_HEADS), qm], axis=1).astype(BF16)


def _proj_a(x, g, w, kvn, t32, t16, tm):
    B, S, D = x.shape
    n = w.shape[1]
    row = lambda width: pl.BlockSpec((None, tm, width), lambda b, i: (b, i, 0))
    const = lambda shape: pl.BlockSpec(shape, lambda b, i: (0,) * len(shape))
    out_shape = (
        jax.ShapeDtypeStruct((B, S, A_HEADS * A_QK_DIM), BF16),
        jax.ShapeDtypeStruct((B, S, A_KV_RANK + LANES), BF16),
        jax.ShapeDtypeStruct((B, S // KEY_CHUNK, A_KV_RANK, KEY_CHUNK), BF16),
        jax.ShapeDtypeStruct((B, S, IDX_HEADS * IDX_DIM), BF16),
        jax.ShapeDtypeStruct((B, S, 2 * LANES), BF16),
        jax.ShapeDtypeStruct((B, S, LANES), F32),
        jax.ShapeDtypeStruct((B, S, MEM_WIDTH), BF16),
    )
    out_specs = (
        row(A_HEADS * A_QK_DIM), row(A_KV_RANK + LANES),
        pl.BlockSpec((None, tm // KEY_CHUNK, A_KV_RANK, KEY_CHUNK), lambda b, i: (b, i, 0, 0)),
        row(IDX_HEADS * IDX_DIM), row(2 * LANES), row(LANES), row(MEM_WIDTH),
    )
    return pl.pallas_call(
        functools.partial(_proj_a_kernel, tm=tm),
        grid=(B, S // tm),
        in_specs=[row(D), const((1, D)), const((D, n)), const((1, A_KV_RANK)), row(3 * LANES), row(3 * LANES)],
        out_specs=out_specs,
        out_shape=out_shape,
        compiler_params=_cparams(("parallel", "parallel")),
        name="proj_a",
    )(x, g, w, kvn, t32, t16)


def _proj_qkv_kernel(x_ref, g_ref, w_ref, tab_ref, q_ref, k_ref, v_ref, qm_ref, *, head_dim, rot, nq, nk, nv):
    hb = _rms(x_ref[...], g_ref[...]).astype(BF16)
    tab = tab_ref[...]

    def mm(a, b):
        return jnp.dot(hb, w_ref[:, a:b], preferred_element_type=F32)

    scale = head_dim ** -0.5
    for j in range(0, nq, 2 * LANES):
        y = mm(j, j + 2 * LANES)
        for u in range(2):
            t = _rope_tile(y[:, u * LANES:(u + 1) * LANES], tab, rot // 2)
            q_ref[:, j + u * LANES:j + (u + 1) * LANES] = (t * scale).astype(BF16)
    for j in range(0, nk, 2 * LANES):
        y = mm(nq + j, nq + j + 2 * LANES)
        for u in range(2):
            t = _rope_tile(y[:, u * LANES:(u + 1) * LANES], tab, rot // 2)
            k_ref[:, j + u * LANES:j + (u + 1) * LANES] = t.astype(BF16)
    for j in range(0, nv, 2 * LANES):
        v_ref[:, j:j + 2 * LANES] = mm(nq + nk + j, nq + nk + j + 2 * LANES).astype(BF16)
    o = nq + nk + nv
    qm_ref[...] = (mm(o, o + MEM_WIDTH) * MEM_HEAD_DIM ** -0.5).astype(BF16)


def _proj_qkv(x, g, w, tab, tm, head_dim, rot, nq, nk, nv):
    B, S, D = x.shape
    n = w.shape[1]
    row = lambda width: pl.BlockSpec((None, tm, width), lambda b, i: (b, i, 0))
    const = lambda shape: pl.BlockSpec(shape, lambda b, i: (0,) * len(shape))
    return pl.pallas_call(
        functools.partial(_proj_qkv_kernel, head_dim=head_dim, rot=rot, nq=nq, nk=nk, nv=nv),
        grid=(B, S // tm),
        in_specs=[row(D), const((1, D)), const((D, n)), row(3 * LANES)],
        out_specs=(row(nq), row(nk), row(nv), row(MEM_WIDTH)),
        out_shape=(jax.ShapeDtypeStruct((B, S, nq), BF16), jax.ShapeDtypeStruct((B, S, nk), BF16),
                   jax.ShapeDtypeStruct((B, S, nv), BF16), jax.ShapeDtypeStruct((B, S, MEM_WIDTH), BF16)),
        compiler_params=_cparams(("parallel", "parallel")),
        name="proj_qkv",
    )(x, g, w, tab)


def _prep_b_w_in(w):
    d = w.shape[0]
    nq, nkv = B_HEADS * B_HEAD_DIM, B_KV_HEADS * B_HEAD_DIM
    q, k, v, qm = w[:, :nq], w[:, nq:nq + nkv], w[:, nq + nkv:nq + 2 * nkv], w[:, nq + 2 * nkv:]
    z = jnp.zeros((d, B_HEAD_DIM), w.dtype)

    def spread(t):
        cols = []
        for h in range(B_KV_HEADS):
            th = t[:, h * B_HEAD_DIM:(h + 1) * B_HEAD_DIM]
            cols += [th, z, z, th]
        return jnp.concatenate(cols, axis=1)

    return jnp.concatenate([q, spread(k), spread(v), qm], axis=1).astype(BF16)


def _dsa_kernel(q_ref, qi_ref, wi_ref, kx_ref, ct_ref, ki_ref, wuk_ref, wuv_ref, o_ref,
                keys_ref, qext_ref, qi2_ref, wrow_ref, acc_ref, sa_ref, sb_ref, *, topk):
    kc = KEY_CHUNK
    hw = A_HEADS * LANES
    i = pl.program_id(1)
    nch = (i * BLOCK + BLOCK + kc - 1) // kc
    lane = lax.broadcasted_iota(I32, (BLOCK, LANES), 1)

    for h in range(A_HEADS):
        rows = slice(h * BLOCK, (h + 1) * BLOCK)
        qh = q_ref[:, h * LANES:(h + 1) * LANES]
        qext_ref[rows, 0:A_KV_RANK] = jnp.dot(qh, wuk_ref[h], preferred_element_type=F32).astype(BF16)
        qext_ref[rows, A_KV_RANK:A_KV_RANK + LANES] = jnp.where(lane < A_ROPE_DIM, qh.astype(F32), 0.0).astype(BF16)
        qi2_ref[rows, :] = qi_ref[:, h * LANES:(h + 1) * LANES]
    w_t = wi_ref[...].T
    for j in range(IDX_HEADS // 2):
        wrow_ref[0:1, j * LANES:(j + 1) * LANES] = w_t[2 * j:2 * j + 1, :]
        wrow_ref[1:2, j * LANES:(j + 1) * LANES] = w_t[2 * j + 1:2 * j + 2, :]

    qpos = i * BLOCK + lax.broadcasted_iota(I32, (kc, BLOCK), 1)
    krow = lax.broadcasted_iota(I32, (kc, BLOCK), 0)

    def score_body(c, carry):
        off = pl.multiple_of(c * kc, kc)
        kk = ki_ref[pl.ds(off, kc), :]
        qi2 = qi2_ref[...]
        dlo = lax.dot_general(kk[:, 0:LANES], qi2, _NT, preferred_element_type=F32)
        dhi = lax.dot_general(kk[:, LANES:2 * LANES], qi2, _NT, preferred_element_type=F32)
        t = jnp.maximum(dlo, 0.0) * wrow_ref[0:1, :] + jnp.maximum(dhi, 0.0) * wrow_ref[1:2, :]
        acc = t[:, 0:LANES]
        for j in range(1, IDX_HEADS // 2):
            acc = acc + t[:, j * LANES:(j + 1) * LANES]
        bits = pltpu.bitcast(acc, I32)
        key = jnp.where(bits < 0, bits ^ 0x7FFFFFFF, bits)
        keys_ref[pl.ds(off, kc), :] = jnp.where(off + krow <= qpos, key, INT_MIN)
        return carry

    lax.fori_loop(0, nch, score_body, 0)

    @pl.when(nch % 2 == 1)
    def _():
        keys_ref[pl.ds(pl.multiple_of(nch * kc, kc), kc), :] = jnp.full((kc, BLOCK), INT_MIN, I32)

    def bit_body(p, res):
        trial = res | lax.shift_left(jnp.int32(1), 31 - p)
        thr = trial ^ INT_MIN

        def cnt_body(c, cnt):
            off = pl.multiple_of(c * 2 * kc, 2 * kc)
            ge = (keys_ref[pl.ds(off, 2 * kc), :] >= thr).astype(I32)
            return cnt + jnp.sum(ge.reshape(2 * kc // 8, 8, BLOCK), axis=0)

        cnt = lax.fori_loop(0, (nch + 1) // 2, cnt_body, jnp.zeros((8, BLOCK), I32))
        return jnp.where(jnp.sum(cnt, axis=0, keepdims=True) >= topk, trial, res)

    res = lax.fori_loop(0, 32, bit_body, jnp.zeros((1, BLOCK), I32))
    thr = jnp.maximum(res ^ INT_MIN, INT_MIN + 1)

    acc_ref[...] = jnp.zeros(acc_ref.shape, F32)

    npairs = (nch + 1) // 2

    def logits(c):
        off = pl.multiple_of(c * kc, kc)
        bias = jnp.where(keys_ref[pl.ds(off, kc), :] >= thr, 0.0, NEG_INF)
        s = lax.dot_general(kx_ref[pl.ds(off, kc), :], qext_ref[...], _NT, preferred_element_type=F32)
        return s + jnp.concatenate([bias] * A_HEADS, axis=1)

    def consume(s, c, carry):
        m_prev, l_prev = carry
        m_new = jnp.maximum(m_prev, jnp.max(s, axis=0, keepdims=True))
        alpha = jnp.exp(m_prev - m_new)
        p = jnp.exp(s - m_new)
        l_new = alpha * l_prev + jnp.sum(p, axis=0, keepdims=True)
        acc_ref[...] = alpha * acc_ref[...] + jnp.dot(ct_ref[c], p.astype(BF16), preferred_element_type=F32)
        return m_new, l_new

    sa_ref[...] = logits(0)

    def pair_body(j, carry):
        c0 = 2 * j
        sb_ref[...] = logits(c0 + 1)
        carry = consume(sa_ref[...], c0, carry)
        sa_ref[...] = logits(jnp.minimum(c0 + 2, 2 * npairs - 1))
        return consume(sb_ref[...], c0 + 1, carry)

    _, l_fin = lax.fori_loop(0, npairs, pair_body,
                             (jnp.full((1, hw), NEG_INF, F32), jnp.zeros((1, hw), F32)))

    inv_l = 1.0 / l_fin
    for h in range(A_HEADS):
        cols = slice(h * LANES, (h + 1) * LANES)
        o_lat = (acc_ref[:, cols] * inv_l[:, cols]).T.astype(BF16)
        o_ref[:, cols] = jnp.dot(o_lat, wuv_ref[h], preferred_element_type=F32).astype(BF16)


def _dsa(q, qi, wi, kx, ct, ki, wuk, wuv):
    B, S, _ = q.shape
    topk = min(TOPK_MAX, S // 4)
    blk = lambda width: pl.BlockSpec((None, BLOCK, width), lambda b, i: (b, i, 0))
    seq = lambda width: pl.BlockSpec((None, S, width), lambda b, i: (b, 0, 0))
    const3 = lambda shape: pl.BlockSpec(shape, lambda b, i: (0, 0, 0))
    return pl.pallas_call(
        functools.partial(_dsa_kernel, topk=topk),
        grid=(B, S // BLOCK),
        in_specs=[blk(A_HEADS * A_QK_DIM), blk(IDX_HEADS * IDX_DIM), blk(LANES),
                  seq(A_KV_RANK + LANES),
                  pl.BlockSpec((None, S // KEY_CHUNK, A_KV_RANK, KEY_CHUNK), lambda b, i: (b, 0, 0, 0)),
                  seq(2 * LANES), const3(wuk.shape), const3(wuv.shape)],
        out_specs=blk(A_HEADS * LANES),
        out_shape=jax.ShapeDtypeStruct((B, S, A_HEADS * LANES), BF16),
        scratch_shapes=[
            pltpu.VMEM((S, BLOCK), I32),
            pltpu.VMEM((A_HEADS * BLOCK, A_KV_RANK + LANES), BF16),
            pltpu.VMEM((A_HEADS * BLOCK, LANES), BF16),
            pltpu.VMEM((8, A_HEADS * LANES), F32),
            pltpu.VMEM((A_KV_RANK, A_HEADS * LANES), F32),
            pltpu.VMEM((KEY_CHUNK, A_HEADS * LANES), F32),
            pltpu.VMEM((KEY_CHUNK, A_HEADS * LANES), F32),
        ],
        compiler_params=_cparams(("parallel", "arbitrary")),
        name="dsa",
    )(q, qi, wi, kx, ct, ki, wuk, wuv)


def _banded_kernel(*refs, subheads, max_dist, has_sink, has_prev):
    refs = list(refs)
    sink_ref = refs.pop(0) if has_sink else None
    q_ref, kp_ref, kc_ref, vp_ref, vc_ref = refs[:5]
    refs = refs[5:]
    if has_prev:
        oprev_ref, lprev_ref = refs[:2]
        refs = refs[2:]
    o_ref, lse_ref = refs

    n = pl.program_id(2)
    qi = lax.broadcasted_iota(I32, (BLOCK, 2 * BLOCK), 0)
    kj = lax.broadcasted_iota(I32, (BLOCK, 2 * BLOCK), 1)
    dist = BLOCK + qi - kj
    mask = (dist >= 0) & (dist <= max_dist) & ((kj >= BLOCK) | (n > 0))
    lane = lax.broadcasted_iota(I32, (BLOCK, LANES), 1)
    lse_tile = jnp.zeros((BLOCK, LANES), F32)

    for j, subs in enumerate(subheads):
        qt = q_ref[:, j * LANES:(j + 1) * LANES]
        o_t = None
        lse_t = None
        for kt, vt, hidx in subs:
            kk = jnp.concatenate([kp_ref[:, kt * LANES:(kt + 1) * LANES],
                                  kc_ref[:, kt * LANES:(kt + 1) * LANES]], axis=0)
            vv = jnp.concatenate([vp_ref[:, vt * LANES:(vt + 1) * LANES],
                                  vc_ref[:, vt * LANES:(vt + 1) * LANES]], axis=0)
            s = lax.dot_general(qt, kk, _NT, preferred_element_type=F32)
            s = jnp.where(mask, s, NEG_INF)
            m = jnp.max(s, axis=-1, keepdims=True)
            if has_sink:
                sk = sink_ref[hidx]
                m = jnp.maximum(m, sk)
            p = jnp.exp(s - m)
            l = jnp.sum(p, axis=-1, keepdims=True)
            if has_sink:
                l = l + jnp.exp(sk - m)
            o_s = jnp.dot(p.astype(BF16), vv, preferred_element_type=F32) / l
            o_t = o_s if o_t is None else o_t + o_s
            lse_t = m + jnp.log(l)
        if has_prev:
            lp = lprev_ref[:, j:j + 1]
            op = oprev_ref[:, j * LANES:(j + 1) * LANES].astype(F32)
            mx = jnp.maximum(lp, lse_t)
            wp, wn = jnp.exp(lp - mx), jnp.exp(lse_t - mx)
            den = wp + wn
            o_t = (wp * op + wn * o_t) / den
            lse_t = mx + jnp.log(den)
        o_ref[:, j * LANES:(j + 1) * LANES] = o_t.astype(o_ref.dtype)
        lse_tile = jnp.where(lane == j, lse_t, lse_tile)
    lse_ref[...] = lse_tile


def _banded(q, k, v, *, dil, subheads, max_dist, sinks=None, prev=None):
    B, S, wq = q.shape
    wk, wv = k.shape[-1], v.shape[-1]
    n = S // dil
    view = lambda t: t.reshape(B, n, dil * t.shape[-1])
    cur = lambda width: pl.BlockSpec((None, BLOCK, width), lambda b, r, i: (b, i, r))
    prv = lambda width: pl.BlockSpec((None, BLOCK, width), lambda b, r, i: (b, jnp.maximum(i - 1, 0), r))
    in_specs = [cur(wq), prv(wk), cur(wk), prv(wv), cur(wv)]
    args = [view(q), view(k), view(k), view(v), view(v)]
    if sinks is not None:
        in_specs.insert(0, pl.BlockSpec(memory_space=pltpu.SMEM))
        args.insert(0, sinks)
    if prev is not None:
        in_specs += [cur(wq), cur(LANES)]
        args += [view(prev[0]), view(prev[1])]
    o, lse = pl.pallas_call(
        functools.partial(_banded_kernel, subheads=subheads, max_dist=max_dist,
                          has_sink=sinks is not None, has_prev=prev is not None),
        grid=(B, dil, n // BLOCK),
        in_specs=in_specs,
        out_specs=(cur(wq), cur(LANES)),
        out_shape=(jax.ShapeDtypeStruct((B, n, dil * wq), BF16), jax.ShapeDtypeStruct((B, n, dil * LANES), F32)),
        compiler_params=_cparams(("parallel", "parallel", "arbitrary")),
        name="banded",
    )(*args)
    return o.reshape(B, S, wq), lse.reshape(B, S, LANES)


def _memkv_kernel(mem_ref, g_ref, w_ref, k_ref, v_ref):
    hb = _rms(mem_ref[...], g_ref[...]).astype(BF16)
    y = jnp.dot(hb, w_ref[...], preferred_element_type=F32)
    lane = lax.broadcasted_iota(I32, (y.shape[0], LANES), 1)
    for t in range(MEM_WIDTH // LANES):
        for out_ref, base in ((k_ref, 0), (v_ref, MEM_WIDTH)):
            tile = y[:, base + t * LANES:base + (t + 1) * LANES]
            out_ref[:, (2 * t) * LANES:(2 * t + 1) * LANES] = jnp.where(lane < MEM_HEAD_DIM, tile, 0.0).astype(BF16)
            out_ref[:, (2 * t + 1) * LANES:(2 * t + 2) * LANES] = jnp.where(lane >= MEM_HEAD_DIM, tile, 0.0).astype(BF16)


def _memkv(mem, g_mem, w):
    B, M, D = mem.shape
    L = w.shape[0]
    out = jax.ShapeDtypeStruct((L, B, M, 2 * MEM_WIDTH), BF16)
    ospec = pl.BlockSpec((None, None, M, 2 * MEM_WIDTH), lambda l, b: (l, b, 0, 0))
    return pl.pallas_call(
        _memkv_kernel,
        grid=(L, B),
        in_specs=[pl.BlockSpec((None, M, D), lambda l, b: (b, 0, 0)),
                  pl.BlockSpec((1, D), lambda l, b: (0, 0)),
                  pl.BlockSpec((None, D, 2 * MEM_WIDTH), lambda l, b: (l, 0, 0))],
        out_specs=(ospec, ospec),
        out_shape=(out, out),
        compiler_params=_cparams(("parallel", "parallel")),
        name="memkv",
    )(mem, g_mem, w)


def _out_kernel(x_ref, mix_ref, qm_ref, mk_ref, mv_ref, w_ref, o_ref):
    mw = mix_ref.shape[-1]
    y = x_ref[...] + jnp.dot(mix_ref[...], w_ref[0:mw, :], preferred_element_type=F32)
    for t in range(MEM_WIDTH // LANES):
        qt = qm_ref[:, t * LANES:(t + 1) * LANES]
        o_t = None
        for u in range(2):
            sl = slice((2 * t + u) * LANES, (2 * t + u + 1) * LANES)
            s = lax.dot_general(qt, mk_ref[:, sl], _NT, preferred_element_type=F32)
            p = jnp.exp(s - jnp.max(s, axis=-1, keepdims=True))
            l = jnp.sum(p, axis=-1, keepdims=True)
            o_s = jnp.dot(p.astype(BF16), mv_ref[:, sl], preferred_element_type=F32) / l
            o_t = o_s if o_t is None else o_t + o_s
        y = y + jnp.dot(o_t.astype(BF16), w_ref[mw + t * LANES:mw + (t + 1) * LANES, :],
                        preferred_element_type=F32)
    o_ref[...] = y


def _out_proj(x, mix, qm, mk, mv, w, tm):
    B, S, D = x.shape
    M = mk.shape[1]
    row = lambda width: pl.BlockSpec((None, tm, width), lambda b, i: (b, i, 0))
    mem = pl.BlockSpec((None, M, 2 * MEM_WIDTH), lambda b, i: (b, 0, 0))
    return pl.pallas_call(
        _out_kernel,
        grid=(B, S // tm),
        in_specs=[row(D), row(mix.shape[-1]), row(MEM_WIDTH), mem, mem,
                  pl.BlockSpec(w.shape, lambda b, i: (0, 0))],
        out_specs=row(D),
        out_shape=jax.ShapeDtypeStruct((B, S, D), F32),
        compiler_params=_cparams(("parallel", "parallel")),
        name="out_proj",
    )(x, mix, qm, mk, mv, w)


def _ffn_kernel(x_ref, g_ref, wup_ref, cw_ref, cb_ref, wdn_ref, gf_ref, o_ref, carry_ref, *, tm, cw, final):
    dff = wdn_ref.shape[0]
    first = pl.program_id(1) == 0
    x = x_ref[...]
    hb = _rms(x, g_ref[...]).astype(BF16)
    row = lax.broadcasted_iota(I32, (tm, cw), 0)
    acc = x
    for c0 in range(0, dff, cw):
        a = jnp.dot(hb, wup_ref[:, c0:c0 + cw], preferred_element_type=F32)
        b = jnp.dot(hb, wup_ref[:, dff + c0:dff + c0 + cw], preferred_element_type=F32)
        prev = jnp.where(first, 0.0, carry_ref[:, c0:c0 + cw])
        p1, p2 = prev[7:8, :], prev[6:7, :]
        a1 = jnp.where(row == 0, p1, pltpu.roll(a, 1, 0))
        a2 = jnp.where(row == 0, p2, jnp.where(row == 1, p1, pltpu.roll(a, 2, 0)))
        carry_ref[:, c0:c0 + cw] = a[tm - 8:tm, :]
        w = cw_ref[:, c0:c0 + cw]
        conv = w[0:1, :] * a2 + w[1:2, :] * a1 + w[2:3, :] * a + cb_ref[:, c0:c0 + cw]
        gated = conv / (1.0 + jnp.exp(-conv)) * b
        acc = acc + jnp.dot(gated.astype(BF16), wdn_ref[c0:c0 + cw, :], preferred_element_type=F32)
    if final:
        acc = _rms(acc, gf_ref[...])
    o_ref[...] = acc


def _ffn(x, g, wup, cw, cb, wdn, gf, tm, final):
    B, S, D = x.shape
    dff = wdn.shape[0]
    row = pl.BlockSpec((None, tm, D), lambda b, i: (b, i, 0))
    const = lambda shape: pl.BlockSpec(shape, lambda b, i: (0, 0), pipeline_mode=pl.Buffered(1))
    return pl.pallas_call(
        functools.partial(_ffn_kernel, tm=tm, cw=2 * LANES, final=final),
        grid=(B, S // tm),
        in_specs=[row, const((1, D)), const(wup.shape), const(cw.shape), const((1, dff)), const(wdn.shape),
                  const((1, D))],
        out_specs=row,
        out_shape=jax.ShapeDtypeStruct((B, S, D), F32),
        scratch_shapes=[pltpu.VMEM((8, dff), F32)],
        compiler_params=_cparams(("arbitrary", "arbitrary")),
        name="ffn",
    )(x, g, wup, cw, cb, wdn, gf)


def _b_subheads():
    return tuple(((2 * (j // 2), 2 * (j // 2), 2 * j), (2 * (j // 2) + 1, 2 * (j // 2) + 1, 2 * j + 1))
                 for j in range(B_HEADS // 2))


def _c_subheads():
    return tuple(((j, j, j),) for j in range(C_HEADS))


def kernel(x, mem, positions, g_mix, g_ffn, g_mem, g_final, w_mem_kv, a_w_in, a_kv_norm, a_w_uk, a_w_uv, a_w_out,
           b_w_in, b_sinks, b_w_out, c_w_in, c_w_out, f_w_up, f_conv_w, f_conv_b, f_w_down):
    B, S, D = x.shape
    depth = g_mix.shape[0]
    tm = min(512, S)
    tab32 = _rope_table(positions, A_QK_DIM, A_ROPE_DIM)
    tab16 = _rope_table(positions, B_HEAD_DIM, B_HEAD_DIM // 4)
    mem_k, mem_v = _memkv(mem, g_mem.reshape(1, D), w_mem_kv.astype(BF16))
    conv_w = jnp.pad(f_conv_w, ((0, 0), (0, 8 - CONV_WIDTH), (0, 0)))
    for i in range(depth):
        kind, j = i % 3, i // 3
        g = g_mix[i].reshape(1, D)
        if kind == 0:
            q, kx, ct, qi, ki, wi, qm = _proj_a(x, g, _prep_a_w_in(a_w_in[j]), a_kv_norm[j].reshape(1, -1),
                                                tab32, tab16, tm)
            wuk = jnp.pad(jnp.transpose(a_w_uk[j], (1, 2, 0)), ((0, 0), (A_ROPE_DIM, 0), (0, 0))).astype(BF16)
            wuv = jnp.transpose(a_w_uv[j], (1, 0, 2)).astype(BF16)
            mix = _dsa(q, qi, wi, kx, ct, ki, wuk, wuv)
            w_out = a_w_out[j]
        elif kind == 1:
            nq, nkv = B_HEADS * B_HEAD_DIM, 4 * B_KV_HEADS * B_HEAD_DIM
            q, k, v, qm = _proj_qkv(x, g, _prep_b_w_in(b_w_in[j]), tab16, tm, B_HEAD_DIM, B_HEAD_DIM // 4,
                                    nq, nkv, nkv)
            mix, _ = _banded(q, k, v, dil=1, subheads=_b_subheads(), max_dist=B_WINDOW - 1, sinks=b_sinks[j])
            w_out = b_w_out[j]
        else:
            nq = C_HEADS * C_HEAD_DIM
            q, k, v, qm = _proj_qkv(x, g, c_w_in[j].astype(BF16), tab32, tm, C_HEAD_DIM, C_HEAD_DIM // 4,
                                    nq, nq, nq)
            prev = None
            for window, dil in C_BRANCHES:
                prev = _banded(q, k, v, dil=dil, subheads=_c_subheads(), max_dist=window // dil, prev=prev)
            mix = prev[0]
            w_out = c_w_out[j]
        x = _out_proj(x, mix, qm, mem_k[i], mem_v[i], w_out.astype(BF16), tm)
        x = _ffn(x, g_ffn[i].reshape(1, D), f_w_up[i].astype(BF16), conv_w[i], f_conv_b[i].reshape(1, -1),
                 f_w_down[i].astype(BF16), g_final.reshape(1, D), min(256, S), i == depth - 1)
    return x
```

```python
import functools

import jax
import jax.numpy as jnp
from jax import lax
from jax.experimental import pallas as pl
from jax.experimental.pallas import tpu as pltpu

F32 = jnp.float32
BF16 = jnp.bfloat16
I32 = jnp.int32
I16 = jnp.int16

LANES = 128
BLOCK = 128
ROPE_THETA = 500000.0
EPS = 1e-6
NEG_INF = -1e30
INT_MIN = -(2**31)

A_HEADS = 8
A_QK_DIM = 128
A_ROPE_DIM = 32
A_KV_RANK = 256
IDX_HEADS = 16
IDX_DIM = 64
TOPK_MAX = 256
B_HEADS = 16
B_KV_HEADS = 4
B_HEAD_DIM = 64
B_WINDOW = 128
C_HEADS = 8
C_HEAD_DIM = 128
C_BRANCHES = ((128, 1), (512, 4), (2048, 16))
MEM_HEADS = 4
MEM_HEAD_DIM = 64
MEM_WIDTH = MEM_HEADS * MEM_HEAD_DIM
CONV_WIDTH = 3
KEY_CHUNK = 256

VMEM_LIMIT = 56 * 1024 * 1024

_NT = (((1,), (1,)), ((), ()))


def _cparams(sem):
    return pltpu.CompilerParams(dimension_semantics=sem, vmem_limit_bytes=VMEM_LIMIT)


def _rms(x, g):
    return x * lax.rsqrt(jnp.mean(x * x, axis=-1, keepdims=True) + EPS) * g


def _rope_tile(t, tab, half):
    c, sa, sb = tab[:, 0:LANES], tab[:, LANES:2 * LANES], tab[:, 2 * LANES:3 * LANES]
    return t * c + pltpu.roll(t, half, 1) * sa + pltpu.roll(t, LANES - half, 1) * sb


def _rope_table(positions, head_dim, rot):
    half = rot // 2
    inv = ROPE_THETA ** (-jnp.arange(0, rot, 2, dtype=F32) / rot)
    ang = positions.astype(F32)[..., None] * inv
    cos, sin = jnp.cos(ang), jnp.sin(ang)
    rest = head_dim - rot
    lead = cos.shape[:-1]
    c = jnp.concatenate([cos, cos, jnp.ones(lead + (rest,), F32)], axis=-1)
    sa = jnp.concatenate([jnp.zeros_like(sin), sin, jnp.zeros(lead + (rest,), F32)], axis=-1)
    sb = jnp.concatenate([-sin, jnp.zeros_like(sin), jnp.zeros(lead + (rest,), F32)], axis=-1)
    reps = LANES // head_dim
    return jnp.concatenate([jnp.tile(t, (1, 1, reps)) for t in (c, sa, sb)], axis=-1)


def _proj_a_kernel(x_ref, g_ref, w_ref, kvn_ref, t32_ref, t16_ref,
                   q_ref, kx_ref, ct_ref, qi_ref, ki_ref, wi_ref, qm_ref, *, tm):
    hb = _rms(x_ref[...], g_ref[...]).astype(BF16)
    t32 = t32_ref[...]
    t16 = t16_ref[...]

    def mm(a, b):
        return jnp.dot(hb, w_ref[:, a:b], preferred_element_type=F32)

    scale = A_QK_DIM ** -0.5
    for j in range(0, A_HEADS * A_QK_DIM, 2 * LANES):
        y = mm(j, j + 2 * LANES)
        for u in range(2):
            t = _rope_tile(y[:, u * LANES:(u + 1) * LANES], t32, A_ROPE_DIM // 2)
            q_ref[:, j + u * LANES:j + (u + 1) * LANES] = (t * scale).astype(BF16)
    o = A_HEADS * A_QK_DIM
    c = _rms(mm(o, o + A_KV_RANK), kvn_ref[...])
    kx_ref[:, 0:A_KV_RANK] = c.astype(BF16)
    for u in range(tm // KEY_CHUNK):
        ct_ref[u] = c[u * KEY_CHUNK:(u + 1) * KEY_CHUNK, :].T.astype(BF16)
    o += A_KV_RANK
    kx_ref[:, A_KV_RANK:A_KV_RANK + LANES] = _rope_tile(mm(o, o + LANES), t32, A_ROPE_DIM // 2).astype(BF16)
    o += LANES
    for j in range(0, IDX_HEADS * IDX_DIM, 2 * LANES):
        y = mm(o + j, o + j + 2 * LANES)
        for u in range(2):
            t = _rope_tile(y[:, u * LANES:(u + 1) * LANES], t16, IDX_DIM // 8)
            qi_ref[:, j + u * LANES:j + (u + 1) * LANES] = t.astype(BF16)
    o += IDX_HEADS * IDX_DIM
    y = mm(o, o + 2 * LANES)
    for u in range(2):
        t = _rope_tile(y[:, u * LANES:(u + 1) * LANES], t16, IDX_DIM // 8).astype(BF16)
        for v in range(tm // KEY_CHUNK):
            ki_ref[v, u * KEY_CHUNK:(u + 1) * KEY_CHUNK, :] = t[v * KEY_CHUNK:(v + 1) * KEY_CHUNK, :]
    o += 2 * LANES
    wi_ref[...] = mm(o, o + LANES) * (IDX_HEADS * IDX_DIM) ** -0.5
    o += LANES
    qm_ref[...] = (mm(o, o + MEM_WIDTH) * MEM_HEAD_DIM ** -0.5).astype(BF16)


def _prep_a_w_in(w):
    d = w.shape[0]
    sizes = (A_HEADS * A_QK_DIM, A_KV_RANK, A_ROPE_DIM, IDX_HEADS * IDX_DIM, IDX_DIM, IDX_HEADS, MEM_WIDTH)
    offs = [0]
    for s in sizes:
        offs.append(offs[-1] + s)
    q, ckv, kr, qi, ki, wi, qm = [w[:, offs[i]:offs[i + 1]] for i in range(len(sizes))]
    z = lambda n: jnp.zeros((d, n), w.dtype)
    return jnp.concatenate([
        q, ckv, kr, z(LANES - A_ROPE_DIM), qi,
        ki, z(LANES - IDX_DIM), z(LANES - IDX_DIM), ki,
        wi, z(LANES - IDX_HEADS), qm], axis=1).astype(BF16)


def _proj_a(x, g, w, kvn, t32, t16, tm):
    B, S, D = x.shape
    n = w.shape[1]
    row = lambda width: pl.BlockSpec((None, tm, width), lambda b, i: (b, i, 0))
    const = lambda shape: pl.BlockSpec(shape, lambda b, i: (0,) * len(shape))
    out_shape = (
        jax.ShapeDtypeStruct((B, S, A_HEADS * A_QK_DIM), BF16),
        jax.ShapeDtypeStruct((B, S, A_KV_RANK + LANES), BF16),
        jax.ShapeDtypeStruct((B, S // KEY_CHUNK, A_KV_RANK, KEY_CHUNK), BF16),
        jax.ShapeDtypeStruct((B, S, IDX_HEADS * IDX_DIM), BF16),
        jax.ShapeDtypeStruct((B, S // KEY_CHUNK, 2 * KEY_CHUNK, LANES), BF16),
        jax.ShapeDtypeStruct((B, S, LANES), F32),
        jax.ShapeDtypeStruct((B, S, MEM_WIDTH), BF16),
    )
    out_specs = (
        row(A_HEADS * A_QK_DIM), row(A_KV_RANK + LANES),
        pl.BlockSpec((None, tm // KEY_CHUNK, A_KV_RANK, KEY_CHUNK), lambda b, i: (b, i, 0, 0)),
        row(IDX_HEADS * IDX_DIM),
        pl.BlockSpec((None, tm // KEY_CHUNK, 2 * KEY_CHUNK, LANES), lambda b, i: (b, i, 0, 0)),
        row(LANES), row(MEM_WIDTH),
    )
    return pl.pallas_call(
        functools.partial(_proj_a_kernel, tm=tm),
        grid=(B, S // tm),
        in_specs=[row(D), const((1, D)), const((D, n)), const((1, A_KV_RANK)), row(3 * LANES), row(3 * LANES)],
        out_specs=out_specs,
        out_shape=out_shape,
        compiler_params=_cparams(("parallel", "parallel")),
        name="proj_a",
    )(x, g, w, kvn, t32, t16)


def _proj_qkv_kernel(x_ref, g_ref, w_ref, tab_ref, q_ref, k_ref, v_ref, qm_ref, *, head_dim, rot, nq, nk, nv):
    hb = _rms(x_ref[...], g_ref[...]).astype(BF16)
    tab = tab_ref[...]

    def mm(a, b):
        return jnp.dot(hb, w_ref[:, a:b], preferred_element_type=F32)

    scale = head_dim ** -0.5
    for j in range(0, nq, 2 * LANES):
        y = mm(j, j + 2 * LANES)
        for u in range(2):
            t = _rope_tile(y[:, u * LANES:(u + 1) * LANES], tab, rot // 2)
            q_ref[:, j + u * LANES:j + (u + 1) * LANES] = (t * scale).astype(BF16)
    for j in range(0, nk, 2 * LANES):
        y = mm(nq + j, nq + j + 2 * LANES)
        for u in range(2):
            t = _rope_tile(y[:, u * LANES:(u + 1) * LANES], tab, rot // 2)
            k_ref[:, j + u * LANES:j + (u + 1) * LANES] = t.astype(BF16)
    for j in range(0, nv, 2 * LANES):
        v_ref[:, j:j + 2 * LANES] = mm(nq + nk + j, nq + nk + j + 2 * LANES).astype(BF16)
    o = nq + nk + nv
    qm_ref[...] = (mm(o, o + MEM_WIDTH) * MEM_HEAD_DIM ** -0.5).astype(BF16)


def _proj_qkv(x, g, w, tab, tm, head_dim, rot, nq, nk, nv):
    B, S, D = x.shape
    n = w.shape[1]
    row = lambda width: pl.BlockSpec((None, tm, width), lambda b, i: (b, i, 0))
    const = lambda shape: pl.BlockSpec(shape, lambda b, i: (0,) * len(shape))
    return pl.pallas_call(
        functools.partial(_proj_qkv_kernel, head_dim=head_dim, rot=rot, nq=nq, nk=nk, nv=nv),
        grid=(B, S // tm),
        in_specs=[row(D), const((1, D)), const((D, n)), row(3 * LANES)],
        out_specs=(row(nq), row(nk), row(nv), row(MEM_WIDTH)),
        out_shape=(jax.ShapeDtypeStruct((B, S, nq), BF16), jax.ShapeDtypeStruct((B, S, nk), BF16),
                   jax.ShapeDtypeStruct((B, S, nv), BF16), jax.ShapeDtypeStruct((B, S, MEM_WIDTH), BF16)),
        compiler_params=_cparams(("parallel", "parallel")),
        name="proj_qkv",
    )(x, g, w, tab)


def _prep_b_w_in(w):
    d = w.shape[0]
    nq, nkv = B_HEADS * B_HEAD_DIM, B_KV_HEADS * B_HEAD_DIM
    q, k, v, qm = w[:, :nq], w[:, nq:nq + nkv], w[:, nq + nkv:nq + 2 * nkv], w[:, nq + 2 * nkv:]
    z = jnp.zeros((d, B_HEAD_DIM), w.dtype)

    def spread(t):
        cols = []
        for h in range(B_KV_HEADS):
            th = t[:, h * B_HEAD_DIM:(h + 1) * B_HEAD_DIM]
            cols += [th, z, z, th]
        return jnp.concatenate(cols, axis=1)

    return jnp.concatenate([q, spread(k), spread(v), qm], axis=1).astype(BF16)


def _dsa_kernel(q_ref, qi_ref, wi_ref, kx_ref, ct_ref, ki_ref, wuk_ref, wuv_ref, o_ref,
                keys_ref, half_ref, qext_ref, qi2_ref, wrow_ref, acc_ref, sa_ref, sb_ref, *, topk):
    kc = KEY_CHUNK
    hw = A_HEADS * LANES
    gw = 2 * LANES
    i = pl.program_id(1)
    nch = (i * BLOCK + BLOCK + kc - 1) // kc
    lane = lax.broadcasted_iota(I32, (BLOCK, LANES), 1)

    for h in range(A_HEADS):
        rows = slice(h * BLOCK, (h + 1) * BLOCK)
        qh = q_ref[:, h * LANES:(h + 1) * LANES]
        qext_ref[rows, 0:A_KV_RANK] = jnp.dot(qh, wuk_ref[h], preferred_element_type=F32).astype(BF16)
        qext_ref[rows, A_KV_RANK:A_KV_RANK + LANES] = jnp.where(lane < A_ROPE_DIM, qh.astype(F32), 0.0).astype(BF16)
        qi2_ref[rows, :] = qi_ref[:, h * LANES:(h + 1) * LANES]
    w_t = wi_ref[...].T
    for j in range(IDX_HEADS // 2):
        wrow_ref[0:1, j * LANES:(j + 1) * LANES] = w_t[2 * j:2 * j + 1, :]
        wrow_ref[1:2, j * LANES:(j + 1) * LANES] = w_t[2 * j + 1:2 * j + 2, :]

    qpos = i * BLOCK + lax.broadcasted_iota(I32, (kc, BLOCK), 1)
    krow = lax.broadcasted_iota(I32, (kc, BLOCK), 0)

    def score_chunk(c):
        off = pl.multiple_of(c * kc, kc)
        kk = ki_ref[c]
        acc = None
        for g in range(0, hw, gw):
            d = lax.dot_general(kk, qi2_ref[g:g + gw, :], _NT, preferred_element_type=F32)
            t = (jnp.maximum(d[0:kc], 0.0) * wrow_ref[0:1, g:g + gw]
                 + jnp.maximum(d[kc:2 * kc], 0.0) * wrow_ref[1:2, g:g + gw])
            for u in range(0, gw, LANES):
                acc = t[:, u:u + LANES] if acc is None else acc + t[:, u:u + LANES]
        bits = pltpu.bitcast(acc, I32)
        key = jnp.where(bits < 0, bits ^ 0x7FFFFFFF, bits)
        key = jnp.where(off + krow <= qpos, key, INT_MIN)
        keys_ref[pl.ds(off, kc), :] = key
        half_ref[pl.ds(off, kc), :] = (key >> 16).astype(I16)

    npairs = (nch + 1) // 2

    def score_body(j, carry):
        score_chunk(2 * j)
        score_chunk(2 * j + 1)
        return carry

    lax.fori_loop(0, npairs, score_body, 0)

    def kth_largest_half():
        def bit_body(p, res):
            trial = res | lax.shift_left(jnp.int32(1), 15 - p)
            thr16 = (trial - 2**15).astype(I16)

            def cnt_body(c, cnt):
                off = pl.multiple_of(c * 2 * kc, 2 * kc)
                ge = jnp.where(half_ref[pl.ds(off, 2 * kc), :] >= thr16, jnp.int16(1), jnp.int16(0))
                parts = [ge[r:r + 16, :] for r in range(0, 2 * kc, 16)]
                while len(parts) > 1:
                    parts = [a + b for a, b in zip(parts[::2], parts[1::2])]
                return cnt + parts[0]

            cnt = lax.fori_loop(0, npairs, cnt_body, jnp.zeros((16, BLOCK), I16))
            total = jnp.sum(cnt.astype(I32), axis=0, keepdims=True)
            return jnp.where(total >= topk, trial, res)

        return lax.fori_loop(0, 16, bit_body, jnp.zeros((1, BLOCK), I32))

    top = kth_largest_half() - 2**15

    def low_body(c, carry):
        off = pl.multiple_of(c * 2 * kc, 2 * kc)
        key = keys_ref[pl.ds(off, 2 * kc), :]
        hi = key >> 16
        low = (key & 0xFFFF) - 2**15
        half_ref[pl.ds(off, 2 * kc), :] = jnp.where(
            hi == top, low, jnp.where(hi > top, 2**15 - 1, -(2**15))).astype(I16)
        return carry

    lax.fori_loop(0, npairs, low_body, 0)
    thr = lax.shift_left(top, 16) | kth_largest_half()
    thr = jnp.maximum(thr, INT_MIN + 1)

    acc_ref[...] = jnp.zeros(acc_ref.shape, F32)

    def logits(c, s_ref):
        off = pl.multiple_of(c * kc, kc)
        bias = jnp.where(keys_ref[pl.ds(off, kc), :] >= thr, 0.0, NEG_INF)
        bias = jnp.concatenate([bias] * (gw // LANES), axis=1)
        kx = kx_ref[pl.ds(off, kc), :]
        for g in range(0, hw, gw):
            s_ref[:, g:g + gw] = lax.dot_general(
                kx, qext_ref[g:g + gw, :], _NT, preferred_element_type=F32) + bias

    def consume(s_ref, c, carry):
        m_prev, l_prev = carry
        ct = ct_ref[c]
        m_out, l_out = [], []
        for g in range(0, hw, gw):
            s = s_ref[:, g:g + gw]
            m_new = jnp.maximum(m_prev[:, g:g + gw], jnp.max(s, axis=0, keepdims=True))
            alpha = jnp.exp(m_prev[:, g:g + gw] - m_new)
            p = jnp.exp(s - m_new)
            l_out.append(alpha * l_prev[:, g:g + gw] + jnp.sum(p, axis=0, keepdims=True))
            m_out.append(m_new)
            acc_ref[:, g:g + gw] = alpha * acc_ref[:, g:g + gw] + jnp.dot(
                ct, p.astype(BF16), preferred_element_type=F32)
        return jnp.concatenate(m_out, axis=1), jnp.concatenate(l_out, axis=1)

    logits(0, sa_ref)

    def pair_body(j, carry):
        c0 = 2 * j
        logits(c0 + 1, sb_ref)
        carry = consume(sa_ref, c0, carry)
        logits(jnp.minimum(c0 + 2, 2 * npairs - 1), sa_ref)
        return consume(sb_ref, c0 + 1, carry)

    _, l_fin = lax.fori_loop(0, npairs, pair_body,
                             (jnp.full((1, hw), NEG_INF, F32), jnp.zeros((1, hw), F32)))

    inv_l = 1.0 / l_fin
    for h in range(A_HEADS):
        cols = slice(h * LANES, (h + 1) * LANES)
        o_lat = (acc_ref[:, cols] * inv_l[:, cols]).T.astype(BF16)
        o_ref[:, cols] = jnp.dot(o_lat, wuv_ref[h], preferred_element_type=F32).astype(BF16)


def _dsa(q, qi, wi, kx, ct, ki, wuk, wuv):
    B, S, _ = q.shape
    topk = min(TOPK_MAX, S // 4)
    blk = lambda width: pl.BlockSpec((None, BLOCK, width), lambda b, i: (b, i, 0))
    seq = lambda width: pl.BlockSpec((None, S, width), lambda b, i: (b, 0, 0))
    const3 = lambda shape: pl.BlockSpec(shape, lambda b, i: (0, 0, 0))
    return pl.pallas_call(
        functools.partial(_dsa_kernel, topk=topk),
        grid=(B, S // BLOCK),
        in_specs=[blk(A_HEADS * A_QK_DIM), blk(IDX_HEADS * IDX_DIM), blk(LANES),
                  seq(A_KV_RANK + LANES),
                  pl.BlockSpec((None, S // KEY_CHUNK, A_KV_RANK, KEY_CHUNK), lambda b, i: (b, 0, 0, 0)),
                  pl.BlockSpec((None, S // KEY_CHUNK, 2 * KEY_CHUNK, LANES), lambda b, i: (b, 0, 0, 0)),
                  const3(wuk.shape), const3(wuv.shape)],
        out_specs=blk(A_HEADS * LANES),
        out_shape=jax.ShapeDtypeStruct((B, S, A_HEADS * LANES), BF16),
        scratch_shapes=[
            pltpu.VMEM((S, BLOCK), I32),
            pltpu.VMEM((S, BLOCK), I16),
            pltpu.VMEM((A_HEADS * BLOCK, A_KV_RANK + LANES), BF16),
            pltpu.VMEM((A_HEADS * BLOCK, LANES), BF16),
            pltpu.VMEM((8, A_HEADS * LANES), F32),
            pltpu.VMEM((A_KV_RANK, A_HEADS * LANES), F32),
            pltpu.VMEM((KEY_CHUNK, A_HEADS * LANES), F32),
            pltpu.VMEM((KEY_CHUNK, A_HEADS * LANES), F32),
        ],
        compiler_params=_cparams(("parallel", "arbitrary")),
        name="dsa",
    )(q, qi, wi, kx, ct, ki, wuk, wuv)


def _banded_kernel(*refs, subheads, max_dist, has_sink, has_prev):
    refs = list(refs)
    sink_ref = refs.pop(0) if has_sink else None
    q_ref, kp_ref, kc_ref, vp_ref, vc_ref = refs[:5]
    refs = refs[5:]
    if has_prev:
        oprev_ref, lprev_ref = refs[:2]
        refs = refs[2:]
    o_ref, lse_ref = refs

    n = pl.program_id(2)
    qi = lax.broadcasted_iota(I32, (BLOCK, 2 * BLOCK), 0)
    kj = lax.broadcasted_iota(I32, (BLOCK, 2 * BLOCK), 1)
    dist = BLOCK + qi - kj
    mask = (dist >= 0) & (dist <= max_dist) & ((kj >= BLOCK) | (n > 0))
    lane = lax.broadcasted_iota(I32, (BLOCK, LANES), 1)
    lse_tile = jnp.zeros((BLOCK, LANES), F32)

    for j, subs in enumerate(subheads):
        qt = q_ref[:, j * LANES:(j + 1) * LANES]
        o_t = None
        lse_t = None
        for kt, vt, hidx in subs:
            kk = jnp.concatenate([kp_ref[:, kt * LANES:(kt + 1) * LANES],
                                  kc_ref[:, kt * LANES:(kt + 1) * LANES]], axis=0)
            vv = jnp.concatenate([vp_ref[:, vt * LANES:(vt + 1) * LANES],
                                  vc_ref[:, vt * LANES:(vt + 1) * LANES]], axis=0)
            s = lax.dot_general(qt, kk, _NT, preferred_element_type=F32)
            s = jnp.where(mask, s, NEG_INF)
            m = jnp.max(s, axis=-1, keepdims=True)
            if has_sink:
                sk = sink_ref[hidx]
                m = jnp.maximum(m, sk)
            p = jnp.exp(s - m)
            l = jnp.sum(p, axis=-1, keepdims=True)
            if has_sink:
                l = l + jnp.exp(sk - m)
            o_s = jnp.dot(p.astype(BF16), vv, preferred_element_type=F32) / l
            o_t = o_s if o_t is None else o_t + o_s
            lse_t = m + jnp.log(l)
        if has_prev:
            lp = lprev_ref[:, j:j + 1]
            op = oprev_ref[:, j * LANES:(j + 1) * LANES].astype(F32)
            mx = jnp.maximum(lp, lse_t)
            wp, wn = jnp.exp(lp - mx), jnp.exp(lse_t - mx)
            den = wp + wn
            o_t = (wp * op + wn * o_t) / den
            lse_t = mx + jnp.log(den)
        o_ref[:, j * LANES:(j + 1) * LANES] = o_t.astype(o_ref.dtype)
        lse_tile = jnp.where(lane == j, lse_t, lse_tile)
    lse_ref[...] = lse_tile


def _banded(q, k, v, *, dil, subheads, max_dist, sinks=None, prev=None):
    B, S, wq = q.shape
    wk, wv = k.shape[-1], v.shape[-1]
    n = S // dil
    view = lambda t: t.reshape(B, n, dil * t.shape[-1])
    cur = lambda width: pl.BlockSpec((None, BLOCK, width), lambda b, r, i: (b, i, r))
    prv = lambda width: pl.BlockSpec((None, BLOCK, width), lambda b, r, i: (b, jnp.maximum(i - 1, 0), r))
    in_specs = [cur(wq), prv(wk), cur(wk), prv(wv), cur(wv)]
    args = [view(q), view(k), view(k), view(v), view(v)]
    if sinks is not None:
        in_specs.insert(0, pl.BlockSpec(memory_space=pltpu.SMEM))
        args.insert(0, sinks)
    if prev is not None:
        in_specs += [cur(wq), cur(LANES)]
        args += [view(prev[0]), view(prev[1])]
    o, lse = pl.pallas_call(
        functools.partial(_banded_kernel, subheads=subheads, max_dist=max_dist,
                          has_sink=sinks is not None, has_prev=prev is not None),
        grid=(B, dil, n // BLOCK),
        in_specs=in_specs,
        out_specs=(cur(wq), cur(LANES)),
        out_shape=(jax.ShapeDtypeStruct((B, n, dil * wq), BF16), jax.ShapeDtypeStruct((B, n, dil * LANES), F32)),
        compiler_params=_cparams(("parallel", "parallel", "arbitrary")),
        name="banded",
    )(*args)
    return o.reshape(B, S, wq), lse.reshape(B, S, LANES)


def _memkv_kernel(mem_ref, g_ref, w_ref, k_ref, v_ref):
    hb = _rms(mem_ref[...], g_ref[...]).astype(BF16)
    y = jnp.dot(hb, w_ref[...], preferred_element_type=F32)
    lane = lax.broadcasted_iota(I32, (y.shape[0], LANES), 1)
    for t in range(MEM_WIDTH // LANES):
        for out_ref, base in ((k_ref, 0), (v_ref, MEM_WIDTH)):
            tile = y[:, base + t * LANES:base + (t + 1) * LANES]
            out_ref[:, (2 * t) * LANES:(2 * t + 1) * LANES] = jnp.where(lane < MEM_HEAD_DIM, tile, 0.0).astype(BF16)
            out_ref[:, (2 * t + 1) * LANES:(2 * t + 2) * LANES] = jnp.where(lane >= MEM_HEAD_DIM, tile, 0.0).astype(BF16)


def _memkv(mem, g_mem, w):
    B, M, D = mem.shape
    L = w.shape[0]
    out = jax.ShapeDtypeStruct((L, B, M, 2 * MEM_WIDTH), BF16)
    ospec = pl.BlockSpec((None, None, M, 2 * MEM_WIDTH), lambda l, b: (l, b, 0, 0))
    return pl.pallas_call(
        _memkv_kernel,
        grid=(L, B),
        in_specs=[pl.BlockSpec((None, M, D), lambda l, b: (b, 0, 0)),
                  pl.BlockSpec((1, D), lambda l, b: (0, 0)),
                  pl.BlockSpec((None, D, 2 * MEM_WIDTH), lambda l, b: (l, 0, 0))],
        out_specs=(ospec, ospec),
        out_shape=(out, out),
        compiler_params=_cparams(("parallel", "parallel")),
        name="memkv",
    )(mem, g_mem, w)


def _out_kernel(x_ref, mix_ref, qm_ref, mk_ref, mv_ref, w_ref, o_ref):
    mw = mix_ref.shape[-1]
    y = x_ref[...] + jnp.dot(mix_ref[...], w_ref[0:mw, :], preferred_element_type=F32)
    for t in range(MEM_WIDTH // LANES):
        qt = qm_ref[:, t * LANES:(t + 1) * LANES]
        o_t = None
        for u in range(2):
            sl = slice((2 * t + u) * LANES, (2 * t + u + 1) * LANES)
            s = lax.dot_general(qt, mk_ref[:, sl], _NT, preferred_element_type=F32)
            p = jnp.exp(s - jnp.max(s, axis=-1, keepdims=True))
            l = jnp.sum(p, axis=-1, keepdims=True)
            o_s = jnp.dot(p.astype(BF16), mv_ref[:, sl], preferred_element_type=F32) / l
            o_t = o_s if o_t is None else o_t + o_s
        y = y + jnp.dot(o_t.astype(BF16), w_ref[mw + t * LANES:mw + (t + 1) * LANES, :],
                        preferred_element_type=F32)
    o_ref[...] = y


def _out_proj(x, mix, qm, mk, mv, w, tm):
    B, S, D = x.shape
    M = mk.shape[1]
    row = lambda width: pl.BlockSpec((None, tm, width), lambda b, i: (b, i, 0))
    mem = pl.BlockSpec((None, M, 2 * MEM_WIDTH), lambda b, i: (b, 0, 0))
    return pl.pallas_call(
        _out_kernel,
        grid=(B, S // tm),
        in_specs=[row(D), row(mix.shape[-1]), row(MEM_WIDTH), mem, mem,
                  pl.BlockSpec(w.shape, lambda b, i: (0, 0))],
        out_specs=row(D),
        out_shape=jax.ShapeDtypeStruct((B, S, D), F32),
        compiler_params=_cparams(("parallel", "parallel")),
        name="out_proj",
    )(x, mix, qm, mk, mv, w)


def _ffn_kernel(x_ref, g_ref, wup_ref, cw_ref, cb_ref, wdn_ref, gf_ref, o_ref, carry_ref, *, tm, cw, final):
    dff = wdn_ref.shape[0]
    first = pl.program_id(1) == 0
    x = x_ref[...]
    hb = _rms(x, g_ref[...]).astype(BF16)
    row = lax.broadcasted_iota(I32, (tm, cw), 0)
    acc = x

    def up(c0):
        return (jnp.dot(hb, wup_ref[:, c0:c0 + cw], preferred_element_type=F32),
                jnp.dot(hb, wup_ref[:, dff + c0:dff + c0 + cw], preferred_element_type=F32))

    nxt = up(0)
    for c0 in range(0, dff, cw):
        a, b = nxt
        if c0 + cw < dff:
            nxt = up(c0 + cw)
        prev = jnp.where(first, 0.0, carry_ref[:, c0:c0 + cw])
        p1, p2 = prev[7:8, :], prev[6:7, :]
        a1 = jnp.where(row == 0, p1, pltpu.roll(a, 1, 0))
        a2 = jnp.where(row == 0, p2, jnp.where(row == 1, p1, pltpu.roll(a, 2, 0)))
        carry_ref[:, c0:c0 + cw] = a[tm - 8:tm, :]
        w = cw_ref[:, c0:c0 + cw]
        conv = w[0:1, :] * a2 + w[1:2, :] * a1 + w[2:3, :] * a + cb_ref[:, c0:c0 + cw]
        gated = conv / (1.0 + jnp.exp(-conv)) * b
        acc = acc + jnp.dot(gated.astype(BF16), wdn_ref[c0:c0 + cw, :], preferred_element_type=F32)
    if final:
        acc = _rms(acc, gf_ref[...])
    o_ref[...] = acc


def _ffn(x, g, wup, cw, cb, wdn, gf, tm, final):
    B, S, D = x.shape
    dff = wdn.shape[0]
    row = pl.BlockSpec((None, tm, D), lambda b, i: (b, i, 0))
    const = lambda shape: pl.BlockSpec(shape, lambda b, i: (0, 0), pipeline_mode=pl.Buffered(1))
    return pl.pallas_call(
        functools.partial(_ffn_kernel, tm=tm, cw=2 * LANES, final=final),
        grid=(B, S // tm),
        in_specs=[row, const((1, D)), const(wup.shape), const(cw.shape), const((1, dff)), const(wdn.shape),
                  const((1, D))],
        out_specs=row,
        out_shape=jax.ShapeDtypeStruct((B, S, D), F32),
        scratch_shapes=[pltpu.VMEM((8, dff), F32)],
        compiler_params=_cparams(("arbitrary", "arbitrary")),
        name="ffn",
    )(x, g, wup, cw, cb, wdn, gf)


def _b_subheads():
    return tuple(((2 * (j // 2), 2 * (j // 2), 2 * j), (2 * (j // 2) + 1, 2 * (j // 2) + 1, 2 * j + 1))
                 for j in range(B_HEADS // 2))


def _c_subheads():
    return tuple(((j, j, j),) for j in range(C_HEADS))


def kernel(x, mem, positions, g_mix, g_ffn, g_mem, g_final, w_mem_kv, a_w_in, a_kv_norm, a_w_uk, a_w_uv, a_w_out,
           b_w_in, b_sinks, b_w_out, c_w_in, c_w_out, f_w_up, f_conv_w, f_conv_b, f_w_down):
    B, S, D = x.shape
    depth = g_mix.shape[0]
    tm = min(512, S)
    tab32 = _rope_table(positions, A_QK_DIM, A_ROPE_DIM)
    tab16 = _rope_table(positions, B_HEAD_DIM, B_HEAD_DIM // 4)
    mem_k, mem_v = _memkv(mem, g_mem.reshape(1, D), w_mem_kv.astype(BF16))
    conv_w = jnp.pad(f_conv_w, ((0, 0), (0, 8 - CONV_WIDTH), (0, 0)))
    for i in range(depth):
        kind, j = i % 3, i // 3
        g = g_mix[i].reshape(1, D)
        if kind == 0:
            q, kx, ct, qi, ki, wi, qm = _proj_a(x, g, _prep_a_w_in(a_w_in[j]), a_kv_norm[j].reshape(1, -1),
                                                tab32, tab16, tm)
            wuk = jnp.pad(jnp.transpose(a_w_uk[j], (1, 2, 0)), ((0, 0), (A_ROPE_DIM, 0), (0, 0))).astype(BF16)
            wuv = jnp.transpose(a_w_uv[j], (1, 0, 2)).astype(BF16)
            mix = _dsa(q, qi, wi, kx, ct, ki, wuk, wuv)
            w_out = a_w_out[j]
        elif kind == 1:
            nq, nkv = B_HEADS * B_HEAD_DIM, 4 * B_KV_HEADS * B_HEAD_DIM
            q, k, v, qm = _proj_qkv(x, g, _prep_b_w_in(b_w_in[j]), tab16, tm, B_HEAD_DIM, B_HEAD_DIM // 4,
                                    nq, nkv, nkv)
            mix, _ = _banded(q, k, v, dil=1, subheads=_b_subheads(), max_dist=B_WINDOW - 1, sinks=b_sinks[j])
            w_out = b_w_out[j]
        else:
            nq = C_HEADS * C_HEAD_DIM
            q, k, v, qm = _proj_qkv(x, g, c_w_in[j].astype(BF16), tab32, tm, C_HEAD_DIM, C_HEAD_DIM // 4,
                                    nq, nq, nq)
            prev = None
            for window, dil in C_BRANCHES:
                prev = _banded(q, k, v, dil=dil, subheads=_c_subheads(), max_dist=window // dil, prev=prev)
            mix = prev[0]
            w_out = c_w_out[j]
        x = _out_proj(x, mix, qm, mem_k[i], mem_v[i], w_out.astype(BF16), tm)
        x = _ffn(x, g_ffn[i].reshape(1, D), f_w_up[i].astype(BF16), conv_w[i], f_conv_b[i].reshape(1, -1),
                 f_w_down[i].astype(BF16), g_final.reshape(1, D), min(256, S), i == depth - 1)
    return x
```

```python
import functools

import jax
import jax.numpy as jnp
from jax import lax
from jax.experimental import pallas as pl
from jax.experimental.pallas import tpu as pltpu

F32 = jnp.float32
BF16 = jnp.bfloat16
I32 = jnp.int32
I16 = jnp.int16

LANES = 128
BLOCK = 128
ROPE_THETA = 500000.0
EPS = 1e-6
NEG_INF = -1e30
INT_MIN = -(2**31)

A_HEADS = 8
A_QK_DIM = 128
A_ROPE_DIM = 32
A_KV_RANK = 256
IDX_HEADS = 16
IDX_DIM = 64
TOPK_MAX = 256
B_HEADS = 16
B_KV_HEADS = 4
B_HEAD_DIM = 64
B_WINDOW = 128
C_HEADS = 8
C_HEAD_DIM = 128
C_BRANCHES = ((128, 1), (512, 4), (2048, 16))
MEM_HEADS = 4
MEM_HEAD_DIM = 64
MEM_WIDTH = MEM_HEADS * MEM_HEAD_DIM
CONV_WIDTH = 3
KEY_CHUNK = 256

VMEM_LIMIT = 56 * 1024 * 1024

_NT = (((1,), (1,)), ((), ()))


def _cparams(sem):
    return pltpu.CompilerParams(dimension_semantics=sem, vmem_limit_bytes=VMEM_LIMIT)


def _rms(x, g):
    return x * lax.rsqrt(jnp.mean(x * x, axis=-1, keepdims=True) + EPS) * g


def _rope_tile(t, tab, half):
    c, sa, sb = tab[:, 0:LANES], tab[:, LANES:2 * LANES], tab[:, 2 * LANES:3 * LANES]
    return t * c + pltpu.roll(t, half, 1) * sa + pltpu.roll(t, LANES - half, 1) * sb


def _rope_table(positions, head_dim, rot):
    half = rot // 2
    inv = ROPE_THETA ** (-jnp.arange(0, rot, 2, dtype=F32) / rot)
    ang = positions.astype(F32)[..., None] * inv
    cos, sin = jnp.cos(ang), jnp.sin(ang)
    rest = head_dim - rot
    lead = cos.shape[:-1]
    c = jnp.concatenate([cos, cos, jnp.ones(lead + (rest,), F32)], axis=-1)
    sa = jnp.concatenate([jnp.zeros_like(sin), sin, jnp.zeros(lead + (rest,), F32)], axis=-1)
    sb = jnp.concatenate([-sin, jnp.zeros_like(sin), jnp.zeros(lead + (rest,), F32)], axis=-1)
    reps = LANES // head_dim
    return jnp.concatenate([jnp.tile(t, (1, 1, reps)) for t in (c, sa, sb)], axis=-1)


def _proj_a_kernel(x_ref, g_ref, w_ref, kvn_ref, t32_ref, t16_ref,
                   q_ref, kx_ref, ct_ref, qi_ref, ki_ref, wi_ref, qm_ref, *, tm):
    hb = _rms(x_ref[...], g_ref[...]).astype(BF16)
    t32 = t32_ref[...]
    t16 = t16_ref[...]

    def mm(a, b):
        return jnp.dot(hb, w_ref[:, a:b], preferred_element_type=F32)

    scale = A_QK_DIM ** -0.5
    for j in range(0, A_HEADS * A_QK_DIM, 2 * LANES):
        y = mm(j, j + 2 * LANES)
        for u in range(2):
            t = _rope_tile(y[:, u * LANES:(u + 1) * LANES], t32, A_ROPE_DIM // 2)
            q_ref[:, j + u * LANES:j + (u + 1) * LANES] = (t * scale).astype(BF16)
    o = A_HEADS * A_QK_DIM
    c = _rms(mm(o, o + A_KV_RANK), kvn_ref[...])
    kx_ref[:, 0:A_KV_RANK] = c.astype(BF16)
    for u in range(tm // KEY_CHUNK):
        ct_ref[u] = c[u * KEY_CHUNK:(u + 1) * KEY_CHUNK, :].T.astype(BF16)
    o += A_KV_RANK
    kx_ref[:, A_KV_RANK:A_KV_RANK + LANES] = _rope_tile(mm(o, o + LANES), t32, A_ROPE_DIM // 2).astype(BF16)
    o += LANES
    for j in range(0, IDX_HEADS * IDX_DIM, 2 * LANES):
        y = mm(o + j, o + j + 2 * LANES)
        for u in range(2):
            t = _rope_tile(y[:, u * LANES:(u + 1) * LANES], t16, IDX_DIM // 8)
            qi_ref[:, j + u * LANES:j + (u + 1) * LANES] = t.astype(BF16)
    o += IDX_HEADS * IDX_DIM
    y = mm(o, o + 2 * LANES)
    for u in range(2):
        t = _rope_tile(y[:, u * LANES:(u + 1) * LANES], t16, IDX_DIM // 8).astype(BF16)
        for v in range(tm // KEY_CHUNK):
            ki_ref[v, u * KEY_CHUNK:(u + 1) * KEY_CHUNK, :] = t[v * KEY_CHUNK:(v + 1) * KEY_CHUNK, :]
    o += 2 * LANES
    wi_ref[...] = mm(o, o + LANES) * (IDX_HEADS * IDX_DIM) ** -0.5
    o += LANES
    qm_ref[...] = (mm(o, o + MEM_WIDTH) * MEM_HEAD_DIM ** -0.5).astype(BF16)


def _prep_a_w_in(w):
    d = w.shape[0]
    sizes = (A_HEADS * A_QK_DIM, A_KV_RANK, A_ROPE_DIM, IDX_HEADS * IDX_DIM, IDX_DIM, IDX_HEADS, MEM_WIDTH)
    offs = [0]
    for s in sizes:
        offs.append(offs[-1] + s)
    q, ckv, kr, qi, ki, wi, qm = [w[:, offs[i]:offs[i + 1]] for i in range(len(sizes))]
    z = lambda n: jnp.zeros((d, n), w.dtype)
    return jnp.concatenate([
        q, ckv, kr, z(LANES - A_ROPE_DIM), qi,
        ki, z(LANES - IDX_DIM), z(LANES - IDX_DIM), ki,
        wi, z(LANES - IDX_HEADS), qm], axis=1).astype(BF16)


def _proj_a(x, g, w, kvn, t32, t16, tm):
    B, S, D = x.shape
    n = w.shape[1]
    row = lambda width: pl.BlockSpec((None, tm, width), lambda b, i: (b, i, 0))
    const = lambda shape: pl.BlockSpec(shape, lambda b, i: (0,) * len(shape))
    out_shape = (
        jax.ShapeDtypeStruct((B, S, A_HEADS * A_QK_DIM), BF16),
        jax.ShapeDtypeStruct((B, S, A_KV_RANK + LANES), BF16),
        jax.ShapeDtypeStruct((B, S // KEY_CHUNK, A_KV_RANK, KEY_CHUNK), BF16),
        jax.ShapeDtypeStruct((B, S, IDX_HEADS * IDX_DIM), BF16),
        jax.ShapeDtypeStruct((B, S // KEY_CHUNK, 2 * KEY_CHUNK, LANES), BF16),
        jax.ShapeDtypeStruct((B, S, LANES), F32),
        jax.ShapeDtypeStruct((B, S, MEM_WIDTH), BF16),
    )
    out_specs = (
        row(A_HEADS * A_QK_DIM), row(A_KV_RANK + LANES),
        pl.BlockSpec((None, tm // KEY_CHUNK, A_KV_RANK, KEY_CHUNK), lambda b, i: (b, i, 0, 0)),
        row(IDX_HEADS * IDX_DIM),
        pl.BlockSpec((None, tm // KEY_CHUNK, 2 * KEY_CHUNK, LANES), lambda b, i: (b, i, 0, 0)),
        row(LANES), row(MEM_WIDTH),
    )
    return pl.pallas_call(
        functools.partial(_proj_a_kernel, tm=tm),
        grid=(B, S // tm),
        in_specs=[row(D), const((1, D)), const((D, n)), const((1, A_KV_RANK)), row(3 * LANES), row(3 * LANES)],
        out_specs=out_specs,
        out_shape=out_shape,
        compiler_params=_cparams(("parallel", "parallel")),
        name="proj_a",
    )(x, g, w, kvn, t32, t16)


def _proj_qkv_kernel(x_ref, g_ref, w_ref, tab_ref, q_ref, k_ref, v_ref, qm_ref, *, head_dim, rot, nq, nk, nv):
    hb = _rms(x_ref[...], g_ref[...]).astype(BF16)
    tab = tab_ref[...]

    def mm(a, b):
        return jnp.dot(hb, w_ref[:, a:b], preferred_element_type=F32)

    scale = head_dim ** -0.5
    for j in range(0, nq, 2 * LANES):
        y = mm(j, j + 2 * LANES)
        for u in range(2):
            t = _rope_tile(y[:, u * LANES:(u + 1) * LANES], tab, rot // 2)
            q_ref[:, j + u * LANES:j + (u + 1) * LANES] = (t * scale).astype(BF16)
    for j in range(0, nk, 2 * LANES):
        y = mm(nq + j, nq + j + 2 * LANES)
        for u in range(2):
            t = _rope_tile(y[:, u * LANES:(u + 1) * LANES], tab, rot // 2)
            k_ref[:, j + u * LANES:j + (u + 1) * LANES] = t.astype(BF16)
    for j in range(0, nv, 2 * LANES):
        v_ref[:, j:j + 2 * LANES] = mm(nq + nk + j, nq + nk + j + 2 * LANES).astype(BF16)
    o = nq + nk + nv
    qm_ref[...] = (mm(o, o + MEM_WIDTH) * MEM_HEAD_DIM ** -0.5).astype(BF16)


def _proj_qkv(x, g, w, tab, tm, head_dim, rot, nq, nk, nv):
    B, S, D = x.shape
    n = w.shape[1]
    row = lambda width: pl.BlockSpec((None, tm, width), lambda b, i: (b, i, 0))
    const = lambda shape: pl.BlockSpec(shape, lambda b, i: (0,) * len(shape))
    return pl.pallas_call(
        functools.partial(_proj_qkv_kernel, head_dim=head_dim, rot=rot, nq=nq, nk=nk, nv=nv),
        grid=(B, S // tm),
        in_specs=[row(D), const((1, D)), const((D, n)), row(3 * LANES)],
        out_specs=(row(nq), row(nk), row(nv), row(MEM_WIDTH)),
        out_shape=(jax.ShapeDtypeStruct((B, S, nq), BF16), jax.ShapeDtypeStruct((B, S, nk), BF16),
                   jax.ShapeDtypeStruct((B, S, nv), BF16), jax.ShapeDtypeStruct((B, S, MEM_WIDTH), BF16)),
        compiler_params=_cparams(("parallel", "parallel")),
        name="proj_qkv",
    )(x, g, w, tab)


def _prep_b_w_in(w):
    d = w.shape[0]
    nq, nkv = B_HEADS * B_HEAD_DIM, B_KV_HEADS * B_HEAD_DIM
    q, k, v, qm = w[:, :nq], w[:, nq:nq + nkv], w[:, nq + nkv:nq + 2 * nkv], w[:, nq + 2 * nkv:]
    z = jnp.zeros((d, B_HEAD_DIM), w.dtype)

    def spread(t):
        cols = []
        for h in range(B_KV_HEADS):
            th = t[:, h * B_HEAD_DIM:(h + 1) * B_HEAD_DIM]
            cols += [th, z, z, th]
        return jnp.concatenate(cols, axis=1)

    return jnp.concatenate([q, spread(k), spread(v), qm], axis=1).astype(BF16)


def _dsa_kernel(q_ref, qi_ref, wi_ref, kx_ref, ct_ref, ki_ref, wuk_ref, wuv_ref, o_ref,
                keys_ref, half_ref, qext_ref, qi2_ref, wrow_ref, acc_ref, sa_ref, sb_ref, *, topk):
    kc = KEY_CHUNK
    hw = A_HEADS * LANES
    gw = 2 * LANES
    i = pl.program_id(1)
    nch = (i * BLOCK + BLOCK + kc - 1) // kc
    lane = lax.broadcasted_iota(I32, (BLOCK, LANES), 1)

    for h in range(A_HEADS):
        rows = slice(h * BLOCK, (h + 1) * BLOCK)
        qh = q_ref[:, h * LANES:(h + 1) * LANES]
        qext_ref[rows, 0:A_KV_RANK] = jnp.dot(qh, wuk_ref[h], preferred_element_type=F32).astype(BF16)
        qext_ref[rows, A_KV_RANK:A_KV_RANK + LANES] = jnp.where(lane < A_ROPE_DIM, qh.astype(F32), 0.0).astype(BF16)
        qi2_ref[rows, :] = qi_ref[:, h * LANES:(h + 1) * LANES]
    w_t = wi_ref[...].T
    for j in range(IDX_HEADS // 2):
        wrow_ref[0:1, j * LANES:(j + 1) * LANES] = w_t[2 * j:2 * j + 1, :]
        wrow_ref[1:2, j * LANES:(j + 1) * LANES] = w_t[2 * j + 1:2 * j + 2, :]

    qpos = i * BLOCK + lax.broadcasted_iota(I32, (kc, BLOCK), 1)
    krow = lax.broadcasted_iota(I32, (kc, BLOCK), 0)

    def score_chunk(c):
        off = pl.multiple_of(c * kc, kc)
        kk = ki_ref[c]
        acc = None
        for g in range(0, hw, gw):
            d = lax.dot_general(kk, qi2_ref[g:g + gw, :], _NT, preferred_element_type=F32)
            t = (jnp.maximum(d[0:kc], 0.0) * wrow_ref[0:1, g:g + gw]
                 + jnp.maximum(d[kc:2 * kc], 0.0) * wrow_ref[1:2, g:g + gw])
            for u in range(0, gw, LANES):
                acc = t[:, u:u + LANES] if acc is None else acc + t[:, u:u + LANES]
        bits = pltpu.bitcast(acc, I32)
        key = jnp.where(bits < 0, bits ^ 0x7FFFFFFF, bits)
        key = jnp.where(off + krow <= qpos, key, INT_MIN)
        keys_ref[pl.ds(off, kc), :] = key
        half_ref[pl.ds(off, kc), :] = (key >> 16).astype(I16)

    npairs = (nch + 1) // 2

    def score_body(j, carry):
        score_chunk(2 * j)
        score_chunk(2 * j + 1)
        return carry

    lax.fori_loop(0, npairs, score_body, 0)

    def bit_body(p, res):
        trial = res | lax.shift_left(jnp.int32(1), 31 - p)
        thr = trial ^ INT_MIN

        def cnt_body(c, cnt):
            off = pl.multiple_of(c * 2 * kc, 2 * kc)
            ge = (keys_ref[pl.ds(off, 2 * kc), :] >= thr).astype(I32)
            return cnt + jnp.sum(ge.reshape(2 * kc // 8, 8, BLOCK), axis=0)

        cnt = lax.fori_loop(0, npairs, cnt_body, jnp.zeros((8, BLOCK), I32))
        return jnp.where(jnp.sum(cnt, axis=0, keepdims=True) >= topk, trial, res)

    res = lax.fori_loop(0, 32, bit_body, jnp.zeros((1, BLOCK), I32))
    thr = jnp.maximum(res ^ INT_MIN, INT_MIN + 1)

    acc_ref[...] = jnp.zeros(acc_ref.shape, F32)

    def logits(c, s_ref):
        off = pl.multiple_of(c * kc, kc)
        bias = jnp.where(keys_ref[pl.ds(off, kc), :] >= thr, 0.0, NEG_INF)
        bias = jnp.concatenate([bias] * (gw // LANES), axis=1)
        kx = kx_ref[pl.ds(off, kc), :]
        for g in range(0, hw, gw):
            s_ref[:, g:g + gw] = lax.dot_general(
                kx, qext_ref[g:g + gw, :], _NT, preferred_element_type=F32) + bias

    def consume(s_ref, c, carry):
        m_prev, l_prev = carry
        ct = ct_ref[c]
        m_out, l_out = [], []
        for g in range(0, hw, gw):
            s = s_ref[:, g:g + gw]
            m_new = jnp.maximum(m_prev[:, g:g + gw], jnp.max(s, axis=0, keepdims=True))
            alpha = jnp.exp(m_prev[:, g:g + gw] - m_new)
            p = jnp.exp(s - m_new)
            l_out.append(alpha * l_prev[:, g:g + gw] + jnp.sum(p, axis=0, keepdims=True))
            m_out.append(m_new)
            acc_ref[:, g:g + gw] = alpha * acc_ref[:, g:g + gw] + jnp.dot(
                ct, p.astype(BF16), preferred_element_type=F32)
        return jnp.concatenate(m_out, axis=1), jnp.concatenate(l_out, axis=1)

    logits(0, sa_ref)

    def pair_body(j, carry):
        c0 = 2 * j
        logits(c0 + 1, sb_ref)
        carry = consume(sa_ref, c0, carry)
        logits(jnp.minimum(c0 + 2, 2 * npairs - 1), sa_ref)
        return consume(sb_ref, c0 + 1, carry)

    _, l_fin = lax.fori_loop(0, npairs, pair_body,
                             (jnp.full((1, hw), NEG_INF, F32), jnp.zeros((1, hw), F32)))

    inv_l = 1.0 / l_fin
    for h in range(A_HEADS):
        cols = slice(h * LANES, (h + 1) * LANES)
        o_lat = (acc_ref[:, cols] * inv_l[:, cols]).T.astype(BF16)
        o_ref[:, cols] = jnp.dot(o_lat, wuv_ref[h], preferred_element_type=F32).astype(BF16)


def _dsa(q, qi, wi, kx, ct, ki, wuk, wuv):
    B, S, _ = q.shape
    topk = min(TOPK_MAX, S // 4)
    blk = lambda width: pl.BlockSpec((None, BLOCK, width), lambda b, i: (b, i, 0))
    seq = lambda width: pl.BlockSpec((None, S, width), lambda b, i: (b, 0, 0))
    const3 = lambda shape: pl.BlockSpec(shape, lambda b, i: (0, 0, 0))
    return pl.pallas_call(
        functools.partial(_dsa_kernel, topk=topk),
        grid=(B, S // BLOCK),
        in_specs=[blk(A_HEADS * A_QK_DIM), blk(IDX_HEADS * IDX_DIM), blk(LANES),
                  seq(A_KV_RANK + LANES),
                  pl.BlockSpec((None, S // KEY_CHUNK, A_KV_RANK, KEY_CHUNK), lambda b, i: (b, 0, 0, 0)),
                  pl.BlockSpec((None, S // KEY_CHUNK, 2 * KEY_CHUNK, LANES), lambda b, i: (b, 0, 0, 0)),
                  const3(wuk.shape), const3(wuv.shape)],
        out_specs=blk(A_HEADS * LANES),
        out_shape=jax.ShapeDtypeStruct((B, S, A_HEADS * LANES), BF16),
        scratch_shapes=[
            pltpu.VMEM((S, BLOCK), I32),
            pltpu.VMEM((S, BLOCK), I16),
            pltpu.VMEM((A_HEADS * BLOCK, A_KV_RANK + LANES), BF16),
            pltpu.VMEM((A_HEADS * BLOCK, LANES), BF16),
            pltpu.VMEM((8, A_HEADS * LANES), F32),
            pltpu.VMEM((A_KV_RANK, A_HEADS * LANES), F32),
            pltpu.VMEM((KEY_CHUNK, A_HEADS * LANES), F32),
            pltpu.VMEM((KEY_CHUNK, A_HEADS * LANES), F32),
        ],
        compiler_params=_cparams(("parallel", "arbitrary")),
        name="dsa",
    )(q, qi, wi, kx, ct, ki, wuk, wuv)


def _banded_kernel(*refs, subheads, max_dist, has_sink, has_prev):
    refs = list(refs)
    sink_ref = refs.pop(0) if has_sink else None
    q_ref, kp_ref, kc_ref, vp_ref, vc_ref = refs[:5]
    refs = refs[5:]
    if has_prev:
        oprev_ref, lprev_ref = refs[:2]
        refs = refs[2:]
    o_ref, lse_ref = refs

    n = pl.program_id(2)
    qi = lax.broadcasted_iota(I32, (BLOCK, 2 * BLOCK), 0)
    kj = lax.broadcasted_iota(I32, (BLOCK, 2 * BLOCK), 1)
    dist = BLOCK + qi - kj
    mask = (dist >= 0) & (dist <= max_dist) & ((kj >= BLOCK) | (n > 0))
    lane = lax.broadcasted_iota(I32, (BLOCK, LANES), 1)
    lse_tile = jnp.zeros((BLOCK, LANES), F32)

    for j, subs in enumerate(subheads):
        qt = q_ref[:, j * LANES:(j + 1) * LANES]
        o_t = None
        lse_t = None
        for kt, vt, hidx in subs:
            kk = jnp.concatenate([kp_ref[:, kt * LANES:(kt + 1) * LANES],
                                  kc_ref[:, kt * LANES:(kt + 1) * LANES]], axis=0)
            vv = jnp.concatenate([vp_ref[:, vt * LANES:(vt + 1) * LANES],
                                  vc_ref[:, vt * LANES:(vt + 1) * LANES]], axis=0)
            s = lax.dot_general(qt, kk, _NT, preferred_element_type=F32)
            s = jnp.where(mask, s, NEG_INF)
            m = jnp.max(s, axis=-1, keepdims=True)
            if has_sink:
                sk = sink_ref[hidx]
                m = jnp.maximum(m, sk)
            p = jnp.exp(s - m)
            l = jnp.sum(p, axis=-1, keepdims=True)
            if has_sink:
                l = l + jnp.exp(sk - m)
            o_s = jnp.dot(p.astype(BF16), vv, preferred_element_type=F32) / l
            o_t = o_s if o_t is None else o_t + o_s
            lse_t = m + jnp.log(l)
        if has_prev:
            lp = lprev_ref[:, j:j + 1]
            op = oprev_ref[:, j * LANES:(j + 1) * LANES].astype(F32)
            mx = jnp.maximum(lp, lse_t)
            wp, wn = jnp.exp(lp - mx), jnp.exp(lse_t - mx)
            den = wp + wn
            o_t = (wp * op + wn * o_t) / den
            lse_t = mx + jnp.log(den)
        o_ref[:, j * LANES:(j + 1) * LANES] = o_t.astype(o_ref.dtype)
        lse_tile = jnp.where(lane == j, lse_t, lse_tile)
    lse_ref[...] = lse_tile


def _banded(q, k, v, *, dil, subheads, max_dist, sinks=None, prev=None):
    B, S, wq = q.shape
    wk, wv = k.shape[-1], v.shape[-1]
    n = S // dil
    view = lambda t: t.reshape(B, n, dil * t.shape[-1])
    cur = lambda width: pl.BlockSpec((None, BLOCK, width), lambda b, r, i: (b, i, r))
    prv = lambda width: pl.BlockSpec((None, BLOCK, width), lambda b, r, i: (b, jnp.maximum(i - 1, 0), r))
    in_specs = [cur(wq), prv(wk), cur(wk), prv(wv), cur(wv)]
    args = [view(q), view(k), view(k), view(v), view(v)]
    if sinks is not None:
        in_specs.insert(0, pl.BlockSpec(memory_space=pltpu.SMEM))
        args.insert(0, sinks)
    if prev is not None:
        in_specs += [cur(wq), cur(LANES)]
        args += [view(prev[0]), view(prev[1])]
    o, lse = pl.pallas_call(
        functools.partial(_banded_kernel, subheads=subheads, max_dist=max_dist,
                          has_sink=sinks is not None, has_prev=prev is not None),
        grid=(B, dil, n // BLOCK),
        in_specs=in_specs,
        out_specs=(cur(wq), cur(LANES)),
        out_shape=(jax.ShapeDtypeStruct((B, n, dil * wq), BF16), jax.ShapeDtypeStruct((B, n, dil * LANES), F32)),
        compiler_params=_cparams(("parallel", "parallel", "arbitrary")),
        name="banded",
    )(*args)
    return o.reshape(B, S, wq), lse.reshape(B, S, LANES)


def _memkv_kernel(mem_ref, g_ref, w_ref, k_ref, v_ref):
    hb = _rms(mem_ref[...], g_ref[...]).astype(BF16)
    y = jnp.dot(hb, w_ref[...], preferred_element_type=F32)
    lane = lax.broadcasted_iota(I32, (y.shape[0], LANES), 1)
    for t in range(MEM_WIDTH // LANES):
        for out_ref, base in ((k_ref, 0), (v_ref, MEM_WIDTH)):
            tile = y[:, base + t * LANES:base + (t + 1) * LANES]
            out_ref[:, (2 * t) * LANES:(2 * t + 1) * LANES] = jnp.where(lane < MEM_HEAD_DIM, tile, 0.0).astype(BF16)
            out_ref[:, (2 * t + 1) * LANES:(2 * t + 2) * LANES] = jnp.where(lane >= MEM_HEAD_DIM, tile, 0.0).astype(BF16)


def _memkv(mem, g_mem, w):
    B, M, D = mem.shape
    L = w.shape[0]
    out = jax.ShapeDtypeStruct((L, B, M, 2 * MEM_WIDTH), BF16)
    ospec = pl.BlockSpec((None, None, M, 2 * MEM_WIDTH), lambda l, b: (l, b, 0, 0))
    return pl.pallas_call(
        _memkv_kernel,
        grid=(L, B),
        in_specs=[pl.BlockSpec((None, M, D), lambda l, b: (b, 0, 0)),
                  pl.BlockSpec((1, D), lambda l, b: (0, 0)),
                  pl.BlockSpec((None, D, 2 * MEM_WIDTH), lambda l, b: (l, 0, 0))],
        out_specs=(ospec, ospec),
        out_shape=(out, out),
        compiler_params=_cparams(("parallel", "parallel")),
        name="memkv",
    )(mem, g_mem, w)


def _out_kernel(x_ref, mix_ref, qm_ref, mk_ref, mv_ref, w_ref, o_ref):
    mw = mix_ref.shape[-1]
    y = x_ref[...] + jnp.dot(mix_ref[...], w_ref[0:mw, :], preferred_element_type=F32)
    for t in range(MEM_WIDTH // LANES):
        qt = qm_ref[:, t * LANES:(t + 1) * LANES]
        o_t = None
        for u in range(2):
            sl = slice((2 * t + u) * LANES, (2 * t + u + 1) * LANES)
            s = lax.dot_general(qt, mk_ref[:, sl], _NT, preferred_element_type=F32)
            p = jnp.exp(s - jnp.max(s, axis=-1, keepdims=True))
            l = jnp.sum(p, axis=-1, keepdims=True)
            o_s = jnp.dot(p.astype(BF16), mv_ref[:, sl], preferred_element_type=F32) / l
            o_t = o_s if o_t is None else o_t + o_s
        y = y + jnp.dot(o_t.astype(BF16), w_ref[mw + t * LANES:mw + (t + 1) * LANES, :],
                        preferred_element_type=F32)
    o_ref[...] = y


def _out_proj(x, mix, qm, mk, mv, w, tm):
    B, S, D = x.shape
    M = mk.shape[1]
    row = lambda width: pl.BlockSpec((None, tm, width), lambda b, i: (b, i, 0))
    mem = pl.BlockSpec((None, M, 2 * MEM_WIDTH), lambda b, i: (b, 0, 0))
    return pl.pallas_call(
        _out_kernel,
        grid=(B, S // tm),
        in_specs=[row(D), row(mix.shape[-1]), row(MEM_WIDTH), mem, mem,
                  pl.BlockSpec(w.shape, lambda b, i: (0, 0))],
        out_specs=row(D),
        out_shape=jax.ShapeDtypeStruct((B, S, D), F32),
        compiler_params=_cparams(("parallel", "parallel")),
        name="out_proj",
    )(x, mix, qm, mk, mv, w)


def _ffn_kernel(x_ref, g_ref, wup_ref, cw_ref, cb_ref, wdn_ref, gf_ref, o_ref, carry_ref, *, tm, cw, final):
    dff = wdn_ref.shape[0]
    first = pl.program_id(1) == 0
    x = x_ref[...]
    hb = _rms(x, g_ref[...]).astype(BF16)
    row = lax.broadcasted_iota(I32, (tm, cw), 0)
    acc = x

    def up(c0):
        return (jnp.dot(hb, wup_ref[:, c0:c0 + cw], preferred_element_type=F32),
                jnp.dot(hb, wup_ref[:, dff + c0:dff + c0 + cw], preferred_element_type=F32))

    nxt = up(0)
    for c0 in range(0, dff, cw):
        a, b = nxt
        if c0 + cw < dff:
            nxt = up(c0 + cw)
        prev = jnp.where(first, 0.0, carry_ref[:, c0:c0 + cw])
        p1, p2 = prev[7:8, :], prev[6:7, :]
        a1 = jnp.where(row == 0, p1, pltpu.roll(a, 1, 0))
        a2 = jnp.where(row == 0, p2, jnp.where(row == 1, p1, pltpu.roll(a, 2, 0)))
        carry_ref[:, c0:c0 + cw] = a[tm - 8:tm, :]
        w = cw_ref[:, c0:c0 + cw]
        conv = w[0:1, :] * a2 + w[1:2, :] * a1 + w[2:3, :] * a + cb_ref[:, c0:c0 + cw]
        gated = conv / (1.0 + jnp.exp(-conv)) * b
        acc = acc + jnp.dot(gated.astype(BF16), wdn_ref[c0:c0 + cw, :], preferred_element_type=F32)
    if final:
        acc = _rms(acc, gf_ref[...])
    o_ref[...] = acc


def _ffn(x, g, wup, cw, cb, wdn, gf, tm, final):
    B, S, D = x.shape
    dff = wdn.shape[0]
    row = pl.BlockSpec((None, tm, D), lambda b, i: (b, i, 0))
    const = lambda shape: pl.BlockSpec(shape, lambda b, i: (0, 0), pipeline_mode=pl.Buffered(1))
    return pl.pallas_call(
        functools.partial(_ffn_kernel, tm=tm, cw=2 * LANES, final=final),
        grid=(B, S // tm),
        in_specs=[row, const((1, D)), const(wup.shape), const(cw.shape), const((1, dff)), const(wdn.shape),
                  const((1, D))],
        out_specs=row,
        out_shape=jax.ShapeDtypeStruct((B, S, D), F32),
        scratch_shapes=[pltpu.VMEM((8, dff), F32)],
        compiler_params=_cparams(("arbitrary", "arbitrary")),
        name="ffn",
    )(x, g, wup, cw, cb, wdn, gf)


def _b_subheads():
    return tuple(((2 * (j // 2), 2 * (j // 2), 2 * j), (2 * (j // 2) + 1, 2 * (j // 2) + 1, 2 * j + 1))
                 for j in range(B_HEADS // 2))


def _c_subheads():
    return tuple(((j, j, j),) for j in range(C_HEADS))


def kernel(x, mem, positions, g_mix, g_ffn, g_mem, g_final, w_mem_kv, a_w_in, a_kv_norm, a_w_uk, a_w_uv, a_w_out,
           b_w_in, b_sinks, b_w_out, c_w_in, c_w_out, f_w_up, f_conv_w, f_conv_b, f_w_down):
    B, S, D = x.shape
    depth = g_mix.shape[0]
    tm = min(512, S)
    tab32 = _rope_table(positions, A_QK_DIM, A_ROPE_DIM)
    tab16 = _rope_table(positions, B_HEAD_DIM, B_HEAD_DIM // 4)
    mem_k, mem_v = _memkv(mem, g_mem.reshape(1, D), w_mem_kv.astype(BF16))
    conv_w = jnp.pad(f_conv_w, ((0, 0), (0, 8 - CONV_WIDTH), (0, 0)))
    for i in range(depth):
        kind, j = i % 3, i // 3
        g = g_mix[i].reshape(1, D)
        if kind == 0:
            q, kx, ct, qi, ki, wi, qm = _proj_a(x, g, _prep_a_w_in(a_w_in[j]), a_kv_norm[j].reshape(1, -1),
                                                tab32, tab16, tm)
            wuk = jnp.pad(jnp.transpose(a_w_uk[j], (1, 2, 0)), ((0, 0), (A_ROPE_DIM, 0), (0, 0))).astype(BF16)
            wuv = jnp.transpose(a_w_uv[j], (1, 0, 2)).astype(BF16)
            mix = _dsa(q, qi, wi, kx, ct, ki, wuk, wuv)
            w_out = a_w_out[j]
        elif kind == 1:
            nq, nkv = B_HEADS * B_HEAD_DIM, 4 * B_KV_HEADS * B_HEAD_DIM
            q, k, v, qm = _proj_qkv(x, g, _prep_b_w_in(b_w_in[j]), tab16, tm, B_HEAD_DIM, B_HEAD_DIM // 4,
                                    nq, nkv, nkv)
            mix, _ = _banded(q, k, v, dil=1, subheads=_b_subheads(), max_dist=B_WINDOW - 1, sinks=b_sinks[j])
            w_out = b_w_out[j]
        else:
            nq = C_HEADS * C_HEAD_DIM
            q, k, v, qm = _proj_qkv(x, g, c_w_in[j].astype(BF16), tab32, tm, C_HEAD_DIM, C_HEAD_DIM // 4,
                                    nq, nq, nq)
            prev = None
            for window, dil in C_BRANCHES:
                prev = _banded(q, k, v, dil=dil, subheads=_c_subheads(), max_dist=window // dil, prev=prev)
            mix = prev[0]
            w_out = c_w_out[j]
        x = _out_proj(x, mix, qm, mem_k[i], mem_v[i], w_out.astype(BF16), tm)
        x = _ffn(x, g_ffn[i].reshape(1, D), f_w_up[i].astype(BF16), conv_w[i], f_conv_b[i].reshape(1, -1),
                 f_w_down[i].astype(BF16), g_final.reshape(1, D), min(256, S), i == depth - 1)
    return x
```

```python
import functools

import jax
import jax.numpy as jnp
from jax import lax
from jax.experimental import pallas as pl
from jax.experimental.pallas import tpu as pltpu

F32 = jnp.float32
BF16 = jnp.bfloat16
I32 = jnp.int32

LANES = 128
BLOCK = 128
ROPE_THETA = 500000.0
EPS = 1e-6
NEG_INF = -1e30
INT_MIN = -(2**31)

A_HEADS = 8
A_QK_DIM = 128
A_ROPE_DIM = 32
A_KV_RANK = 256
IDX_HEADS = 16
IDX_DIM = 64
TOPK_MAX = 256
B_HEADS = 16
B_KV_HEADS = 4
B_HEAD_DIM = 64
B_WINDOW = 128
C_HEADS = 8
C_HEAD_DIM = 128
C_BRANCHES = ((128, 1), (512, 4), (2048, 16))
MEM_HEADS = 4
MEM_HEAD_DIM = 64
MEM_WIDTH = MEM_HEADS * MEM_HEAD_DIM
CONV_WIDTH = 3
KEY_CHUNK = 256

VMEM_LIMIT = 56 * 1024 * 1024

_NT = (((1,), (1,)), ((), ()))


def _cparams(sem):
    return pltpu.CompilerParams(dimension_semantics=sem, vmem_limit_bytes=VMEM_LIMIT)


def _rms(x, g):
    return x * lax.rsqrt(jnp.mean(x * x, axis=-1, keepdims=True) + EPS) * g


def _rope_tile(t, tab, half):
    c, sa, sb = tab[:, 0:LANES], tab[:, LANES:2 * LANES], tab[:, 2 * LANES:3 * LANES]
    return t * c + pltpu.roll(t, half, 1) * sa + pltpu.roll(t, LANES - half, 1) * sb


def _rope_table(positions, head_dim, rot):
    half = rot // 2
    inv = ROPE_THETA ** (-jnp.arange(0, rot, 2, dtype=F32) / rot)
    ang = positions.astype(F32)[..., None] * inv
    cos, sin = jnp.cos(ang), jnp.sin(ang)
    rest = head_dim - rot
    lead = cos.shape[:-1]
    c = jnp.concatenate([cos, cos, jnp.ones(lead + (rest,), F32)], axis=-1)
    sa = jnp.concatenate([jnp.zeros_like(sin), sin, jnp.zeros(lead + (rest,), F32)], axis=-1)
    sb = jnp.concatenate([-sin, jnp.zeros_like(sin), jnp.zeros(lead + (rest,), F32)], axis=-1)
    reps = LANES // head_dim
    return jnp.concatenate([jnp.tile(t, (1, 1, reps)) for t in (c, sa, sb)], axis=-1)


def _proj_a_kernel(x_ref, g_ref, w_ref, kvn_ref, t32_ref, t16_ref,
                   q_ref, kx_ref, ct_ref, qi_ref, ki_ref, wi_ref, qm_ref, *, tm):
    hb = _rms(x_ref[...], g_ref[...]).astype(BF16)
    t32 = t32_ref[...]
    t16 = t16_ref[...]

    def mm(a, b):
        return jnp.dot(hb, w_ref[:, a:b], preferred_element_type=F32)

    scale = A_QK_DIM ** -0.5
    for j in range(0, A_HEADS * A_QK_DIM, 2 * LANES):
        y = mm(j, j + 2 * LANES)
        for u in range(2):
            t = _rope_tile(y[:, u * LANES:(u + 1) * LANES], t32, A_ROPE_DIM // 2)
            q_ref[:, j + u * LANES:j + (u + 1) * LANES] = (t * scale).astype(BF16)
    o = A_HEADS * A_QK_DIM
    c = _rms(mm(o, o + A_KV_RANK), kvn_ref[...])
    kx_ref[:, 0:A_KV_RANK] = c.astype(BF16)
    for u in range(tm // KEY_CHUNK):
        ct_ref[u] = c[u * KEY_CHUNK:(u + 1) * KEY_CHUNK, :].T.astype(BF16)
    o += A_KV_RANK
    kx_ref[:, A_KV_RANK:A_KV_RANK + LANES] = _rope_tile(mm(o, o + LANES), t32, A_ROPE_DIM // 2).astype(BF16)
    o += LANES
    for j in range(0, IDX_HEADS * IDX_DIM, 2 * LANES):
        y = mm(o + j, o + j + 2 * LANES)
        for u in range(2):
            t = _rope_tile(y[:, u * LANES:(u + 1) * LANES], t16, IDX_DIM // 8)
            qi_ref[:, j + u * LANES:j + (u + 1) * LANES] = t.astype(BF16)
    o += IDX_HEADS * IDX_DIM
    y = mm(o, o + 2 * LANES)
    for u in range(2):
        t = _rope_tile(y[:, u * LANES:(u + 1) * LANES], t16, IDX_DIM // 8).astype(BF16)
        for v in range(tm // KEY_CHUNK):
            ki_ref[v, u * KEY_CHUNK:(u + 1) * KEY_CHUNK, :] = t[v * KEY_CHUNK:(v + 1) * KEY_CHUNK, :]
    o += 2 * LANES
    wi_ref[...] = mm(o, o + LANES) * (IDX_HEADS * IDX_DIM) ** -0.5
    o += LANES
    qm_ref[...] = (mm(o, o + MEM_WIDTH) * MEM_HEAD_DIM ** -0.5).astype(BF16)


def _prep_a_w_in(w):
    d = w.shape[0]
    sizes = (A_HEADS * A_QK_DIM, A_KV_RANK, A_ROPE_DIM, IDX_HEADS * IDX_DIM, IDX_DIM, IDX_HEADS, MEM_WIDTH)
    offs = [0]
    for s in sizes:
        offs.append(offs[-1] + s)
    q, ckv, kr, qi, ki, wi, qm = [w[:, offs[i]:offs[i + 1]] for i in range(len(sizes))]
    z = lambda n: jnp.zeros((d, n), w.dtype)
    return jnp.concatenate([
        q, ckv, kr, z(LANES - A_ROPE_DIM), qi,
        ki, z(LANES - IDX_DIM), z(LANES - IDX_DIM), ki,
        wi, z(LANES - IDX_HEADS), qm], axis=1).astype(BF16)


def _proj_a(x, g, w, kvn, t32, t16, tm):
    B, S, D = x.shape
    n = w.shape[1]
    row = lambda width: pl.BlockSpec((None, tm, width), lambda b, i: (b, i, 0))
    const = lambda shape: pl.BlockSpec(shape, lambda b, i: (0,) * len(shape))
    out_shape = (
        jax.ShapeDtypeStruct((B, S, A_HEADS * A_QK_DIM), BF16),
        jax.ShapeDtypeStruct((B, S, A_KV_RANK + LANES), BF16),
        jax.ShapeDtypeStruct((B, S // KEY_CHUNK, A_KV_RANK, KEY_CHUNK), BF16),
        jax.ShapeDtypeStruct((B, S, IDX_HEADS * IDX_DIM), BF16),
        jax.ShapeDtypeStruct((B, S // KEY_CHUNK, 2 * KEY_CHUNK, LANES), BF16),
        jax.ShapeDtypeStruct((B, S, LANES), F32),
        jax.ShapeDtypeStruct((B, S, MEM_WIDTH), BF16),
    )
    out_specs = (
        row(A_HEADS * A_QK_DIM), row(A_KV_RANK + LANES),
        pl.BlockSpec((None, tm // KEY_CHUNK, A_KV_RANK, KEY_CHUNK), lambda b, i: (b, i, 0, 0)),
        row(IDX_HEADS * IDX_DIM),
        pl.BlockSpec((None, tm // KEY_CHUNK, 2 * KEY_CHUNK, LANES), lambda b, i: (b, i, 0, 0)),
        row(LANES), row(MEM_WIDTH),
    )
    return pl.pallas_call(
        functools.partial(_proj_a_kernel, tm=tm),
        grid=(B, S // tm),
        in_specs=[row(D), const((1, D)), const((D, n)), const((1, A_KV_RANK)), row(3 * LANES), row(3 * LANES)],
        out_specs=out_specs,
        out_shape=out_shape,
        compiler_params=_cparams(("parallel", "parallel")),
        name="proj_a",
    )(x, g, w, kvn, t32, t16)


def _proj_qkv_kernel(x_ref, g_ref, w_ref, tab_ref, q_ref, k_ref, v_ref, qm_ref, *,
                     head_dim, rot, nq, nk, nv, v_transposed):
    hb = _rms(x_ref[...], g_ref[...]).astype(BF16)
    tm = hb.shape[0]
    tab = tab_ref[...]

    def mm(a, b):
        return jnp.dot(hb, w_ref[:, a:b], preferred_element_type=F32)

    scale = head_dim ** -0.5
    for j in range(0, nq, 2 * LANES):
        y = mm(j, j + 2 * LANES)
        for u in range(2):
            t = _rope_tile(y[:, u * LANES:(u + 1) * LANES], tab, rot // 2)
            q_ref[:, j + u * LANES:j + (u + 1) * LANES] = (t * scale).astype(BF16)
    for j in range(0, nk, 2 * LANES):
        y = mm(nq + j, nq + j + 2 * LANES)
        for u in range(2):
            t = _rope_tile(y[:, u * LANES:(u + 1) * LANES], tab, rot // 2)
            k_ref[:, j + u * LANES:j + (u + 1) * LANES] = t.astype(BF16)
    for j in range(0, nv, 2 * LANES):
        y = mm(nq + nk + j, nq + nk + j + 2 * LANES)
        if v_transposed:
            for u in range(tm // KEY_CHUNK):
                v_ref[u, j:j + 2 * LANES, :] = y[u * KEY_CHUNK:(u + 1) * KEY_CHUNK, :].T.astype(BF16)
        else:
            v_ref[:, j:j + 2 * LANES] = y.astype(BF16)
    o = nq + nk + nv
    qm_ref[...] = (mm(o, o + MEM_WIDTH) * MEM_HEAD_DIM ** -0.5).astype(BF16)


def _proj_qkv(x, g, w, tab, tm, head_dim, rot, nq, nk, nv, v_transposed):
    B, S, D = x.shape
    n = w.shape[1]
    row = lambda width: pl.BlockSpec((None, tm, width), lambda b, i: (b, i, 0))
    const = lambda shape: pl.BlockSpec(shape, lambda b, i: (0,) * len(shape))
    if v_transposed:
        v_spec = pl.BlockSpec((None, tm // KEY_CHUNK, nv, KEY_CHUNK), lambda b, i: (b, i, 0, 0))
        v_shape = jax.ShapeDtypeStruct((B, S // KEY_CHUNK, nv, KEY_CHUNK), BF16)
    else:
        v_spec, v_shape = row(nv), jax.ShapeDtypeStruct((B, S, nv), BF16)
    return pl.pallas_call(
        functools.partial(_proj_qkv_kernel, head_dim=head_dim, rot=rot, nq=nq, nk=nk, nv=nv,
                          v_transposed=v_transposed),
        grid=(B, S // tm),
        in_specs=[row(D), const((1, D)), const((D, n)), row(3 * LANES)],
        out_specs=(row(nq), row(nk), v_spec, row(MEM_WIDTH)),
        out_shape=(jax.ShapeDtypeStruct((B, S, nq), BF16), jax.ShapeDtypeStruct((B, S, nk), BF16), v_shape,
                   jax.ShapeDtypeStruct((B, S, MEM_WIDTH), BF16)),
        compiler_params=_cparams(("parallel", "parallel")),
        name="proj_qkv",
    )(x, g, w, tab)


def _prep_b_w_in(w):
    d = w.shape[0]
    nq, nkv = B_HEADS * B_HEAD_DIM, B_KV_HEADS * B_HEAD_DIM
    q, k, v, qm = w[:, :nq], w[:, nq:nq + nkv], w[:, nq + nkv:nq + 2 * nkv], w[:, nq + 2 * nkv:]
    z = jnp.zeros((d, B_HEAD_DIM), w.dtype)

    def spread(t):
        cols = []
        for h in range(B_KV_HEADS):
            th = t[:, h * B_HEAD_DIM:(h + 1) * B_HEAD_DIM]
            cols += [th, z, z, th]
        return jnp.concatenate(cols, axis=1)

    return jnp.concatenate([q, spread(k), spread(v), qm], axis=1).astype(BF16)


def _dsa_kernel(q_ref, qi_ref, wi_ref, kx_ref, ct_ref, ki_ref, wuk_ref, wuv_ref, o_ref,
                keys_ref, qext_ref, qi2_ref, wrow_ref, acc_ref, sa_ref, sb_ref, *, topk):
    kc = KEY_CHUNK
    hw = A_HEADS * LANES
    gw = 2 * LANES
    i = pl.program_id(1)
    nch = (i * BLOCK + BLOCK + kc - 1) // kc
    lane = lax.broadcasted_iota(I32, (BLOCK, LANES), 1)

    for h in range(A_HEADS):
        rows = slice(h * BLOCK, (h + 1) * BLOCK)
        qh = q_ref[:, h * LANES:(h + 1) * LANES]
        qext_ref[rows, 0:A_KV_RANK] = jnp.dot(qh, wuk_ref[h], preferred_element_type=F32).astype(BF16)
        qext_ref[rows, A_KV_RANK:A_KV_RANK + LANES] = jnp.where(lane < A_ROPE_DIM, qh.astype(F32), 0.0).astype(BF16)
        qi2_ref[rows, :] = qi_ref[:, h * LANES:(h + 1) * LANES]
    w_t = wi_ref[...].T
    for j in range(IDX_HEADS // 2):
        wrow_ref[0:1, j * LANES:(j + 1) * LANES] = w_t[2 * j:2 * j + 1, :]
        wrow_ref[1:2, j * LANES:(j + 1) * LANES] = w_t[2 * j + 1:2 * j + 2, :]

    qpos = i * BLOCK + lax.broadcasted_iota(I32, (kc, BLOCK), 1)
    krow = lax.broadcasted_iota(I32, (kc, BLOCK), 0)

    def score_chunk(c):
        off = pl.multiple_of(c * kc, kc)
        kk = ki_ref[c]
        acc = None
        for g in range(0, hw, gw):
            d = lax.dot_general(kk, qi2_ref[g:g + gw, :], _NT, preferred_element_type=F32)
            t = (jnp.maximum(d[0:kc], 0.0) * wrow_ref[0:1, g:g + gw]
                 + jnp.maximum(d[kc:2 * kc], 0.0) * wrow_ref[1:2, g:g + gw])
            for u in range(0, gw, LANES):
                acc = t[:, u:u + LANES] if acc is None else acc + t[:, u:u + LANES]
        bits = pltpu.bitcast(acc, I32)
        key = jnp.where(bits < 0, bits ^ 0x7FFFFFFF, bits)
        key = jnp.where(off + krow <= qpos, key, INT_MIN)
        keys_ref[pl.ds(off, kc), :] = key

    npairs = (nch + 1) // 2

    def score_body(j, carry):
        score_chunk(2 * j)
        score_chunk(2 * j + 1)
        return carry

    lax.fori_loop(0, npairs, score_body, 0)

    def bit_body(p, res):
        trial = res | lax.shift_left(jnp.int32(1), 31 - p)
        thr = trial ^ INT_MIN

        def cnt_body(c, cnt):
            off = pl.multiple_of(c * 2 * kc, 2 * kc)
            ge = (keys_ref[pl.ds(off, 2 * kc), :] >= thr).astype(I32)
            return cnt + jnp.sum(ge.reshape(2 * kc // 8, 8, BLOCK), axis=0)

        cnt = lax.fori_loop(0, npairs, cnt_body, jnp.zeros((8, BLOCK), I32))
        return jnp.where(jnp.sum(cnt, axis=0, keepdims=True) >= topk, trial, res)

    res = lax.fori_loop(0, 32, bit_body, jnp.zeros((1, BLOCK), I32))
    thr = jnp.maximum(res ^ INT_MIN, INT_MIN + 1)

    acc_ref[...] = jnp.zeros(acc_ref.shape, F32)

    def logits(c, s_ref):
        off = pl.multiple_of(c * kc, kc)
        bias = jnp.where(keys_ref[pl.ds(off, kc), :] >= thr, 0.0, NEG_INF)
        bias = jnp.concatenate([bias] * (gw // LANES), axis=1)
        kx = kx_ref[pl.ds(off, kc), :]
        for g in range(0, hw, gw):
            s_ref[:, g:g + gw] = lax.dot_general(
                kx, qext_ref[g:g + gw, :], _NT, preferred_element_type=F32) + bias

    def consume(s_ref, c, carry):
        m_prev, l_prev = carry
        ct = ct_ref[c]
        m_out, l_out = [], []
        for g in range(0, hw, gw):
            s = s_ref[:, g:g + gw]
            m_new = jnp.maximum(m_prev[:, g:g + gw], jnp.max(s, axis=0, keepdims=True))
            alpha = jnp.exp(m_prev[:, g:g + gw] - m_new)
            p = jnp.exp(s - m_new)
            l_out.append(alpha * l_prev[:, g:g + gw] + jnp.sum(p, axis=0, keepdims=True))
            m_out.append(m_new)
            acc_ref[:, g:g + gw] = alpha * acc_ref[:, g:g + gw] + jnp.dot(
                ct, p.astype(BF16), preferred_element_type=F32)
        return jnp.concatenate(m_out, axis=1), jnp.concatenate(l_out, axis=1)

    logits(0, sa_ref)

    def pair_body(j, carry):
        c0 = 2 * j
        logits(c0 + 1, sb_ref)
        carry = consume(sa_ref, c0, carry)
        logits(jnp.minimum(c0 + 2, 2 * npairs - 1), sa_ref)
        return consume(sb_ref, c0 + 1, carry)

    _, l_fin = lax.fori_loop(0, npairs, pair_body,
                             (jnp.full((1, hw), NEG_INF, F32), jnp.zeros((1, hw), F32)))

    inv_l = 1.0 / l_fin
    for h in range(A_HEADS):
        cols = slice(h * LANES, (h + 1) * LANES)
        o_lat = (acc_ref[:, cols] * inv_l[:, cols]).T.astype(BF16)
        o_ref[:, cols] = jnp.dot(o_lat, wuv_ref[h], preferred_element_type=F32).astype(BF16)


def _dsa(q, qi, wi, kx, ct, ki, wuk, wuv):
    B, S, _ = q.shape
    topk = min(TOPK_MAX, S // 4)
    blk = lambda width: pl.BlockSpec((None, BLOCK, width), lambda b, i: (b, i, 0))
    seq = lambda width: pl.BlockSpec((None, S, width), lambda b, i: (b, 0, 0))
    const3 = lambda shape: pl.BlockSpec(shape, lambda b, i: (0, 0, 0))
    return pl.pallas_call(
        functools.partial(_dsa_kernel, topk=topk),
        grid=(B, S // BLOCK),
        in_specs=[blk(A_HEADS * A_QK_DIM), blk(IDX_HEADS * IDX_DIM), blk(LANES),
                  seq(A_KV_RANK + LANES),
                  pl.BlockSpec((None, S // KEY_CHUNK, A_KV_RANK, KEY_CHUNK), lambda b, i: (b, 0, 0, 0)),
                  pl.BlockSpec((None, S // KEY_CHUNK, 2 * KEY_CHUNK, LANES), lambda b, i: (b, 0, 0, 0)),
                  const3(wuk.shape), const3(wuv.shape)],
        out_specs=blk(A_HEADS * LANES),
        out_shape=jax.ShapeDtypeStruct((B, S, A_HEADS * LANES), BF16),
        scratch_shapes=[
            pltpu.VMEM((S, BLOCK), I32),
            pltpu.VMEM((A_HEADS * BLOCK, A_KV_RANK + LANES), BF16),
            pltpu.VMEM((A_HEADS * BLOCK, LANES), BF16),
            pltpu.VMEM((8, A_HEADS * LANES), F32),
            pltpu.VMEM((A_KV_RANK, A_HEADS * LANES), F32),
            pltpu.VMEM((KEY_CHUNK, A_HEADS * LANES), F32),
            pltpu.VMEM((KEY_CHUNK, A_HEADS * LANES), F32),
        ],
        compiler_params=_cparams(("parallel", "arbitrary")),
        name="dsa",
    )(q, qi, wi, kx, ct, ki, wuk, wuv)


def _window_bias(branches, nrel):
    par = jnp.arange(2, dtype=I32)[:, None, None, None]
    rel = jnp.arange(nrel, dtype=I32)[None, :, None, None]
    krow = jnp.arange(KEY_CHUNK, dtype=I32)[None, None, :, None]
    qcol = jnp.arange(BLOCK, dtype=I32)[None, None, None, :]
    dist = BLOCK * par + KEY_CHUNK * (nrel - 1 - rel) + qcol - krow
    mult = sum(((dist >= 0) & (dist <= window) & (dist % dil == 0)).astype(F32) for window, dil in branches)
    return jnp.where(mult > 0, jnp.log(jnp.maximum(mult, 1.0)), NEG_INF)


def _wattn_kernel(q_ref, k_ref, vt_ref, bias_ref, o_ref, acc_ref, s_ref, p_ref, *, nh, nrel):
    kc = KEY_CHUNK
    i = pl.program_id(1)
    par, top = i % 2, i // 2
    c_lo = jnp.maximum(top - (nrel - 1), 0)

    acc_ref[...] = jnp.zeros(acc_ref.shape, F32)

    def body(c, carry):
        m_prev, l_prev = carry
        off = pl.multiple_of(c * kc, kc)
        bias = bias_ref[par, c - top + (nrel - 1)]
        for h in range(nh):
            cols = slice(h * LANES, (h + 1) * LANES)
            s_ref[:, cols] = lax.dot_general(k_ref[pl.ds(off, kc), cols], q_ref[:, cols], _NT,
                                             preferred_element_type=F32) + bias
        s = s_ref[...]
        m_new = jnp.maximum(m_prev, jnp.max(s, axis=0, keepdims=True))
        alpha = jnp.exp(m_prev - m_new)
        p = jnp.exp(s - m_new)
        l_new = alpha * l_prev + jnp.sum(p, axis=0, keepdims=True)
        p_ref[...] = p.astype(BF16)
        for h in range(nh):
            cols = slice(h * LANES, (h + 1) * LANES)
            acc_ref[:, cols] = alpha[:, cols] * acc_ref[:, cols] + jnp.dot(
                vt_ref[c, cols, :], p_ref[:, cols], preferred_element_type=F32)
        return m_new, l_new

    _, l_fin = lax.fori_loop(c_lo, top + 1, body,
                             (jnp.full((1, nh * LANES), NEG_INF, F32), jnp.zeros((1, nh * LANES), F32)))

    inv_l = 1.0 / l_fin
    for h in range(nh):
        cols = slice(h * LANES, (h + 1) * LANES)
        o_ref[:, cols] = (acc_ref[:, cols] * inv_l[:, cols]).T.astype(BF16)


def _swa_kernel(sink_ref, q_ref, kp_ref, kc_ref, vp_ref, vc_ref, o_ref, *, subheads, max_dist):
    n = pl.program_id(1)
    qi = lax.broadcasted_iota(I32, (BLOCK, 2 * BLOCK), 0)
    kj = lax.broadcasted_iota(I32, (BLOCK, 2 * BLOCK), 1)
    dist = BLOCK + qi - kj
    mask = (dist >= 0) & (dist <= max_dist) & ((kj >= BLOCK) | (n > 0))
    for j, subs in enumerate(subheads):
        qt = q_ref[:, j * LANES:(j + 1) * LANES]
        o_t = None
        for kt, vt, hidx in subs:
            kk = jnp.concatenate([kp_ref[:, kt * LANES:(kt + 1) * LANES],
                                  kc_ref[:, kt * LANES:(kt + 1) * LANES]], axis=0)
            vv = jnp.concatenate([vp_ref[:, vt * LANES:(vt + 1) * LANES],
                                  vc_ref[:, vt * LANES:(vt + 1) * LANES]], axis=0)
            s = lax.dot_general(qt, kk, _NT, preferred_element_type=F32)
            s = jnp.where(mask, s, NEG_INF)
            sk = sink_ref[hidx]
            m = jnp.maximum(jnp.max(s, axis=-1, keepdims=True), sk)
            p = jnp.exp(s - m)
            l = jnp.sum(p, axis=-1, keepdims=True) + jnp.exp(sk - m)
            o_s = jnp.dot(p.astype(BF16), vv, preferred_element_type=F32) / l
            o_t = o_s if o_t is None else o_t + o_s
        o_ref[:, j * LANES:(j + 1) * LANES] = o_t.astype(BF16)


def _swa(q, k, v, sinks, *, subheads, max_dist):
    B, S, wq = q.shape
    wk, wv = k.shape[-1], v.shape[-1]
    cur = lambda width: pl.BlockSpec((None, BLOCK, width), lambda b, i: (b, i, 0))
    prv = lambda width: pl.BlockSpec((None, BLOCK, width), lambda b, i: (b, jnp.maximum(i - 1, 0), 0))
    return pl.pallas_call(
        functools.partial(_swa_kernel, subheads=subheads, max_dist=max_dist),
        grid=(B, S // BLOCK),
        in_specs=[pl.BlockSpec(memory_space=pltpu.SMEM), cur(wq), prv(wk), cur(wk), prv(wv), cur(wv)],
        out_specs=cur(wq),
        out_shape=jax.ShapeDtypeStruct((B, S, wq), BF16),
        compiler_params=_cparams(("parallel", "arbitrary")),
        name="swa",
    )(sinks, q, k, k, v, v)


def _wattn(q, k, vt, branches):
    B, S, wq = q.shape
    widest = max(w for w, _ in branches)
    nrel = min(-(-widest // KEY_CHUNK) + 1, S // KEY_CHUNK)
    bias = _window_bias(branches, nrel)
    nh = wq // LANES
    blk = lambda width: pl.BlockSpec((None, BLOCK, width), lambda b, i: (b, i, 0))
    return pl.pallas_call(
        functools.partial(_wattn_kernel, nh=nh, nrel=nrel),
        grid=(B, S // BLOCK),
        in_specs=[blk(wq),
                  pl.BlockSpec((None, S, wq), lambda b, i: (b, 0, 0)),
                  pl.BlockSpec((None,) + vt.shape[1:], lambda b, i: (b, 0, 0, 0)),
                  pl.BlockSpec(bias.shape, lambda b, i: (0, 0, 0, 0))],
        out_specs=blk(wq),
        out_shape=jax.ShapeDtypeStruct((B, S, wq), BF16),
        scratch_shapes=[pltpu.VMEM((LANES, nh * LANES), F32),
                        pltpu.VMEM((KEY_CHUNK, nh * LANES), F32),
                        pltpu.VMEM((KEY_CHUNK, nh * LANES), BF16)],
        compiler_params=_cparams(("parallel", "arbitrary")),
        name="wattn",
    )(q, k, vt, bias)


def _memkv_kernel(mem_ref, g_ref, w_ref, k_ref, v_ref):
    hb = _rms(mem_ref[...], g_ref[...]).astype(BF16)
    y = jnp.dot(hb, w_ref[...], preferred_element_type=F32)
    lane = lax.broadcasted_iota(I32, (y.shape[0], LANES), 1)
    for t in range(MEM_WIDTH // LANES):
        for out_ref, base in ((k_ref, 0), (v_ref, MEM_WIDTH)):
            tile = y[:, base + t * LANES:base + (t + 1) * LANES]
            out_ref[:, (2 * t) * LANES:(2 * t + 1) * LANES] = jnp.where(lane < MEM_HEAD_DIM, tile, 0.0).astype(BF16)
            out_ref[:, (2 * t + 1) * LANES:(2 * t + 2) * LANES] = jnp.where(lane >= MEM_HEAD_DIM, tile, 0.0).astype(BF16)


def _memkv(mem, g_mem, w):
    B, M, D = mem.shape
    L = w.shape[0]
    out = jax.ShapeDtypeStruct((L, B, M, 2 * MEM_WIDTH), BF16)
    ospec = pl.BlockSpec((None, None, M, 2 * MEM_WIDTH), lambda l, b: (l, b, 0, 0))
    return pl.pallas_call(
        _memkv_kernel,
        grid=(L, B),
        in_specs=[pl.BlockSpec((None, M, D), lambda l, b: (b, 0, 0)),
                  pl.BlockSpec((1, D), lambda l, b: (0, 0)),
                  pl.BlockSpec((None, D, 2 * MEM_WIDTH), lambda l, b: (l, 0, 0))],
        out_specs=(ospec, ospec),
        out_shape=(out, out),
        compiler_params=_cparams(("parallel", "parallel")),
        name="memkv",
    )(mem, g_mem, w)


def _out_kernel(x_ref, mix_ref, qm_ref, mk_ref, mv_ref, w_ref, o_ref):
    mw = mix_ref.shape[-1]
    y = x_ref[...] + jnp.dot(mix_ref[...], w_ref[0:mw, :], preferred_element_type=F32)
    for t in range(MEM_WIDTH // LANES):
        qt = qm_ref[:, t * LANES:(t + 1) * LANES]
        o_t = None
        for u in range(2):
            sl = slice((2 * t + u) * LANES, (2 * t + u + 1) * LANES)
            s = lax.dot_general(qt, mk_ref[:, sl], _NT, preferred_element_type=F32)
            p = jnp.exp(s - jnp.max(s, axis=-1, keepdims=True))
            l = jnp.sum(p, axis=-1, keepdims=True)
            o_s = jnp.dot(p.astype(BF16), mv_ref[:, sl], preferred_element_type=F32) / l
            o_t = o_s if o_t is None else o_t + o_s
        y = y + jnp.dot(o_t.astype(BF16), w_ref[mw + t * LANES:mw + (t + 1) * LANES, :],
                        preferred_element_type=F32)
    o_ref[...] = y


def _out_proj(x, mix, qm, mk, mv, w, tm):
    B, S, D = x.shape
    M = mk.shape[1]
    row = lambda width: pl.BlockSpec((None, tm, width), lambda b, i: (b, i, 0))
    mem = pl.BlockSpec((None, M, 2 * MEM_WIDTH), lambda b, i: (b, 0, 0))
    return pl.pallas_call(
        _out_kernel,
        grid=(B, S // tm),
        in_specs=[row(D), row(mix.shape[-1]), row(MEM_WIDTH), mem, mem,
                  pl.BlockSpec(w.shape, lambda b, i: (0, 0))],
        out_specs=row(D),
        out_shape=jax.ShapeDtypeStruct((B, S, D), F32),
        compiler_params=_cparams(("parallel", "parallel")),
        name="out_proj",
    )(x, mix, qm, mk, mv, w)


def _ffn_kernel(x_ref, g_ref, wup_ref, cw_ref, cb_ref, wdn_ref, gf_ref, o_ref, carry_ref, *, tm, cw, final):
    dff = wdn_ref.shape[0]
    first = pl.program_id(1) == 0
    x = x_ref[...]
    hb = _rms(x, g_ref[...]).astype(BF16)
    row = lax.broadcasted_iota(I32, (tm, cw), 0)
    acc = x

    def up(c0):
        return (jnp.dot(hb, wup_ref[:, c0:c0 + cw], preferred_element_type=F32),
                jnp.dot(hb, wup_ref[:, dff + c0:dff + c0 + cw], preferred_element_type=F32))

    nxt = up(0)
    for c0 in range(0, dff, cw):
        a, b = nxt
        if c0 + cw < dff:
            nxt = up(c0 + cw)
        prev = jnp.where(first, 0.0, carry_ref[:, c0:c0 + cw])
        p1, p2 = prev[7:8, :], prev[6:7, :]
        a1 = jnp.where(row == 0, p1, pltpu.roll(a, 1, 0))
        a2 = jnp.where(row == 0, p2, jnp.where(row == 1, p1, pltpu.roll(a, 2, 0)))
        carry_ref[:, c0:c0 + cw] = a[tm - 8:tm, :]
        w = cw_ref[:, c0:c0 + cw]
        conv = w[0:1, :] * a2 + w[1:2, :] * a1 + w[2:3, :] * a + cb_ref[:, c0:c0 + cw]
        gated = conv / (1.0 + jnp.exp(-conv)) * b
        acc = acc + jnp.dot(gated.astype(BF16), wdn_ref[c0:c0 + cw, :], preferred_element_type=F32)
    if final:
        acc = _rms(acc, gf_ref[...])
    o_ref[...] = acc


def _ffn(x, g, wup, cw, cb, wdn, gf, tm, final):
    B, S, D = x.shape
    dff = wdn.shape[0]
    row = pl.BlockSpec((None, tm, D), lambda b, i: (b, i, 0))
    const = lambda shape: pl.BlockSpec(shape, lambda b, i: (0, 0), pipeline_mode=pl.Buffered(1))
    return pl.pallas_call(
        functools.partial(_ffn_kernel, tm=tm, cw=2 * LANES, final=final),
        grid=(B, S // tm),
        in_specs=[row, const((1, D)), const(wup.shape), const(cw.shape), const((1, dff)), const(wdn.shape),
                  const((1, D))],
        out_specs=row,
        out_shape=jax.ShapeDtypeStruct((B, S, D), F32),
        scratch_shapes=[pltpu.VMEM((8, dff), F32)],
        compiler_params=_cparams(("arbitrary", "arbitrary")),
        name="ffn",
    )(x, g, wup, cw, cb, wdn, gf)


def _b_subheads():
    return tuple(((2 * (j // 2), 2 * (j // 2), 2 * j), (2 * (j // 2) + 1, 2 * (j // 2) + 1, 2 * j + 1))
                 for j in range(B_HEADS // 2))


def kernel(x, mem, positions, g_mix, g_ffn, g_mem, g_final, w_mem_kv, a_w_in, a_kv_norm, a_w_uk, a_w_uv, a_w_out,
           b_w_in, b_sinks, b_w_out, c_w_in, c_w_out, f_w_up, f_conv_w, f_conv_b, f_w_down):
    B, S, D = x.shape
    depth = g_mix.shape[0]
    tm = min(512, S)
    tab32 = _rope_table(positions, A_QK_DIM, A_ROPE_DIM)
    tab16 = _rope_table(positions, B_HEAD_DIM, B_HEAD_DIM // 4)
    mem_k, mem_v = _memkv(mem, g_mem.reshape(1, D), w_mem_kv.astype(BF16))
    conv_w = jnp.pad(f_conv_w, ((0, 0), (0, 8 - CONV_WIDTH), (0, 0)))
    for i in range(depth):
        kind, j = i % 3, i // 3
        g = g_mix[i].reshape(1, D)
        if kind == 0:
            q, kx, ct, qi, ki, wi, qm = _proj_a(x, g, _prep_a_w_in(a_w_in[j]), a_kv_norm[j].reshape(1, -1),
                                                tab32, tab16, tm)
            wuk = jnp.pad(jnp.transpose(a_w_uk[j], (1, 2, 0)), ((0, 0), (A_ROPE_DIM, 0), (0, 0))).astype(BF16)
            wuv = jnp.transpose(a_w_uv[j], (1, 0, 2)).astype(BF16)
            mix = _dsa(q, qi, wi, kx, ct, ki, wuk, wuv)
            w_out = a_w_out[j]
        elif kind == 1:
            nq, nkv = B_HEADS * B_HEAD_DIM, 4 * B_KV_HEADS * B_HEAD_DIM
            q, k, v, qm = _proj_qkv(x, g, _prep_b_w_in(b_w_in[j]), tab16, tm, B_HEAD_DIM, B_HEAD_DIM // 4,
                                    nq, nkv, nkv, False)
            mix = _swa(q, k, v, b_sinks[j], subheads=_b_subheads(), max_dist=B_WINDOW - 1)
            w_out = b_w_out[j]
        else:
            nq = C_HEADS * C_HEAD_DIM
            q, k, v, qm = _proj_qkv(x, g, c_w_in[j].astype(BF16), tab32, tm, C_HEAD_DIM, C_HEAD_DIM // 4,
                                    nq, nq, nq, True)
            mix = _wattn(q, k, v, C_BRANCHES)
            w_out = c_w_out[j]
        x = _out_proj(x, mix, qm, mem_k[i], mem_v[i], w_out.astype(BF16), tm)
        x = _ffn(x, g_ffn[i].reshape(1, D), f_w_up[i].astype(BF16), conv_w[i], f_conv_b[i].reshape(1, -1),
                 f_w_down[i].astype(BF16), g_final.reshape(1, D), min(256, S), i == depth - 1)
    return x
```

```python
import functools

import jax
import jax.numpy as jnp
import numpy as np
from jax import lax
from jax.experimental import pallas as pl
from jax.experimental.pallas import tpu as pltpu

F32 = jnp.float32
BF16 = jnp.bfloat16
I32 = jnp.int32

LANES = 128
BLOCK = 128
ROPE_THETA = 500000.0
EPS = 1e-6
NEG_INF = -1e30
INT_MIN = -(2**31)

A_HEADS = 8
A_QK_DIM = 128
A_ROPE_DIM = 32
A_KV_RANK = 256
IDX_HEADS = 16
IDX_DIM = 64
TOPK_MAX = 256
B_HEADS = 16
B_KV_HEADS = 4
B_HEAD_DIM = 64
B_WINDOW = 128
C_HEADS = 8
C_HEAD_DIM = 128
C_BRANCHES = ((128, 1), (512, 4), (2048, 16))
MEM_HEADS = 4
MEM_HEAD_DIM = 64
MEM_WIDTH = MEM_HEADS * MEM_HEAD_DIM
CONV_WIDTH = 3
KEY_CHUNK = 256

VMEM_LIMIT = 56 * 1024 * 1024

_NT = (((1,), (1,)), ((), ()))


def _cparams(sem):
    return pltpu.CompilerParams(dimension_semantics=sem, vmem_limit_bytes=VMEM_LIMIT)


def _rms(x, g):
    return x * lax.rsqrt(jnp.mean(x * x, axis=-1, keepdims=True) + EPS) * g


def _loop_two_per_trip(lo, hi, body, carry):
    trips = (hi - lo) // 2

    def double(t, c):
        return body(lo + 2 * t + 1, body(lo + 2 * t, c))

    carry = lax.fori_loop(0, trips, double, carry)
    return lax.fori_loop(lo + 2 * trips, hi, body, carry)


def _rope_tile(t, tab, half):
    c, sa, sb = tab[:, 0:LANES], tab[:, LANES:2 * LANES], tab[:, 2 * LANES:3 * LANES]
    return t * c + pltpu.roll(t, half, 1) * sa + pltpu.roll(t, LANES - half, 1) * sb


def _rope_table(positions, head_dim, rot):
    half = rot // 2
    inv = ROPE_THETA ** (-jnp.arange(0, rot, 2, dtype=F32) / rot)
    ang = positions.astype(F32)[..., None] * inv
    cs = jnp.concatenate([jnp.cos(ang), jnp.sin(ang)], axis=-1)
    sel = np.zeros((rot, 3 * LANES), np.float32)
    one = np.zeros((3 * LANES,), np.float32)
    for l in range(LANES):
        j = l % head_dim
        if j < half:
            sel[j, l] = 1.0
            sel[half + j, 2 * LANES + l] = -1.0
        elif j < rot:
            sel[j - half, l] = 1.0
            sel[j, LANES + l] = 1.0
        else:
            one[l] = 1.0
    return jnp.dot(cs, jnp.asarray(sel), precision=lax.Precision.HIGHEST) + jnp.asarray(one)


def _proj_a_kernel(x_ref, g_ref, w_ref, kvn_ref, t32_ref, t16_ref,
                   q_ref, kx_ref, ct_ref, qi_ref, ki_ref, wi_ref, qm_ref, *, tm):
    hb = _rms(x_ref[...], g_ref[...]).astype(BF16)
    t32 = t32_ref[...]
    t16 = t16_ref[...]

    def mm(a, b):
        return jnp.dot(hb, w_ref[:, a:b], preferred_element_type=F32)

    scale = A_QK_DIM ** -0.5
    for j in range(0, A_HEADS * A_QK_DIM, 2 * LANES):
        y = mm(j, j + 2 * LANES)
        for u in range(2):
            t = _rope_tile(y[:, u * LANES:(u + 1) * LANES], t32, A_ROPE_DIM // 2)
            q_ref[:, j + u * LANES:j + (u + 1) * LANES] = (t * scale).astype(BF16)
    o = A_HEADS * A_QK_DIM
    c = _rms(mm(o, o + A_KV_RANK), kvn_ref[...])
    kx_ref[:, 0:A_KV_RANK] = c.astype(BF16)
    for u in range(tm // KEY_CHUNK):
        ct_ref[u] = c[u * KEY_CHUNK:(u + 1) * KEY_CHUNK, :].T.astype(BF16)
    o += A_KV_RANK
    kx_ref[:, A_KV_RANK:A_KV_RANK + LANES] = _rope_tile(mm(o, o + LANES), t32, A_ROPE_DIM // 2).astype(BF16)
    o += LANES
    for j in range(0, IDX_HEADS * IDX_DIM, 2 * LANES):
        y = mm(o + j, o + j + 2 * LANES)
        for u in range(2):
            t = _rope_tile(y[:, u * LANES:(u + 1) * LANES], t16, IDX_DIM // 8)
            qi_ref[:, j + u * LANES:j + (u + 1) * LANES] = t.astype(BF16)
    o += IDX_HEADS * IDX_DIM
    y = mm(o, o + 2 * LANES)
    for u in range(2):
        t = _rope_tile(y[:, u * LANES:(u + 1) * LANES], t16, IDX_DIM // 8).astype(BF16)
        for v in range(tm // KEY_CHUNK):
            ki_ref[v, u * KEY_CHUNK:(u + 1) * KEY_CHUNK, :] = t[v * KEY_CHUNK:(v + 1) * KEY_CHUNK, :]
    o += 2 * LANES
    wi_ref[...] = mm(o, o + LANES) * (IDX_HEADS * IDX_DIM) ** -0.5
    o += LANES
    qm_ref[...] = (mm(o, o + MEM_WIDTH) * MEM_HEAD_DIM ** -0.5).astype(BF16)


def _prep_a_w_in(w):
    d = w.shape[0]
    sizes = (A_HEADS * A_QK_DIM, A_KV_RANK, A_ROPE_DIM, IDX_HEADS * IDX_DIM, IDX_DIM, IDX_HEADS, MEM_WIDTH)
    offs = [0]
    for s in sizes:
        offs.append(offs[-1] + s)
    q, ckv, kr, qi, ki, wi, qm = [w[:, offs[i]:offs[i + 1]] for i in range(len(sizes))]
    z = lambda n: jnp.zeros((d, n), w.dtype)
    return jnp.concatenate([
        q, ckv, kr, z(LANES - A_ROPE_DIM), qi,
        ki, z(LANES - IDX_DIM), z(LANES - IDX_DIM), ki,
        wi, z(LANES - IDX_HEADS), qm], axis=1).astype(BF16)


def _proj_a(x, g, w, kvn, t32, t16, tm):
    B, S, D = x.shape
    n = w.shape[1]
    row = lambda width: pl.BlockSpec((None, tm, width), lambda b, i: (b, i, 0))
    const = lambda shape: pl.BlockSpec(shape, lambda b, i: (0,) * len(shape))
    out_shape = (
        jax.ShapeDtypeStruct((B, S, A_HEADS * A_QK_DIM), BF16),
        jax.ShapeDtypeStruct((B, S, A_KV_RANK + LANES), BF16),
        jax.ShapeDtypeStruct((B, S // KEY_CHUNK, A_KV_RANK, KEY_CHUNK), BF16),
        jax.ShapeDtypeStruct((B, S, IDX_HEADS * IDX_DIM), BF16),
        jax.ShapeDtypeStruct((B, S // KEY_CHUNK, 2 * KEY_CHUNK, LANES), BF16),
        jax.ShapeDtypeStruct((B, S, LANES), F32),
        jax.ShapeDtypeStruct((B, S, MEM_WIDTH), BF16),
    )
    out_specs = (
        row(A_HEADS * A_QK_DIM), row(A_KV_RANK + LANES),
        pl.BlockSpec((None, tm // KEY_CHUNK, A_KV_RANK, KEY_CHUNK), lambda b, i: (b, i, 0, 0)),
        row(IDX_HEADS * IDX_DIM),
        pl.BlockSpec((None, tm // KEY_CHUNK, 2 * KEY_CHUNK, LANES), lambda b, i: (b, i, 0, 0)),
        row(LANES), row(MEM_WIDTH),
    )
    return pl.pallas_call(
        functools.partial(_proj_a_kernel, tm=tm),
        grid=(B, S // tm),
        in_specs=[row(D), const((1, D)), const((D, n)), const((1, A_KV_RANK)), row(3 * LANES), row(3 * LANES)],
        out_specs=out_specs,
        out_shape=out_shape,
        compiler_params=_cparams(("parallel", "parallel")),
        name="proj_a",
    )(x, g, w, kvn, t32, t16)


def _proj_qkv_kernel(x_ref, g_ref, w_ref, tab_ref, q_ref, k_ref, v_ref, qm_ref, *,
                     head_dim, rot, nq, nk, nv, v_transposed):
    hb = _rms(x_ref[...], g_ref[...]).astype(BF16)
    tm = hb.shape[0]
    tab = tab_ref[...]

    def mm(a, b):
        return jnp.dot(hb, w_ref[:, a:b], preferred_element_type=F32)

    scale = head_dim ** -0.5
    for j in range(0, nq, 2 * LANES):
        y = mm(j, j + 2 * LANES)
        for u in range(2):
            t = _rope_tile(y[:, u * LANES:(u + 1) * LANES], tab, rot // 2)
            q_ref[:, j + u * LANES:j + (u + 1) * LANES] = (t * scale).astype(BF16)
    for j in range(0, nk, 2 * LANES):
        y = mm(nq + j, nq + j + 2 * LANES)
        for u in range(2):
            t = _rope_tile(y[:, u * LANES:(u + 1) * LANES], tab, rot // 2)
            k_ref[:, j + u * LANES:j + (u + 1) * LANES] = t.astype(BF16)
    for j in range(0, nv, 2 * LANES):
        y = mm(nq + nk + j, nq + nk + j + 2 * LANES)
        if v_transposed:
            for u in range(tm // KEY_CHUNK):
                v_ref[u, j:j + 2 * LANES, :] = y[u * KEY_CHUNK:(u + 1) * KEY_CHUNK, :].T.astype(BF16)
        else:
            v_ref[:, j:j + 2 * LANES] = y.astype(BF16)
    o = nq + nk + nv
    qm_ref[...] = (mm(o, o + MEM_WIDTH) * MEM_HEAD_DIM ** -0.5).astype(BF16)


def _proj_qkv(x, g, w, tab, tm, head_dim, rot, nq, nk, nv, v_transposed):
    B, S, D = x.shape
    n = w.shape[1]
    row = lambda width: pl.BlockSpec((None, tm, width), lambda b, i: (b, i, 0))
    const = lambda shape: pl.BlockSpec(shape, lambda b, i: (0,) * len(shape))
    if v_transposed:
        v_spec = pl.BlockSpec((None, tm // KEY_CHUNK, nv, KEY_CHUNK), lambda b, i: (b, i, 0, 0))
        v_shape = jax.ShapeDtypeStruct((B, S // KEY_CHUNK, nv, KEY_CHUNK), BF16)
    else:
        v_spec, v_shape = row(nv), jax.ShapeDtypeStruct((B, S, nv), BF16)
    return pl.pallas_call(
        functools.partial(_proj_qkv_kernel, head_dim=head_dim, rot=rot, nq=nq, nk=nk, nv=nv,
                          v_transposed=v_transposed),
        grid=(B, S // tm),
        in_specs=[row(D), const((1, D)), const((D, n)), row(3 * LANES)],
        out_specs=(row(nq), row(nk), v_spec, row(MEM_WIDTH)),
        out_shape=(jax.ShapeDtypeStruct((B, S, nq), BF16), jax.ShapeDtypeStruct((B, S, nk), BF16), v_shape,
                   jax.ShapeDtypeStruct((B, S, MEM_WIDTH), BF16)),
        compiler_params=_cparams(("parallel", "parallel")),
        name="proj_qkv",
    )(x, g, w, tab)


def _prep_b_w_in(w):
    d = w.shape[0]
    nq, nkv = B_HEADS * B_HEAD_DIM, B_KV_HEADS * B_HEAD_DIM
    q, k, v, qm = w[:, :nq], w[:, nq:nq + nkv], w[:, nq + nkv:nq + 2 * nkv], w[:, nq + 2 * nkv:]
    z = jnp.zeros((d, B_HEAD_DIM), w.dtype)

    def spread(t):
        cols = []
        for h in range(B_KV_HEADS):
            th = t[:, h * B_HEAD_DIM:(h + 1) * B_HEAD_DIM]
            cols += [th, z, z, th]
        return jnp.concatenate(cols, axis=1)

    return jnp.concatenate([q, spread(k), spread(v), qm], axis=1).astype(BF16)


def _dsa_kernel(q_ref, qi_ref, wi_ref, kx_ref, ct_ref, ki_ref, wuk_ref, wuv_ref, o_ref,
                keys_ref, qext_ref, qi2_ref, wrow_ref, acc_ref, sa_ref, sb_ref, *, topk):
    kc = KEY_CHUNK
    hw = A_HEADS * LANES
    gw = 2 * LANES
    i = pl.program_id(1)
    nch = (i * BLOCK + BLOCK + kc - 1) // kc
    lane = lax.broadcasted_iota(I32, (BLOCK, LANES), 1)

    for h in range(A_HEADS):
        rows = slice(h * BLOCK, (h + 1) * BLOCK)
        qh = q_ref[:, h * LANES:(h + 1) * LANES]
        qext_ref[rows, 0:A_KV_RANK] = jnp.dot(qh, wuk_ref[h], preferred_element_type=F32).astype(BF16)
        qext_ref[rows, A_KV_RANK:A_KV_RANK + LANES] = jnp.where(lane < A_ROPE_DIM, qh.astype(F32), 0.0).astype(BF16)
        qi2_ref[rows, :] = qi_ref[:, h * LANES:(h + 1) * LANES]
    w_t = wi_ref[...].T
    for j in range(IDX_HEADS // 2):
        wrow_ref[0:1, j * LANES:(j + 1) * LANES] = w_t[2 * j:2 * j + 1, :]
        wrow_ref[1:2, j * LANES:(j + 1) * LANES] = w_t[2 * j + 1:2 * j + 2, :]

    qpos = i * BLOCK + lax.broadcasted_iota(I32, (kc, BLOCK), 1)
    krow = lax.broadcasted_iota(I32, (kc, BLOCK), 0)

    def score_chunk(c):
        off = pl.multiple_of(c * kc, kc)
        kk = ki_ref[c]
        acc = None
        for g in range(0, hw, gw):
            d = lax.dot_general(kk, qi2_ref[g:g + gw, :], _NT, preferred_element_type=F32)
            t = (jnp.maximum(d[0:kc], 0.0) * wrow_ref[0:1, g:g + gw]
                 + jnp.maximum(d[kc:2 * kc], 0.0) * wrow_ref[1:2, g:g + gw])
            for u in range(0, gw, LANES):
                acc = t[:, u:u + LANES] if acc is None else acc + t[:, u:u + LANES]
        bits = pltpu.bitcast(acc, I32)
        key = jnp.where(bits < 0, bits ^ 0x7FFFFFFF, bits)
        key = jnp.where(off + krow <= qpos, key, INT_MIN)
        keys_ref[pl.ds(off, kc), :] = key

    npairs = (nch + 1) // 2

    def score_body(j, carry):
        score_chunk(2 * j)
        score_chunk(2 * j + 1)
        return carry

    _loop_two_per_trip(0, npairs, score_body, 0)

    def bit_body(p, res):
        trial = res | lax.shift_left(jnp.int32(1), 31 - p)
        thr = trial ^ INT_MIN

        def cnt_body(c, cnt):
            off = pl.multiple_of(c * 2 * kc, 2 * kc)
            ge = (keys_ref[pl.ds(off, 2 * kc), :] >= thr).astype(I32)
            return cnt + jnp.sum(ge.reshape(2 * kc // 8, 8, BLOCK), axis=0)

        cnt = lax.fori_loop(0, npairs, cnt_body, jnp.zeros((8, BLOCK), I32))
        return jnp.where(jnp.sum(cnt, axis=0, keepdims=True) >= topk, trial, res)

    res = lax.fori_loop(0, 32, bit_body, jnp.zeros((1, BLOCK), I32))
    thr = jnp.maximum(res ^ INT_MIN, INT_MIN + 1)

    acc_ref[...] = jnp.zeros(acc_ref.shape, F32)

    def logits(c, s_ref):
        off = pl.multiple_of(c * kc, kc)
        bias = jnp.where(keys_ref[pl.ds(off, kc), :] >= thr, 0.0, NEG_INF)
        bias = jnp.concatenate([bias] * (gw // LANES), axis=1)
        kx = kx_ref[pl.ds(off, kc), :]
        for g in range(0, hw, gw):
            s_ref[:, g:g + gw] = lax.dot_general(
                kx, qext_ref[g:g + gw, :], _NT, preferred_element_type=F32) + bias

    def consume(s_ref, c, carry):
        m_prev, l_prev = carry
        ct = ct_ref[c]
        m_out, l_out = [], []
        for g in range(0, hw, gw):
            s = s_ref[:, g:g + gw]
            m_new = jnp.maximum(m_prev[:, g:g + gw], jnp.max(s, axis=0, keepdims=True))
            alpha = jnp.exp(m_prev[:, g:g + gw] - m_new)
            p = jnp.exp(s - m_new)
            l_out.append(alpha * l_prev[:, g:g + gw] + jnp.sum(p, axis=0, keepdims=True))
            m_out.append(m_new)
            acc_ref[:, g:g + gw] = alpha * acc_ref[:, g:g + gw] + jnp.dot(
                ct, p.astype(BF16), preferred_element_type=F32)
        return jnp.concatenate(m_out, axis=1), jnp.concatenate(l_out, axis=1)

    logits(0, sa_ref)

    def pair_body(j, carry):
        c0 = 2 * j
        logits(c0 + 1, sb_ref)
        carry = consume(sa_ref, c0, carry)
        logits(jnp.minimum(c0 + 2, 2 * npairs - 1), sa_ref)
        return consume(sb_ref, c0 + 1, carry)

    _, l_fin = _loop_two_per_trip(0, npairs, pair_body,
                                  (jnp.full((1, hw), NEG_INF, F32), jnp.zeros((1, hw), F32)))

    inv_l = 1.0 / l_fin
    for h in range(A_HEADS):
        cols = slice(h * LANES, (h + 1) * LANES)
        o_lat = (acc_ref[:, cols] * inv_l[:, cols]).T.astype(BF16)
        o_ref[:, cols] = jnp.dot(o_lat, wuv_ref[h], preferred_element_type=F32).astype(BF16)


def _dsa(q, qi, wi, kx, ct, ki, wuk, wuv):
    B, S, _ = q.shape
    topk = min(TOPK_MAX, S // 4)
    blk = lambda width: pl.BlockSpec((None, BLOCK, width), lambda b, i: (b, i, 0))
    seq = lambda width: pl.BlockSpec((None, S, width), lambda b, i: (b, 0, 0))
    const3 = lambda shape: pl.BlockSpec(shape, lambda b, i: (0, 0, 0))
    return pl.pallas_call(
        functools.partial(_dsa_kernel, topk=topk),
        grid=(B, S // BLOCK),
        in_specs=[blk(A_HEADS * A_QK_DIM), blk(IDX_HEADS * IDX_DIM), blk(LANES),
                  seq(A_KV_RANK + LANES),
                  pl.BlockSpec((None, S // KEY_CHUNK, A_KV_RANK, KEY_CHUNK), lambda b, i: (b, 0, 0, 0)),
                  pl.BlockSpec((None, S // KEY_CHUNK, 2 * KEY_CHUNK, LANES), lambda b, i: (b, 0, 0, 0)),
                  const3(wuk.shape), const3(wuv.shape)],
        out_specs=blk(A_HEADS * LANES),
        out_shape=jax.ShapeDtypeStruct((B, S, A_HEADS * LANES), BF16),
        scratch_shapes=[
            pltpu.VMEM((S, BLOCK), I32),
            pltpu.VMEM((A_HEADS * BLOCK, A_KV_RANK + LANES), BF16),
            pltpu.VMEM((A_HEADS * BLOCK, LANES), BF16),
            pltpu.VMEM((8, A_HEADS * LANES), F32),
            pltpu.VMEM((A_KV_RANK, A_HEADS * LANES), F32),
            pltpu.VMEM((KEY_CHUNK, A_HEADS * LANES), F32),
            pltpu.VMEM((KEY_CHUNK, A_HEADS * LANES), F32),
        ],
        compiler_params=_cparams(("parallel", "arbitrary")),
        name="dsa",
    )(q, qi, wi, kx, ct, ki, wuk, wuv)


def _window_bias(branches, nrel):
    par = jnp.arange(2, dtype=I32)[:, None, None, None]
    rel = jnp.arange(nrel + 1, dtype=I32)[None, :, None, None]
    krow = jnp.arange(KEY_CHUNK, dtype=I32)[None, None, :, None]
    qcol = jnp.arange(BLOCK, dtype=I32)[None, None, None, :]
    dist = BLOCK * par + KEY_CHUNK * (nrel - 1 - rel) + qcol - krow
    mult = sum(((dist >= 0) & (dist <= window) & (dist % dil == 0)).astype(F32) for window, dil in branches)
    return jnp.where(mult > 0, jnp.log(jnp.maximum(mult, 1.0)), NEG_INF)


def _wattn_kernel(q_ref, k_ref, vt_ref, bias_ref, o_ref, acc_ref, sa_ref, sb_ref, p_ref, *, nh, nrel):
    kc = KEY_CHUNK
    last = k_ref.shape[0] // kc - 1
    i = pl.program_id(1)
    par, top = i % 2, i // 2
    c_lo = jnp.maximum(top - (nrel - 1), 0)
    npairs = (top + 2 - c_lo) // 2

    acc_ref[...] = jnp.zeros(acc_ref.shape, F32)

    def logits(c, s_ref):
        off = pl.multiple_of(jnp.minimum(c, last) * kc, kc)
        bias = bias_ref[par, c - top + (nrel - 1)]
        for h in range(nh):
            cols = slice(h * LANES, (h + 1) * LANES)
            s_ref[:, cols] = lax.dot_general(k_ref[pl.ds(off, kc), cols], q_ref[:, cols], _NT,
                                             preferred_element_type=F32) + bias

    def consume(s_ref, c, carry):
        m_prev, l_prev = carry
        s = s_ref[...]
        m_new = jnp.maximum(m_prev, jnp.max(s, axis=0, keepdims=True))
        alpha = jnp.exp(m_prev - m_new)
        p = jnp.exp(s - m_new)
        l_new = alpha * l_prev + jnp.sum(p, axis=0, keepdims=True)
        p_ref[...] = p.astype(BF16)
        cd = jnp.minimum(c, last)
        for h in range(nh):
            cols = slice(h * LANES, (h + 1) * LANES)
            acc_ref[:, cols] = alpha[:, cols] * acc_ref[:, cols] + jnp.dot(
                vt_ref[cd, cols, :], p_ref[:, cols], preferred_element_type=F32)
        return m_new, l_new

    logits(c_lo, sa_ref)

    def pair_body(j, carry):
        c0 = c_lo + 2 * j
        logits(c0 + 1, sb_ref)
        carry = consume(sa_ref, c0, carry)
        logits(jnp.minimum(c0 + 2, c_lo + 2 * npairs - 1), sa_ref)
        return consume(sb_ref, c0 + 1, carry)

    _, l_fin = _loop_two_per_trip(0, npairs, pair_body,
                                  (jnp.full((1, nh * LANES), NEG_INF, F32), jnp.zeros((1, nh * LANES), F32)))

    inv_l = 1.0 / l_fin
    for h in range(nh):
        cols = slice(h * LANES, (h + 1) * LANES)
        o_ref[:, cols] = (acc_ref[:, cols] * inv_l[:, cols]).T.astype(BF16)


def _swa_kernel(sink_ref, q_ref, kp_ref, kc_ref, vp_ref, vc_ref, o_ref, *, subheads, max_dist):
    n = pl.program_id(1)
    qi = lax.broadcasted_iota(I32, (BLOCK, 2 * BLOCK), 0)
    kj = lax.broadcasted_iota(I32, (BLOCK, 2 * BLOCK), 1)
    dist = BLOCK + qi - kj
    mask = (dist >= 0) & (dist <= max_dist) & ((kj >= BLOCK) | (n > 0))
    for j, subs in enumerate(subheads):
        qt = q_ref[:, j * LANES:(j + 1) * LANES]
        o_t = None
        for kt, vt, hidx in subs:
            kk = jnp.concatenate([kp_ref[:, kt * LANES:(kt + 1) * LANES],
                                  kc_ref[:, kt * LANES:(kt + 1) * LANES]], axis=0)
            vv = jnp.concatenate([vp_ref[:, vt * LANES:(vt + 1) * LANES],
                                  vc_ref[:, vt * LANES:(vt + 1) * LANES]], axis=0)
            s = lax.dot_general(qt, kk, _NT, preferred_element_type=F32)
            s = jnp.where(mask, s, NEG_INF)
            sk = sink_ref[hidx]
            m = jnp.maximum(jnp.max(s, axis=-1, keepdims=True), sk)
            p = jnp.exp(s - m)
            l = jnp.sum(p, axis=-1, keepdims=True) + jnp.exp(sk - m)
            o_s = jnp.dot(p.astype(BF16), vv, preferred_element_type=F32) / l
            o_t = o_s if o_t is None else o_t + o_s
        o_ref[:, j * LANES:(j + 1) * LANES] = o_t.astype(BF16)


def _swa(q, k, v, sinks, *, subheads, max_dist):
    B, S, wq = q.shape
    wk, wv = k.shape[-1], v.shape[-1]
    cur = lambda width: pl.BlockSpec((None, BLOCK, width), lambda b, i: (b, i, 0))
    prv = lambda width: pl.BlockSpec((None, BLOCK, width), lambda b, i: (b, jnp.maximum(i - 1, 0), 0))
    return pl.pallas_call(
        functools.partial(_swa_kernel, subheads=subheads, max_dist=max_dist),
        grid=(B, S // BLOCK),
        in_specs=[pl.BlockSpec(memory_space=pltpu.SMEM), cur(wq), prv(wk), cur(wk), prv(wv), cur(wv)],
        out_specs=cur(wq),
        out_shape=jax.ShapeDtypeStruct((B, S, wq), BF16),
        compiler_params=_cparams(("parallel", "arbitrary")),
        name="swa",
    )(sinks, q, k, k, v, v)


def _wattn(q, k, vt, branches):
    B, S, wq = q.shape
    widest = max(w for w, _ in branches)
    nrel = min(-(-widest // KEY_CHUNK) + 1, S // KEY_CHUNK)
    bias = _window_bias(branches, nrel)
    nh = wq // LANES
    blk = lambda width: pl.BlockSpec((None, BLOCK, width), lambda b, i: (b, i, 0))
    return pl.pallas_call(
        functools.partial(_wattn_kernel, nh=nh, nrel=nrel),
        grid=(B, S // BLOCK),
        in_specs=[blk(wq),
                  pl.BlockSpec((None, S, wq), lambda b, i: (b, 0, 0)),
                  pl.BlockSpec((None,) + vt.shape[1:], lambda b, i: (b, 0, 0, 0)),
                  pl.BlockSpec(bias.shape, lambda b, i: (0, 0, 0, 0))],
        out_specs=blk(wq),
        out_shape=jax.ShapeDtypeStruct((B, S, wq), BF16),
        scratch_shapes=[pltpu.VMEM((LANES, nh * LANES), F32),
                        pltpu.VMEM((KEY_CHUNK, nh * LANES), F32),
                        pltpu.VMEM((KEY_CHUNK, nh * LANES), F32),
                        pltpu.VMEM((KEY_CHUNK, nh * LANES), BF16)],
        compiler_params=_cparams(("parallel", "arbitrary")),
        name="wattn",
    )(q, k, vt, bias)


def _memkv_kernel(mem_ref, g_ref, w_ref, k_ref, v_ref):
    hb = _rms(mem_ref[...], g_ref[...]).astype(BF16)
    y = jnp.dot(hb, w_ref[...], preferred_element_type=F32)
    lane = lax.broadcasted_iota(I32, (y.shape[0], LANES), 1)
    for t in range(MEM_WIDTH // LANES):
        for out_ref, base in ((k_ref, 0), (v_ref, MEM_WIDTH)):
            tile = y[:, base + t * LANES:base + (t + 1) * LANES]
            out_ref[:, (2 * t) * LANES:(2 * t + 1) * LANES] = jnp.where(lane < MEM_HEAD_DIM, tile, 0.0).astype(BF16)
            out_ref[:, (2 * t + 1) * LANES:(2 * t + 2) * LANES] = jnp.where(lane >= MEM_HEAD_DIM, tile, 0.0).astype(BF16)


def _memkv(mem, g_mem, w):
    B, M, D = mem.shape
    L = w.shape[0]
    out = jax.ShapeDtypeStruct((L, B, M, 2 * MEM_WIDTH), BF16)
    ospec = pl.BlockSpec((None, None, M, 2 * MEM_WIDTH), lambda l, b: (l, b, 0, 0))
    return pl.pallas_call(
        _memkv_kernel,
        grid=(L, B),
        in_specs=[pl.BlockSpec((None, M, D), lambda l, b: (b, 0, 0)),
                  pl.BlockSpec((1, D), lambda l, b: (0, 0)),
                  pl.BlockSpec((None, D, 2 * MEM_WIDTH), lambda l, b: (l, 0, 0))],
        out_specs=(ospec, ospec),
        out_shape=(out, out),
        compiler_params=_cparams(("parallel", "parallel")),
        name="memkv",
    )(mem, g_mem, w)


def _out_kernel(x_ref, mix_ref, qm_ref, mk_ref, mv_ref, w_ref, o_ref):
    mw = mix_ref.shape[-1]
    y = x_ref[...] + jnp.dot(mix_ref[...], w_ref[0:mw, :], preferred_element_type=F32)
    for t in range(MEM_WIDTH // LANES):
        qt = qm_ref[:, t * LANES:(t + 1) * LANES]
        o_t = None
        for u in range(2):
            sl = slice((2 * t + u) * LANES, (2 * t + u + 1) * LANES)
            s = lax.dot_general(qt, mk_ref[:, sl], _NT, preferred_element_type=F32)
            p = jnp.exp(s - jnp.max(s, axis=-1, keepdims=True))
            l = jnp.sum(p, axis=-1, keepdims=True)
            o_s = jnp.dot(p.astype(BF16), mv_ref[:, sl], preferred_element_type=F32) / l
            o_t = o_s if o_t is None else o_t + o_s
        y = y + jnp.dot(o_t.astype(BF16), w_ref[mw + t * LANES:mw + (t + 1) * LANES, :],
                        preferred_element_type=F32)
    o_ref[...] = y


def _out_proj(x, mix, qm, mk, mv, w, tm):
    B, S, D = x.shape
    M = mk.shape[1]
    row = lambda width: pl.BlockSpec((None, tm, width), lambda b, i: (b, i, 0))
    mem = pl.BlockSpec((None, M, 2 * MEM_WIDTH), lambda b, i: (b, 0, 0))
    return pl.pallas_call(
        _out_kernel,
        grid=(B, S // tm),
        in_specs=[row(D), row(mix.shape[-1]), row(MEM_WIDTH), mem, mem,
                  pl.BlockSpec(w.shape, lambda b, i: (0, 0))],
        out_specs=row(D),
        out_shape=jax.ShapeDtypeStruct((B, S, D), F32),
        compiler_params=_cparams(("parallel", "parallel")),
        name="out_proj",
    )(x, mix, qm, mk, mv, w)


def _ffn_kernel(x_ref, g_ref, wup_ref, cw_ref, cb_ref, wdn_ref, gf_ref, o_ref, carry_ref, *, tm, cw, final):
    dff = wdn_ref.shape[0]
    first = pl.program_id(1) == 0
    x = x_ref[...]
    hb = _rms(x, g_ref[...]).astype(BF16)
    row = lax.broadcasted_iota(I32, (tm, cw), 0)
    acc = x

    def up(c0):
        return (jnp.dot(hb, wup_ref[:, c0:c0 + cw], preferred_element_type=F32),
                jnp.dot(hb, wup_ref[:, dff + c0:dff + c0 + cw], preferred_element_type=F32))

    nxt = up(0)
    for c0 in range(0, dff, cw):
        a, b = nxt
        if c0 + cw < dff:
            nxt = up(c0 + cw)
        prev = jnp.where(first, 0.0, carry_ref[:, c0:c0 + cw])
        p1, p2 = prev[7:8, :], prev[6:7, :]
        a1 = jnp.where(row == 0, p1, pltpu.roll(a, 1, 0))
        a2 = jnp.where(row == 0, p2, jnp.where(row == 1, p1, pltpu.roll(a, 2, 0)))
        carry_ref[:, c0:c0 + cw] = a[tm - 8:tm, :]
        w = cw_ref[:, c0:c0 + cw]
        conv = w[0:1, :] * a2 + w[1:2, :] * a1 + w[2:3, :] * a + cb_ref[:, c0:c0 + cw]
        gated = conv / (1.0 + jnp.exp(-conv)) * b
        acc = acc + jnp.dot(gated.astype(BF16), wdn_ref[c0:c0 + cw, :], preferred_element_type=F32)
    if final:
        acc = _rms(acc, gf_ref[...])
    o_ref[...] = acc


def _ffn(x, g, wup, cw, cb, wdn, gf, tm, final):
    B, S, D = x.shape
    dff = wdn.shape[0]
    row = pl.BlockSpec((None, tm, D), lambda b, i: (b, i, 0))
    const = lambda shape: pl.BlockSpec(shape, lambda b, i: (0, 0), pipeline_mode=pl.Buffered(1))
    return pl.pallas_call(
        functools.partial(_ffn_kernel, tm=tm, cw=2 * LANES, final=final),
        grid=(B, S // tm),
        in_specs=[row, const((1, D)), const(wup.shape), const(cw.shape), const((1, dff)), const(wdn.shape),
                  const((1, D))],
        out_specs=row,
        out_shape=jax.ShapeDtypeStruct((B, S, D), F32),
        scratch_shapes=[pltpu.VMEM((8, dff), F32)],
        compiler_params=_cparams(("arbitrary", "arbitrary")),
        name="ffn",
    )(x, g, wup, cw, cb, wdn, gf)


def _b_subheads():
    return tuple(((2 * (j // 2), 2 * (j // 2), 2 * j), (2 * (j // 2) + 1, 2 * (j // 2) + 1, 2 * j + 1))
                 for j in range(B_HEADS // 2))


def kernel(x, mem, positions, g_mix, g_ffn, g_mem, g_final, w_mem_kv, a_w_in, a_kv_norm, a_w_uk, a_w_uv, a_w_out,
           b_w_in, b_sinks, b_w_out, c_w_in, c_w_out, f_w_up, f_conv_w, f_conv_b, f_w_down):
    B, S, D = x.shape
    depth = g_mix.shape[0]
    tm = min(512, S)
    tab32 = _rope_table(positions, A_QK_DIM, A_ROPE_DIM)
    tab16 = _rope_table(positions, B_HEAD_DIM, B_HEAD_DIM // 4)
    mem_k, mem_v = _memkv(mem, g_mem.reshape(1, D), w_mem_kv.astype(BF16))
    conv_w = jnp.pad(f_conv_w, ((0, 0), (0, 8 - CONV_WIDTH), (0, 0)))
    for i in range(depth):
        kind, j = i % 3, i // 3
        g = g_mix[i].reshape(1, D)
        if kind == 0:
            q, kx, ct, qi, ki, wi, qm = _proj_a(x, g, _prep_a_w_in(a_w_in[j]), a_kv_norm[j].reshape(1, -1),
                                                tab32, tab16, tm)
            wuk = jnp.pad(jnp.transpose(a_w_uk[j], (1, 2, 0)), ((0, 0), (A_ROPE_DIM, 0), (0, 0))).astype(BF16)
            wuv = jnp.transpose(a_w_uv[j], (1, 0, 2)).astype(BF16)
            mix = _dsa(q, qi, wi, kx, ct, ki, wuk, wuv)
            w_out = a_w_out[j]
        elif kind == 1:
            nq, nkv = B_HEADS * B_HEAD_DIM, 4 * B_KV_HEADS * B_HEAD_DIM
            q, k, v, qm = _proj_qkv(x, g, _prep_b_w_in(b_w_in[j]), tab16, tm, B_HEAD_DIM, B_HEAD_DIM // 4,
                                    nq, nkv, nkv, False)
            mix = _swa(q, k, v, b_sinks[j], subheads=_b_subheads(), max_dist=B_WINDOW - 1)
            w_out = b_w_out[j]
        else:
            nq = C_HEADS * C_HEAD_DIM
            q, k, v, qm = _proj_qkv(x, g, c_w_in[j].astype(BF16), tab32, tm, C_HEAD_DIM, C_HEAD_DIM // 4,
                                    nq, nq, nq, True)
            mix = _wattn(q, k, v, C_BRANCHES)
            w_out = c_w_out[j]
        x = _out_proj(x, mix, qm, mem_k[i], mem_v[i], w_out.astype(BF16), tm)
        x = _ffn(x, g_ffn[i].reshape(1, D), f_w_up[i].astype(BF16), conv_w[i], f_conv_b[i].reshape(1, -1),
                 f_w_down[i].astype(BF16), g_final.reshape(1, D), min(256, S), i == depth - 1)
    return x
```

```python
import functools

import jax
import jax.numpy as jnp
import numpy as np
from jax import lax
from jax.experimental import pallas as pl
from jax.experimental.pallas import tpu as pltpu

F32 = jnp.float32
BF16 = jnp.bfloat16
I32 = jnp.int32

LANES = 128
BLOCK = 128
ROPE_THETA = 500000.0
EPS = 1e-6
NEG_INF = -1e30
INT_MIN = -(2**31)

A_HEADS = 8
A_QK_DIM = 128
A_ROPE_DIM = 32
A_KV_RANK = 256
IDX_HEADS = 16
IDX_DIM = 64
TOPK_MAX = 256
B_HEADS = 16
B_KV_HEADS = 4
B_HEAD_DIM = 64
B_WINDOW = 128
C_HEADS = 8
C_HEAD_DIM = 128
C_BRANCHES = ((128, 1), (512, 4), (2048, 16))
MEM_HEADS = 4
MEM_HEAD_DIM = 64
MEM_WIDTH = MEM_HEADS * MEM_HEAD_DIM
CONV_WIDTH = 3
KEY_CHUNK = 256

VMEM_LIMIT = 56 * 1024 * 1024

_NT = (((1,), (1,)), ((), ()))


def _cparams(sem):
    return pltpu.CompilerParams(dimension_semantics=sem, vmem_limit_bytes=VMEM_LIMIT)


def _rms(x, g):
    return x * lax.rsqrt(jnp.mean(x * x, axis=-1, keepdims=True) + EPS) * g


def _loop_two_per_trip(lo, hi, body, carry):
    trips = (hi - lo) // 2

    def double(t, c):
        return body(lo + 2 * t + 1, body(lo + 2 * t, c))

    carry = lax.fori_loop(0, trips, double, carry)
    return lax.fori_loop(lo + 2 * trips, hi, body, carry)


def _bit_transpose32(words):
    a = list(words)
    j, m = 16, 0x0000FFFF
    while j:
        mask = jnp.int32(m - (1 << 32) if m >= (1 << 31) else m)
        for k in range(32):
            if k & j == 0:
                t = (a[k] ^ lax.shift_right_logical(a[k + j], j)) & mask
                a[k] = a[k] ^ t
                a[k + j] = a[k + j] ^ lax.shift_left(t, j)
        j >>= 1
        m = (m ^ (m << j)) & 0xFFFFFFFF
    return a


def _rope_tile(t, tab, half):
    c, sa, sb = tab[:, 0:LANES], tab[:, LANES:2 * LANES], tab[:, 2 * LANES:3 * LANES]
    return t * c + pltpu.roll(t, half, 1) * sa + pltpu.roll(t, LANES - half, 1) * sb


def _rope_table(positions, head_dim, rot):
    half = rot // 2
    inv = ROPE_THETA ** (-jnp.arange(0, rot, 2, dtype=F32) / rot)
    ang = positions.astype(F32)[..., None] * inv
    cs = jnp.concatenate([jnp.cos(ang), jnp.sin(ang)], axis=-1)
    sel = np.zeros((rot, 3 * LANES), np.float32)
    one = np.zeros((3 * LANES,), np.float32)
    for l in range(LANES):
        j = l % head_dim
        if j < half:
            sel[j, l] = 1.0
            sel[half + j, 2 * LANES + l] = -1.0
        elif j < rot:
            sel[j - half, l] = 1.0
            sel[j, LANES + l] = 1.0
        else:
            one[l] = 1.0
    return jnp.dot(cs, jnp.asarray(sel), precision=lax.Precision.HIGHEST) + jnp.asarray(one)


def _proj_a_kernel(x_ref, g_ref, w_ref, kvn_ref, t32_ref, t16_ref,
                   q_ref, kx_ref, ct_ref, qi_ref, ki_ref, wi_ref, qm_ref, *, tm):
    hb = _rms(x_ref[...], g_ref[...]).astype(BF16)
    t32 = t32_ref[...]
    t16 = t16_ref[...]

    def mm(a, b):
        return jnp.dot(hb, w_ref[:, a:b], preferred_element_type=F32)

    scale = A_QK_DIM ** -0.5
    for j in range(0, A_HEADS * A_QK_DIM, 2 * LANES):
        y = mm(j, j + 2 * LANES)
        for u in range(2):
            t = _rope_tile(y[:, u * LANES:(u + 1) * LANES], t32, A_ROPE_DIM // 2)
            q_ref[:, j + u * LANES:j + (u + 1) * LANES] = (t * scale).astype(BF16)
    o = A_HEADS * A_QK_DIM
    c = _rms(mm(o, o + A_KV_RANK), kvn_ref[...])
    kx_ref[:, 0:A_KV_RANK] = c.astype(BF16)
    for u in range(tm // KEY_CHUNK):
        ct_ref[u] = c[u * KEY_CHUNK:(u + 1) * KEY_CHUNK, :].T.astype(BF16)
    o += A_KV_RANK
    kx_ref[:, A_KV_RANK:A_KV_RANK + LANES] = _rope_tile(mm(o, o + LANES), t32, A_ROPE_DIM // 2).astype(BF16)
    o += LANES
    for j in range(0, IDX_HEADS * IDX_DIM, 2 * LANES):
        y = mm(o + j, o + j + 2 * LANES)
        for u in range(2):
            t = _rope_tile(y[:, u * LANES:(u + 1) * LANES], t16, IDX_DIM // 8)
            qi_ref[:, j + u * LANES:j + (u + 1) * LANES] = t.astype(BF16)
    o += IDX_HEADS * IDX_DIM
    y = mm(o, o + 2 * LANES)
    for u in range(2):
        t = _rope_tile(y[:, u * LANES:(u + 1) * LANES], t16, IDX_DIM // 8).astype(BF16)
        for v in range(tm // KEY_CHUNK):
            ki_ref[v, u * KEY_CHUNK:(u + 1) * KEY_CHUNK, :] = t[v * KEY_CHUNK:(v + 1) * KEY_CHUNK, :]
    o += 2 * LANES
    wi_ref[...] = mm(o, o + LANES) * (IDX_HEADS * IDX_DIM) ** -0.5
    o += LANES
    qm_ref[...] = (mm(o, o + MEM_WIDTH) * MEM_HEAD_DIM ** -0.5).astype(BF16)


def _prep_a_w_in(w):
    d = w.shape[0]
    sizes = (A_HEADS * A_QK_DIM, A_KV_RANK, A_ROPE_DIM, IDX_HEADS * IDX_DIM, IDX_DIM, IDX_HEADS, MEM_WIDTH)
    offs = [0]
    for s in sizes:
        offs.append(offs[-1] + s)
    q, ckv, kr, qi, ki, wi, qm = [w[:, offs[i]:offs[i + 1]] for i in range(len(sizes))]
    z = lambda n: jnp.zeros((d, n), w.dtype)
    return jnp.concatenate([
        q, ckv, kr, z(LANES - A_ROPE_DIM), qi,
        ki, z(LANES - IDX_DIM), z(LANES - IDX_DIM), ki,
        wi, z(LANES - IDX_HEADS), qm], axis=1).astype(BF16)


def _proj_a(x, g, w, kvn, t32, t16, tm):
    B, S, D = x.shape
    n = w.shape[1]
    row = lambda width: pl.BlockSpec((None, tm, width), lambda b, i: (b, i, 0))
    const = lambda shape: pl.BlockSpec(shape, lambda b, i: (0,) * len(shape))
    out_shape = (
        jax.ShapeDtypeStruct((B, S, A_HEADS * A_QK_DIM), BF16),
        jax.ShapeDtypeStruct((B, S, A_KV_RANK + LANES), BF16),
        jax.ShapeDtypeStruct((B, S // KEY_CHUNK, A_KV_RANK, KEY_CHUNK), BF16),
        jax.ShapeDtypeStruct((B, S, IDX_HEADS * IDX_DIM), BF16),
        jax.ShapeDtypeStruct((B, S // KEY_CHUNK, 2 * KEY_CHUNK, LANES), BF16),
        jax.ShapeDtypeStruct((B, S, LANES), F32),
        jax.ShapeDtypeStruct((B, S, MEM_WIDTH), BF16),
    )
    out_specs = (
        row(A_HEADS * A_QK_DIM), row(A_KV_RANK + LANES),
        pl.BlockSpec((None, tm // KEY_CHUNK, A_KV_RANK, KEY_CHUNK), lambda b, i: (b, i, 0, 0)),
        row(IDX_HEADS * IDX_DIM),
        pl.BlockSpec((None, tm // KEY_CHUNK, 2 * KEY_CHUNK, LANES), lambda b, i: (b, i, 0, 0)),
        row(LANES), row(MEM_WIDTH),
    )
    return pl.pallas_call(
        functools.partial(_proj_a_kernel, tm=tm),
        grid=(B, S // tm),
        in_specs=[row(D), const((1, D)), const((D, n)), const((1, A_KV_RANK)), row(3 * LANES), row(3 * LANES)],
        out_specs=out_specs,
        out_shape=out_shape,
        compiler_params=_cparams(("parallel", "parallel")),
        name="proj_a",
    )(x, g, w, kvn, t32, t16)


def _proj_qkv_kernel(x_ref, g_ref, w_ref, tab_ref, q_ref, k_ref, v_ref, qm_ref, *,
                     head_dim, rot, nq, nk, nv, v_transposed):
    hb = _rms(x_ref[...], g_ref[...]).astype(BF16)
    tm = hb.shape[0]
    tab = tab_ref[...]

    def mm(a, b):
        return jnp.dot(hb, w_ref[:, a:b], preferred_element_type=F32)

    scale = head_dim ** -0.5
    for j in range(0, nq, 2 * LANES):
        y = mm(j, j + 2 * LANES)
        for u in range(2):
            t = _rope_tile(y[:, u * LANES:(u + 1) * LANES], tab, rot // 2)
            q_ref[:, j + u * LANES:j + (u + 1) * LANES] = (t * scale).astype(BF16)
    for j in range(0, nk, 2 * LANES):
        y = mm(nq + j, nq + j + 2 * LANES)
        for u in range(2):
            t = _rope_tile(y[:, u * LANES:(u + 1) * LANES], tab, rot // 2)
            k_ref[:, j + u * LANES:j + (u + 1) * LANES] = t.astype(BF16)
    for j in range(0, nv, 2 * LANES):
        y = mm(nq + nk + j, nq + nk + j + 2 * LANES)
        if v_transposed:
            for u in range(tm // KEY_CHUNK):
                v_ref[u, j:j + 2 * LANES, :] = y[u * KEY_CHUNK:(u + 1) * KEY_CHUNK, :].T.astype(BF16)
        else:
            v_ref[:, j:j + 2 * LANES] = y.astype(BF16)
    o = nq + nk + nv
    qm_ref[...] = (mm(o, o + MEM_WIDTH) * MEM_HEAD_DIM ** -0.5).astype(BF16)


def _proj_qkv(x, g, w, tab, tm, head_dim, rot, nq, nk, nv, v_transposed):
    B, S, D = x.shape
    n = w.shape[1]
    row = lambda width: pl.BlockSpec((None, tm, width), lambda b, i: (b, i, 0))
    const = lambda shape: pl.BlockSpec(shape, lambda b, i: (0,) * len(shape))
    if v_transposed:
        v_spec = pl.BlockSpec((None, tm // KEY_CHUNK, nv, KEY_CHUNK), lambda b, i: (b, i, 0, 0))
        v_shape = jax.ShapeDtypeStruct((B, S // KEY_CHUNK, nv, KEY_CHUNK), BF16)
    else:
        v_spec, v_shape = row(nv), jax.ShapeDtypeStruct((B, S, nv), BF16)
    return pl.pallas_call(
        functools.partial(_proj_qkv_kernel, head_dim=head_dim, rot=rot, nq=nq, nk=nk, nv=nv,
                          v_transposed=v_transposed),
        grid=(B, S // tm),
        in_specs=[row(D), const((1, D)), const((D, n)), row(3 * LANES)],
        out_specs=(row(nq), row(nk), v_spec, row(MEM_WIDTH)),
        out_shape=(jax.ShapeDtypeStruct((B, S, nq), BF16), jax.ShapeDtypeStruct((B, S, nk), BF16), v_shape,
                   jax.ShapeDtypeStruct((B, S, MEM_WIDTH), BF16)),
        compiler_params=_cparams(("parallel", "parallel")),
        name="proj_qkv",
    )(x, g, w, tab)


def _prep_b_w_in(w):
    d = w.shape[0]
    nq, nkv = B_HEADS * B_HEAD_DIM, B_KV_HEADS * B_HEAD_DIM
    q, k, v, qm = w[:, :nq], w[:, nq:nq + nkv], w[:, nq + nkv:nq + 2 * nkv], w[:, nq + 2 * nkv:]
    z = jnp.zeros((d, B_HEAD_DIM), w.dtype)

    def spread(t):
        cols = []
        for h in range(B_KV_HEADS):
            th = t[:, h * B_HEAD_DIM:(h + 1) * B_HEAD_DIM]
            cols += [th, z, z, th]
        return jnp.concatenate(cols, axis=1)

    return jnp.concatenate([q, spread(k), spread(v), qm], axis=1).astype(BF16)


def _dsa_kernel(q_ref, qi_ref, wi_ref, kx_ref, ct_ref, ki_ref, wuk_ref, wuv_ref, o_ref,
                keys_ref, planes_ref, qext_ref, qi2_ref, wrow_ref, acc_ref, sa_ref, sb_ref, *, topk):
    kc = KEY_CHUNK
    hw = A_HEADS * LANES
    gw = 2 * LANES
    i = pl.program_id(1)
    nch = (i * BLOCK + BLOCK + kc - 1) // kc
    lane = lax.broadcasted_iota(I32, (BLOCK, LANES), 1)

    for h in range(A_HEADS):
        rows = slice(h * BLOCK, (h + 1) * BLOCK)
        qh = q_ref[:, h * LANES:(h + 1) * LANES]
        qext_ref[rows, 0:A_KV_RANK] = jnp.dot(qh, wuk_ref[h], preferred_element_type=F32).astype(BF16)
        qext_ref[rows, A_KV_RANK:A_KV_RANK + LANES] = jnp.where(lane < A_ROPE_DIM, qh.astype(F32), 0.0).astype(BF16)
        qi2_ref[rows, :] = qi_ref[:, h * LANES:(h + 1) * LANES]
    w_t = wi_ref[...].T
    for j in range(IDX_HEADS // 2):
        wrow_ref[0:1, j * LANES:(j + 1) * LANES] = w_t[2 * j:2 * j + 1, :]
        wrow_ref[1:2, j * LANES:(j + 1) * LANES] = w_t[2 * j + 1:2 * j + 2, :]

    qpos = i * BLOCK + lax.broadcasted_iota(I32, (kc, BLOCK), 1)
    krow = lax.broadcasted_iota(I32, (kc, BLOCK), 0)

    def score_chunk(c):
        off = pl.multiple_of(c * kc, kc)
        kk = ki_ref[c]
        acc = None
        for g in range(0, hw, gw):
            d = lax.dot_general(kk, qi2_ref[g:g + gw, :], _NT, preferred_element_type=F32)
            t = (jnp.maximum(d[0:kc], 0.0) * wrow_ref[0:1, g:g + gw]
                 + jnp.maximum(d[kc:2 * kc], 0.0) * wrow_ref[1:2, g:g + gw])
            for u in range(0, gw, LANES):
                acc = t[:, u:u + LANES] if acc is None else acc + t[:, u:u + LANES]
        bits = pltpu.bitcast(acc, I32)
        key = jnp.where(bits < 0, bits ^ 0x7FFFFFFF, bits)
        key = jnp.where(off + krow <= qpos, key, INT_MIN)
        keys_ref[pl.ds(off, kc), :] = key
        planes = _bit_transpose32([(key[8 * r:8 * r + 8, :] ^ INT_MIN) for r in range(kc // 8)])
        for p in range(32):
            planes_ref[c, p] = planes[p]

    npairs = (nch + 1) // 2

    @pl.when(i == 0)
    def _():
        planes_ref[...] = jnp.zeros(planes_ref.shape, I32)

    def score_body(j, carry):
        score_chunk(2 * j)
        score_chunk(2 * j + 1)
        return carry

    _loop_two_per_trip(0, npairs, score_body, 0)

    n_chunks = planes_ref.shape[0]

    def bit_body(p, carry):
        alive, need, res = carry
        ones = [a & planes_ref[c, p] for c, a in enumerate(alive)]
        cnt = ones[0] * 0
        for o in ones:
            cnt = cnt + lax.population_count(o)
        cnt = jnp.sum(cnt, axis=0, keepdims=True)
        take = cnt >= need
        alive = [jnp.where(take, o, a ^ o) for o, a in zip(ones, alive)]
        need = jnp.where(take, need, need - cnt)
        res = jnp.where(take, res | lax.shift_left(jnp.int32(1), 31 - p), res)
        return alive, need, res

    alive0 = [jnp.where(c < 2 * npairs, jnp.full((8, BLOCK), -1, I32), 0) for c in range(n_chunks)]
    _, _, res = lax.fori_loop(0, 32, bit_body,
                              (alive0, jnp.full((1, BLOCK), topk, I32), jnp.zeros((1, BLOCK), I32)))
    thr = jnp.maximum(res ^ INT_MIN, INT_MIN + 1)

    acc_ref[...] = jnp.zeros(acc_ref.shape, F32)

    def logits(c, s_ref):
        off = pl.multiple_of(c * kc, kc)
        bias = jnp.where(keys_ref[pl.ds(off, kc), :] >= thr, 0.0, NEG_INF)
        bias = jnp.concatenate([bias] * (gw // LANES), axis=1)
        kx = kx_ref[pl.ds(off, kc), :]
        for g in range(0, hw, gw):
            s_ref[:, g:g + gw] = lax.dot_general(
                kx, qext_ref[g:g + gw, :], _NT, preferred_element_type=F32) + bias

    def consume(s_ref, c, carry):
        m_prev, l_prev = carry
        ct = ct_ref[c]
        m_out, l_out = [], []
        for g in range(0, hw, gw):
            s = s_ref[:, g:g + gw]
            m_new = jnp.maximum(m_prev[:, g:g + gw], jnp.max(s, axis=0, keepdims=True))
            alpha = jnp.exp(m_prev[:, g:g + gw] - m_new)
            p = jnp.exp(s - m_new)
            l_out.append(alpha * l_prev[:, g:g + gw] + jnp.sum(p, axis=0, keepdims=True))
            m_out.append(m_new)
            acc_ref[:, g:g + gw] = alpha * acc_ref[:, g:g + gw] + jnp.dot(
                ct, p.astype(BF16), preferred_element_type=F32)
        return jnp.concatenate(m_out, axis=1), jnp.concatenate(l_out, axis=1)

    logits(0, sa_ref)

    def pair_body(j, carry):
        c0 = 2 * j
        logits(c0 + 1, sb_ref)
        carry = consume(sa_ref, c0, carry)
        logits(jnp.minimum(c0 + 2, 2 * npairs - 1), sa_ref)
        return consume(sb_ref, c0 + 1, carry)

    _, l_fin = _loop_two_per_trip(0, npairs, pair_body,
                                  (jnp.full((1, hw), NEG_INF, F32), jnp.zeros((1, hw), F32)))

    inv_l = 1.0 / l_fin
    for h in range(A_HEADS):
        cols = slice(h * LANES, (h + 1) * LANES)
        o_lat = (acc_ref[:, cols] * inv_l[:, cols]).T.astype(BF16)
        o_ref[:, cols] = jnp.dot(o_lat, wuv_ref[h], preferred_element_type=F32).astype(BF16)


def _dsa(q, qi, wi, kx, ct, ki, wuk, wuv):
    B, S, _ = q.shape
    topk = min(TOPK_MAX, S // 4)
    blk = lambda width: pl.BlockSpec((None, BLOCK, width), lambda b, i: (b, i, 0))
    seq = lambda width: pl.BlockSpec((None, S, width), lambda b, i: (b, 0, 0))
    const3 = lambda shape: pl.BlockSpec(shape, lambda b, i: (0, 0, 0))
    return pl.pallas_call(
        functools.partial(_dsa_kernel, topk=topk),
        grid=(B, S // BLOCK),
        in_specs=[blk(A_HEADS * A_QK_DIM), blk(IDX_HEADS * IDX_DIM), blk(LANES),
                  seq(A_KV_RANK + LANES),
                  pl.BlockSpec((None, S // KEY_CHUNK, A_KV_RANK, KEY_CHUNK), lambda b, i: (b, 0, 0, 0)),
                  pl.BlockSpec((None, S // KEY_CHUNK, 2 * KEY_CHUNK, LANES), lambda b, i: (b, 0, 0, 0)),
                  const3(wuk.shape), const3(wuv.shape)],
        out_specs=blk(A_HEADS * LANES),
        out_shape=jax.ShapeDtypeStruct((B, S, A_HEADS * LANES), BF16),
        scratch_shapes=[
            pltpu.VMEM((S, BLOCK), I32),
            pltpu.VMEM((S // KEY_CHUNK, 32, 8, BLOCK), I32),
            pltpu.VMEM((A_HEADS * BLOCK, A_KV_RANK + LANES), BF16),
            pltpu.VMEM((A_HEADS * BLOCK, LANES), BF16),
            pltpu.VMEM((8, A_HEADS * LANES), F32),
            pltpu.VMEM((A_KV_RANK, A_HEADS * LANES), F32),
            pltpu.VMEM((KEY_CHUNK, A_HEADS * LANES), F32),
            pltpu.VMEM((KEY_CHUNK, A_HEADS * LANES), F32),
        ],
        compiler_params=_cparams(("parallel", "arbitrary")),
        name="dsa",
    )(q, qi, wi, kx, ct, ki, wuk, wuv)


def _window_bias(branches, nrel):
    par = jnp.arange(2, dtype=I32)[:, None, None, None]
    rel = jnp.arange(nrel + 1, dtype=I32)[None, :, None, None]
    krow = jnp.arange(KEY_CHUNK, dtype=I32)[None, None, :, None]
    qcol = jnp.arange(BLOCK, dtype=I32)[None, None, None, :]
    dist = BLOCK * par + KEY_CHUNK * (nrel - 1 - rel) + qcol - krow
    mult = sum(((dist >= 0) & (dist <= window) & (dist % dil == 0)).astype(F32) for window, dil in branches)
    return jnp.where(mult > 0, jnp.log(jnp.maximum(mult, 1.0)), NEG_INF)


def _wattn_kernel(q_ref, k_ref, vt_ref, bias_ref, o_ref, acc_ref, sa_ref, sb_ref, p_ref, *, nh, nrel):
    kc = KEY_CHUNK
    last = k_ref.shape[0] // kc - 1
    i = pl.program_id(1)
    par, top = i % 2, i // 2
    c_lo = jnp.maximum(top - (nrel - 1), 0)
    npairs = (top + 2 - c_lo) // 2

    acc_ref[...] = jnp.zeros(acc_ref.shape, F32)

    def logits(c, s_ref):
        off = pl.multiple_of(jnp.minimum(c, last) * kc, kc)
        bias = bias_ref[par, c - top + (nrel - 1)]
        for h in range(nh):
            cols = slice(h * LANES, (h + 1) * LANES)
            s_ref[:, cols] = lax.dot_general(k_ref[pl.ds(off, kc), cols], q_ref[:, cols], _NT,
                                             preferred_element_type=F32) + bias

    def consume(s_ref, c, carry):
        m_prev, l_prev = carry
        s = s_ref[...]
        m_new = jnp.maximum(m_prev, jnp.max(s, axis=0, keepdims=True))
        alpha = jnp.exp(m_prev - m_new)
        p = jnp.exp(s - m_new)
        l_new = alpha * l_prev + jnp.sum(p, axis=0, keepdims=True)
        p_ref[...] = p.astype(BF16)
        cd = jnp.minimum(c, last)
        for h in range(nh):
            cols = slice(h * LANES, (h + 1) * LANES)
            acc_ref[:, cols] = alpha[:, cols] * acc_ref[:, cols] + jnp.dot(
                vt_ref[cd, cols, :], p_ref[:, cols], preferred_element_type=F32)
        return m_new, l_new

    logits(c_lo, sa_ref)

    def pair_body(j, carry):
        c0 = c_lo + 2 * j
        logits(c0 + 1, sb_ref)
        carry = consume(sa_ref, c0, carry)
        logits(jnp.minimum(c0 + 2, c_lo + 2 * npairs - 1), sa_ref)
        return consume(sb_ref, c0 + 1, carry)

    _, l_fin = _loop_two_per_trip(0, npairs, pair_body,
                                  (jnp.full((1, nh * LANES), NEG_INF, F32), jnp.zeros((1, nh * LANES), F32)))

    inv_l = 1.0 / l_fin
    for h in range(nh):
        cols = slice(h * LANES, (h + 1) * LANES)
        o_ref[:, cols] = (acc_ref[:, cols] * inv_l[:, cols]).T.astype(BF16)


def _swa_kernel(sink_ref, q_ref, kp_ref, kc_ref, vp_ref, vc_ref, o_ref, *, subheads, max_dist):
    n = pl.program_id(1)
    qi = lax.broadcasted_iota(I32, (BLOCK, 2 * BLOCK), 0)
    kj = lax.broadcasted_iota(I32, (BLOCK, 2 * BLOCK), 1)
    dist = BLOCK + qi - kj
    mask = (dist >= 0) & (dist <= max_dist) & ((kj >= BLOCK) | (n > 0))
    for j, subs in enumerate(subheads):
        qt = q_ref[:, j * LANES:(j + 1) * LANES]
        o_t = None
        for kt, vt, hidx in subs:
            kk = jnp.concatenate([kp_ref[:, kt * LANES:(kt + 1) * LANES],
                                  kc_ref[:, kt * LANES:(kt + 1) * LANES]], axis=0)
            vv = jnp.concatenate([vp_ref[:, vt * LANES:(vt + 1) * LANES],
                                  vc_ref[:, vt * LANES:(vt + 1) * LANES]], axis=0)
            s = lax.dot_general(qt, kk, _NT, preferred_element_type=F32)
            s = jnp.where(mask, s, NEG_INF)
            sk = sink_ref[hidx]
            m = jnp.maximum(jnp.max(s, axis=-1, keepdims=True), sk)
            p = jnp.exp(s - m)
            l = jnp.sum(p, axis=-1, keepdims=True) + jnp.exp(sk - m)
            o_s = jnp.dot(p.astype(BF16), vv, preferred_element_type=F32) / l
            o_t = o_s if o_t is None else o_t + o_s
        o_ref[:, j * LANES:(j + 1) * LANES] = o_t.astype(BF16)


def _swa(q, k, v, sinks, *, subheads, max_dist):
    B, S, wq = q.shape
    wk, wv = k.shape[-1], v.shape[-1]
    cur = lambda width: pl.BlockSpec((None, BLOCK, width), lambda b, i: (b, i, 0))
    prv = lambda width: pl.BlockSpec((None, BLOCK, width), lambda b, i: (b, jnp.maximum(i - 1, 0), 0))
    return pl.pallas_call(
        functools.partial(_swa_kernel, subheads=subheads, max_dist=max_dist),
        grid=(B, S // BLOCK),
        in_specs=[pl.BlockSpec(memory_space=pltpu.SMEM), cur(wq), prv(wk), cur(wk), prv(wv), cur(wv)],
        out_specs=cur(wq),
        out_shape=jax.ShapeDtypeStruct((B, S, wq), BF16),
        compiler_params=_cparams(("parallel", "arbitrary")),
        name="swa",
    )(sinks, q, k, k, v, v)


def _wattn(q, k, vt, branches):
    B, S, wq = q.shape
    widest = max(w for w, _ in branches)
    nrel = min(-(-widest // KEY_CHUNK) + 1, S // KEY_CHUNK)
    bias = _window_bias(branches, nrel)
    nh = wq // LANES
    blk = lambda width: pl.BlockSpec((None, BLOCK, width), lambda b, i: (b, i, 0))
    return pl.pallas_call(
        functools.partial(_wattn_kernel, nh=nh, nrel=nrel),
        grid=(B, S // BLOCK),
        in_specs=[blk(wq),
                  pl.BlockSpec((None, S, wq), lambda b, i: (b, 0, 0)),
                  pl.BlockSpec((None,) + vt.shape[1:], lambda b, i: (b, 0, 0, 0)),
                  pl.BlockSpec(bias.shape, lambda b, i: (0, 0, 0, 0))],
        out_specs=blk(wq),
        out_shape=jax.ShapeDtypeStruct((B, S, wq), BF16),
        scratch_shapes=[pltpu.VMEM((LANES, nh * LANES), F32),
                        pltpu.VMEM((KEY_CHUNK, nh * LANES), F32),
                        pltpu.VMEM((KEY_CHUNK, nh * LANES), F32),
                        pltpu.VMEM((KEY_CHUNK, nh * LANES), BF16)],
        compiler_params=_cparams(("parallel", "arbitrary")),
        name="wattn",
    )(q, k, vt, bias)


def _memkv_kernel(mem_ref, g_ref, w_ref, k_ref, v_ref):
    hb = _rms(mem_ref[...], g_ref[...]).astype(BF16)
    y = jnp.dot(hb, w_ref[...], preferred_element_type=F32)
    lane = lax.broadcasted_iota(I32, (y.shape[0], LANES), 1)
    for t in range(MEM_WIDTH // LANES):
        for out_ref, base in ((k_ref, 0), (v_ref, MEM_WIDTH)):
            tile = y[:, base + t * LANES:base + (t + 1) * LANES]
            out_ref[:, (2 * t) * LANES:(2 * t + 1) * LANES] = jnp.where(lane < MEM_HEAD_DIM, tile, 0.0).astype(BF16)
            out_ref[:, (2 * t + 1) * LANES:(2 * t + 2) * LANES] = jnp.where(lane >= MEM_HEAD_DIM, tile, 0.0).astype(BF16)


def _memkv(mem, g_mem, w):
    B, M, D = mem.shape
    L = w.shape[0]
    out = jax.ShapeDtypeStruct((L, B, M, 2 * MEM_WIDTH), BF16)
    ospec = pl.BlockSpec((None, None, M, 2 * MEM_WIDTH), lambda l, b: (l, b, 0, 0))
    return pl.pallas_call(
        _memkv_kernel,
        grid=(L, B),
        in_specs=[pl.BlockSpec((None, M, D), lambda l, b: (b, 0, 0)),
                  pl.BlockSpec((1, D), lambda l, b: (0, 0)),
                  pl.BlockSpec((None, D, 2 * MEM_WIDTH), lambda l, b: (l, 0, 0))],
        out_specs=(ospec, ospec),
        out_shape=(out, out),
        compiler_params=_cparams(("parallel", "parallel")),
        name="memkv",
    )(mem, g_mem, w)


def _out_kernel(x_ref, mix_ref, qm_ref, mk_ref, mv_ref, w_ref, o_ref):
    mw = mix_ref.shape[-1]
    y = x_ref[...] + jnp.dot(mix_ref[...], w_ref[0:mw, :], preferred_element_type=F32)
    for t in range(MEM_WIDTH // LANES):
        qt = qm_ref[:, t * LANES:(t + 1) * LANES]
        o_t = None
        for u in range(2):
            sl = slice((2 * t + u) * LANES, (2 * t + u + 1) * LANES)
            s = lax.dot_general(qt, mk_ref[:, sl], _NT, preferred_element_type=F32)
            p = jnp.exp(s - jnp.max(s, axis=-1, keepdims=True))
            l = jnp.sum(p, axis=-1, keepdims=True)
            o_s = jnp.dot(p.astype(BF16), mv_ref[:, sl], preferred_element_type=F32) / l
            o_t = o_s if o_t is None else o_t + o_s
        y = y + jnp.dot(o_t.astype(BF16), w_ref[mw + t * LANES:mw + (t + 1) * LANES, :],
                        preferred_element_type=F32)
    o_ref[...] = y


def _out_proj(x, mix, qm, mk, mv, w, tm):
    B, S, D = x.shape
    M = mk.shape[1]
    row = lambda width: pl.BlockSpec((None, tm, width), lambda b, i: (b, i, 0))
    mem = pl.BlockSpec((None, M, 2 * MEM_WIDTH), lambda b, i: (b, 0, 0))
    return pl.pallas_call(
        _out_kernel,
        grid=(B, S // tm),
        in_specs=[row(D), row(mix.shape[-1]), row(MEM_WIDTH), mem, mem,
                  pl.BlockSpec(w.shape, lambda b, i: (0, 0))],
        out_specs=row(D),
        out_shape=jax.ShapeDtypeStruct((B, S, D), F32),
        compiler_params=_cparams(("parallel", "parallel")),
        name="out_proj",
    )(x, mix, qm, mk, mv, w)


def _ffn_kernel(x_ref, g_ref, wup_ref, cw_ref, cb_ref, wdn_ref, gf_ref, o_ref, carry_ref, *, tm, cw, final):
    dff = wdn_ref.shape[0]
    first = pl.program_id(1) == 0
    x = x_ref[...]
    hb = _rms(x, g_ref[...]).astype(BF16)
    row = lax.broadcasted_iota(I32, (tm, cw), 0)
    acc = x

    def up(c0):
        return (jnp.dot(hb, wup_ref[:, c0:c0 + cw], preferred_element_type=F32),
                jnp.dot(hb, wup_ref[:, dff + c0:dff + c0 + cw], preferred_element_type=F32))

    nxt = up(0)
    for c0 in range(0, dff, cw):
        a, b = nxt
        if c0 + cw < dff:
            nxt = up(c0 + cw)
        prev = jnp.where(first, 0.0, carry_ref[:, c0:c0 + cw])
        p1, p2 = prev[7:8, :], prev[6:7, :]
        a1 = jnp.where(row == 0, p1, pltpu.roll(a, 1, 0))
        a2 = jnp.where(row == 0, p2, jnp.where(row == 1, p1, pltpu.roll(a, 2, 0)))
        carry_ref[:, c0:c0 + cw] = a[tm - 8:tm, :]
        w = cw_ref[:, c0:c0 + cw]
        conv = w[0:1, :] * a2 + w[1:2, :] * a1 + w[2:3, :] * a + cb_ref[:, c0:c0 + cw]
        gated = conv / (1.0 + jnp.exp(-conv)) * b
        acc = acc + jnp.dot(gated.astype(BF16), wdn_ref[c0:c0 + cw, :], preferred_element_type=F32)
    if final:
        acc = _rms(acc, gf_ref[...])
    o_ref[...] = acc


def _ffn(x, g, wup, cw, cb, wdn, gf, tm, final):
    B, S, D = x.shape
    dff = wdn.shape[0]
    row = pl.BlockSpec((None, tm, D), lambda b, i: (b, i, 0))
    const = lambda shape: pl.BlockSpec(shape, lambda b, i: (0, 0), pipeline_mode=pl.Buffered(1))
    return pl.pallas_call(
        functools.partial(_ffn_kernel, tm=tm, cw=2 * LANES, final=final),
        grid=(B, S // tm),
        in_specs=[row, const((1, D)), const(wup.shape), const(cw.shape), const((1, dff)), const(wdn.shape),
                  const((1, D))],
        out_specs=row,
        out_shape=jax.ShapeDtypeStruct((B, S, D), F32),
        scratch_shapes=[pltpu.VMEM((8, dff), F32)],
        compiler_params=_cparams(("arbitrary", "arbitrary")),
        name="ffn",
    )(x, g, wup, cw, cb, wdn, gf)


def _b_subheads():
    return tuple(((2 * (j // 2), 2 * (j // 2), 2 * j), (2 * (j // 2) + 1, 2 * (j // 2) + 1, 2 * j + 1))
                 for j in range(B_HEADS // 2))


def kernel(x, mem, positions, g_mix, g_ffn, g_mem, g_final, w_mem_kv, a_w_in, a_kv_norm, a_w_uk, a_w_uv, a_w_out,
           b_w_in, b_sinks, b_w_out, c_w_in, c_w_out, f_w_up, f_conv_w, f_conv_b, f_w_down):
    B, S, D = x.shape
    depth = g_mix.shape[0]
    tm = min(512, S)
    tab32 = _rope_table(positions, A_QK_DIM, A_ROPE_DIM)
    tab16 = _rope_table(positions, B_HEAD_DIM, B_HEAD_DIM // 4)
    mem_k, mem_v = _memkv(mem, g_mem.reshape(1, D), w_mem_kv.astype(BF16))
    conv_w = jnp.pad(f_conv_w, ((0, 0), (0, 8 - CONV_WIDTH), (0, 0)))
    for i in range(depth):
        kind, j = i % 3, i // 3
        g = g_mix[i].reshape(1, D)
        if kind == 0:
            q, kx, ct, qi, ki, wi, qm = _proj_a(x, g, _prep_a_w_in(a_w_in[j]), a_kv_norm[j].reshape(1, -1),
                                                tab32, tab16, tm)
            wuk = jnp.pad(jnp.transpose(a_w_uk[j], (1, 2, 0)), ((0, 0), (A_ROPE_DIM, 0), (0, 0))).astype(BF16)
            wuv = jnp.transpose(a_w_uv[j], (1, 0, 2)).astype(BF16)
            mix = _dsa(q, qi, wi, kx, ct, ki, wuk, wuv)
            w_out = a_w_out[j]
        elif kind == 1:
            nq, nkv = B_HEADS * B_HEAD_DIM, 4 * B_KV_HEADS * B_HEAD_DIM
            q, k, v, qm = _proj_qkv(x, g, _prep_b_w_in(b_w_in[j]), tab16, tm, B_HEAD_DIM, B_HEAD_DIM // 4,
                                    nq, nkv, nkv, False)
            mix = _swa(q, k, v, b_sinks[j], subheads=_b_subheads(), max_dist=B_WINDOW - 1)
            w_out = b_w_out[j]
        else:
            nq = C_HEADS * C_HEAD_DIM
            q, k, v, qm = _proj_qkv(x, g, c_w_in[j].astype(BF16), tab32, tm, C_HEAD_DIM, C_HEAD_DIM // 4,
                                    nq, nq, nq, True)
            mix = _wattn(q, k, v, C_BRANCHES)
            w_out = c_w_out[j]
        x = _out_proj(x, mix, qm, mem_k[i], mem_v[i], w_out.astype(BF16), tm)
        x = _ffn(x, g_ffn[i].reshape(1, D), f_w_up[i].astype(BF16), conv_w[i], f_conv_b[i].reshape(1, -1),
                 f_w_down[i].astype(BF16), g_final.reshape(1, D), min(256, S), i == depth - 1)
    return x
```

```python
import functools

import jax
import jax.numpy as jnp
import numpy as np
from jax import lax
from jax.experimental import pallas as pl
from jax.experimental.pallas import tpu as pltpu

F32 = jnp.float32
BF16 = jnp.bfloat16
I32 = jnp.int32

LANES = 128
BLOCK = 128
ROPE_THETA = 500000.0
EPS = 1e-6
NEG_INF = -1e30
LOG2E = 1.4426950408889634
INT_MIN = -(2**31)

A_HEADS = 8
A_QK_DIM = 128
A_ROPE_DIM = 32
A_KV_RANK = 256
IDX_HEADS = 16
IDX_DIM = 64
TOPK_MAX = 256
B_HEADS = 16
B_KV_HEADS = 4
B_HEAD_DIM = 64
B_WINDOW = 128
C_HEADS = 8
C_HEAD_DIM = 128
C_BRANCHES = ((128, 1), (512, 4), (2048, 16))
MEM_HEADS = 4
MEM_HEAD_DIM = 64
MEM_WIDTH = MEM_HEADS * MEM_HEAD_DIM
CONV_WIDTH = 3
KEY_CHUNK = 256

VMEM_LIMIT = 56 * 1024 * 1024

_NT = (((1,), (1,)), ((), ()))


def _cparams(sem):
    return pltpu.CompilerParams(dimension_semantics=sem, vmem_limit_bytes=VMEM_LIMIT)


def _rms(x, g):
    return x * lax.rsqrt(jnp.mean(x * x, axis=-1, keepdims=True) + EPS) * g


def _loop_two_per_trip(lo, hi, body, carry):
    trips = (hi - lo) // 2

    def double(t, c):
        return body(lo + 2 * t + 1, body(lo + 2 * t, c))

    carry = lax.fori_loop(0, trips, double, carry)
    return lax.fori_loop(lo + 2 * trips, hi, body, carry)


def _bit_transpose32(words):
    a = list(words)
    j, m = 16, 0x0000FFFF
    while j:
        mask = jnp.int32(m - (1 << 32) if m >= (1 << 31) else m)
        for k in range(32):
            if k & j == 0:
                t = (a[k] ^ lax.shift_right_logical(a[k + j], j)) & mask
                a[k] = a[k] ^ t
                a[k + j] = a[k + j] ^ lax.shift_left(t, j)
        j >>= 1
        m = (m ^ (m << j)) & 0xFFFFFFFF
    return a


def _rope_tile(t, tab, half):
    c, sa, sb = tab[:, 0:LANES], tab[:, LANES:2 * LANES], tab[:, 2 * LANES:3 * LANES]
    return t * c + pltpu.roll(t, half, 1) * sa + pltpu.roll(t, LANES - half, 1) * sb


def _rope_table(positions, head_dim, rot):
    half = rot // 2
    inv = ROPE_THETA ** (-jnp.arange(0, rot, 2, dtype=F32) / rot)
    ang = positions.astype(F32)[..., None] * inv
    cs = jnp.concatenate([jnp.cos(ang), jnp.sin(ang)], axis=-1)
    sel = np.zeros((rot, 3 * LANES), np.float32)
    one = np.zeros((3 * LANES,), np.float32)
    for l in range(LANES):
        j = l % head_dim
        if j < half:
            sel[j, l] = 1.0
            sel[half + j, 2 * LANES + l] = -1.0
        elif j < rot:
            sel[j - half, l] = 1.0
            sel[j, LANES + l] = 1.0
        else:
            one[l] = 1.0
    return jnp.dot(cs, jnp.asarray(sel), precision=lax.Precision.HIGHEST) + jnp.asarray(one)


def _proj_a_kernel(x_ref, g_ref, w_ref, kvn_ref, t32_ref, t16_ref,
                   q_ref, kx_ref, ct_ref, qi_ref, ki_ref, wi_ref, qm_ref, *, tm):
    hb = _rms(x_ref[...], g_ref[...]).astype(BF16)
    t32 = t32_ref[...]
    t16 = t16_ref[...]

    def mm(a, b):
        return jnp.dot(hb, w_ref[:, a:b], preferred_element_type=F32)

    scale = A_QK_DIM ** -0.5 * LOG2E
    for j in range(0, A_HEADS * A_QK_DIM, 2 * LANES):
        y = mm(j, j + 2 * LANES)
        for u in range(2):
            t = _rope_tile(y[:, u * LANES:(u + 1) * LANES], t32, A_ROPE_DIM // 2)
            q_ref[:, j + u * LANES:j + (u + 1) * LANES] = (t * scale).astype(BF16)
    o = A_HEADS * A_QK_DIM
    c = _rms(mm(o, o + A_KV_RANK), kvn_ref[...])
    kx_ref[:, 0:A_KV_RANK] = c.astype(BF16)
    for u in range(tm // KEY_CHUNK):
        ct_ref[u] = c[u * KEY_CHUNK:(u + 1) * KEY_CHUNK, :].T.astype(BF16)
    o += A_KV_RANK
    kx_ref[:, A_KV_RANK:A_KV_RANK + LANES] = _rope_tile(mm(o, o + LANES), t32, A_ROPE_DIM // 2).astype(BF16)
    o += LANES
    for j in range(0, IDX_HEADS * IDX_DIM, 2 * LANES):
        y = mm(o + j, o + j + 2 * LANES)
        for u in range(2):
            t = _rope_tile(y[:, u * LANES:(u + 1) * LANES], t16, IDX_DIM // 8)
            qi_ref[:, j + u * LANES:j + (u + 1) * LANES] = t.astype(BF16)
    o += IDX_HEADS * IDX_DIM
    y = mm(o, o + 2 * LANES)
    for u in range(2):
        t = _rope_tile(y[:, u * LANES:(u + 1) * LANES], t16, IDX_DIM // 8).astype(BF16)
        for v in range(tm // KEY_CHUNK):
            ki_ref[v, u * KEY_CHUNK:(u + 1) * KEY_CHUNK, :] = t[v * KEY_CHUNK:(v + 1) * KEY_CHUNK, :]
    o += 2 * LANES
    wi_ref[...] = mm(o, o + LANES) * (IDX_HEADS * IDX_DIM) ** -0.5
    o += LANES
    qm_ref[...] = (mm(o, o + MEM_WIDTH) * (MEM_HEAD_DIM ** -0.5 * LOG2E)).astype(BF16)


def _prep_a_w_in(w):
    d = w.shape[0]
    sizes = (A_HEADS * A_QK_DIM, A_KV_RANK, A_ROPE_DIM, IDX_HEADS * IDX_DIM, IDX_DIM, IDX_HEADS, MEM_WIDTH)
    offs = [0]
    for s in sizes:
        offs.append(offs[-1] + s)
    q, ckv, kr, qi, ki, wi, qm = [w[:, offs[i]:offs[i + 1]] for i in range(len(sizes))]
    z = lambda n: jnp.zeros((d, n), w.dtype)
    return jnp.concatenate([
        q, ckv, kr, z(LANES - A_ROPE_DIM), qi,
        ki, z(LANES - IDX_DIM), z(LANES - IDX_DIM), ki,
        wi, z(LANES - IDX_HEADS), qm], axis=1).astype(BF16)


def _proj_a(x, g, w, kvn, t32, t16, tm):
    B, S, D = x.shape
    n = w.shape[1]
    row = lambda width: pl.BlockSpec((None, tm, width), lambda b, i: (b, i, 0))
    const = lambda shape: pl.BlockSpec(shape, lambda b, i: (0,) * len(shape))
    out_shape = (
        jax.ShapeDtypeStruct((B, S, A_HEADS * A_QK_DIM), BF16),
        jax.ShapeDtypeStruct((B, S, A_KV_RANK + LANES), BF16),
        jax.ShapeDtypeStruct((B, S // KEY_CHUNK, A_KV_RANK, KEY_CHUNK), BF16),
        jax.ShapeDtypeStruct((B, S, IDX_HEADS * IDX_DIM), BF16),
        jax.ShapeDtypeStruct((B, S // KEY_CHUNK, 2 * KEY_CHUNK, LANES), BF16),
        jax.ShapeDtypeStruct((B, S, LANES), F32),
        jax.ShapeDtypeStruct((B, S, MEM_WIDTH), BF16),
    )
    out_specs = (
        row(A_HEADS * A_QK_DIM), row(A_KV_RANK + LANES),
        pl.BlockSpec((None, tm // KEY_CHUNK, A_KV_RANK, KEY_CHUNK), lambda b, i: (b, i, 0, 0)),
        row(IDX_HEADS * IDX_DIM),
        pl.BlockSpec((None, tm // KEY_CHUNK, 2 * KEY_CHUNK, LANES), lambda b, i: (b, i, 0, 0)),
        row(LANES), row(MEM_WIDTH),
    )
    return pl.pallas_call(
        functools.partial(_proj_a_kernel, tm=tm),
        grid=(B, S // tm),
        in_specs=[row(D), const((1, D)), const((D, n)), const((1, A_KV_RANK)), row(3 * LANES), row(3 * LANES)],
        out_specs=out_specs,
        out_shape=out_shape,
        compiler_params=_cparams(("parallel", "parallel")),
        name="proj_a",
    )(x, g, w, kvn, t32, t16)


def _proj_qkv_kernel(x_ref, g_ref, w_ref, tab_ref, q_ref, k_ref, v_ref, qm_ref, *,
                     head_dim, rot, nq, nk, nv, v_transposed):
    hb = _rms(x_ref[...], g_ref[...]).astype(BF16)
    tm = hb.shape[0]
    tab = tab_ref[...]

    def mm(a, b):
        return jnp.dot(hb, w_ref[:, a:b], preferred_element_type=F32)

    scale = head_dim ** -0.5 * LOG2E
    for j in range(0, nq, 2 * LANES):
        y = mm(j, j + 2 * LANES)
        for u in range(2):
            t = _rope_tile(y[:, u * LANES:(u + 1) * LANES], tab, rot // 2)
            q_ref[:, j + u * LANES:j + (u + 1) * LANES] = (t * scale).astype(BF16)
    for j in range(0, nk, 2 * LANES):
        y = mm(nq + j, nq + j + 2 * LANES)
        for u in range(2):
            t = _rope_tile(y[:, u * LANES:(u + 1) * LANES], tab, rot // 2)
            k_ref[:, j + u * LANES:j + (u + 1) * LANES] = t.astype(BF16)
    for j in range(0, nv, 2 * LANES):
        y = mm(nq + nk + j, nq + nk + j + 2 * LANES)
        if v_transposed:
            for u in range(tm // KEY_CHUNK):
                v_ref[u, j:j + 2 * LANES, :] = y[u * KEY_CHUNK:(u + 1) * KEY_CHUNK, :].T.astype(BF16)
        else:
            v_ref[:, j:j + 2 * LANES] = y.astype(BF16)
    o = nq + nk + nv
    qm_ref[...] = (mm(o, o + MEM_WIDTH) * (MEM_HEAD_DIM ** -0.5 * LOG2E)).astype(BF16)


def _proj_qkv(x, g, w, tab, tm, head_dim, rot, nq, nk, nv, v_transposed):
    B, S, D = x.shape
    n = w.shape[1]
    row = lambda width: pl.BlockSpec((None, tm, width), lambda b, i: (b, i, 0))
    const = lambda shape: pl.BlockSpec(shape, lambda b, i: (0,) * len(shape))
    if v_transposed:
        v_spec = pl.BlockSpec((None, tm // KEY_CHUNK, nv, KEY_CHUNK), lambda b, i: (b, i, 0, 0))
        v_shape = jax.ShapeDtypeStruct((B, S // KEY_CHUNK, nv, KEY_CHUNK), BF16)
    else:
        v_spec, v_shape = row(nv), jax.ShapeDtypeStruct((B, S, nv), BF16)
    return pl.pallas_call(
        functools.partial(_proj_qkv_kernel, head_dim=head_dim, rot=rot, nq=nq, nk=nk, nv=nv,
                          v_transposed=v_transposed),
        grid=(B, S // tm),
        in_specs=[row(D), const((1, D)), const((D, n)), row(3 * LANES)],
        out_specs=(row(nq), row(nk), v_spec, row(MEM_WIDTH)),
        out_shape=(jax.ShapeDtypeStruct((B, S, nq), BF16), jax.ShapeDtypeStruct((B, S, nk), BF16), v_shape,
                   jax.ShapeDtypeStruct((B, S, MEM_WIDTH), BF16)),
        compiler_params=_cparams(("parallel", "parallel")),
        name="proj_qkv",
    )(x, g, w, tab)


def _prep_b_w_in(w):
    d = w.shape[0]
    nq, nkv = B_HEADS * B_HEAD_DIM, B_KV_HEADS * B_HEAD_DIM
    q, k, v, qm = w[:, :nq], w[:, nq:nq + nkv], w[:, nq + nkv:nq + 2 * nkv], w[:, nq + 2 * nkv:]
    z = jnp.zeros((d, B_HEAD_DIM), w.dtype)

    def spread(t):
        cols = []
        for h in range(B_KV_HEADS):
            th = t[:, h * B_HEAD_DIM:(h + 1) * B_HEAD_DIM]
            cols += [th, z, z, th]
        return jnp.concatenate(cols, axis=1)

    return jnp.concatenate([q, spread(k), spread(v), qm], axis=1).astype(BF16)


def _dsa_kernel(q_ref, qi_ref, wi_ref, kx_ref, ct_ref, ki_ref, wuk_ref, wuv_ref, o_ref,
                keys_ref, planes_ref, qext_ref, qi2_ref, wrow_ref, acc_ref, sa_ref, sb_ref, *, topk):
    kc = KEY_CHUNK
    hw = A_HEADS * LANES
    gw = 2 * LANES
    i = pl.program_id(1)
    nch = (i * BLOCK + BLOCK + kc - 1) // kc
    lane = lax.broadcasted_iota(I32, (BLOCK, LANES), 1)

    for h in range(A_HEADS):
        rows = slice(h * BLOCK, (h + 1) * BLOCK)
        qh = q_ref[:, h * LANES:(h + 1) * LANES]
        qext_ref[rows, 0:A_KV_RANK] = jnp.dot(qh, wuk_ref[h], preferred_element_type=F32).astype(BF16)
        qext_ref[rows, A_KV_RANK:A_KV_RANK + LANES] = jnp.where(lane < A_ROPE_DIM, qh.astype(F32), 0.0).astype(BF16)
        qi2_ref[rows, :] = qi_ref[:, h * LANES:(h + 1) * LANES]
    w_t = wi_ref[...].T
    for j in range(IDX_HEADS // 2):
        wrow_ref[0:1, j * LANES:(j + 1) * LANES] = w_t[2 * j:2 * j + 1, :]
        wrow_ref[1:2, j * LANES:(j + 1) * LANES] = w_t[2 * j + 1:2 * j + 2, :]

    qpos = i * BLOCK + lax.broadcasted_iota(I32, (kc, BLOCK), 1)
    krow = lax.broadcasted_iota(I32, (kc, BLOCK), 0)

    def score_chunk(c):
        off = pl.multiple_of(c * kc, kc)
        kk = ki_ref[c]
        acc = None
        for g in range(0, hw, gw):
            d = lax.dot_general(kk, qi2_ref[g:g + gw, :], _NT, preferred_element_type=F32)
            t = (jnp.maximum(d[0:kc], 0.0) * wrow_ref[0:1, g:g + gw]
                 + jnp.maximum(d[kc:2 * kc], 0.0) * wrow_ref[1:2, g:g + gw])
            for u in range(0, gw, LANES):
                acc = t[:, u:u + LANES] if acc is None else acc + t[:, u:u + LANES]
        bits = pltpu.bitcast(acc, I32)
        key = jnp.where(bits < 0, bits ^ 0x7FFFFFFF, bits)
        key = jnp.where(off + krow <= qpos, key, INT_MIN)
        keys_ref[pl.ds(off, kc), :] = key
        planes = _bit_transpose32([(key[8 * r:8 * r + 8, :] ^ INT_MIN) for r in range(kc // 8)])
        for p in range(32):
            planes_ref[c, p] = planes[p]

    npairs = (nch + 1) // 2

    @pl.when(i == 0)
    def _():
        planes_ref[...] = jnp.zeros(planes_ref.shape, I32)

    def score_body(j, carry):
        score_chunk(2 * j)
        score_chunk(2 * j + 1)
        return carry

    _loop_two_per_trip(0, npairs, score_body, 0)

    n_chunks = planes_ref.shape[0]

    def bit_body(p, carry):
        alive, need, res = carry
        ones = [a & planes_ref[c, p] for c, a in enumerate(alive)]
        cnt = ones[0] * 0
        for o in ones:
            cnt = cnt + lax.population_count(o)
        cnt = jnp.sum(cnt, axis=0, keepdims=True)
        take = cnt >= need
        alive = [jnp.where(take, o, a ^ o) for o, a in zip(ones, alive)]
        need = jnp.where(take, need, need - cnt)
        res = jnp.where(take, res | lax.shift_left(jnp.int32(1), 31 - p), res)
        return alive, need, res

    alive0 = [jnp.where(c < 2 * npairs, jnp.full((8, BLOCK), -1, I32), 0) for c in range(n_chunks)]
    _, _, res = lax.fori_loop(0, 32, bit_body,
                              (alive0, jnp.full((1, BLOCK), topk, I32), jnp.zeros((1, BLOCK), I32)))
    thr = jnp.maximum(res ^ INT_MIN, INT_MIN + 1)

    acc_ref[...] = jnp.zeros(acc_ref.shape, F32)

    def logits(c, s_ref):
        off = pl.multiple_of(c * kc, kc)
        bias = jnp.where(keys_ref[pl.ds(off, kc), :] >= thr, 0.0, NEG_INF)
        bias = jnp.concatenate([bias] * (gw // LANES), axis=1)
        kx = kx_ref[pl.ds(off, kc), :]
        for g in range(0, hw, gw):
            s_ref[:, g:g + gw] = lax.dot_general(
                kx, qext_ref[g:g + gw, :], _NT, preferred_element_type=F32) + bias

    def consume(s_ref, c, carry):
        m_prev, l_prev = carry
        ct = ct_ref[c]
        m_out, l_out = [], []
        for g in range(0, hw, gw):
            s = s_ref[:, g:g + gw]
            m_new = jnp.maximum(m_prev[:, g:g + gw], jnp.max(s, axis=0, keepdims=True))
            alpha = jnp.exp2(m_prev[:, g:g + gw] - m_new)
            p = jnp.exp2(s - m_new)
            l_out.append(alpha * l_prev[:, g:g + gw] + jnp.sum(p, axis=0, keepdims=True))
            m_out.append(m_new)
            acc_ref[:, g:g + gw] = alpha * acc_ref[:, g:g + gw] + jnp.dot(
                ct, p.astype(BF16), preferred_element_type=F32)
        return jnp.concatenate(m_out, axis=1), jnp.concatenate(l_out, axis=1)

    logits(0, sa_ref)

    def pair_body(j, carry):
        c0 = 2 * j
        logits(c0 + 1, sb_ref)
        carry = consume(sa_ref, c0, carry)
        logits(jnp.minimum(c0 + 2, 2 * npairs - 1), sa_ref)
        return consume(sb_ref, c0 + 1, carry)

    _, l_fin = _loop_two_per_trip(0, npairs, pair_body,
                                  (jnp.full((1, hw), NEG_INF, F32), jnp.zeros((1, hw), F32)))

    inv_l = 1.0 / l_fin
    for h in range(A_HEADS):
        cols = slice(h * LANES, (h + 1) * LANES)
        o_lat = (acc_ref[:, cols] * inv_l[:, cols]).T.astype(BF16)
        o_ref[:, cols] = jnp.dot(o_lat, wuv_ref[h], preferred_element_type=F32).astype(BF16)


def _dsa(q, qi, wi, kx, ct, ki, wuk, wuv):
    B, S, _ = q.shape
    topk = min(TOPK_MAX, S // 4)
    blk = lambda width: pl.BlockSpec((None, BLOCK, width), lambda b, i: (b, i, 0))
    seq = lambda width: pl.BlockSpec((None, S, width), lambda b, i: (b, 0, 0))
    const3 = lambda shape: pl.BlockSpec(shape, lambda b, i: (0, 0, 0))
    return pl.pallas_call(
        functools.partial(_dsa_kernel, topk=topk),
        grid=(B, S // BLOCK),
        in_specs=[blk(A_HEADS * A_QK_DIM), blk(IDX_HEADS * IDX_DIM), blk(LANES),
                  seq(A_KV_RANK + LANES),
                  pl.BlockSpec((None, S // KEY_CHUNK, A_KV_RANK, KEY_CHUNK), lambda b, i: (b, 0, 0, 0)),
                  pl.BlockSpec((None, S // KEY_CHUNK, 2 * KEY_CHUNK, LANES), lambda b, i: (b, 0, 0, 0)),
                  const3(wuk.shape), const3(wuv.shape)],
        out_specs=blk(A_HEADS * LANES),
        out_shape=jax.ShapeDtypeStruct((B, S, A_HEADS * LANES), BF16),
        scratch_shapes=[
            pltpu.VMEM((S, BLOCK), I32),
            pltpu.VMEM((S // KEY_CHUNK, 32, 8, BLOCK), I32),
            pltpu.VMEM((A_HEADS * BLOCK, A_KV_RANK + LANES), BF16),
            pltpu.VMEM((A_HEADS * BLOCK, LANES), BF16),
            pltpu.VMEM((8, A_HEADS * LANES), F32),
            pltpu.VMEM((A_KV_RANK, A_HEADS * LANES), F32),
            pltpu.VMEM((KEY_CHUNK, A_HEADS * LANES), F32),
            pltpu.VMEM((KEY_CHUNK, A_HEADS * LANES), F32),
        ],
        compiler_params=_cparams(("parallel", "arbitrary")),
        name="dsa",
    )(q, qi, wi, kx, ct, ki, wuk, wuv)


def _window_bias(branches, nrel):
    par = jnp.arange(2, dtype=I32)[:, None, None, None]
    rel = jnp.arange(nrel + 1, dtype=I32)[None, :, None, None]
    krow = jnp.arange(KEY_CHUNK, dtype=I32)[None, None, :, None]
    qcol = jnp.arange(BLOCK, dtype=I32)[None, None, None, :]
    dist = BLOCK * par + KEY_CHUNK * (nrel - 1 - rel) + qcol - krow
    mult = sum(((dist >= 0) & (dist <= window) & (dist % dil == 0)).astype(F32) for window, dil in branches)
    return jnp.where(mult > 0, jnp.log2(jnp.maximum(mult, 1.0)), NEG_INF)


def _wattn_kernel(q_ref, k_ref, vt_ref, bias_ref, o_ref, acc_ref, sa_ref, sb_ref, p_ref, *, nh, nrel):
    kc = KEY_CHUNK
    last = k_ref.shape[0] // kc - 1
    i = pl.program_id(1)
    par, top = i % 2, i // 2
    c_lo = jnp.maximum(top - (nrel - 1), 0)
    npairs = (top + 2 - c_lo) // 2

    acc_ref[...] = jnp.zeros(acc_ref.shape, F32)

    def logits(c, s_ref):
        off = pl.multiple_of(jnp.minimum(c, last) * kc, kc)
        bias = bias_ref[par, c - top + (nrel - 1)]
        for h in range(nh):
            cols = slice(h * LANES, (h + 1) * LANES)
            s_ref[:, cols] = lax.dot_general(k_ref[pl.ds(off, kc), cols], q_ref[:, cols], _NT,
                                             preferred_element_type=F32) + bias

    def consume(s_ref, c, carry):
        m_prev, l_prev = carry
        s = s_ref[...]
        m_new = jnp.maximum(m_prev, jnp.max(s, axis=0, keepdims=True))
        alpha = jnp.exp2(m_prev - m_new)
        p_ref[...] = jnp.exp2(s - m_new).astype(BF16)
        l_new = alpha * l_prev + jnp.dot(jnp.ones((16, kc), BF16), p_ref[...], preferred_element_type=F32)[0:1, :]
        cd = jnp.minimum(c, last)
        for h in range(nh):
            cols = slice(h * LANES, (h + 1) * LANES)
            acc_ref[:, cols] = alpha[:, cols] * acc_ref[:, cols] + jnp.dot(
                vt_ref[cd, cols, :], p_ref[:, cols], preferred_element_type=F32)
        return m_new, l_new

    logits(c_lo, sa_ref)

    def pair_body(j, carry):
        c0 = c_lo + 2 * j
        logits(c0 + 1, sb_ref)
        carry = consume(sa_ref, c0, carry)
        logits(jnp.minimum(c0 + 2, c_lo + 2 * npairs - 1), sa_ref)
        return consume(sb_ref, c0 + 1, carry)

    _, l_fin = _loop_two_per_trip(0, npairs, pair_body,
                                  (jnp.full((1, nh * LANES), NEG_INF, F32), jnp.zeros((1, nh * LANES), F32)))

    inv_l = 1.0 / l_fin
    for h in range(nh):
        cols = slice(h * LANES, (h + 1) * LANES)
        o_ref[:, cols] = (acc_ref[:, cols] * inv_l[:, cols]).T.astype(BF16)


def _swa_kernel(sink_ref, q_ref, kp_ref, kc_ref, vp_ref, vc_ref, o_ref, *, subheads, max_dist):
    n = pl.program_id(1)
    qi = lax.broadcasted_iota(I32, (BLOCK, 2 * BLOCK), 0)
    kj = lax.broadcasted_iota(I32, (BLOCK, 2 * BLOCK), 1)
    dist = BLOCK + qi - kj
    mask = (dist >= 0) & (dist <= max_dist) & ((kj >= BLOCK) | (n > 0))
    for j, subs in enumerate(subheads):
        qt = q_ref[:, j * LANES:(j + 1) * LANES]
        o_t = None
        for kt, vt, hidx in subs:
            kk = jnp.concatenate([kp_ref[:, kt * LANES:(kt + 1) * LANES],
                                  kc_ref[:, kt * LANES:(kt + 1) * LANES]], axis=0)
            vv = jnp.concatenate([vp_ref[:, vt * LANES:(vt + 1) * LANES],
                                  vc_ref[:, vt * LANES:(vt + 1) * LANES]], axis=0)
            s = lax.dot_general(qt, kk, _NT, preferred_element_type=F32)
            s = jnp.where(mask, s, NEG_INF)
            sk = sink_ref[hidx] * LOG2E
            m = jnp.maximum(jnp.max(s, axis=-1, keepdims=True), sk)
            p = jnp.exp2(s - m)
            l = jnp.sum(p, axis=-1, keepdims=True) + jnp.exp2(sk - m)
            o_s = jnp.dot(p.astype(BF16), vv, preferred_element_type=F32) / l
            o_t = o_s if o_t is None else o_t + o_s
        o_ref[:, j * LANES:(j + 1) * LANES] = o_t.astype(BF16)


def _swa(q, k, v, sinks, *, subheads, max_dist):
    B, S, wq = q.shape
    wk, wv = k.shape[-1], v.shape[-1]
    cur = lambda width: pl.BlockSpec((None, BLOCK, width), lambda b, i: (b, i, 0))
    prv = lambda width: pl.BlockSpec((None, BLOCK, width), lambda b, i: (b, jnp.maximum(i - 1, 0), 0))
    return pl.pallas_call(
        functools.partial(_swa_kernel, subheads=subheads, max_dist=max_dist),
        grid=(B, S // BLOCK),
        in_specs=[pl.BlockSpec(memory_space=pltpu.SMEM), cur(wq), prv(wk), cur(wk), prv(wv), cur(wv)],
        out_specs=cur(wq),
        out_shape=jax.ShapeDtypeStruct((B, S, wq), BF16),
        compiler_params=_cparams(("parallel", "arbitrary")),
        name="swa",
    )(sinks, q, k, k, v, v)


def _wattn(q, k, vt, branches):
    B, S, wq = q.shape
    widest = max(w for w, _ in branches)
    nrel = min(-(-widest // KEY_CHUNK) + 1, S // KEY_CHUNK)
    bias = _window_bias(branches, nrel)
    nh = wq // LANES
    blk = lambda width: pl.BlockSpec((None, BLOCK, width), lambda b, i: (b, i, 0))
    return pl.pallas_call(
        functools.partial(_wattn_kernel, nh=nh, nrel=nrel),
        grid=(B, S // BLOCK),
        in_specs=[blk(wq),
                  pl.BlockSpec((None, S, wq), lambda b, i: (b, 0, 0)),
                  pl.BlockSpec((None,) + vt.shape[1:], lambda b, i: (b, 0, 0, 0)),
                  pl.BlockSpec(bias.shape, lambda b, i: (0, 0, 0, 0))],
        out_specs=blk(wq),
        out_shape=jax.ShapeDtypeStruct((B, S, wq), BF16),
        scratch_shapes=[pltpu.VMEM((LANES, nh * LANES), F32),
                        pltpu.VMEM((KEY_CHUNK, nh * LANES), F32),
                        pltpu.VMEM((KEY_CHUNK, nh * LANES), F32),
                        pltpu.VMEM((KEY_CHUNK, nh * LANES), BF16)],
        compiler_params=_cparams(("parallel", "arbitrary")),
        name="wattn",
    )(q, k, vt, bias)


def _memkv_kernel(mem_ref, g_ref, w_ref, k_ref, v_ref):
    hb = _rms(mem_ref[...], g_ref[...]).astype(BF16)
    y = jnp.dot(hb, w_ref[...], preferred_element_type=F32)
    lane = lax.broadcasted_iota(I32, (y.shape[0], LANES), 1)
    for t in range(MEM_WIDTH // LANES):
        for out_ref, base in ((k_ref, 0), (v_ref, MEM_WIDTH)):
            tile = y[:, base + t * LANES:base + (t + 1) * LANES]
            out_ref[:, (2 * t) * LANES:(2 * t + 1) * LANES] = jnp.where(lane < MEM_HEAD_DIM, tile, 0.0).astype(BF16)
            out_ref[:, (2 * t + 1) * LANES:(2 * t + 2) * LANES] = jnp.where(lane >= MEM_HEAD_DIM, tile, 0.0).astype(BF16)


def _memkv(mem, g_mem, w):
    B, M, D = mem.shape
    L = w.shape[0]
    out = jax.ShapeDtypeStruct((L, B, M, 2 * MEM_WIDTH), BF16)
    ospec = pl.BlockSpec((None, None, M, 2 * MEM_WIDTH), lambda l, b: (l, b, 0, 0))
    return pl.pallas_call(
        _memkv_kernel,
        grid=(L, B),
        in_specs=[pl.BlockSpec((None, M, D), lambda l, b: (b, 0, 0)),
                  pl.BlockSpec((1, D), lambda l, b: (0, 0)),
                  pl.BlockSpec((None, D, 2 * MEM_WIDTH), lambda l, b: (l, 0, 0))],
        out_specs=(ospec, ospec),
        out_shape=(out, out),
        compiler_params=_cparams(("parallel", "parallel")),
        name="memkv",
    )(mem, g_mem, w)


def _mixer_residual(x_ref, mix_ref, qm_ref, mk_ref, mv_ref, w_ref):
    mw = mix_ref.shape[-1]
    n_sub = 2 * MEM_WIDTH // LANES
    logits = [lax.dot_general(qm_ref[:, (u // 2) * LANES:(u // 2 + 1) * LANES], mk_ref[:, u * LANES:(u + 1) * LANES],
                              _NT, preferred_element_type=F32) for u in range(n_sub)]
    y = x_ref[...] + jnp.dot(mix_ref[...], w_ref[0:mw, :], preferred_element_type=F32)
    for t in range(n_sub // 2):
        o_t = None
        for u in (2 * t, 2 * t + 1):
            p = jnp.exp2(logits[u] - jnp.max(logits[u], axis=-1, keepdims=True))
            l = jnp.sum(p, axis=-1, keepdims=True)
            o_s = jnp.dot(p.astype(BF16), mv_ref[:, u * LANES:(u + 1) * LANES], preferred_element_type=F32) / l
            o_t = o_s if o_t is None else o_t + o_s
        y = y + jnp.dot(o_t.astype(BF16), w_ref[mw + t * LANES:mw + (t + 1) * LANES, :],
                        preferred_element_type=F32)
    return y


def _out_kernel(x_ref, mix_ref, qm_ref, mk_ref, mv_ref, w_ref, o_ref):
    o_ref[...] = _mixer_residual(x_ref, mix_ref, qm_ref, mk_ref, mv_ref, w_ref)


def _out_proj(x, mix, qm, mk, mv, w, tm):
    B, S, D = x.shape
    row = lambda width: pl.BlockSpec((None, tm, width), lambda b, i: (b, i, 0))
    mem = pl.BlockSpec((None, mk.shape[1], 2 * MEM_WIDTH), lambda b, i: (b, 0, 0))
    return pl.pallas_call(
        _out_kernel,
        grid=(B, S // tm),
        in_specs=[row(D), row(mix.shape[-1]), row(MEM_WIDTH), mem, mem,
                  pl.BlockSpec(w.shape, lambda b, i: (0, 0))],
        out_specs=row(D),
        out_shape=jax.ShapeDtypeStruct((B, S, D), F32),
        compiler_params=_cparams(("parallel", "parallel")),
        name="out_proj",
    )(x, mix, qm, mk, mv, w)


def _ffn_kernel(x_ref, g_ref, wup_ref, cw_ref, cb_ref, wdn_ref, gf_ref, o_ref, carry_ref, *, tm, cw, final):
    dff = wdn_ref.shape[0]
    first = pl.program_id(1) == 0
    x = x_ref[...]
    hb = _rms(x, g_ref[...]).astype(BF16)
    row = lax.broadcasted_iota(I32, (tm, cw), 0)
    acc = x

    def up(c0):
        return (jnp.dot(hb, wup_ref[:, c0:c0 + cw], preferred_element_type=F32),
                jnp.dot(hb, wup_ref[:, dff + c0:dff + c0 + cw], preferred_element_type=F32))

    nxt = up(0)
    for c0 in range(0, dff, cw):
        a, b = nxt
        if c0 + cw < dff:
            nxt = up(c0 + cw)
        prev = jnp.where(first, 0.0, carry_ref[:, c0:c0 + cw])
        p1, p2 = prev[7:8, :], prev[6:7, :]
        a1 = jnp.where(row == 0, p1, pltpu.roll(a, 1, 0))
        a2 = jnp.where(row == 0, p2, jnp.where(row == 1, p1, pltpu.roll(a, 2, 0)))
        carry_ref[:, c0:c0 + cw] = a[tm - 8:tm, :]
        w = cw_ref[:, c0:c0 + cw]
        conv = w[0:1, :] * a2 + w[1:2, :] * a1 + w[2:3, :] * a + cb_ref[:, c0:c0 + cw]
        gated = conv / (1.0 + jnp.exp(-conv)) * b
        acc = acc + jnp.dot(gated.astype(BF16), wdn_ref[c0:c0 + cw, :], preferred_element_type=F32)
    if final:
        acc = _rms(acc, gf_ref[...])
    o_ref[...] = acc


def _ffn(x, g, wup, cw, cb, wdn, gf, tm, final):
    B, S, D = x.shape
    dff = wdn.shape[0]
    row = pl.BlockSpec((None, tm, D), lambda b, i: (b, i, 0))
    const = lambda shape: pl.BlockSpec(shape, lambda b, i: (0, 0), pipeline_mode=pl.Buffered(1))
    return pl.pallas_call(
        functools.partial(_ffn_kernel, tm=tm, cw=2 * LANES, final=final),
        grid=(B, S // tm),
        in_specs=[row, const((1, D)), const(wup.shape), const(cw.shape), const((1, dff)), const(wdn.shape),
                  const((1, D))],
        out_specs=row,
        out_shape=jax.ShapeDtypeStruct((B, S, D), F32),
        scratch_shapes=[pltpu.VMEM((8, dff), F32)],
        compiler_params=_cparams(("arbitrary", "arbitrary")),
        name="ffn",
    )(x, g, wup, cw, cb, wdn, gf)


def _b_subheads():
    return tuple(((2 * (j // 2), 2 * (j // 2), 2 * j), (2 * (j // 2) + 1, 2 * (j // 2) + 1, 2 * j + 1))
                 for j in range(B_HEADS // 2))


def kernel(x, mem, positions, g_mix, g_ffn, g_mem, g_final, w_mem_kv, a_w_in, a_kv_norm, a_w_uk, a_w_uv, a_w_out,
           b_w_in, b_sinks, b_w_out, c_w_in, c_w_out, f_w_up, f_conv_w, f_conv_b, f_w_down):
    B, S, D = x.shape
    depth = g_mix.shape[0]
    tm = min(512, S)
    tab32 = _rope_table(positions, A_QK_DIM, A_ROPE_DIM)
    tab16 = _rope_table(positions, B_HEAD_DIM, B_HEAD_DIM // 4)
    mem_k, mem_v = _memkv(mem, g_mem.reshape(1, D), w_mem_kv.astype(BF16))
    conv_w = jnp.pad(f_conv_w, ((0, 0), (0, 8 - CONV_WIDTH), (0, 0)))
    for i in range(depth):
        kind, j = i % 3, i // 3
        g = g_mix[i].reshape(1, D)
        if kind == 0:
            q, kx, ct, qi, ki, wi, qm = _proj_a(x, g, _prep_a_w_in(a_w_in[j]), a_kv_norm[j].reshape(1, -1),
                                                tab32, tab16, tm)
            wuk = jnp.pad(jnp.transpose(a_w_uk[j], (1, 2, 0)), ((0, 0), (A_ROPE_DIM, 0), (0, 0))).astype(BF16)
            wuv = jnp.transpose(a_w_uv[j], (1, 0, 2)).astype(BF16)
            mix = _dsa(q, qi, wi, kx, ct, ki, wuk, wuv)
            w_out = a_w_out[j]
        elif kind == 1:
            nq, nkv = B_HEADS * B_HEAD_DIM, 4 * B_KV_HEADS * B_HEAD_DIM
            q, k, v, qm = _proj_qkv(x, g, _prep_b_w_in(b_w_in[j]), tab16, tm, B_HEAD_DIM, B_HEAD_DIM // 4,
                                    nq, nkv, nkv, False)
            mix = _swa(q, k, v, b_sinks[j], subheads=_b_subheads(), max_dist=B_WINDOW - 1)
            w_out = b_w_out[j]
        else:
            nq = C_HEADS * C_HEAD_DIM
            q, k, v, qm = _proj_qkv(x, g, c_w_in[j].astype(BF16), tab32, tm, C_HEAD_DIM, C_HEAD_DIM // 4,
                                    nq, nq, nq, True)
            mix = _wattn(q, k, v, C_BRANCHES)
            w_out = c_w_out[j]
        x = _out_proj(x, mix, qm, mem_k[i], mem_v[i], w_out.astype(BF16), tm)
        x = _ffn(x, g_ffn[i].reshape(1, D), f_w_up[i].astype(BF16), conv_w[i], f_conv_b[i].reshape(1, -1),
                 f_w_down[i].astype(BF16), g_final.reshape(1, D), min(256, S), i == depth - 1)
    return x
```

```python
import functools

import jax
import jax.numpy as jnp
import numpy as np
from jax import lax
from jax.experimental import pallas as pl
from jax.experimental.pallas import tpu as pltpu

F32 = jnp.float32
BF16 = jnp.bfloat16
I32 = jnp.int32

LANES = 128
BLOCK = 128
ROPE_THETA = 500000.0
EPS = 1e-6
NEG_INF = -1e30
LOG2E = 1.4426950408889634
INT_MIN = -(2**31)

A_HEADS = 8
A_QK_DIM = 128
A_ROPE_DIM = 32
A_KV_RANK = 256
IDX_HEADS = 16
IDX_DIM = 64
TOPK_MAX = 256
B_HEADS = 16
B_KV_HEADS = 4
B_HEAD_DIM = 64
B_WINDOW = 128
C_HEADS = 8
C_HEAD_DIM = 128
C_BRANCHES = ((128, 1), (512, 4), (2048, 16))
MEM_HEADS = 4
MEM_HEAD_DIM = 64
MEM_WIDTH = MEM_HEADS * MEM_HEAD_DIM
CONV_WIDTH = 3
KEY_CHUNK = 256

VMEM_LIMIT = 56 * 1024 * 1024

_NT = (((1,), (1,)), ((), ()))


def _cparams(sem):
    return pltpu.CompilerParams(dimension_semantics=sem, vmem_limit_bytes=VMEM_LIMIT)


def _rms(x, g):
    return x * lax.rsqrt(jnp.mean(x * x, axis=-1, keepdims=True) + EPS) * g


def _loop_two_per_trip(lo, hi, body, carry):
    trips = (hi - lo) // 2

    def double(t, c):
        return body(lo + 2 * t + 1, body(lo + 2 * t, c))

    carry = lax.fori_loop(0, trips, double, carry)
    return lax.fori_loop(lo + 2 * trips, hi, body, carry)


def _bit_transpose32(words):
    a = list(words)
    j, m = 16, 0x0000FFFF
    while j:
        mask = jnp.int32(m - (1 << 32) if m >= (1 << 31) else m)
        for k in range(32):
            if k & j == 0:
                t = (a[k] ^ lax.shift_right_logical(a[k + j], j)) & mask
                a[k] = a[k] ^ t
                a[k + j] = a[k + j] ^ lax.shift_left(t, j)
        j >>= 1
        m = (m ^ (m << j)) & 0xFFFFFFFF
    return a


def _rope_tile(t, tab, half):
    c, sa, sb = tab[:, 0:LANES], tab[:, LANES:2 * LANES], tab[:, 2 * LANES:3 * LANES]
    return t * c + pltpu.roll(t, half, 1) * sa + pltpu.roll(t, LANES - half, 1) * sb


def _rope_table(positions, head_dim, rot):
    half = rot // 2
    inv = ROPE_THETA ** (-jnp.arange(0, rot, 2, dtype=F32) / rot)
    ang = positions.astype(F32)[..., None] * inv
    cs = jnp.concatenate([jnp.cos(ang), jnp.sin(ang)], axis=-1)
    sel = np.zeros((rot, 3 * LANES), np.float32)
    one = np.zeros((3 * LANES,), np.float32)
    for l in range(LANES):
        j = l % head_dim
        if j < half:
            sel[j, l] = 1.0
            sel[half + j, 2 * LANES + l] = -1.0
        elif j < rot:
            sel[j - half, l] = 1.0
            sel[j, LANES + l] = 1.0
        else:
            one[l] = 1.0
    return jnp.dot(cs, jnp.asarray(sel), precision=lax.Precision.HIGHEST) + jnp.asarray(one)


def _staggered(mm, jobs):
    nxt = mm(jobs[0][0], jobs[0][1])
    for n, (_, _, epilogue) in enumerate(jobs):
        y = nxt
        if n + 1 < len(jobs):
            nxt = mm(jobs[n + 1][0], jobs[n + 1][1])
        epilogue(y)


def _proj_a_kernel(x_ref, g_ref, w_ref, kvn_ref, t32_ref, t16_ref,
                   q_ref, kx_ref, ct_ref, qi_ref, ki_ref, wi_ref, qm_ref, *, tm):
    hb = _rms(x_ref[...], g_ref[...]).astype(BF16)
    t32 = t32_ref[...]
    t16 = t16_ref[...]

    def mm(a, b):
        return jnp.dot(hb, w_ref[:, a:b], preferred_element_type=F32)

    def roped(ref, col, tab, half, scale):
        def epilogue(y):
            for u in range(y.shape[1] // LANES):
                t = _rope_tile(y[:, u * LANES:(u + 1) * LANES], tab, half)
                if scale != 1.0:
                    t = t * scale
                ref[:, col + u * LANES:col + (u + 1) * LANES] = t.astype(BF16)
        return epilogue

    def latent(y):
        c = _rms(y, kvn_ref[...])
        kx_ref[:, 0:A_KV_RANK] = c.astype(BF16)
        for u in range(tm // KEY_CHUNK):
            ct_ref[u] = c[u * KEY_CHUNK:(u + 1) * KEY_CHUNK, :].T.astype(BF16)

    def index_keys(y):
        for u in range(2):
            t = _rope_tile(y[:, u * LANES:(u + 1) * LANES], t16, IDX_DIM // 8).astype(BF16)
            for v in range(tm // KEY_CHUNK):
                ki_ref[v, u * KEY_CHUNK:(u + 1) * KEY_CHUNK, :] = t[v * KEY_CHUNK:(v + 1) * KEY_CHUNK, :]

    def index_weights(y):
        wi_ref[...] = y * (IDX_HEADS * IDX_DIM) ** -0.5

    def mem_query(y):
        qm_ref[...] = (y * (MEM_HEAD_DIM ** -0.5 * LOG2E)).astype(BF16)

    jobs, o = [], 0
    for j in range(0, A_HEADS * A_QK_DIM, 2 * LANES):
        jobs.append((o + j, o + j + 2 * LANES, roped(q_ref, j, t32, A_ROPE_DIM // 2, A_QK_DIM ** -0.5 * LOG2E)))
    o += A_HEADS * A_QK_DIM
    jobs.append((o, o + A_KV_RANK, latent))
    o += A_KV_RANK
    jobs.append((o, o + LANES, roped(kx_ref, A_KV_RANK, t32, A_ROPE_DIM // 2, 1.0)))
    o += LANES
    for j in range(0, IDX_HEADS * IDX_DIM, 2 * LANES):
        jobs.append((o + j, o + j + 2 * LANES, roped(qi_ref, j, t16, IDX_DIM // 8, 1.0)))
    o += IDX_HEADS * IDX_DIM
    jobs.append((o, o + 2 * LANES, index_keys))
    o += 2 * LANES
    jobs.append((o, o + LANES, index_weights))
    o += LANES
    jobs.append((o, o + MEM_WIDTH, mem_query))
    _staggered(mm, jobs)


def _prep_a_w_in(w):
    d = w.shape[0]
    sizes = (A_HEADS * A_QK_DIM, A_KV_RANK, A_ROPE_DIM, IDX_HEADS * IDX_DIM, IDX_DIM, IDX_HEADS, MEM_WIDTH)
    offs = [0]
    for s in sizes:
        offs.append(offs[-1] + s)
    q, ckv, kr, qi, ki, wi, qm = [w[:, offs[i]:offs[i + 1]] for i in range(len(sizes))]
    z = lambda n: jnp.zeros((d, n), w.dtype)
    return jnp.concatenate([
        q, ckv, kr, z(LANES - A_ROPE_DIM), qi,
        ki, z(LANES - IDX_DIM), z(LANES - IDX_DIM), ki,
        wi, z(LANES - IDX_HEADS), qm], axis=1).astype(BF16)


def _proj_a(x, g, w, kvn, t32, t16, tm):
    B, S, D = x.shape
    n = w.shape[1]
    row = lambda width: pl.BlockSpec((None, tm, width), lambda b, i: (b, i, 0))
    const = lambda shape: pl.BlockSpec(shape, lambda b, i: (0,) * len(shape))
    out_shape = (
        jax.ShapeDtypeStruct((B, S, A_HEADS * A_QK_DIM), BF16),
        jax.ShapeDtypeStruct((B, S, A_KV_RANK + LANES), BF16),
        jax.ShapeDtypeStruct((B, S // KEY_CHUNK, A_KV_RANK, KEY_CHUNK), BF16),
        jax.ShapeDtypeStruct((B, S, IDX_HEADS * IDX_DIM), BF16),
        jax.ShapeDtypeStruct((B, S // KEY_CHUNK, 2 * KEY_CHUNK, LANES), BF16),
        jax.ShapeDtypeStruct((B, S, LANES), F32),
        jax.ShapeDtypeStruct((B, S, MEM_WIDTH), BF16),
    )
    out_specs = (
        row(A_HEADS * A_QK_DIM), row(A_KV_RANK + LANES),
        pl.BlockSpec((None, tm // KEY_CHUNK, A_KV_RANK, KEY_CHUNK), lambda b, i: (b, i, 0, 0)),
        row(IDX_HEADS * IDX_DIM),
        pl.BlockSpec((None, tm // KEY_CHUNK, 2 * KEY_CHUNK, LANES), lambda b, i: (b, i, 0, 0)),
        row(LANES), row(MEM_WIDTH),
    )
    return pl.pallas_call(
        functools.partial(_proj_a_kernel, tm=tm),
        grid=(B, S // tm),
        in_specs=[row(D), const((1, D)), const((D, n)), const((1, A_KV_RANK)), row(3 * LANES), row(3 * LANES)],
        out_specs=out_specs,
        out_shape=out_shape,
        compiler_params=_cparams(("parallel", "parallel")),
        name="proj_a",
    )(x, g, w, kvn, t32, t16)


def _proj_qkv_kernel(x_ref, g_ref, w_ref, tab_ref, q_ref, k_ref, v_ref, qm_ref, *,
                     head_dim, rot, nq, nk, nv, v_transposed):
    hb = _rms(x_ref[...], g_ref[...]).astype(BF16)
    tm = hb.shape[0]
    tab = tab_ref[...]

    def mm(a, b):
        return jnp.dot(hb, w_ref[:, a:b], preferred_element_type=F32)

    def roped(ref, col, scale):
        def epilogue(y):
            for u in range(2):
                t = _rope_tile(y[:, u * LANES:(u + 1) * LANES], tab, rot // 2)
                if scale != 1.0:
                    t = t * scale
                ref[:, col + u * LANES:col + (u + 1) * LANES] = t.astype(BF16)
        return epilogue

    def value(col):
        def epilogue(y):
            if v_transposed:
                for u in range(tm // KEY_CHUNK):
                    v_ref[u, col:col + 2 * LANES, :] = y[u * KEY_CHUNK:(u + 1) * KEY_CHUNK, :].T.astype(BF16)
            else:
                v_ref[:, col:col + 2 * LANES] = y.astype(BF16)
        return epilogue

    def mem_query(y):
        qm_ref[...] = (y * (MEM_HEAD_DIM ** -0.5 * LOG2E)).astype(BF16)

    jobs = [(j, j + 2 * LANES, roped(q_ref, j, head_dim ** -0.5 * LOG2E)) for j in range(0, nq, 2 * LANES)]
    jobs += [(nq + j, nq + j + 2 * LANES, roped(k_ref, j, 1.0)) for j in range(0, nk, 2 * LANES)]
    jobs += [(nq + nk + j, nq + nk + j + 2 * LANES, value(j)) for j in range(0, nv, 2 * LANES)]
    jobs.append((nq + nk + nv, nq + nk + nv + MEM_WIDTH, mem_query))
    _staggered(mm, jobs)


def _proj_qkv(x, g, w, tab, tm, head_dim, rot, nq, nk, nv, v_transposed):
    B, S, D = x.shape
    n = w.shape[1]
    row = lambda width: pl.BlockSpec((None, tm, width), lambda b, i: (b, i, 0))
    const = lambda shape: pl.BlockSpec(shape, lambda b, i: (0,) * len(shape))
    if v_transposed:
        v_spec = pl.BlockSpec((None, tm // KEY_CHUNK, nv, KEY_CHUNK), lambda b, i: (b, i, 0, 0))
        v_shape = jax.ShapeDtypeStruct((B, S // KEY_CHUNK, nv, KEY_CHUNK), BF16)
    else:
        v_spec, v_shape = row(nv), jax.ShapeDtypeStruct((B, S, nv), BF16)
    return pl.pallas_call(
        functools.partial(_proj_qkv_kernel, head_dim=head_dim, rot=rot, nq=nq, nk=nk, nv=nv,
                          v_transposed=v_transposed),
        grid=(B, S // tm),
        in_specs=[row(D), const((1, D)), const((D, n)), row(3 * LANES)],
        out_specs=(row(nq), row(nk), v_spec, row(MEM_WIDTH)),
        out_shape=(jax.ShapeDtypeStruct((B, S, nq), BF16), jax.ShapeDtypeStruct((B, S, nk), BF16), v_shape,
                   jax.ShapeDtypeStruct((B, S, MEM_WIDTH), BF16)),
        compiler_params=_cparams(("parallel", "parallel")),
        name="proj_qkv",
    )(x, g, w, tab)


def _prep_b_w_in(w):
    d = w.shape[0]
    nq, nkv = B_HEADS * B_HEAD_DIM, B_KV_HEADS * B_HEAD_DIM
    q, k, v, qm = w[:, :nq], w[:, nq:nq + nkv], w[:, nq + nkv:nq + 2 * nkv], w[:, nq + 2 * nkv:]
    z = jnp.zeros((d, B_HEAD_DIM), w.dtype)

    def spread(t):
        cols = []
        for h in range(B_KV_HEADS):
            th = t[:, h * B_HEAD_DIM:(h + 1) * B_HEAD_DIM]
            cols += [th, z, z, th]
        return jnp.concatenate(cols, axis=1)

    return jnp.concatenate([q, spread(k), spread(v), qm], axis=1).astype(BF16)


def _dsa_kernel(q_ref, qi_ref, wi_ref, kx_ref, ct_ref, ki_ref, wuk_ref, wuv_ref, o_ref,
                keys_ref, planes_ref, qext_ref, qi2_ref, wrow_ref, acc_ref, sa_ref, sb_ref, *, topk):
    kc = KEY_CHUNK
    hw = A_HEADS * LANES
    gw = 2 * LANES
    i = pl.program_id(1)
    nch = (i * BLOCK + BLOCK + kc - 1) // kc
    lane = lax.broadcasted_iota(I32, (BLOCK, LANES), 1)

    for h in range(A_HEADS):
        rows = slice(h * BLOCK, (h + 1) * BLOCK)
        qh = q_ref[:, h * LANES:(h + 1) * LANES]
        qext_ref[rows, 0:A_KV_RANK] = jnp.dot(qh, wuk_ref[h], preferred_element_type=F32).astype(BF16)
        qext_ref[rows, A_KV_RANK:A_KV_RANK + LANES] = jnp.where(lane < A_ROPE_DIM, qh.astype(F32), 0.0).astype(BF16)
        qi2_ref[rows, :] = qi_ref[:, h * LANES:(h + 1) * LANES]
    w_t = wi_ref[...].T
    for j in range(IDX_HEADS // 2):
        wrow_ref[0:1, j * LANES:(j + 1) * LANES] = w_t[2 * j:2 * j + 1, :]
        wrow_ref[1:2, j * LANES:(j + 1) * LANES] = w_t[2 * j + 1:2 * j + 2, :]

    qpos = i * BLOCK + lax.broadcasted_iota(I32, (kc, BLOCK), 1)
    krow = lax.broadcasted_iota(I32, (kc, BLOCK), 0)

    def score_chunk(c):
        off = pl.multiple_of(c * kc, kc)
        kk = ki_ref[c]
        acc = None
        for g in range(0, hw, gw):
            d = lax.dot_general(kk, qi2_ref[g:g + gw, :], _NT, preferred_element_type=F32)
            t = (jnp.maximum(d[0:kc], 0.0) * wrow_ref[0:1, g:g + gw]
                 + jnp.maximum(d[kc:2 * kc], 0.0) * wrow_ref[1:2, g:g + gw])
            for u in range(0, gw, LANES):
                acc = t[:, u:u + LANES] if acc is None else acc + t[:, u:u + LANES]
        bits = pltpu.bitcast(acc, I32)
        key = jnp.where(bits < 0, bits ^ 0x7FFFFFFF, bits)
        key = jnp.where(off + krow <= qpos, key, INT_MIN)
        keys_ref[pl.ds(off, kc), :] = key
        planes = _bit_transpose32([(key[8 * r:8 * r + 8, :] ^ INT_MIN) for r in range(kc // 8)])
        for p in range(32):
            planes_ref[c, p] = planes[p]

    npairs = (nch + 1) // 2

    @pl.when(i == 0)
    def _():
        planes_ref[...] = jnp.zeros(planes_ref.shape, I32)

    def score_body(j, carry):
        score_chunk(2 * j)
        score_chunk(2 * j + 1)
        return carry

    _loop_two_per_trip(0, npairs, score_body, 0)

    n_chunks = planes_ref.shape[0]

    def bit_body(p, carry):
        alive, need, res = carry
        ones = [a & planes_ref[c, p] for c, a in enumerate(alive)]
        cnt = ones[0] * 0
        for o in ones:
            cnt = cnt + lax.population_count(o)
        cnt = jnp.sum(cnt, axis=0, keepdims=True)
        take = cnt >= need
        alive = [jnp.where(take, o, a ^ o) for o, a in zip(ones, alive)]
        need = jnp.where(take, need, need - cnt)
        res = jnp.where(take, res | lax.shift_left(jnp.int32(1), 31 - p), res)
        return alive, need, res

    alive0 = [jnp.where(c < 2 * npairs, jnp.full((8, BLOCK), -1, I32), 0) for c in range(n_chunks)]
    _, _, res = lax.fori_loop(0, 32, bit_body,
                              (alive0, jnp.full((1, BLOCK), topk, I32), jnp.zeros((1, BLOCK), I32)))
    thr = jnp.maximum(res ^ INT_MIN, INT_MIN + 1)

    acc_ref[...] = jnp.zeros(acc_ref.shape, F32)

    def logits(c, s_ref):
        off = pl.multiple_of(c * kc, kc)
        bias = jnp.where(keys_ref[pl.ds(off, kc), :] >= thr, 0.0, NEG_INF)
        bias = jnp.concatenate([bias] * (gw // LANES), axis=1)
        kx = kx_ref[pl.ds(off, kc), :]
        for g in range(0, hw, gw):
            t = lax.dot_general(kx, qext_ref[g:g + gw, :], _NT, preferred_element_type=F32) + bias
            s_ref[0:kc, g:g + gw] = t
            s_ref[kc:kc + 1, g:g + gw] = jnp.max(t, axis=0, keepdims=True)

    def consume(s_ref, c, carry):
        m_prev, l_prev = carry
        ct = ct_ref[c]
        m_out, l_out = [], []
        for g in range(0, hw, gw):
            s = s_ref[0:kc, g:g + gw]
            m_new = jnp.maximum(m_prev[:, g:g + gw], s_ref[kc:kc + 1, g:g + gw])
            alpha = jnp.exp2(m_prev[:, g:g + gw] - m_new)
            p = jnp.exp2(s - m_new)
            l_out.append(alpha * l_prev[:, g:g + gw] + jnp.sum(p, axis=0, keepdims=True))
            m_out.append(m_new)
            acc_ref[:, g:g + gw] = alpha * acc_ref[:, g:g + gw] + jnp.dot(
                ct, p.astype(BF16), preferred_element_type=F32)
        return jnp.concatenate(m_out, axis=1), jnp.concatenate(l_out, axis=1)

    logits(0, sa_ref)

    def pair_body(j, carry):
        c0 = 2 * j
        logits(c0 + 1, sb_ref)
        carry = consume(sa_ref, c0, carry)
        logits(jnp.minimum(c0 + 2, 2 * npairs - 1), sa_ref)
        return consume(sb_ref, c0 + 1, carry)

    _, l_fin = _loop_two_per_trip(0, npairs, pair_body,
                                  (jnp.full((1, hw), NEG_INF, F32), jnp.zeros((1, hw), F32)))

    inv_l = 1.0 / l_fin
    for h in range(A_HEADS):
        cols = slice(h * LANES, (h + 1) * LANES)
        o_lat = (acc_ref[:, cols] * inv_l[:, cols]).T.astype(BF16)
        o_ref[:, cols] = jnp.dot(o_lat, wuv_ref[h], preferred_element_type=F32).astype(BF16)


def _dsa(q, qi, wi, kx, ct, ki, wuk, wuv):
    B, S, _ = q.shape
    topk = min(TOPK_MAX, S // 4)
    blk = lambda width: pl.BlockSpec((None, BLOCK, width), lambda b, i: (b, i, 0))
    seq = lambda width: pl.BlockSpec((None, S, width), lambda b, i: (b, 0, 0))
    const3 = lambda shape: pl.BlockSpec(shape, lambda b, i: (0, 0, 0))
    return pl.pallas_call(
        functools.partial(_dsa_kernel, topk=topk),
        grid=(B, S // BLOCK),
        in_specs=[blk(A_HEADS * A_QK_DIM), blk(IDX_HEADS * IDX_DIM), blk(LANES),
                  seq(A_KV_RANK + LANES),
                  pl.BlockSpec((None, S // KEY_CHUNK, A_KV_RANK, KEY_CHUNK), lambda b, i: (b, 0, 0, 0)),
                  pl.BlockSpec((None, S // KEY_CHUNK, 2 * KEY_CHUNK, LANES), lambda b, i: (b, 0, 0, 0)),
                  const3(wuk.shape), const3(wuv.shape)],
        out_specs=blk(A_HEADS * LANES),
        out_shape=jax.ShapeDtypeStruct((B, S, A_HEADS * LANES), BF16),
        scratch_shapes=[
            pltpu.VMEM((S, BLOCK), I32),
            pltpu.VMEM((S // KEY_CHUNK, 32, 8, BLOCK), I32),
            pltpu.VMEM((A_HEADS * BLOCK, A_KV_RANK + LANES), BF16),
            pltpu.VMEM((A_HEADS * BLOCK, LANES), BF16),
            pltpu.VMEM((8, A_HEADS * LANES), F32),
            pltpu.VMEM((A_KV_RANK, A_HEADS * LANES), F32),
            pltpu.VMEM((KEY_CHUNK + 8, A_HEADS * LANES), F32),
            pltpu.VMEM((KEY_CHUNK + 8, A_HEADS * LANES), F32),
        ],
        compiler_params=_cparams(("parallel", "arbitrary")),
        name="dsa",
    )(q, qi, wi, kx, ct, ki, wuk, wuv)


def _window_bias(branches, nrel):
    rel = jnp.arange(nrel + 1, dtype=I32)[:, None, None]
    krow = jnp.arange(KEY_CHUNK, dtype=I32)[None, :, None]
    qcol = jnp.arange(KEY_CHUNK, dtype=I32)[None, None, :]
    dist = KEY_CHUNK * (nrel - 1 - rel) + qcol - krow
    mult = sum(((dist >= 0) & (dist <= window) & (dist % dil == 0)).astype(F32) for window, dil in branches)
    return jnp.where(mult > 0, jnp.log2(jnp.maximum(mult, 1.0)), NEG_INF)


def _wattn_kernel(q_ref, k_ref, vt_ref, bias_ref, o_ref, acc_ref, sa_ref, sb_ref, p_ref, *, nh, nrel):
    kc = qb = KEY_CHUNK
    last = k_ref.shape[0] // kc - 1
    top = pl.program_id(1)
    c_lo = jnp.maximum(top - (nrel - 1), 0)
    npairs = (top + 2 - c_lo) // 2

    acc_ref[...] = jnp.zeros(acc_ref.shape, F32)

    def logits(c, s_ref):
        off = pl.multiple_of(jnp.minimum(c, last) * kc, kc)
        bias = bias_ref[c - top + (nrel - 1)]
        for h in range(nh):
            cols = slice(h * LANES, (h + 1) * LANES)
            t = lax.dot_general(k_ref[pl.ds(off, kc), cols], q_ref[:, cols], _NT,
                                preferred_element_type=F32) + bias
            s_ref[0:kc, h * qb:(h + 1) * qb] = t
            s_ref[kc:kc + 1, h * qb:(h + 1) * qb] = jnp.max(t, axis=0, keepdims=True)

    def consume(s_ref, c, carry):
        m_prev, l_prev = carry
        m_new = jnp.maximum(m_prev, s_ref[kc:kc + 1, :])
        alpha = jnp.exp2(m_prev - m_new)
        p_ref[...] = jnp.exp2(s_ref[0:kc, :] - m_new).astype(BF16)
        l_new = alpha * l_prev + jnp.dot(jnp.ones((16, kc), BF16), p_ref[...], preferred_element_type=F32)[0:1, :]
        cd = jnp.minimum(c, last)
        for h in range(nh):
            cols = slice(h * qb, (h + 1) * qb)
            acc_ref[:, cols] = alpha[:, cols] * acc_ref[:, cols] + jnp.dot(
                vt_ref[cd, h * LANES:(h + 1) * LANES, :], p_ref[:, cols], preferred_element_type=F32)
        return m_new, l_new

    logits(c_lo, sa_ref)

    def pair_body(j, carry):
        c0 = c_lo + 2 * j
        logits(c0 + 1, sb_ref)
        carry = consume(sa_ref, c0, carry)
        logits(jnp.minimum(c0 + 2, c_lo + 2 * npairs - 1), sa_ref)
        return consume(sb_ref, c0 + 1, carry)

    _, l_fin = _loop_two_per_trip(0, npairs, pair_body,
                                  (jnp.full((1, nh * qb), NEG_INF, F32), jnp.zeros((1, nh * qb), F32)))

    inv_l = 1.0 / l_fin
    for h in range(nh):
        cols = slice(h * qb, (h + 1) * qb)
        o_ref[:, h * LANES:(h + 1) * LANES] = (acc_ref[:, cols] * inv_l[:, cols]).T.astype(BF16)


def _swa_kernel(sink_ref, q_ref, kp_ref, kc_ref, vp_ref, vc_ref, o_ref, *, subheads, max_dist):
    n = pl.program_id(1)
    qi = lax.broadcasted_iota(I32, (BLOCK, 2 * BLOCK), 0)
    kj = lax.broadcasted_iota(I32, (BLOCK, 2 * BLOCK), 1)
    dist = BLOCK + qi - kj
    mask = (dist >= 0) & (dist <= max_dist) & ((kj >= BLOCK) | (n > 0))
    for j, subs in enumerate(subheads):
        qt = q_ref[:, j * LANES:(j + 1) * LANES]
        o_t = None
        for kt, vt, hidx in subs:
            kk = jnp.concatenate([kp_ref[:, kt * LANES:(kt + 1) * LANES],
                                  kc_ref[:, kt * LANES:(kt + 1) * LANES]], axis=0)
            vv = jnp.concatenate([vp_ref[:, vt * LANES:(vt + 1) * LANES],
                                  vc_ref[:, vt * LANES:(vt + 1) * LANES]], axis=0)
            s = lax.dot_general(qt, kk, _NT, preferred_element_type=F32)
            s = jnp.where(mask, s, NEG_INF)
            sk = sink_ref[hidx] * LOG2E
            m = jnp.maximum(jnp.max(s, axis=-1, keepdims=True), sk)
            p = jnp.exp2(s - m)
            l = jnp.sum(p, axis=-1, keepdims=True) + jnp.exp2(sk - m)
            o_s = jnp.dot(p.astype(BF16), vv, preferred_element_type=F32) / l
            o_t = o_s if o_t is None else o_t + o_s
        o_ref[:, j * LANES:(j + 1) * LANES] = o_t.astype(BF16)


def _swa(q, k, v, sinks, *, subheads, max_dist):
    B, S, wq = q.shape
    wk, wv = k.shape[-1], v.shape[-1]
    cur = lambda width: pl.BlockSpec((None, BLOCK, width), lambda b, i: (b, i, 0))
    prv = lambda width: pl.BlockSpec((None, BLOCK, width), lambda b, i: (b, jnp.maximum(i - 1, 0), 0))
    return pl.pallas_call(
        functools.partial(_swa_kernel, subheads=subheads, max_dist=max_dist),
        grid=(B, S // BLOCK),
        in_specs=[pl.BlockSpec(memory_space=pltpu.SMEM), cur(wq), prv(wk), cur(wk), prv(wv), cur(wv)],
        out_specs=cur(wq),
        out_shape=jax.ShapeDtypeStruct((B, S, wq), BF16),
        compiler_params=_cparams(("parallel", "arbitrary")),
        name="swa",
    )(sinks, q, k, k, v, v)


def _wattn(q, k, vt, branches):
    B, S, wq = q.shape
    widest = max(w for w, _ in branches)
    nrel = min(-(-widest // KEY_CHUNK) + 1, S // KEY_CHUNK)
    bias = _window_bias(branches, nrel)
    nh = wq // LANES
    blk = lambda width: pl.BlockSpec((None, KEY_CHUNK, width), lambda b, i: (b, i, 0))
    return pl.pallas_call(
        functools.partial(_wattn_kernel, nh=nh, nrel=nrel),
        grid=(B, S // KEY_CHUNK),
        in_specs=[blk(wq),
                  pl.BlockSpec((None, S, wq), lambda b, i: (b, 0, 0)),
                  pl.BlockSpec((None,) + vt.shape[1:], lambda b, i: (b, 0, 0, 0)),
                  pl.BlockSpec(bias.shape, lambda b, i: (0, 0, 0))],
        out_specs=blk(wq),
        out_shape=jax.ShapeDtypeStruct((B, S, wq), BF16),
        scratch_shapes=[pltpu.VMEM((LANES, nh * KEY_CHUNK), F32),
                        pltpu.VMEM((KEY_CHUNK + 8, nh * KEY_CHUNK), F32),
                        pltpu.VMEM((KEY_CHUNK + 8, nh * KEY_CHUNK), F32),
                        pltpu.VMEM((KEY_CHUNK, nh * KEY_CHUNK), BF16)],
        compiler_params=_cparams(("parallel", "arbitrary")),
        name="wattn",
    )(q, k, vt, bias)


def _memkv_kernel(mem_ref, g_ref, w_ref, k_ref, v_ref):
    hb = _rms(mem_ref[...], g_ref[...]).astype(BF16)
    y = jnp.dot(hb, w_ref[...], preferred_element_type=F32)
    lane = lax.broadcasted_iota(I32, (y.shape[0], LANES), 1)
    for t in range(MEM_WIDTH // LANES):
        for out_ref, base in ((k_ref, 0), (v_ref, MEM_WIDTH)):
            tile = y[:, base + t * LANES:base + (t + 1) * LANES]
            out_ref[:, (2 * t) * LANES:(2 * t + 1) * LANES] = jnp.where(lane < MEM_HEAD_DIM, tile, 0.0).astype(BF16)
            out_ref[:, (2 * t + 1) * LANES:(2 * t + 2) * LANES] = jnp.where(lane >= MEM_HEAD_DIM, tile, 0.0).astype(BF16)


def _memkv(mem, g_mem, w):
    B, M, D = mem.shape
    L = w.shape[0]
    out = jax.ShapeDtypeStruct((L, B, M, 2 * MEM_WIDTH), BF16)
    ospec = pl.BlockSpec((None, None, M, 2 * MEM_WIDTH), lambda l, b: (l, b, 0, 0))
    return pl.pallas_call(
        _memkv_kernel,
        grid=(L, B),
        in_specs=[pl.BlockSpec((None, M, D), lambda l, b: (b, 0, 0)),
                  pl.BlockSpec((1, D), lambda l, b: (0, 0)),
                  pl.BlockSpec((None, D, 2 * MEM_WIDTH), lambda l, b: (l, 0, 0))],
        out_specs=(ospec, ospec),
        out_shape=(out, out),
        compiler_params=_cparams(("parallel", "parallel")),
        name="memkv",
    )(mem, g_mem, w)


def _mixer_residual(x_ref, mix_ref, qm_ref, mk_ref, mv_ref, w_ref):
    mw = mix_ref.shape[-1]
    n_sub = 2 * MEM_WIDTH // LANES
    logits = [lax.dot_general(qm_ref[:, (u // 2) * LANES:(u // 2 + 1) * LANES], mk_ref[:, u * LANES:(u + 1) * LANES],
                              _NT, preferred_element_type=F32) for u in range(n_sub)]
    y = x_ref[...] + jnp.dot(mix_ref[...], w_ref[0:mw, :], preferred_element_type=F32)
    tiles = []
    for t in range(n_sub // 2):
        o_t = None
        for u in (2 * t, 2 * t + 1):
            p = jnp.exp2(logits[u] - jnp.max(logits[u], axis=-1, keepdims=True))
            l = jnp.sum(p, axis=-1, keepdims=True)
            o_s = jnp.dot(p.astype(BF16), mv_ref[:, u * LANES:(u + 1) * LANES], preferred_element_type=F32) / l
            o_t = o_s if o_t is None else o_t + o_s
        tiles.append(o_t.astype(BF16))
    return y + jnp.dot(jnp.concatenate(tiles, axis=1), w_ref[mw:mw + MEM_WIDTH, :], preferred_element_type=F32)


def _out_kernel(x_ref, mix_ref, qm_ref, mk_ref, mv_ref, w_ref, o_ref):
    o_ref[...] = _mixer_residual(x_ref, mix_ref, qm_ref, mk_ref, mv_ref, w_ref)


def _out_proj(x, mix, qm, mk, mv, w, tm):
    B, S, D = x.shape
    row = lambda width: pl.BlockSpec((None, tm, width), lambda b, i: (b, i, 0))
    mem = pl.BlockSpec((None, mk.shape[1], 2 * MEM_WIDTH), lambda b, i: (b, 0, 0))
    return pl.pallas_call(
        _out_kernel,
        grid=(B, S // tm),
        in_specs=[row(D), row(mix.shape[-1]), row(MEM_WIDTH), mem, mem,
                  pl.BlockSpec(w.shape, lambda b, i: (0, 0))],
        out_specs=row(D),
        out_shape=jax.ShapeDtypeStruct((B, S, D), F32),
        compiler_params=_cparams(("parallel", "parallel")),
        name="out_proj",
    )(x, mix, qm, mk, mv, w)


def _ffn_kernel(x_ref, g_ref, wup_ref, cw_ref, cb_ref, wdn_ref, gf_ref, o_ref, carry_ref, *, tm, cw, final):
    dff = wdn_ref.shape[0]
    first = pl.program_id(1) == 0
    x = x_ref[...]
    hb = _rms(x, g_ref[...]).astype(BF16)
    row = lax.broadcasted_iota(I32, (tm, cw), 0)
    acc = x

    def up(c0):
        return (jnp.dot(hb, wup_ref[:, c0:c0 + cw], preferred_element_type=F32),
                jnp.dot(hb, wup_ref[:, dff + c0:dff + c0 + cw], preferred_element_type=F32))

    nxt = up(0)
    for c0 in range(0, dff, cw):
        a, b = nxt
        if c0 + cw < dff:
            nxt = up(c0 + cw)
        prev = jnp.where(first, 0.0, carry_ref[:, c0:c0 + cw])
        p1, p2 = prev[7:8, :], prev[6:7, :]
        a1 = jnp.where(row == 0, p1, pltpu.roll(a, 1, 0))
        a2 = jnp.where(row == 0, p2, jnp.where(row == 1, p1, pltpu.roll(a, 2, 0)))
        carry_ref[:, c0:c0 + cw] = a[tm - 8:tm, :]
        w = cw_ref[:, c0:c0 + cw]
        conv = w[0:1, :] * a2 + w[1:2, :] * a1 + w[2:3, :] * a + cb_ref[:, c0:c0 + cw]
        gated = conv / (1.0 + jnp.exp(-conv)) * b
        acc = acc + jnp.dot(gated.astype(BF16), wdn_ref[c0:c0 + cw, :], preferred_element_type=F32)
    if final:
        acc = _rms(acc, gf_ref[...])
    o_ref[...] = acc


def _ffn(x, g, wup, cw, cb, wdn, gf, tm, final):
    B, S, D = x.shape
    dff = wdn.shape[0]
    row = pl.BlockSpec((None, tm, D), lambda b, i: (b, i, 0))
    const = lambda shape: pl.BlockSpec(shape, lambda b, i: (0, 0), pipeline_mode=pl.Buffered(1))
    return pl.pallas_call(
        functools.partial(_ffn_kernel, tm=tm, cw=2 * LANES, final=final),
        grid=(B, S // tm),
        in_specs=[row, const((1, D)), const(wup.shape), const(cw.shape), const((1, dff)), const(wdn.shape),
                  const((1, D))],
        out_specs=row,
        out_shape=jax.ShapeDtypeStruct((B, S, D), F32),
        scratch_shapes=[pltpu.VMEM((8, dff), F32)],
        compiler_params=_cparams(("arbitrary", "arbitrary")),
        name="ffn",
    )(x, g, wup, cw, cb, wdn, gf)


def _b_subheads():
    return tuple(((2 * (j // 2), 2 * (j // 2), 2 * j), (2 * (j // 2) + 1, 2 * (j // 2) + 1, 2 * j + 1))
                 for j in range(B_HEADS // 2))


def kernel(x, mem, positions, g_mix, g_ffn, g_mem, g_final, w_mem_kv, a_w_in, a_kv_norm, a_w_uk, a_w_uv, a_w_out,
           b_w_in, b_sinks, b_w_out, c_w_in, c_w_out, f_w_up, f_conv_w, f_conv_b, f_w_down):
    B, S, D = x.shape
    depth = g_mix.shape[0]
    tm = min(512, S)
    tab32 = _rope_table(positions, A_QK_DIM, A_ROPE_DIM)
    tab16 = _rope_table(positions, B_HEAD_DIM, B_HEAD_DIM // 4)
    mem_k, mem_v = _memkv(mem, g_mem.reshape(1, D), w_mem_kv.astype(BF16))
    conv_w = jnp.pad(f_conv_w, ((0, 0), (0, 8 - CONV_WIDTH), (0, 0)))
    for i in range(depth):
        kind, j = i % 3, i // 3
        g = g_mix[i].reshape(1, D)
        if kind == 0:
            q, kx, ct, qi, ki, wi, qm = _proj_a(x, g, _prep_a_w_in(a_w_in[j]), a_kv_norm[j].reshape(1, -1),
                                                tab32, tab16, tm)
            wuk = jnp.pad(jnp.transpose(a_w_uk[j], (1, 2, 0)), ((0, 0), (A_ROPE_DIM, 0), (0, 0))).astype(BF16)
            wuv = jnp.transpose(a_w_uv[j], (1, 0, 2)).astype(BF16)
            mix = _dsa(q, qi, wi, kx, ct, ki, wuk, wuv)
            w_out = a_w_out[j]
        elif kind == 1:
            nq, nkv = B_HEADS * B_HEAD_DIM, 4 * B_KV_HEADS * B_HEAD_DIM
            q, k, v, qm = _proj_qkv(x, g, _prep_b_w_in(b_w_in[j]), tab16, tm, B_HEAD_DIM, B_HEAD_DIM // 4,
                                    nq, nkv, nkv, False)
            mix = _swa(q, k, v, b_sinks[j], subheads=_b_subheads(), max_dist=B_WINDOW - 1)
            w_out = b_w_out[j]
        else:
            nq = C_HEADS * C_HEAD_DIM
            q, k, v, qm = _proj_qkv(x, g, c_w_in[j].astype(BF16), tab32, tm, C_HEAD_DIM, C_HEAD_DIM // 4,
                                    nq, nq, nq, True)
            mix = _wattn(q, k, v, C_BRANCHES)
            w_out = c_w_out[j]
        x = _out_proj(x, mix, qm, mem_k[i], mem_v[i], w_out.astype(BF16), tm)
        x = _ffn(x, g_ffn[i].reshape(1, D), f_w_up[i].astype(BF16), conv_w[i], f_conv_b[i].reshape(1, -1),
                 f_w_down[i].astype(BF16), g_final.reshape(1, D), min(256, S), i == depth - 1)
    return x
```

```python
import functools

import jax
import jax.numpy as jnp
import numpy as np
from jax import lax
from jax.experimental import pallas as pl
from jax.experimental.pallas import tpu as pltpu

F32 = jnp.float32
BF16 = jnp.bfloat16
I32 = jnp.int32

LANES = 128
BLOCK = 128
ROPE_THETA = 500000.0
EPS = 1e-6
NEG_INF = -1e30
LOG2E = 1.4426950408889634
INT_MIN = -(2**31)

A_HEADS = 8
A_QK_DIM = 128
A_ROPE_DIM = 32
A_KV_RANK = 256
IDX_HEADS = 16
IDX_DIM = 64
TOPK_MAX = 256
B_HEADS = 16
B_KV_HEADS = 4
B_HEAD_DIM = 64
B_WINDOW = 128
C_HEADS = 8
C_HEAD_DIM = 128
C_BRANCHES = ((128, 1), (512, 4), (2048, 16))
MEM_HEADS = 4
MEM_HEAD_DIM = 64
MEM_WIDTH = MEM_HEADS * MEM_HEAD_DIM
CONV_WIDTH = 3
KEY_CHUNK = 256

VMEM_LIMIT = 56 * 1024 * 1024

_NT = (((1,), (1,)), ((), ()))


def _cparams(sem):
    return pltpu.CompilerParams(dimension_semantics=sem, vmem_limit_bytes=VMEM_LIMIT)


def _rms(x, g):
    return x * lax.rsqrt(jnp.mean(x * x, axis=-1, keepdims=True) + EPS) * g


def _loop_two_per_trip(lo, hi, body, carry):
    trips = (hi - lo) // 2

    def double(t, c):
        return body(lo + 2 * t + 1, body(lo + 2 * t, c))

    carry = lax.fori_loop(0, trips, double, carry)
    return lax.fori_loop(lo + 2 * trips, hi, body, carry)


def _bit_transpose32(words):
    a = list(words)
    j, m = 16, 0x0000FFFF
    while j:
        mask = jnp.int32(m - (1 << 32) if m >= (1 << 31) else m)
        for k in range(32):
            if k & j == 0:
                t = (a[k] ^ lax.shift_right_logical(a[k + j], j)) & mask
                a[k] = a[k] ^ t
                a[k + j] = a[k + j] ^ lax.shift_left(t, j)
        j >>= 1
        m = (m ^ (m << j)) & 0xFFFFFFFF
    return a


def _rope_tile(t, tab, half):
    c, sa, sb = tab[:, 0:LANES], tab[:, LANES:2 * LANES], tab[:, 2 * LANES:3 * LANES]
    return t * c + pltpu.roll(t, half, 1) * sa + pltpu.roll(t, LANES - half, 1) * sb


def _rope_table(positions, head_dim, rot):
    half = rot // 2
    inv = ROPE_THETA ** (-jnp.arange(0, rot, 2, dtype=F32) / rot)
    ang = positions.astype(F32)[..., None] * inv
    cs = jnp.concatenate([jnp.cos(ang), jnp.sin(ang)], axis=-1)
    sel = np.zeros((rot, 3 * LANES), np.float32)
    one = np.zeros((3 * LANES,), np.float32)
    for l in range(LANES):
        j = l % head_dim
        if j < half:
            sel[j, l] = 1.0
            sel[half + j, 2 * LANES + l] = -1.0
        elif j < rot:
            sel[j - half, l] = 1.0
            sel[j, LANES + l] = 1.0
        else:
            one[l] = 1.0
    return jnp.dot(cs, jnp.asarray(sel), precision=lax.Precision.HIGHEST) + jnp.asarray(one)


def _staggered(mm, jobs):
    nxt = mm(jobs[0][0], jobs[0][1])
    for n, (_, _, epilogue) in enumerate(jobs):
        y = nxt
        if n + 1 < len(jobs):
            nxt = mm(jobs[n + 1][0], jobs[n + 1][1])
        epilogue(y)


def _proj_a_kernel(x_ref, g_ref, w_ref, kvn_ref, t32_ref, t16_ref,
                   q_ref, kx_ref, ct_ref, qi_ref, ki_ref, wi_ref, qm_ref, *, tm):
    hb = _rms(x_ref[...], g_ref[...]).astype(BF16)
    t32 = t32_ref[...]
    t16 = t16_ref[...]

    def mm(a, b):
        return jnp.dot(hb, w_ref[:, a:b], preferred_element_type=F32)

    def roped(ref, col, tab, half, scale):
        def epilogue(y):
            for u in range(y.shape[1] // LANES):
                t = _rope_tile(y[:, u * LANES:(u + 1) * LANES], tab, half)
                if scale != 1.0:
                    t = t * scale
                ref[:, col + u * LANES:col + (u + 1) * LANES] = t.astype(BF16)
        return epilogue

    def latent(y):
        c = _rms(y, kvn_ref[...])
        kx_ref[:, 0:A_KV_RANK] = c.astype(BF16)
        for u in range(tm // KEY_CHUNK):
            ct_ref[u] = c[u * KEY_CHUNK:(u + 1) * KEY_CHUNK, :].T.astype(BF16)

    def index_keys(y):
        for u in range(2):
            t = _rope_tile(y[:, u * LANES:(u + 1) * LANES], t16, IDX_DIM // 8).astype(BF16)
            for v in range(tm // KEY_CHUNK):
                ki_ref[v, u * KEY_CHUNK:(u + 1) * KEY_CHUNK, :] = t[v * KEY_CHUNK:(v + 1) * KEY_CHUNK, :]

    def index_weights(y):
        wi_ref[...] = y * (IDX_HEADS * IDX_DIM) ** -0.5

    def mem_query(y):
        qm_ref[...] = (y * (MEM_HEAD_DIM ** -0.5 * LOG2E)).astype(BF16)

    jobs, o = [], 0
    for j in range(0, A_HEADS * A_QK_DIM, 2 * LANES):
        jobs.append((o + j, o + j + 2 * LANES, roped(q_ref, j, t32, A_ROPE_DIM // 2, A_QK_DIM ** -0.5 * LOG2E)))
    o += A_HEADS * A_QK_DIM
    jobs.append((o, o + A_KV_RANK, latent))
    o += A_KV_RANK
    jobs.append((o, o + LANES, roped(kx_ref, A_KV_RANK, t32, A_ROPE_DIM // 2, 1.0)))
    o += LANES
    for j in range(0, IDX_HEADS * IDX_DIM, 2 * LANES):
        jobs.append((o + j, o + j + 2 * LANES, roped(qi_ref, j, t16, IDX_DIM // 8, 1.0)))
    o += IDX_HEADS * IDX_DIM
    jobs.append((o, o + 2 * LANES, index_keys))
    o += 2 * LANES
    jobs.append((o, o + LANES, index_weights))
    o += LANES
    jobs.append((o, o + MEM_WIDTH, mem_query))
    _staggered(mm, jobs)


def _prep_a_w_in(w):
    d = w.shape[0]
    sizes = (A_HEADS * A_QK_DIM, A_KV_RANK, A_ROPE_DIM, IDX_HEADS * IDX_DIM, IDX_DIM, IDX_HEADS, MEM_WIDTH)
    offs = [0]
    for s in sizes:
        offs.append(offs[-1] + s)
    q, ckv, kr, qi, ki, wi, qm = [w[:, offs[i]:offs[i + 1]] for i in range(len(sizes))]
    z = lambda n: jnp.zeros((d, n), w.dtype)
    return jnp.concatenate([
        q, ckv, kr, z(LANES - A_ROPE_DIM), qi,
        ki, z(LANES - IDX_DIM), z(LANES - IDX_DIM), ki,
        wi, z(LANES - IDX_HEADS), qm], axis=1).astype(BF16)


def _proj_a(x, g, w, kvn, t32, t16, tm):
    B, S, D = x.shape
    n = w.shape[1]
    row = lambda width: pl.BlockSpec((None, tm, width), lambda b, i: (b, i, 0))
    const = lambda shape: pl.BlockSpec(shape, lambda b, i: (0,) * len(shape))
    out_shape = (
        jax.ShapeDtypeStruct((B, S, A_HEADS * A_QK_DIM), BF16),
        jax.ShapeDtypeStruct((B, S, A_KV_RANK + LANES), BF16),
        jax.ShapeDtypeStruct((B, S // KEY_CHUNK, A_KV_RANK, KEY_CHUNK), BF16),
        jax.ShapeDtypeStruct((B, S, IDX_HEADS * IDX_DIM), BF16),
        jax.ShapeDtypeStruct((B, S // KEY_CHUNK, 2 * KEY_CHUNK, LANES), BF16),
        jax.ShapeDtypeStruct((B, S, LANES), F32),
        jax.ShapeDtypeStruct((B, S, MEM_WIDTH), BF16),
    )
    out_specs = (
        row(A_HEADS * A_QK_DIM), row(A_KV_RANK + LANES),
        pl.BlockSpec((None, tm // KEY_CHUNK, A_KV_RANK, KEY_CHUNK), lambda b, i: (b, i, 0, 0)),
        row(IDX_HEADS * IDX_DIM),
        pl.BlockSpec((None, tm // KEY_CHUNK, 2 * KEY_CHUNK, LANES), lambda b, i: (b, i, 0, 0)),
        row(LANES), row(MEM_WIDTH),
    )
    return pl.pallas_call(
        functools.partial(_proj_a_kernel, tm=tm),
        grid=(B, S // tm),
        in_specs=[row(D), const((1, D)), const((D, n)), const((1, A_KV_RANK)), row(3 * LANES), row(3 * LANES)],
        out_specs=out_specs,
        out_shape=out_shape,
        compiler_params=_cparams(("parallel", "parallel")),
        name="proj_a",
    )(x, g, w, kvn, t32, t16)


def _proj_qkv_kernel(x_ref, g_ref, w_ref, tab_ref, q_ref, k_ref, v_ref, qm_ref, *,
                     head_dim, rot, nq, nk, nv, v_transposed):
    hb = _rms(x_ref[...], g_ref[...]).astype(BF16)
    tm = hb.shape[0]
    tab = tab_ref[...]

    def mm(a, b):
        return jnp.dot(hb, w_ref[:, a:b], preferred_element_type=F32)

    def roped(ref, col, scale):
        def epilogue(y):
            for u in range(2):
                t = _rope_tile(y[:, u * LANES:(u + 1) * LANES], tab, rot // 2)
                if scale != 1.0:
                    t = t * scale
                ref[:, col + u * LANES:col + (u + 1) * LANES] = t.astype(BF16)
        return epilogue

    def value(col):
        def epilogue(y):
            if v_transposed:
                for u in range(tm // KEY_CHUNK):
                    v_ref[u, col:col + 2 * LANES, :] = y[u * KEY_CHUNK:(u + 1) * KEY_CHUNK, :].T.astype(BF16)
            else:
                v_ref[:, col:col + 2 * LANES] = y.astype(BF16)
        return epilogue

    def mem_query(y):
        qm_ref[...] = (y * (MEM_HEAD_DIM ** -0.5 * LOG2E)).astype(BF16)

    jobs = [(j, j + 2 * LANES, roped(q_ref, j, head_dim ** -0.5 * LOG2E)) for j in range(0, nq, 2 * LANES)]
    jobs += [(nq + j, nq + j + 2 * LANES, roped(k_ref, j, 1.0)) for j in range(0, nk, 2 * LANES)]
    jobs += [(nq + nk + j, nq + nk + j + 2 * LANES, value(j)) for j in range(0, nv, 2 * LANES)]
    jobs.append((nq + nk + nv, nq + nk + nv + MEM_WIDTH, mem_query))
    _staggered(mm, jobs)


def _proj_qkv(x, g, w, tab, tm, head_dim, rot, nq, nk, nv, v_transposed):
    B, S, D = x.shape
    n = w.shape[1]
    row = lambda width: pl.BlockSpec((None, tm, width), lambda b, i: (b, i, 0))
    const = lambda shape: pl.BlockSpec(shape, lambda b, i: (0,) * len(shape))
    if v_transposed:
        v_spec = pl.BlockSpec((None, tm // KEY_CHUNK, nv, KEY_CHUNK), lambda b, i: (b, i, 0, 0))
        v_shape = jax.ShapeDtypeStruct((B, S // KEY_CHUNK, nv, KEY_CHUNK), BF16)
    else:
        v_spec, v_shape = row(nv), jax.ShapeDtypeStruct((B, S, nv), BF16)
    return pl.pallas_call(
        functools.partial(_proj_qkv_kernel, head_dim=head_dim, rot=rot, nq=nq, nk=nk, nv=nv,
                          v_transposed=v_transposed),
        grid=(B, S // tm),
        in_specs=[row(D), const((1, D)), const((D, n)), row(3 * LANES)],
        out_specs=(row(nq), row(nk), v_spec, row(MEM_WIDTH)),
        out_shape=(jax.ShapeDtypeStruct((B, S, nq), BF16), jax.ShapeDtypeStruct((B, S, nk), BF16), v_shape,
                   jax.ShapeDtypeStruct((B, S, MEM_WIDTH), BF16)),
        compiler_params=_cparams(("parallel", "parallel")),
        name="proj_qkv",
    )(x, g, w, tab)


def _prep_b_w_in(w):
    d = w.shape[0]
    nq, nkv = B_HEADS * B_HEAD_DIM, B_KV_HEADS * B_HEAD_DIM
    q, k, v, qm = w[:, :nq], w[:, nq:nq + nkv], w[:, nq + nkv:nq + 2 * nkv], w[:, nq + 2 * nkv:]
    z = jnp.zeros((d, B_HEAD_DIM), w.dtype)

    def spread(t):
        cols = []
        for h in range(B_KV_HEADS):
            th = t[:, h * B_HEAD_DIM:(h + 1) * B_HEAD_DIM]
            cols += [th, z, z, th]
        return jnp.concatenate(cols, axis=1)

    return jnp.concatenate([q, spread(k), spread(v), qm], axis=1).astype(BF16)


def _dsa_kernel(q_ref, qi_ref, wi_ref, kx_ref, ct_ref, ki_ref, wuk_ref, wuv_ref, o_ref,
                keys_ref, planes_ref, qext_ref, qi2_ref, wrow_ref, acc_ref, sa_ref, sb_ref, *, topk):
    kc = KEY_CHUNK
    hw = A_HEADS * LANES
    gw = 2 * LANES
    i = pl.program_id(1)
    nch = (i * BLOCK + BLOCK + kc - 1) // kc
    lane = lax.broadcasted_iota(I32, (BLOCK, LANES), 1)

    for h in range(A_HEADS):
        rows = slice(h * BLOCK, (h + 1) * BLOCK)
        qh = q_ref[:, h * LANES:(h + 1) * LANES]
        qext_ref[rows, 0:A_KV_RANK] = jnp.dot(qh, wuk_ref[h], preferred_element_type=F32).astype(BF16)
        qext_ref[rows, A_KV_RANK:A_KV_RANK + LANES] = jnp.where(lane < A_ROPE_DIM, qh.astype(F32), 0.0).astype(BF16)
        qi2_ref[rows, :] = qi_ref[:, h * LANES:(h + 1) * LANES]
    w_t = wi_ref[...].T
    for j in range(IDX_HEADS // 2):
        wrow_ref[0:1, j * LANES:(j + 1) * LANES] = w_t[2 * j:2 * j + 1, :]
        wrow_ref[1:2, j * LANES:(j + 1) * LANES] = w_t[2 * j + 1:2 * j + 2, :]

    qpos = i * BLOCK + lax.broadcasted_iota(I32, (kc, BLOCK), 1)
    krow = lax.broadcasted_iota(I32, (kc, BLOCK), 0)

    def score_chunk(c):
        off = pl.multiple_of(c * kc, kc)
        kk = ki_ref[c]
        acc = None
        for g in range(0, hw, gw):
            d = lax.dot_general(kk, qi2_ref[g:g + gw, :], _NT, preferred_element_type=F32)
            t = (jnp.maximum(d[0:kc], 0.0) * wrow_ref[0:1, g:g + gw]
                 + jnp.maximum(d[kc:2 * kc], 0.0) * wrow_ref[1:2, g:g + gw])
            for u in range(0, gw, LANES):
                acc = t[:, u:u + LANES] if acc is None else acc + t[:, u:u + LANES]
        bits = pltpu.bitcast(acc, I32)
        key = jnp.where(bits < 0, bits ^ 0x7FFFFFFF, bits)
        key = jnp.where(off + krow <= qpos, key, INT_MIN)
        keys_ref[pl.ds(off, kc), :] = key
        planes = _bit_transpose32([(key[8 * r:8 * r + 8, :] ^ INT_MIN) for r in range(kc // 8)])
        for p in range(32):
            planes_ref[c, p] = planes[p]

    npairs = (nch + 1) // 2

    @pl.when(i == 0)
    def _():
        planes_ref[...] = jnp.zeros(planes_ref.shape, I32)

    def score_body(j, carry):
        score_chunk(2 * j)
        score_chunk(2 * j + 1)
        return carry

    _loop_two_per_trip(0, npairs, score_body, 0)

    n_chunks = planes_ref.shape[0]

    def bit_body(p, carry):
        alive, need, res = carry
        ones = [a & planes_ref[c, p] for c, a in enumerate(alive)]
        cnt = ones[0] * 0
        for o in ones:
            cnt = cnt + lax.population_count(o)
        cnt = jnp.sum(cnt, axis=0, keepdims=True)
        take = cnt >= need
        alive = [jnp.where(take, o, a ^ o) for o, a in zip(ones, alive)]
        need = jnp.where(take, need, need - cnt)
        res = jnp.where(take, res | lax.shift_left(jnp.int32(1), 31 - p), res)
        return alive, need, res

    kx0 = kx_ref[0:kc, :]
    for g in range(0, hw, gw):
        sa_ref[0:kc, g:g + gw] = lax.dot_general(kx0, qext_ref[g:g + gw, :], _NT, preferred_element_type=F32)

    carry = ([jnp.where(c < 2 * npairs, jnp.full((8, BLOCK), -1, I32), 0) for c in range(n_chunks)],
             jnp.full((1, BLOCK), topk, I32), jnp.zeros((1, BLOCK), I32))
    for p in range(32):
        carry = bit_body(p, carry)
    thr = jnp.maximum(carry[2] ^ INT_MIN, INT_MIN + 1)

    acc_ref[...] = jnp.zeros(acc_ref.shape, F32)
    bias0 = jnp.where(keys_ref[0:kc, :] >= thr, 0.0, NEG_INF)
    bias0 = jnp.concatenate([bias0] * (gw // LANES), axis=1)
    for g in range(0, hw, gw):
        t = sa_ref[0:kc, g:g + gw] + bias0
        sa_ref[0:kc, g:g + gw] = t
        sa_ref[kc:kc + 1, g:g + gw] = jnp.max(t, axis=0, keepdims=True)

    def logits(c, s_ref):
        off = pl.multiple_of(c * kc, kc)
        bias = jnp.where(keys_ref[pl.ds(off, kc), :] >= thr, 0.0, NEG_INF)
        bias = jnp.concatenate([bias] * (gw // LANES), axis=1)
        kx = kx_ref[pl.ds(off, kc), :]
        for g in range(0, hw, gw):
            t = lax.dot_general(kx, qext_ref[g:g + gw, :], _NT, preferred_element_type=F32) + bias
            s_ref[0:kc, g:g + gw] = t
            s_ref[kc:kc + 1, g:g + gw] = jnp.max(t, axis=0, keepdims=True)

    def consume(s_ref, c, carry):
        m_prev, l_prev = carry
        ct = ct_ref[c]
        m_out, l_out = [], []
        for g in range(0, hw, gw):
            s = s_ref[0:kc, g:g + gw]
            m_new = jnp.maximum(m_prev[:, g:g + gw], s_ref[kc:kc + 1, g:g + gw])
            alpha = jnp.exp2(m_prev[:, g:g + gw] - m_new)
            p = jnp.exp2(s - m_new)
            l_out.append(alpha * l_prev[:, g:g + gw] + jnp.sum(p, axis=0, keepdims=True))
            m_out.append(m_new)
            acc_ref[:, g:g + gw] = alpha * acc_ref[:, g:g + gw] + jnp.dot(
                ct, p.astype(BF16), preferred_element_type=F32)
        return jnp.concatenate(m_out, axis=1), jnp.concatenate(l_out, axis=1)

    def pair_body(j, carry):
        c0 = 2 * j
        logits(c0 + 1, sb_ref)
        carry = consume(sa_ref, c0, carry)
        logits(jnp.minimum(c0 + 2, 2 * npairs - 1), sa_ref)
        return consume(sb_ref, c0 + 1, carry)

    _, l_fin = _loop_two_per_trip(0, npairs, pair_body,
                                  (jnp.full((1, hw), NEG_INF, F32), jnp.zeros((1, hw), F32)))

    inv_l = 1.0 / l_fin
    for h in range(A_HEADS):
        cols = slice(h * LANES, (h + 1) * LANES)
        o_lat = (acc_ref[:, cols] * inv_l[:, cols]).T.astype(BF16)
        o_ref[:, cols] = jnp.dot(o_lat, wuv_ref[h], preferred_element_type=F32).astype(BF16)


def _dsa(q, qi, wi, kx, ct, ki, wuk, wuv):
    B, S, _ = q.shape
    topk = min(TOPK_MAX, S // 4)
    blk = lambda width: pl.BlockSpec((None, BLOCK, width), lambda b, i: (b, i, 0))
    seq = lambda width: pl.BlockSpec((None, S, width), lambda b, i: (b, 0, 0))
    const3 = lambda shape: pl.BlockSpec(shape, lambda b, i: (0, 0, 0))
    return pl.pallas_call(
        functools.partial(_dsa_kernel, topk=topk),
        grid=(B, S // BLOCK),
        in_specs=[blk(A_HEADS * A_QK_DIM), blk(IDX_HEADS * IDX_DIM), blk(LANES),
                  seq(A_KV_RANK + LANES),
                  pl.BlockSpec((None, S // KEY_CHUNK, A_KV_RANK, KEY_CHUNK), lambda b, i: (b, 0, 0, 0)),
                  pl.BlockSpec((None, S // KEY_CHUNK, 2 * KEY_CHUNK, LANES), lambda b, i: (b, 0, 0, 0)),
                  const3(wuk.shape), const3(wuv.shape)],
        out_specs=blk(A_HEADS * LANES),
        out_shape=jax.ShapeDtypeStruct((B, S, A_HEADS * LANES), BF16),
        scratch_shapes=[
            pltpu.VMEM((S, BLOCK), I32),
            pltpu.VMEM((S // KEY_CHUNK, 32, 8, BLOCK), I32),
            pltpu.VMEM((A_HEADS * BLOCK, A_KV_RANK + LANES), BF16),
            pltpu.VMEM((A_HEADS * BLOCK, LANES), BF16),
            pltpu.VMEM((8, A_HEADS * LANES), F32),
            pltpu.VMEM((A_KV_RANK, A_HEADS * LANES), F32),
            pltpu.VMEM((KEY_CHUNK + 8, A_HEADS * LANES), F32),
            pltpu.VMEM((KEY_CHUNK + 8, A_HEADS * LANES), F32),
        ],
        compiler_params=_cparams(("parallel", "arbitrary")),
        name="dsa",
    )(q, qi, wi, kx, ct, ki, wuk, wuv)


def _window_bias(branches, nrel):
    rel = jnp.arange(nrel + 1, dtype=I32)[:, None, None]
    krow = jnp.arange(KEY_CHUNK, dtype=I32)[None, :, None]
    qcol = jnp.arange(KEY_CHUNK, dtype=I32)[None, None, :]
    dist = KEY_CHUNK * (nrel - 1 - rel) + qcol - krow
    mult = sum(((dist >= 0) & (dist <= window) & (dist % dil == 0)).astype(F32) for window, dil in branches)
    return jnp.where(mult > 0, jnp.log2(jnp.maximum(mult, 1.0)), NEG_INF)


def _wattn_kernel(q_ref, k_ref, vt_ref, bias_ref, o_ref, acc_ref, sa_ref, sb_ref, p_ref, *, nh, nrel):
    kc = qb = KEY_CHUNK
    last = k_ref.shape[0] // kc - 1
    top = pl.program_id(1)
    c_lo = jnp.maximum(top - (nrel - 1), 0)
    npairs = (top + 2 - c_lo) // 2

    acc_ref[...] = jnp.zeros(acc_ref.shape, F32)

    def logits(c, s_ref):
        off = pl.multiple_of(jnp.minimum(c, last) * kc, kc)
        bias = bias_ref[c - top + (nrel - 1)]
        for h in range(nh):
            cols = slice(h * LANES, (h + 1) * LANES)
            t = lax.dot_general(k_ref[pl.ds(off, kc), cols], q_ref[:, cols], _NT,
                                preferred_element_type=F32) + bias
            s_ref[0:kc, h * qb:(h + 1) * qb] = t
            s_ref[kc:kc + 1, h * qb:(h + 1) * qb] = jnp.max(t, axis=0, keepdims=True)

    def consume(s_ref, c, carry):
        m_prev, l_prev = carry
        m_new = jnp.maximum(m_prev, s_ref[kc:kc + 1, :])
        alpha = jnp.exp2(m_prev - m_new)
        p_ref[...] = jnp.exp2(s_ref[0:kc, :] - m_new).astype(BF16)
        l_new = alpha * l_prev + jnp.dot(jnp.ones((16, kc), BF16), p_ref[...], preferred_element_type=F32)[0:1, :]
        cd = jnp.minimum(c, last)
        for h in range(nh):
            cols = slice(h * qb, (h + 1) * qb)
            acc_ref[:, cols] = alpha[:, cols] * acc_ref[:, cols] + jnp.dot(
                vt_ref[cd, h * LANES:(h + 1) * LANES, :], p_ref[:, cols], preferred_element_type=F32)
        return m_new, l_new

    logits(c_lo, sa_ref)

    def pair_body(j, carry):
        c0 = c_lo + 2 * j
        logits(c0 + 1, sb_ref)
        carry = consume(sa_ref, c0, carry)
        logits(jnp.minimum(c0 + 2, c_lo + 2 * npairs - 1), sa_ref)
        return consume(sb_ref, c0 + 1, carry)

    _, l_fin = _loop_two_per_trip(0, npairs, pair_body,
                                  (jnp.full((1, nh * qb), NEG_INF, F32), jnp.zeros((1, nh * qb), F32)))

    inv_l = 1.0 / l_fin
    for h in range(nh):
        cols = slice(h * qb, (h + 1) * qb)
        o_ref[:, h * LANES:(h + 1) * LANES] = (acc_ref[:, cols] * inv_l[:, cols]).T.astype(BF16)


def _swa_kernel(sink_ref, q_ref, kp_ref, kc_ref, vp_ref, vc_ref, o_ref, *, subheads, max_dist):
    n = pl.program_id(1)
    qi = lax.broadcasted_iota(I32, (BLOCK, 2 * BLOCK), 0)
    kj = lax.broadcasted_iota(I32, (BLOCK, 2 * BLOCK), 1)
    dist = BLOCK + qi - kj
    mask = (dist >= 0) & (dist <= max_dist) & ((kj >= BLOCK) | (n > 0))
    for j, subs in enumerate(subheads):
        qt = q_ref[:, j * LANES:(j + 1) * LANES]
        o_t = None
        for kt, vt, hidx in subs:
            kk = jnp.concatenate([kp_ref[:, kt * LANES:(kt + 1) * LANES],
                                  kc_ref[:, kt * LANES:(kt + 1) * LANES]], axis=0)
            vv = jnp.concatenate([vp_ref[:, vt * LANES:(vt + 1) * LANES],
                                  vc_ref[:, vt * LANES:(vt + 1) * LANES]], axis=0)
            s = lax.dot_general(qt, kk, _NT, preferred_element_type=F32)
            s = jnp.where(mask, s, NEG_INF)
            sk = sink_ref[hidx] * LOG2E
            m = jnp.maximum(jnp.max(s, axis=-1, keepdims=True), sk)
            p = jnp.exp2(s - m)
            l = jnp.sum(p, axis=-1, keepdims=True) + jnp.exp2(sk - m)
            o_s = jnp.dot(p.astype(BF16), vv, preferred_element_type=F32) / l
            o_t = o_s if o_t is None else o_t + o_s
        o_ref[:, j * LANES:(j + 1) * LANES] = o_t.astype(BF16)


def _swa(q, k, v, sinks, *, subheads, max_dist):
    B, S, wq = q.shape
    wk, wv = k.shape[-1], v.shape[-1]
    cur = lambda width: pl.BlockSpec((None, BLOCK, width), lambda b, i: (b, i, 0))
    prv = lambda width: pl.BlockSpec((None, BLOCK, width), lambda b, i: (b, jnp.maximum(i - 1, 0), 0))
    return pl.pallas_call(
        functools.partial(_swa_kernel, subheads=subheads, max_dist=max_dist),
        grid=(B, S // BLOCK),
        in_specs=[pl.BlockSpec(memory_space=pltpu.SMEM), cur(wq), prv(wk), cur(wk), prv(wv), cur(wv)],
        out_specs=cur(wq),
        out_shape=jax.ShapeDtypeStruct((B, S, wq), BF16),
        compiler_params=_cparams(("parallel", "arbitrary")),
        name="swa",
    )(sinks, q, k, k, v, v)


def _wattn(q, k, vt, branches):
    B, S, wq = q.shape
    widest = max(w for w, _ in branches)
    nrel = min(-(-widest // KEY_CHUNK) + 1, S // KEY_CHUNK)
    bias = _window_bias(branches, nrel)
    nh = wq // LANES
    blk = lambda width: pl.BlockSpec((None, KEY_CHUNK, width), lambda b, i: (b, i, 0))
    return pl.pallas_call(
        functools.partial(_wattn_kernel, nh=nh, nrel=nrel),
        grid=(B, S // KEY_CHUNK),
        in_specs=[blk(wq),
                  pl.BlockSpec((None, S, wq), lambda b, i: (b, 0, 0)),
                  pl.BlockSpec((None,) + vt.shape[1:], lambda b, i: (b, 0, 0, 0)),
                  pl.BlockSpec(bias.shape, lambda b, i: (0, 0, 0))],
        out_specs=blk(wq),
        out_shape=jax.ShapeDtypeStruct((B, S, wq), BF16),
        scratch_shapes=[pltpu.VMEM((LANES, nh * KEY_CHUNK), F32),
                        pltpu.VMEM((KEY_CHUNK + 8, nh * KEY_CHUNK), F32),
                        pltpu.VMEM((KEY_CHUNK + 8, nh * KEY_CHUNK), F32),
                        pltpu.VMEM((KEY_CHUNK, nh * KEY_CHUNK), BF16)],
        compiler_params=_cparams(("parallel", "arbitrary")),
        name="wattn",
    )(q, k, vt, bias)


def _memkv_kernel(mem_ref, g_ref, w_ref, k_ref, v_ref):
    hb = _rms(mem_ref[...], g_ref[...]).astype(BF16)
    y = jnp.dot(hb, w_ref[...], preferred_element_type=F32)
    lane = lax.broadcasted_iota(I32, (y.shape[0], LANES), 1)
    for t in range(MEM_WIDTH // LANES):
        for out_ref, base in ((k_ref, 0), (v_ref, MEM_WIDTH)):
            tile = y[:, base + t * LANES:base + (t + 1) * LANES]
            out_ref[:, (2 * t) * LANES:(2 * t + 1) * LANES] = jnp.where(lane < MEM_HEAD_DIM, tile, 0.0).astype(BF16)
            out_ref[:, (2 * t + 1) * LANES:(2 * t + 2) * LANES] = jnp.where(lane >= MEM_HEAD_DIM, tile, 0.0).astype(BF16)


def _memkv(mem, g_mem, w):
    B, M, D = mem.shape
    L = w.shape[0]
    out = jax.ShapeDtypeStruct((L, B, M, 2 * MEM_WIDTH), BF16)
    ospec = pl.BlockSpec((None, None, M, 2 * MEM_WIDTH), lambda l, b: (l, b, 0, 0))
    return pl.pallas_call(
        _memkv_kernel,
        grid=(L, B),
        in_specs=[pl.BlockSpec((None, M, D), lambda l, b: (b, 0, 0)),
                  pl.BlockSpec((1, D), lambda l, b: (0, 0)),
                  pl.BlockSpec((None, D, 2 * MEM_WIDTH), lambda l, b: (l, 0, 0))],
        out_specs=(ospec, ospec),
        out_shape=(out, out),
        compiler_params=_cparams(("parallel", "parallel")),
        name="memkv",
    )(mem, g_mem, w)


def _mixer_residual(x_ref, mix_ref, qm_ref, mk_ref, mv_ref, w_ref):
    mw = mix_ref.shape[-1]
    n_sub = 2 * MEM_WIDTH // LANES
    logits = [lax.dot_general(qm_ref[:, (u // 2) * LANES:(u // 2 + 1) * LANES], mk_ref[:, u * LANES:(u + 1) * LANES],
                              _NT, preferred_element_type=F32) for u in range(n_sub)]
    y = x_ref[...] + jnp.dot(mix_ref[...], w_ref[0:mw, :], preferred_element_type=F32)
    tiles = []
    for t in range(n_sub // 2):
        o_t = None
        for u in (2 * t, 2 * t + 1):
            p = jnp.exp2(logits[u] - jnp.max(logits[u], axis=-1, keepdims=True))
            l = jnp.sum(p, axis=-1, keepdims=True)
            o_s = jnp.dot(p.astype(BF16), mv_ref[:, u * LANES:(u + 1) * LANES], preferred_element_type=F32) / l
            o_t = o_s if o_t is None else o_t + o_s
        tiles.append(o_t.astype(BF16))
    return y + jnp.dot(jnp.concatenate(tiles, axis=1), w_ref[mw:mw + MEM_WIDTH, :], preferred_element_type=F32)


def _out_kernel(x_ref, mix_ref, qm_ref, mk_ref, mv_ref, w_ref, o_ref):
    o_ref[...] = _mixer_residual(x_ref, mix_ref, qm_ref, mk_ref, mv_ref, w_ref)


def _out_proj(x, mix, qm, mk, mv, w, tm):
    B, S, D = x.shape
    row = lambda width: pl.BlockSpec((None, tm, width), lambda b, i: (b, i, 0))
    mem = pl.BlockSpec((None, mk.shape[1], 2 * MEM_WIDTH), lambda b, i: (b, 0, 0))
    return pl.pallas_call(
        _out_kernel,
        grid=(B, S // tm),
        in_specs=[row(D), row(mix.shape[-1]), row(MEM_WIDTH), mem, mem,
                  pl.BlockSpec(w.shape, lambda b, i: (0, 0))],
        out_specs=row(D),
        out_shape=jax.ShapeDtypeStruct((B, S, D), F32),
        compiler_params=_cparams(("parallel", "parallel")),
        name="out_proj",
    )(x, mix, qm, mk, mv, w)


def _ffn_kernel(x_ref, g_ref, wup_ref, cw_ref, cb_ref, wdn_ref, gf_ref, o_ref, carry_ref, *, tm, cw, final):
    dff = wdn_ref.shape[0]
    first = pl.program_id(1) == 0
    x = x_ref[...]
    hb = _rms(x, g_ref[...]).astype(BF16)
    row = lax.broadcasted_iota(I32, (tm, cw), 0)
    acc = x

    def up(c0):
        return (jnp.dot(hb, wup_ref[:, c0:c0 + cw], preferred_element_type=F32),
                jnp.dot(hb, wup_ref[:, dff + c0:dff + c0 + cw], preferred_element_type=F32))

    nxt = up(0)
    for c0 in range(0, dff, cw):
        a, b = nxt
        if c0 + cw < dff:
            nxt = up(c0 + cw)
        prev = jnp.where(first, 0.0, carry_ref[:, c0:c0 + cw])
        p1, p2 = prev[7:8, :], prev[6:7, :]
        a1 = jnp.where(row == 0, p1, pltpu.roll(a, 1, 0))
        a2 = jnp.where(row == 0, p2, jnp.where(row == 1, p1, pltpu.roll(a, 2, 0)))
        carry_ref[:, c0:c0 + cw] = a[tm - 8:tm, :]
        w = cw_ref[:, c0:c0 + cw]
        conv = w[0:1, :] * a2 + w[1:2, :] * a1 + w[2:3, :] * a + cb_ref[:, c0:c0 + cw]
        gated = conv / (1.0 + jnp.exp(-conv)) * b
        acc = acc + jnp.dot(gated.astype(BF16), wdn_ref[c0:c0 + cw, :], preferred_element_type=F32)
    if final:
        acc = _rms(acc, gf_ref[...])
    o_ref[...] = acc


def _ffn(x, g, wup, cw, cb, wdn, gf, tm, final):
    B, S, D = x.shape
    dff = wdn.shape[0]
    row = pl.BlockSpec((None, tm, D), lambda b, i: (b, i, 0))
    const = lambda shape: pl.BlockSpec(shape, lambda b, i: (0, 0), pipeline_mode=pl.Buffered(1))
    return pl.pallas_call(
        functools.partial(_ffn_kernel, tm=tm, cw=2 * LANES, final=final),
        grid=(B, S // tm),
        in_specs=[row, const((1, D)), const(wup.shape), const(cw.shape), const((1, dff)), const(wdn.shape),
                  const((1, D))],
        out_specs=row,
        out_shape=jax.ShapeDtypeStruct((B, S, D), F32),
        scratch_shapes=[pltpu.VMEM((8, dff), F32)],
        compiler_params=_cparams(("arbitrary", "arbitrary")),
        name="ffn",
    )(x, g, wup, cw, cb, wdn, gf)


def _b_subheads():
    return tuple(((2 * (j // 2), 2 * (j // 2), 2 * j), (2 * (j // 2) + 1, 2 * (j // 2) + 1, 2 * j + 1))
                 for j in range(B_HEADS // 2))


def kernel(x, mem, positions, g_mix, g_ffn, g_mem, g_final, w_mem_kv, a_w_in, a_kv_norm, a_w_uk, a_w_uv, a_w_out,
           b_w_in, b_sinks, b_w_out, c_w_in, c_w_out, f_w_up, f_conv_w, f_conv_b, f_w_down):
    B, S, D = x.shape
    depth = g_mix.shape[0]
    tm = min(512, S)
    tab32 = _rope_table(positions, A_QK_DIM, A_ROPE_DIM)
    tab16 = _rope_table(positions, B_HEAD_DIM, B_HEAD_DIM // 4)
    mem_k, mem_v = _memkv(mem, g_mem.reshape(1, D), w_mem_kv.astype(BF16))
    conv_w = jnp.pad(f_conv_w, ((0, 0), (0, 8 - CONV_WIDTH), (0, 0)))
    for i in range(depth):
        kind, j = i % 3, i // 3
        g = g_mix[i].reshape(1, D)
        if kind == 0:
            q, kx, ct, qi, ki, wi, qm = _proj_a(x, g, _prep_a_w_in(a_w_in[j]), a_kv_norm[j].reshape(1, -1),
                                                tab32, tab16, tm)
            wuk = jnp.pad(jnp.transpose(a_w_uk[j], (1, 2, 0)), ((0, 0), (A_ROPE_DIM, 0), (0, 0))).astype(BF16)
            wuv = jnp.transpose(a_w_uv[j], (1, 0, 2)).astype(BF16)
            mix = _dsa(q, qi, wi, kx, ct, ki, wuk, wuv)
            w_out = a_w_out[j]
        elif kind == 1:
            nq, nkv = B_HEADS * B_HEAD_DIM, 4 * B_KV_HEADS * B_HEAD_DIM
            q, k, v, qm = _proj_qkv(x, g, _prep_b_w_in(b_w_in[j]), tab16, tm, B_HEAD_DIM, B_HEAD_DIM // 4,
                                    nq, nkv, nkv, False)
            mix = _swa(q, k, v, b_sinks[j], subheads=_b_subheads(), max_dist=B_WINDOW - 1)
            w_out = b_w_out[j]
        else:
            nq = C_HEADS * C_HEAD_DIM
            q, k, v, qm = _proj_qkv(x, g, c_w_in[j].astype(BF16), tab32, tm, C_HEAD_DIM, C_HEAD_DIM // 4,
                                    nq, nq, nq, True)
            mix = _wattn(q, k, v, C_BRANCHES)
            w_out = c_w_out[j]
        x = _out_proj(x, mix, qm, mem_k[i], mem_v[i], w_out.astype(BF16), tm)
        x = _ffn(x, g_ffn[i].reshape(1, D), f_w_up[i].astype(BF16), conv_w[i], f_conv_b[i].reshape(1, -1),
                 f_w_down[i].astype(BF16), g_final.reshape(1, D), min(256, S), i == depth - 1)
    return x
```

```python
import functools

import jax
import jax.numpy as jnp
import numpy as np
from jax import lax
from jax.experimental import pallas as pl
from jax.experimental.pallas import tpu as pltpu

F32 = jnp.float32
BF16 = jnp.bfloat16
I32 = jnp.int32

LANES = 128
BLOCK = 128
ROPE_THETA = 500000.0
EPS = 1e-6
NEG_INF = -1e30
LOG2E = 1.4426950408889634
INT_MIN = -(2**31)

A_HEADS = 8
A_QK_DIM = 128
A_ROPE_DIM = 32
A_KV_RANK = 256
IDX_HEADS = 16
IDX_DIM = 64
TOPK_MAX = 256
B_HEADS = 16
B_KV_HEADS = 4
B_HEAD_DIM = 64
B_WINDOW = 128
C_HEADS = 8
C_HEAD_DIM = 128
C_BRANCHES = ((128, 1), (512, 4), (2048, 16))
MEM_HEADS = 4
MEM_HEAD_DIM = 64
MEM_WIDTH = MEM_HEADS * MEM_HEAD_DIM
CONV_WIDTH = 3
KEY_CHUNK = 256

VMEM_LIMIT = 56 * 1024 * 1024

_NT = (((1,), (1,)), ((), ()))


def _cparams(sem):
    return pltpu.CompilerParams(dimension_semantics=sem, vmem_limit_bytes=VMEM_LIMIT)


def _rms(x, g):
    return x * lax.rsqrt(jnp.mean(x * x, axis=-1, keepdims=True) + EPS) * g


def _loop_two_per_trip(lo, hi, body, carry):
    trips = (hi - lo) // 2

    def double(t, c):
        return body(lo + 2 * t + 1, body(lo + 2 * t, c))

    carry = lax.fori_loop(0, trips, double, carry)
    return lax.fori_loop(lo + 2 * trips, hi, body, carry)


def _bit_transpose32(words):
    a = list(words)
    j, m = 16, 0x0000FFFF
    while j:
        mask = jnp.int32(m - (1 << 32) if m >= (1 << 31) else m)
        for k in range(32):
            if k & j == 0:
                t = (a[k] ^ lax.shift_right_logical(a[k + j], j)) & mask
                a[k] = a[k] ^ t
                a[k + j] = a[k + j] ^ lax.shift_left(t, j)
        j >>= 1
        m = (m ^ (m << j)) & 0xFFFFFFFF
    return a


def _rope_tile(t, tab, half):
    c, sa, sb = tab[:, 0:LANES], tab[:, LANES:2 * LANES], tab[:, 2 * LANES:3 * LANES]
    return t * c + pltpu.roll(t, half, 1) * sa + pltpu.roll(t, LANES - half, 1) * sb


def _rope_table(positions, head_dim, rot):
    half = rot // 2
    inv = ROPE_THETA ** (-jnp.arange(0, rot, 2, dtype=F32) / rot)
    ang = positions.astype(F32)[..., None] * inv
    cs = jnp.concatenate([jnp.cos(ang), jnp.sin(ang)], axis=-1)
    sel = np.zeros((rot, 3 * LANES), np.float32)
    one = np.zeros((3 * LANES,), np.float32)
    for l in range(LANES):
        j = l % head_dim
        if j < half:
            sel[j, l] = 1.0
            sel[half + j, 2 * LANES + l] = -1.0
        elif j < rot:
            sel[j - half, l] = 1.0
            sel[j, LANES + l] = 1.0
        else:
            one[l] = 1.0
    return jnp.dot(cs, jnp.asarray(sel), precision=lax.Precision.HIGHEST) + jnp.asarray(one)


def _staggered(mm, jobs):
    nxt = mm(jobs[0][0], jobs[0][1])
    for n, (_, _, epilogue) in enumerate(jobs):
        y = nxt
        if n + 1 < len(jobs):
            nxt = mm(jobs[n + 1][0], jobs[n + 1][1])
        epilogue(y)


def _proj_a_kernel(x_ref, g_ref, w_ref, kvn_ref, wuk_ref, t32_ref, t16_ref,
                   q_ref, kf_ref, ct_ref, qi_ref, ki_ref, wi_ref, qm_ref, *, tm):
    hb = _rms(x_ref[...], g_ref[...]).astype(BF16)
    t32 = t32_ref[...]
    t16 = t16_ref[...]

    def mm(a, b):
        return jnp.dot(hb, w_ref[:, a:b], preferred_element_type=F32)

    def roped(ref, col, tab, half, scale):
        def epilogue(y):
            for u in range(y.shape[1] // LANES):
                t = _rope_tile(y[:, u * LANES:(u + 1) * LANES], tab, half)
                if scale != 1.0:
                    t = t * scale
                ref[:, col + u * LANES:col + (u + 1) * LANES] = t.astype(BF16)
        return epilogue

    latent_bf16 = []

    def latent(y):
        c = _rms(y, kvn_ref[...])
        latent_bf16.append(c.astype(BF16))
        for u in range(tm // KEY_CHUNK):
            ct_ref[u] = c[u * KEY_CHUNK:(u + 1) * KEY_CHUNK, :].T.astype(BF16)

    def full_keys(y):
        k_rope = _rope_tile(y, t32, A_ROPE_DIM // 2)
        k_nope = jnp.dot(latent_bf16[0], wuk_ref[...], preferred_element_type=F32)
        for h in range(A_HEADS):
            kf_ref[:, h * LANES:(h + 1) * LANES] = (k_nope[:, h * LANES:(h + 1) * LANES] + k_rope).astype(BF16)

    def index_keys(y):
        for u in range(2):
            t = _rope_tile(y[:, u * LANES:(u + 1) * LANES], t16, IDX_DIM // 8).astype(BF16)
            for v in range(tm // KEY_CHUNK):
                ki_ref[v, u * KEY_CHUNK:(u + 1) * KEY_CHUNK, :] = t[v * KEY_CHUNK:(v + 1) * KEY_CHUNK, :]

    def index_weights(y):
        wi_ref[...] = y * (IDX_HEADS * IDX_DIM) ** -0.5

    def mem_query(y):
        qm_ref[...] = (y * (MEM_HEAD_DIM ** -0.5 * LOG2E)).astype(BF16)

    jobs, o = [], 0
    for j in range(0, A_HEADS * A_QK_DIM, 2 * LANES):
        jobs.append((o + j, o + j + 2 * LANES, roped(q_ref, j, t32, A_ROPE_DIM // 2, A_QK_DIM ** -0.5 * LOG2E)))
    o += A_HEADS * A_QK_DIM
    jobs.append((o, o + A_KV_RANK, latent))
    o += A_KV_RANK
    jobs.append((o, o + LANES, full_keys))
    o += LANES
    for j in range(0, IDX_HEADS * IDX_DIM, 2 * LANES):
        jobs.append((o + j, o + j + 2 * LANES, roped(qi_ref, j, t16, IDX_DIM // 8, 1.0)))
    o += IDX_HEADS * IDX_DIM
    jobs.append((o, o + 2 * LANES, index_keys))
    o += 2 * LANES
    jobs.append((o, o + LANES, index_weights))
    o += LANES
    jobs.append((o, o + MEM_WIDTH, mem_query))
    _staggered(mm, jobs)


def _prep_a_w_in(w):
    d = w.shape[0]
    sizes = (A_HEADS * A_QK_DIM, A_KV_RANK, A_ROPE_DIM, IDX_HEADS * IDX_DIM, IDX_DIM, IDX_HEADS, MEM_WIDTH)
    offs = [0]
    for s in sizes:
        offs.append(offs[-1] + s)
    q, ckv, kr, qi, ki, wi, qm = [w[:, offs[i]:offs[i + 1]] for i in range(len(sizes))]
    z = lambda n: jnp.zeros((d, n), w.dtype)
    return jnp.concatenate([
        q, ckv, kr, z(LANES - A_ROPE_DIM), qi,
        ki, z(LANES - IDX_DIM), z(LANES - IDX_DIM), ki,
        wi, z(LANES - IDX_HEADS), qm], axis=1).astype(BF16)


def _proj_a(x, g, w, kvn, wuk, t32, t16, tm):
    B, S, D = x.shape
    n = w.shape[1]
    row = lambda width: pl.BlockSpec((None, tm, width), lambda b, i: (b, i, 0))
    const = lambda shape: pl.BlockSpec(shape, lambda b, i: (0,) * len(shape))
    out_shape = (
        jax.ShapeDtypeStruct((B, S, A_HEADS * A_QK_DIM), BF16),
        jax.ShapeDtypeStruct((B, S, A_HEADS * A_QK_DIM), BF16),
        jax.ShapeDtypeStruct((B, S // KEY_CHUNK, A_KV_RANK, KEY_CHUNK), BF16),
        jax.ShapeDtypeStruct((B, S, IDX_HEADS * IDX_DIM), BF16),
        jax.ShapeDtypeStruct((B, S // KEY_CHUNK, 2 * KEY_CHUNK, LANES), BF16),
        jax.ShapeDtypeStruct((B, S, LANES), F32),
        jax.ShapeDtypeStruct((B, S, MEM_WIDTH), BF16),
    )
    out_specs = (
        row(A_HEADS * A_QK_DIM), row(A_HEADS * A_QK_DIM),
        pl.BlockSpec((None, tm // KEY_CHUNK, A_KV_RANK, KEY_CHUNK), lambda b, i: (b, i, 0, 0)),
        row(IDX_HEADS * IDX_DIM),
        pl.BlockSpec((None, tm // KEY_CHUNK, 2 * KEY_CHUNK, LANES), lambda b, i: (b, i, 0, 0)),
        row(LANES), row(MEM_WIDTH),
    )
    return pl.pallas_call(
        functools.partial(_proj_a_kernel, tm=tm),
        grid=(B, S // tm),
        in_specs=[row(D), const((1, D)), const((D, n)), const((1, A_KV_RANK)), const(wuk.shape),
                  row(3 * LANES), row(3 * LANES)],
        out_specs=out_specs,
        out_shape=out_shape,
        compiler_params=_cparams(("parallel", "parallel")),
        name="proj_a",
    )(x, g, w, kvn, wuk, t32, t16)


def _proj_qkv_kernel(x_ref, g_ref, w_ref, tab_ref, q_ref, k_ref, v_ref, qm_ref, *,
                     head_dim, rot, nq, nk, nv, v_transposed):
    hb = _rms(x_ref[...], g_ref[...]).astype(BF16)
    tm = hb.shape[0]
    tab = tab_ref[...]

    def mm(a, b):
        return jnp.dot(hb, w_ref[:, a:b], preferred_element_type=F32)

    def roped(ref, col, scale):
        def epilogue(y):
            for u in range(2):
                t = _rope_tile(y[:, u * LANES:(u + 1) * LANES], tab, rot // 2)
                if scale != 1.0:
                    t = t * scale
                ref[:, col + u * LANES:col + (u + 1) * LANES] = t.astype(BF16)
        return epilogue

    def value(col):
        def epilogue(y):
            if v_transposed:
                for u in range(tm // KEY_CHUNK):
                    v_ref[u, col:col + 2 * LANES, :] = y[u * KEY_CHUNK:(u + 1) * KEY_CHUNK, :].T.astype(BF16)
            else:
                v_ref[:, col:col + 2 * LANES] = y.astype(BF16)
        return epilogue

    def mem_query(y):
        qm_ref[...] = (y * (MEM_HEAD_DIM ** -0.5 * LOG2E)).astype(BF16)

    jobs = [(j, j + 2 * LANES, roped(q_ref, j, head_dim ** -0.5 * LOG2E)) for j in range(0, nq, 2 * LANES)]
    jobs += [(nq + j, nq + j + 2 * LANES, roped(k_ref, j, 1.0)) for j in range(0, nk, 2 * LANES)]
    jobs += [(nq + nk + j, nq + nk + j + 2 * LANES, value(j)) for j in range(0, nv, 2 * LANES)]
    jobs.append((nq + nk + nv, nq + nk + nv + MEM_WIDTH, mem_query))
    _staggered(mm, jobs)


def _proj_qkv(x, g, w, tab, tm, head_dim, rot, nq, nk, nv, v_transposed):
    B, S, D = x.shape
    n = w.shape[1]
    row = lambda width: pl.BlockSpec((None, tm, width), lambda b, i: (b, i, 0))
    const = lambda shape: pl.BlockSpec(shape, lambda b, i: (0,) * len(shape))
    if v_transposed:
        v_spec = pl.BlockSpec((None, tm // KEY_CHUNK, nv, KEY_CHUNK), lambda b, i: (b, i, 0, 0))
        v_shape = jax.ShapeDtypeStruct((B, S // KEY_CHUNK, nv, KEY_CHUNK), BF16)
    else:
        v_spec, v_shape = row(nv), jax.ShapeDtypeStruct((B, S, nv), BF16)
    return pl.pallas_call(
        functools.partial(_proj_qkv_kernel, head_dim=head_dim, rot=rot, nq=nq, nk=nk, nv=nv,
                          v_transposed=v_transposed),
        grid=(B, S // tm),
        in_specs=[row(D), const((1, D)), const((D, n)), row(3 * LANES)],
        out_specs=(row(nq), row(nk), v_spec, row(MEM_WIDTH)),
        out_shape=(jax.ShapeDtypeStruct((B, S, nq), BF16), jax.ShapeDtypeStruct((B, S, nk), BF16), v_shape,
                   jax.ShapeDtypeStruct((B, S, MEM_WIDTH), BF16)),
        compiler_params=_cparams(("parallel", "parallel")),
        name="proj_qkv",
    )(x, g, w, tab)


def _prep_b_w_in(w):
    d = w.shape[0]
    nq, nkv = B_HEADS * B_HEAD_DIM, B_KV_HEADS * B_HEAD_DIM
    q, k, v, qm = w[:, :nq], w[:, nq:nq + nkv], w[:, nq + nkv:nq + 2 * nkv], w[:, nq + 2 * nkv:]
    z = jnp.zeros((d, B_HEAD_DIM), w.dtype)

    def spread(t):
        cols = []
        for h in range(B_KV_HEADS):
            th = t[:, h * B_HEAD_DIM:(h + 1) * B_HEAD_DIM]
            cols += [th, z, z, th]
        return jnp.concatenate(cols, axis=1)

    return jnp.concatenate([q, spread(k), spread(v), qm], axis=1).astype(BF16)


def _dsa_kernel(q_ref, qi_ref, wi_ref, kf_ref, ct_ref, ki_ref, wuv_ref, o_ref,
                keys_ref, planes_ref, qbd_ref, qi2_ref, wrow_ref, acc_ref, sa_ref, sb_ref, *, topk):
    kc = KEY_CHUNK
    hw = A_HEADS * LANES
    gw = 2 * LANES
    i = pl.program_id(1)
    nch = (i * BLOCK + BLOCK + kc - 1) // kc

    qbd_ref[...] = jnp.zeros(qbd_ref.shape, BF16)
    for h in range(A_HEADS):
        rows = slice(h * BLOCK, (h + 1) * BLOCK)
        qbd_ref[h // 2, (h % 2) * BLOCK:(h % 2 + 1) * BLOCK, (h % 2) * LANES:(h % 2 + 1) * LANES] = (
            q_ref[:, h * LANES:(h + 1) * LANES])
        qi2_ref[rows, :] = qi_ref[:, h * LANES:(h + 1) * LANES]
    w_t = wi_ref[...].T
    for j in range(IDX_HEADS // 2):
        wrow_ref[0:1, j * LANES:(j + 1) * LANES] = w_t[2 * j:2 * j + 1, :]
        wrow_ref[1:2, j * LANES:(j + 1) * LANES] = w_t[2 * j + 1:2 * j + 2, :]

    qpos = i * BLOCK + lax.broadcasted_iota(I32, (kc, BLOCK), 1)
    krow = lax.broadcasted_iota(I32, (kc, BLOCK), 0)

    def score_chunk(c):
        off = pl.multiple_of(c * kc, kc)
        kk = ki_ref[c]
        acc = None
        for g in range(0, hw, gw):
            d = lax.dot_general(kk, qi2_ref[g:g + gw, :], _NT, preferred_element_type=F32)
            t = (jnp.maximum(d[0:kc], 0.0) * wrow_ref[0:1, g:g + gw]
                 + jnp.maximum(d[kc:2 * kc], 0.0) * wrow_ref[1:2, g:g + gw])
            for u in range(0, gw, LANES):
                acc = t[:, u:u + LANES] if acc is None else acc + t[:, u:u + LANES]
        bits = pltpu.bitcast(acc, I32)
        key = jnp.where(bits < 0, bits ^ 0x7FFFFFFF, bits)
        key = jnp.where(off + krow <= qpos, key, INT_MIN)
        keys_ref[pl.ds(off, kc), :] = key
        planes = _bit_transpose32([(key[8 * r:8 * r + 8, :] ^ INT_MIN) for r in range(kc // 8)])
        for p in range(32):
            planes_ref[c, p] = planes[p]

    npairs = (nch + 1) // 2

    @pl.when(i == 0)
    def _():
        planes_ref[...] = jnp.zeros(planes_ref.shape, I32)

    def score_body(j, carry):
        score_chunk(2 * j)
        score_chunk(2 * j + 1)
        return carry

    _loop_two_per_trip(0, npairs, score_body, 0)

    n_chunks = planes_ref.shape[0]

    def bit_body(p, carry):
        alive, need, res = carry
        ones = [a & planes_ref[c, p] for c, a in enumerate(alive)]
        cnt = ones[0] * 0
        for o in ones:
            cnt = cnt + lax.population_count(o)
        cnt = jnp.sum(cnt, axis=0, keepdims=True)
        take = cnt >= need
        alive = [jnp.where(take, o, a ^ o) for o, a in zip(ones, alive)]
        need = jnp.where(take, need, need - cnt)
        res = jnp.where(take, res | lax.shift_left(jnp.int32(1), 31 - p), res)
        return alive, need, res

    for g in range(0, hw, gw):
        sa_ref[0:kc, g:g + gw] = lax.dot_general(kf_ref[0:kc, g:g + gw], qbd_ref[g // gw], _NT,
                                                 preferred_element_type=F32)

    carry = ([jnp.where(c < 2 * npairs, jnp.full((8, BLOCK), -1, I32), 0) for c in range(n_chunks)],
             jnp.full((1, BLOCK), topk, I32), jnp.zeros((1, BLOCK), I32))
    for p in range(32):
        carry = bit_body(p, carry)
    thr = jnp.maximum(carry[2] ^ INT_MIN, INT_MIN + 1)

    acc_ref[...] = jnp.zeros(acc_ref.shape, F32)
    bias0 = jnp.where(keys_ref[0:kc, :] >= thr, 0.0, NEG_INF)
    bias0 = jnp.concatenate([bias0] * (gw // LANES), axis=1)
    for g in range(0, hw, gw):
        t = sa_ref[0:kc, g:g + gw] + bias0
        sa_ref[0:kc, g:g + gw] = t
        sa_ref[kc:kc + 1, g:g + gw] = jnp.max(t, axis=0, keepdims=True)

    def logits(c, s_ref):
        off = pl.multiple_of(c * kc, kc)
        bias = jnp.where(keys_ref[pl.ds(off, kc), :] >= thr, 0.0, NEG_INF)
        bias = jnp.concatenate([bias] * (gw // LANES), axis=1)
        for g in range(0, hw, gw):
            t = lax.dot_general(kf_ref[pl.ds(off, kc), g:g + gw], qbd_ref[g // gw], _NT,
                                preferred_element_type=F32) + bias
            s_ref[0:kc, g:g + gw] = t
            s_ref[kc:kc + 1, g:g + gw] = jnp.max(t, axis=0, keepdims=True)

    def consume(s_ref, c, carry):
        m_prev, l_prev = carry
        ct = ct_ref[c]
        m_out, l_out = [], []
        for g in range(0, hw, gw):
            s = s_ref[0:kc, g:g + gw]
            m_new = jnp.maximum(m_prev[:, g:g + gw], s_ref[kc:kc + 1, g:g + gw])
            alpha = jnp.exp2(m_prev[:, g:g + gw] - m_new)
            p = jnp.exp2(s - m_new)
            l_out.append(alpha * l_prev[:, g:g + gw] + jnp.sum(p, axis=0, keepdims=True))
            m_out.append(m_new)
            acc_ref[:, g:g + gw] = alpha * acc_ref[:, g:g + gw] + jnp.dot(
                ct, p.astype(BF16), preferred_element_type=F32)
        return jnp.concatenate(m_out, axis=1), jnp.concatenate(l_out, axis=1)

    def pair_body(j, carry):
        c0 = 2 * j
        logits(c0 + 1, sb_ref)
        carry = consume(sa_ref, c0, carry)
        logits(jnp.minimum(c0 + 2, 2 * npairs - 1), sa_ref)
        return consume(sb_ref, c0 + 1, carry)

    _, l_fin = _loop_two_per_trip(0, npairs, pair_body,
                                  (jnp.full((1, hw), NEG_INF, F32), jnp.zeros((1, hw), F32)))

    inv_l = 1.0 / l_fin
    for h in range(A_HEADS):
        cols = slice(h * LANES, (h + 1) * LANES)
        o_lat = (acc_ref[:, cols] * inv_l[:, cols]).T.astype(BF16)
        o_ref[:, cols] = jnp.dot(o_lat, wuv_ref[h], preferred_element_type=F32).astype(BF16)


def _dsa(q, qi, wi, kf, ct, ki, wuv):
    B, S, _ = q.shape
    topk = min(TOPK_MAX, S // 4)
    blk = lambda width: pl.BlockSpec((None, BLOCK, width), lambda b, i: (b, i, 0))
    seq = lambda width: pl.BlockSpec((None, S, width), lambda b, i: (b, 0, 0))
    const3 = lambda shape: pl.BlockSpec(shape, lambda b, i: (0, 0, 0))
    return pl.pallas_call(
        functools.partial(_dsa_kernel, topk=topk),
        grid=(B, S // BLOCK),
        in_specs=[blk(A_HEADS * A_QK_DIM), blk(IDX_HEADS * IDX_DIM), blk(LANES),
                  seq(A_HEADS * A_QK_DIM),
                  pl.BlockSpec((None, S // KEY_CHUNK, A_KV_RANK, KEY_CHUNK), lambda b, i: (b, 0, 0, 0)),
                  pl.BlockSpec((None, S // KEY_CHUNK, 2 * KEY_CHUNK, LANES), lambda b, i: (b, 0, 0, 0)),
                  const3(wuv.shape)],
        out_specs=blk(A_HEADS * LANES),
        out_shape=jax.ShapeDtypeStruct((B, S, A_HEADS * LANES), BF16),
        scratch_shapes=[
            pltpu.VMEM((S, BLOCK), I32),
            pltpu.VMEM((S // KEY_CHUNK, 32, 8, BLOCK), I32),
            pltpu.VMEM((A_HEADS // 2, 2 * BLOCK, 2 * LANES), BF16),
            pltpu.VMEM((A_HEADS * BLOCK, LANES), BF16),
            pltpu.VMEM((8, A_HEADS * LANES), F32),
            pltpu.VMEM((A_KV_RANK, A_HEADS * LANES), F32),
            pltpu.VMEM((KEY_CHUNK + 8, A_HEADS * LANES), F32),
            pltpu.VMEM((KEY_CHUNK + 8, A_HEADS * LANES), F32),
        ],
        compiler_params=_cparams(("parallel", "arbitrary")),
        name="dsa",
    )(q, qi, wi, kf, ct, ki, wuv)


def _window_bias(branches, nrel):
    rel = jnp.arange(nrel + 1, dtype=I32)[:, None, None]
    krow = jnp.arange(KEY_CHUNK, dtype=I32)[None, :, None]
    qcol = jnp.arange(KEY_CHUNK, dtype=I32)[None, None, :]
    dist = KEY_CHUNK * (nrel - 1 - rel) + qcol - krow
    mult = sum(((dist >= 0) & (dist <= window) & (dist % dil == 0)).astype(F32) for window, dil in branches)
    return jnp.where(mult > 0, jnp.log2(jnp.maximum(mult, 1.0)), NEG_INF)


def _wattn_kernel(q_ref, k_ref, vt_ref, bias_ref, o_ref, acc_ref, sa_ref, sb_ref, p_ref, *, nh, nrel):
    kc = qb = KEY_CHUNK
    last = k_ref.shape[0] // kc - 1
    top = pl.program_id(1)
    c_lo = jnp.maximum(top - (nrel - 1), 0)
    npairs = (top + 2 - c_lo) // 2

    acc_ref[...] = jnp.zeros(acc_ref.shape, F32)

    def logits(c, s_ref):
        off = pl.multiple_of(jnp.minimum(c, last) * kc, kc)
        bias = bias_ref[c - top + (nrel - 1)]
        for h in range(nh):
            cols = slice(h * LANES, (h + 1) * LANES)
            t = lax.dot_general(k_ref[pl.ds(off, kc), cols], q_ref[:, cols], _NT,
                                preferred_element_type=F32) + bias
            s_ref[0:kc, h * qb:(h + 1) * qb] = t
            s_ref[kc:kc + 1, h * qb:(h + 1) * qb] = jnp.max(t, axis=0, keepdims=True)

    def consume(s_ref, c, carry):
        m_prev, l_prev = carry
        m_new = jnp.maximum(m_prev, s_ref[kc:kc + 1, :])
        alpha = jnp.exp2(m_prev - m_new)
        p_ref[...] = jnp.exp2(s_ref[0:kc, :] - m_new).astype(BF16)
        l_new = alpha * l_prev + jnp.dot(jnp.ones((16, kc), BF16), p_ref[...], preferred_element_type=F32)[0:1, :]
        cd = jnp.minimum(c, last)
        for h in range(nh):
            cols = slice(h * qb, (h + 1) * qb)
            acc_ref[:, cols] = alpha[:, cols] * acc_ref[:, cols] + jnp.dot(
                vt_ref[cd, h * LANES:(h + 1) * LANES, :], p_ref[:, cols], preferred_element_type=F32)
        return m_new, l_new

    logits(c_lo, sa_ref)

    def pair_body(j, carry):
        c0 = c_lo + 2 * j
        logits(c0 + 1, sb_ref)
        carry = consume(sa_ref, c0, carry)
        logits(jnp.minimum(c0 + 2, c_lo + 2 * npairs - 1), sa_ref)
        return consume(sb_ref, c0 + 1, carry)

    _, l_fin = _loop_two_per_trip(0, npairs, pair_body,
                                  (jnp.full((1, nh * qb), NEG_INF, F32), jnp.zeros((1, nh * qb), F32)))

    inv_l = 1.0 / l_fin
    for h in range(nh):
        cols = slice(h * qb, (h + 1) * qb)
        o_ref[:, h * LANES:(h + 1) * LANES] = (acc_ref[:, cols] * inv_l[:, cols]).T.astype(BF16)


def _swa_kernel(sink_ref, q_ref, kp_ref, kc_ref, vp_ref, vc_ref, o_ref, *, subheads, max_dist):
    n = pl.program_id(1)
    qi = lax.broadcasted_iota(I32, (BLOCK, 2 * BLOCK), 0)
    kj = lax.broadcasted_iota(I32, (BLOCK, 2 * BLOCK), 1)
    dist = BLOCK + qi - kj
    mask = (dist >= 0) & (dist <= max_dist) & ((kj >= BLOCK) | (n > 0))
    for j, subs in enumerate(subheads):
        qt = q_ref[:, j * LANES:(j + 1) * LANES]
        o_t = None
        for kt, vt, hidx in subs:
            kk = jnp.concatenate([kp_ref[:, kt * LANES:(kt + 1) * LANES],
                                  kc_ref[:, kt * LANES:(kt + 1) * LANES]], axis=0)
            vv = jnp.concatenate([vp_ref[:, vt * LANES:(vt + 1) * LANES],
                                  vc_ref[:, vt * LANES:(vt + 1) * LANES]], axis=0)
            s = lax.dot_general(qt, kk, _NT, preferred_element_type=F32)
            s = jnp.where(mask, s, NEG_INF)
            sk = sink_ref[hidx] * LOG2E
            m = jnp.maximum(jnp.max(s, axis=-1, keepdims=True), sk)
            p = jnp.exp2(s - m)
            l = jnp.sum(p, axis=-1, keepdims=True) + jnp.exp2(sk - m)
            o_s = jnp.dot(p.astype(BF16), vv, preferred_element_type=F32) / l
            o_t = o_s if o_t is None else o_t + o_s
        o_ref[:, j * LANES:(j + 1) * LANES] = o_t.astype(BF16)


def _swa(q, k, v, sinks, *, subheads, max_dist):
    B, S, wq = q.shape
    wk, wv = k.shape[-1], v.shape[-1]
    cur = lambda width: pl.BlockSpec((None, BLOCK, width), lambda b, i: (b, i, 0))
    prv = lambda width: pl.BlockSpec((None, BLOCK, width), lambda b, i: (b, jnp.maximum(i - 1, 0), 0))
    return pl.pallas_call(
        functools.partial(_swa_kernel, subheads=subheads, max_dist=max_dist),
        grid=(B, S // BLOCK),
        in_specs=[pl.BlockSpec(memory_space=pltpu.SMEM), cur(wq), prv(wk), cur(wk), prv(wv), cur(wv)],
        out_specs=cur(wq),
        out_shape=jax.ShapeDtypeStruct((B, S, wq), BF16),
        compiler_params=_cparams(("parallel", "arbitrary")),
        name="swa",
    )(sinks, q, k, k, v, v)


def _wattn(q, k, vt, branches):
    B, S, wq = q.shape
    widest = max(w for w, _ in branches)
    nrel = min(-(-widest // KEY_CHUNK) + 1, S // KEY_CHUNK)
    bias = _window_bias(branches, nrel)
    nh = wq // LANES
    blk = lambda width: pl.BlockSpec((None, KEY_CHUNK, width), lambda b, i: (b, i, 0))
    return pl.pallas_call(
        functools.partial(_wattn_kernel, nh=nh, nrel=nrel),
        grid=(B, S // KEY_CHUNK),
        in_specs=[blk(wq),
                  pl.BlockSpec((None, S, wq), lambda b, i: (b, 0, 0)),
                  pl.BlockSpec((None,) + vt.shape[1:], lambda b, i: (b, 0, 0, 0)),
                  pl.BlockSpec(bias.shape, lambda b, i: (0, 0, 0))],
        out_specs=blk(wq),
        out_shape=jax.ShapeDtypeStruct((B, S, wq), BF16),
        scratch_shapes=[pltpu.VMEM((LANES, nh * KEY_CHUNK), F32),
                        pltpu.VMEM((KEY_CHUNK + 8, nh * KEY_CHUNK), F32),
                        pltpu.VMEM((KEY_CHUNK + 8, nh * KEY_CHUNK), F32),
                        pltpu.VMEM((KEY_CHUNK, nh * KEY_CHUNK), BF16)],
        compiler_params=_cparams(("parallel", "arbitrary")),
        name="wattn",
    )(q, k, vt, bias)


def _memkv_kernel(mem_ref, g_ref, w_ref, k_ref, v_ref):
    hb = _rms(mem_ref[...], g_ref[...]).astype(BF16)
    y = jnp.dot(hb, w_ref[...], preferred_element_type=F32)
    lane = lax.broadcasted_iota(I32, (y.shape[0], LANES), 1)
    for t in range(MEM_WIDTH // LANES):
        for out_ref, base in ((k_ref, 0), (v_ref, MEM_WIDTH)):
            tile = y[:, base + t * LANES:base + (t + 1) * LANES]
            out_ref[:, (2 * t) * LANES:(2 * t + 1) * LANES] = jnp.where(lane < MEM_HEAD_DIM, tile, 0.0).astype(BF16)
            out_ref[:, (2 * t + 1) * LANES:(2 * t + 2) * LANES] = jnp.where(lane >= MEM_HEAD_DIM, tile, 0.0).astype(BF16)


def _memkv(mem, g_mem, w):
    B, M, D = mem.shape
    L = w.shape[0]
    out = jax.ShapeDtypeStruct((L, B, M, 2 * MEM_WIDTH), BF16)
    ospec = pl.BlockSpec((None, None, M, 2 * MEM_WIDTH), lambda l, b: (l, b, 0, 0))
    return pl.pallas_call(
        _memkv_kernel,
        grid=(L, B),
        in_specs=[pl.BlockSpec((None, M, D), lambda l, b: (b, 0, 0)),
                  pl.BlockSpec((1, D), lambda l, b: (0, 0)),
                  pl.BlockSpec((None, D, 2 * MEM_WIDTH), lambda l, b: (l, 0, 0))],
        out_specs=(ospec, ospec),
        out_shape=(out, out),
        compiler_params=_cparams(("parallel", "parallel")),
        name="memkv",
    )(mem, g_mem, w)


def _mixer_residual(x_ref, mix_ref, qm_ref, mk_ref, mv_ref, w_ref):
    mw = mix_ref.shape[-1]
    n_sub = 2 * MEM_WIDTH // LANES
    logits = [lax.dot_general(qm_ref[:, (u // 2) * LANES:(u // 2 + 1) * LANES], mk_ref[:, u * LANES:(u + 1) * LANES],
                              _NT, preferred_element_type=F32) for u in range(n_sub)]
    y = x_ref[...] + jnp.dot(mix_ref[...], w_ref[0:mw, :], preferred_element_type=F32)
    tiles = []
    for t in range(n_sub // 2):
        o_t = None
        for u in (2 * t, 2 * t + 1):
            p = jnp.exp2(logits[u] - jnp.max(logits[u], axis=-1, keepdims=True))
            l = jnp.sum(p, axis=-1, keepdims=True)
            o_s = jnp.dot(p.astype(BF16), mv_ref[:, u * LANES:(u + 1) * LANES], preferred_element_type=F32) / l
            o_t = o_s if o_t is None else o_t + o_s
        tiles.append(o_t.astype(BF16))
    return y + jnp.dot(jnp.concatenate(tiles, axis=1), w_ref[mw:mw + MEM_WIDTH, :], preferred_element_type=F32)


def _out_kernel(x_ref, mix_ref, qm_ref, mk_ref, mv_ref, w_ref, o_ref):
    o_ref[...] = _mixer_residual(x_ref, mix_ref, qm_ref, mk_ref, mv_ref, w_ref)


def _out_proj(x, mix, qm, mk, mv, w, tm):
    B, S, D = x.shape
    row = lambda width: pl.BlockSpec((None, tm, width), lambda b, i: (b, i, 0))
    mem = pl.BlockSpec((None, mk.shape[1], 2 * MEM_WIDTH), lambda b, i: (b, 0, 0))
    return pl.pallas_call(
        _out_kernel,
        grid=(B, S // tm),
        in_specs=[row(D), row(mix.shape[-1]), row(MEM_WIDTH), mem, mem,
                  pl.BlockSpec(w.shape, lambda b, i: (0, 0))],
        out_specs=row(D),
        out_shape=jax.ShapeDtypeStruct((B, S, D), F32),
        compiler_params=_cparams(("parallel", "parallel")),
        name="out_proj",
    )(x, mix, qm, mk, mv, w)


def _ffn_kernel(x_ref, g_ref, wup_ref, cw_ref, cb_ref, wdn_ref, gf_ref, o_ref, carry_ref, *, tm, cw, final):
    dff = wdn_ref.shape[0]
    first = pl.program_id(1) == 0
    x = x_ref[...]
    hb = _rms(x, g_ref[...]).astype(BF16)
    row = lax.broadcasted_iota(I32, (tm, cw), 0)
    acc = x

    def up(c0):
        return (jnp.dot(hb, wup_ref[:, c0:c0 + cw], preferred_element_type=F32),
                jnp.dot(hb, wup_ref[:, dff + c0:dff + c0 + cw], preferred_element_type=F32))

    nxt = up(0)
    for c0 in range(0, dff, cw):
        a, b = nxt
        if c0 + cw < dff:
            nxt = up(c0 + cw)
        prev = jnp.where(first, 0.0, carry_ref[:, c0:c0 + cw])
        p1, p2 = prev[7:8, :], prev[6:7, :]
        a1 = jnp.where(row == 0, p1, pltpu.roll(a, 1, 0))
        a2 = jnp.where(row == 0, p2, jnp.where(row == 1, p1, pltpu.roll(a, 2, 0)))
        carry_ref[:, c0:c0 + cw] = a[tm - 8:tm, :]
        w = cw_ref[:, c0:c0 + cw]
        conv = w[0:1, :] * a2 + w[1:2, :] * a1 + w[2:3, :] * a + cb_ref[:, c0:c0 + cw]
        gated = conv / (1.0 + jnp.exp(-conv)) * b
        acc = acc + jnp.dot(gated.astype(BF16), wdn_ref[c0:c0 + cw, :], preferred_element_type=F32)
    if final:
        acc = _rms(acc, gf_ref[...])
    o_ref[...] = acc


def _ffn(x, g, wup, cw, cb, wdn, gf, tm, final):
    B, S, D = x.shape
    dff = wdn.shape[0]
    row = pl.BlockSpec((None, tm, D), lambda b, i: (b, i, 0))
    const = lambda shape: pl.BlockSpec(shape, lambda b, i: (0, 0), pipeline_mode=pl.Buffered(1))
    return pl.pallas_call(
        functools.partial(_ffn_kernel, tm=tm, cw=2 * LANES, final=final),
        grid=(B, S // tm),
        in_specs=[row, const((1, D)), const(wup.shape), const(cw.shape), const((1, dff)), const(wdn.shape),
                  const((1, D))],
        out_specs=row,
        out_shape=jax.ShapeDtypeStruct((B, S, D), F32),
        scratch_shapes=[pltpu.VMEM((8, dff), F32)],
        compiler_params=_cparams(("arbitrary", "arbitrary")),
        name="ffn",
    )(x, g, wup, cw, cb, wdn, gf)


def _b_subheads():
    return tuple(((2 * (j // 2), 2 * (j // 2), 2 * j), (2 * (j // 2) + 1, 2 * (j // 2) + 1, 2 * j + 1))
                 for j in range(B_HEADS // 2))


def kernel(x, mem, positions, g_mix, g_ffn, g_mem, g_final, w_mem_kv, a_w_in, a_kv_norm, a_w_uk, a_w_uv, a_w_out,
           b_w_in, b_sinks, b_w_out, c_w_in, c_w_out, f_w_up, f_conv_w, f_conv_b, f_w_down):
    B, S, D = x.shape
    depth = g_mix.shape[0]
    tm = min(512, S)
    tab32 = _rope_table(positions, A_QK_DIM, A_ROPE_DIM)
    tab16 = _rope_table(positions, B_HEAD_DIM, B_HEAD_DIM // 4)
    mem_k, mem_v = _memkv(mem, g_mem.reshape(1, D), w_mem_kv.astype(BF16))
    conv_w = jnp.pad(f_conv_w, ((0, 0), (0, 8 - CONV_WIDTH), (0, 0)))
    for i in range(depth):
        kind, j = i % 3, i // 3
        g = g_mix[i].reshape(1, D)
        if kind == 0:
            wuk = jnp.pad(a_w_uk[j], ((0, 0), (0, 0), (A_ROPE_DIM, 0))).reshape(A_KV_RANK, -1).astype(BF16)
            wuv = jnp.transpose(a_w_uv[j], (1, 0, 2)).astype(BF16)
            q, kf, ct, qi, ki, wi, qm = _proj_a(x, g, _prep_a_w_in(a_w_in[j]), a_kv_norm[j].reshape(1, -1), wuk,
                                                tab32, tab16, tm)
            mix = _dsa(q, qi, wi, kf, ct, ki, wuv)
            w_out = a_w_out[j]
        elif kind == 1:
            nq, nkv = B_HEADS * B_HEAD_DIM, 4 * B_KV_HEADS * B_HEAD_DIM
            q, k, v, qm = _proj_qkv(x, g, _prep_b_w_in(b_w_in[j]), tab16, tm, B_HEAD_DIM, B_HEAD_DIM // 4,
                                    nq, nkv, nkv, False)
            mix = _swa(q, k, v, b_sinks[j], subheads=_b_subheads(), max_dist=B_WINDOW - 1)
            w_out = b_w_out[j]
        else:
            nq = C_HEADS * C_HEAD_DIM
            q, k, v, qm = _proj_qkv(x, g, c_w_in[j].astype(BF16), tab32, tm, C_HEAD_DIM, C_HEAD_DIM // 4,
                                    nq, nq, nq, True)
            mix = _wattn(q, k, v, C_BRANCHES)
            w_out = c_w_out[j]
        x = _out_proj(x, mix, qm, mem_k[i], mem_v[i], w_out.astype(BF16), tm)
        x = _ffn(x, g_ffn[i].reshape(1, D), f_w_up[i].astype(BF16), conv_w[i], f_conv_b[i].reshape(1, -1),
                 f_w_down[i].astype(BF16), g_final.reshape(1, D), min(256, S), i == depth - 1)
    return x
```

```python
import functools

import jax
import jax.numpy as jnp
import numpy as np
from jax import lax
from jax.experimental import pallas as pl
from jax.experimental.pallas import tpu as pltpu

F32 = jnp.float32
BF16 = jnp.bfloat16
I32 = jnp.int32

LANES = 128
BLOCK = 128
ROPE_THETA = 500000.0
EPS = 1e-6
NEG_INF = -1e30
LOG2E = 1.4426950408889634
INT_MIN = -(2**31)

A_HEADS = 8
A_QK_DIM = 128
A_ROPE_DIM = 32
A_KV_RANK = 256
IDX_HEADS = 16
IDX_DIM = 64
TOPK_MAX = 256
B_HEADS = 16
B_KV_HEADS = 4
B_HEAD_DIM = 64
B_WINDOW = 128
C_HEADS = 8
C_HEAD_DIM = 128
C_BRANCHES = ((128, 1), (512, 4), (2048, 16))
MEM_HEADS = 4
MEM_HEAD_DIM = 64
MEM_WIDTH = MEM_HEADS * MEM_HEAD_DIM
CONV_WIDTH = 3
KEY_CHUNK = 256

VMEM_LIMIT = 56 * 1024 * 1024

_NT = (((1,), (1,)), ((), ()))


def _cparams(sem):
    return pltpu.CompilerParams(dimension_semantics=sem, vmem_limit_bytes=VMEM_LIMIT)


def _rms(x, g):
    return x * lax.rsqrt(jnp.mean(x * x, axis=-1, keepdims=True) + EPS) * g


def _loop_k_per_trip(lo, hi, body, carry, k=2):
    if k == 1:
        return lax.fori_loop(lo, hi, body, carry)
    trips = (hi - lo) // k

    def several(t, c):
        for u in range(k):
            c = body(lo + k * t + u, c)
        return c

    carry = lax.fori_loop(0, trips, several, carry)
    return _loop_k_per_trip(lo + k * trips, hi, body, carry, k // 2)


def _bit_transpose32(words):
    a = list(words)
    j, m = 16, 0x0000FFFF
    while j:
        mask = jnp.int32(m - (1 << 32) if m >= (1 << 31) else m)
        for k in range(32):
            if k & j == 0:
                t = (a[k] ^ lax.shift_right_logical(a[k + j], j)) & mask
                a[k] = a[k] ^ t
                a[k + j] = a[k + j] ^ lax.shift_left(t, j)
        j >>= 1
        m = (m ^ (m << j)) & 0xFFFFFFFF
    return a


def _rope_tile(t, tab, half):
    c, sa, sb = tab[:, 0:LANES], tab[:, LANES:2 * LANES], tab[:, 2 * LANES:3 * LANES]
    return t * c + pltpu.roll(t, half, 1) * sa + pltpu.roll(t, LANES - half, 1) * sb


def _rope_table(positions, head_dim, rot):
    half = rot // 2
    inv = ROPE_THETA ** (-jnp.arange(0, rot, 2, dtype=F32) / rot)
    ang = positions.astype(F32)[..., None] * inv
    cs = jnp.concatenate([jnp.cos(ang), jnp.sin(ang)], axis=-1)
    sel = np.zeros((rot, 3 * LANES), np.float32)
    one = np.zeros((3 * LANES,), np.float32)
    for l in range(LANES):
        j = l % head_dim
        if j < half:
            sel[j, l] = 1.0
            sel[half + j, 2 * LANES + l] = -1.0
        elif j < rot:
            sel[j - half, l] = 1.0
            sel[j, LANES + l] = 1.0
        else:
            one[l] = 1.0
    return jnp.dot(cs, jnp.asarray(sel), precision=lax.Precision.HIGHEST) + jnp.asarray(one)


def _staggered(mm, jobs):
    nxt = mm(jobs[0][0], jobs[0][1])
    for n, (_, _, epilogue) in enumerate(jobs):
        y = nxt
        if n + 1 < len(jobs):
            nxt = mm(jobs[n + 1][0], jobs[n + 1][1])
        epilogue(y)


def _proj_a_kernel(x_ref, g_ref, w_ref, kvn_ref, wuk_ref, t32_ref, t16_ref,
                   q_ref, kf_ref, ct_ref, qi_ref, ki_ref, wi_ref, qm_ref, *, tm):
    hb = _rms(x_ref[...], g_ref[...]).astype(BF16)
    t32 = t32_ref[...]
    t16 = t16_ref[...]

    def mm(a, b):
        return jnp.dot(hb, w_ref[:, a:b], preferred_element_type=F32)

    def roped(ref, col, tab, half, scale):
        def epilogue(y):
            for u in range(y.shape[1] // LANES):
                t = _rope_tile(y[:, u * LANES:(u + 1) * LANES], tab, half)
                if scale != 1.0:
                    t = t * scale
                ref[:, col + u * LANES:col + (u + 1) * LANES] = t.astype(BF16)
        return epilogue

    latent_bf16 = []

    def latent(y):
        c = _rms(y, kvn_ref[...])
        latent_bf16.append(c.astype(BF16))
        for u in range(tm // KEY_CHUNK):
            ct_ref[u] = c[u * KEY_CHUNK:(u + 1) * KEY_CHUNK, :].T.astype(BF16)

    def full_keys(y):
        k_rope = _rope_tile(y, t32, A_ROPE_DIM // 2)
        k_nope = jnp.dot(latent_bf16[0], wuk_ref[...], preferred_element_type=F32)
        for h in range(A_HEADS):
            kf_ref[:, h * LANES:(h + 1) * LANES] = (k_nope[:, h * LANES:(h + 1) * LANES] + k_rope).astype(BF16)

    def index_keys(y):
        for u in range(2):
            t = _rope_tile(y[:, u * LANES:(u + 1) * LANES], t16, IDX_DIM // 8).astype(BF16)
            for v in range(tm // KEY_CHUNK):
                ki_ref[v, u * KEY_CHUNK:(u + 1) * KEY_CHUNK, :] = t[v * KEY_CHUNK:(v + 1) * KEY_CHUNK, :]

    def index_weights(y):
        wi_ref[...] = y * (IDX_HEADS * IDX_DIM) ** -0.5

    def mem_query(y):
        qm_ref[...] = (y * (MEM_HEAD_DIM ** -0.5 * LOG2E)).astype(BF16)

    jobs, o = [], 0
    for j in range(0, A_HEADS * A_QK_DIM, 2 * LANES):
        jobs.append((o + j, o + j + 2 * LANES, roped(q_ref, j, t32, A_ROPE_DIM // 2, A_QK_DIM ** -0.5 * LOG2E)))
    o += A_HEADS * A_QK_DIM
    jobs.append((o, o + A_KV_RANK, latent))
    o += A_KV_RANK
    jobs.append((o, o + LANES, full_keys))
    o += LANES
    for j in range(0, IDX_HEADS * IDX_DIM, 2 * LANES):
        jobs.append((o + j, o + j + 2 * LANES, roped(qi_ref, j, t16, IDX_DIM // 8, 1.0)))
    o += IDX_HEADS * IDX_DIM
    jobs.append((o, o + 2 * LANES, index_keys))
    o += 2 * LANES
    jobs.append((o, o + LANES, index_weights))
    o += LANES
    jobs.append((o, o + MEM_WIDTH, mem_query))
    _staggered(mm, jobs)


def _prep_a_w_in(w):
    d = w.shape[0]
    sizes = (A_HEADS * A_QK_DIM, A_KV_RANK, A_ROPE_DIM, IDX_HEADS * IDX_DIM, IDX_DIM, IDX_HEADS, MEM_WIDTH)
    offs = [0]
    for s in sizes:
        offs.append(offs[-1] + s)
    q, ckv, kr, qi, ki, wi, qm = [w[:, offs[i]:offs[i + 1]] for i in range(len(sizes))]
    z = lambda n: jnp.zeros((d, n), w.dtype)
    return jnp.concatenate([
        q, ckv, kr, z(LANES - A_ROPE_DIM), qi,
        ki, z(LANES - IDX_DIM), z(LANES - IDX_DIM), ki,
        wi, z(LANES - IDX_HEADS), qm], axis=1).astype(BF16)


def _proj_a(x, g, w, kvn, wuk, t32, t16, tm):
    B, S, D = x.shape
    n = w.shape[1]
    row = lambda width: pl.BlockSpec((None, tm, width), lambda b, i: (b, i, 0))
    const = lambda shape: pl.BlockSpec(shape, lambda b, i: (0,) * len(shape))
    out_shape = (
        jax.ShapeDtypeStruct((B, S, A_HEADS * A_QK_DIM), BF16),
        jax.ShapeDtypeStruct((B, S, A_HEADS * A_QK_DIM), BF16),
        jax.ShapeDtypeStruct((B, S // KEY_CHUNK, A_KV_RANK, KEY_CHUNK), BF16),
        jax.ShapeDtypeStruct((B, S, IDX_HEADS * IDX_DIM), BF16),
        jax.ShapeDtypeStruct((B, S // KEY_CHUNK, 2 * KEY_CHUNK, LANES), BF16),
        jax.ShapeDtypeStruct((B, S, LANES), F32),
        jax.ShapeDtypeStruct((B, S, MEM_WIDTH), BF16),
    )
    out_specs = (
        row(A_HEADS * A_QK_DIM), row(A_HEADS * A_QK_DIM),
        pl.BlockSpec((None, tm // KEY_CHUNK, A_KV_RANK, KEY_CHUNK), lambda b, i: (b, i, 0, 0)),
        row(IDX_HEADS * IDX_DIM),
        pl.BlockSpec((None, tm // KEY_CHUNK, 2 * KEY_CHUNK, LANES), lambda b, i: (b, i, 0, 0)),
        row(LANES), row(MEM_WIDTH),
    )
    return pl.pallas_call(
        functools.partial(_proj_a_kernel, tm=tm),
        grid=(B, S // tm),
        in_specs=[row(D), const((1, D)), const((D, n)), const((1, A_KV_RANK)), const(wuk.shape),
                  row(3 * LANES), row(3 * LANES)],
        out_specs=out_specs,
        out_shape=out_shape,
        compiler_params=_cparams(("parallel", "parallel")),
        name="proj_a",
    )(x, g, w, kvn, wuk, t32, t16)


def _proj_qkv_kernel(x_ref, g_ref, w_ref, tab_ref, q_ref, k_ref, v_ref, qm_ref, *,
                     head_dim, rot, nq, nk, nv, v_transposed):
    hb = _rms(x_ref[...], g_ref[...]).astype(BF16)
    tm = hb.shape[0]
    tab = tab_ref[...]

    def mm(a, b):
        return jnp.dot(hb, w_ref[:, a:b], preferred_element_type=F32)

    def roped(ref, col, scale):
        def epilogue(y):
            for u in range(2):
                t = _rope_tile(y[:, u * LANES:(u + 1) * LANES], tab, rot // 2)
                if scale != 1.0:
                    t = t * scale
                ref[:, col + u * LANES:col + (u + 1) * LANES] = t.astype(BF16)
        return epilogue

    def value(col):
        def epilogue(y):
            if v_transposed:
                for u in range(tm // KEY_CHUNK):
                    v_ref[u, col:col + 2 * LANES, :] = y[u * KEY_CHUNK:(u + 1) * KEY_CHUNK, :].T.astype(BF16)
            else:
                v_ref[:, col:col + 2 * LANES] = y.astype(BF16)
        return epilogue

    def mem_query(y):
        qm_ref[...] = (y * (MEM_HEAD_DIM ** -0.5 * LOG2E)).astype(BF16)

    jobs = [(j, j + 2 * LANES, roped(q_ref, j, head_dim ** -0.5 * LOG2E)) for j in range(0, nq, 2 * LANES)]
    jobs += [(nq + j, nq + j + 2 * LANES, roped(k_ref, j, 1.0)) for j in range(0, nk, 2 * LANES)]
    jobs += [(nq + nk + j, nq + nk + j + 2 * LANES, value(j)) for j in range(0, nv, 2 * LANES)]
    jobs.append((nq + nk + nv, nq + nk + nv + MEM_WIDTH, mem_query))
    _staggered(mm, jobs)


def _proj_qkv(x, g, w, tab, tm, head_dim, rot, nq, nk, nv, v_transposed):
    B, S, D = x.shape
    n = w.shape[1]
    row = lambda width: pl.BlockSpec((None, tm, width), lambda b, i: (b, i, 0))
    const = lambda shape: pl.BlockSpec(shape, lambda b, i: (0,) * len(shape))
    if v_transposed:
        v_spec = pl.BlockSpec((None, tm // KEY_CHUNK, nv, KEY_CHUNK), lambda b, i: (b, i, 0, 0))
        v_shape = jax.ShapeDtypeStruct((B, S // KEY_CHUNK, nv, KEY_CHUNK), BF16)
    else:
        v_spec, v_shape = row(nv), jax.ShapeDtypeStruct((B, S, nv), BF16)
    return pl.pallas_call(
        functools.partial(_proj_qkv_kernel, head_dim=head_dim, rot=rot, nq=nq, nk=nk, nv=nv,
                          v_transposed=v_transposed),
        grid=(B, S // tm),
        in_specs=[row(D), const((1, D)), const((D, n)), row(3 * LANES)],
        out_specs=(row(nq), row(nk), v_spec, row(MEM_WIDTH)),
        out_shape=(jax.ShapeDtypeStruct((B, S, nq), BF16), jax.ShapeDtypeStruct((B, S, nk), BF16), v_shape,
                   jax.ShapeDtypeStruct((B, S, MEM_WIDTH), BF16)),
        compiler_params=_cparams(("parallel", "parallel")),
        name="proj_qkv",
    )(x, g, w, tab)


def _prep_b_w_in(w):
    d = w.shape[0]
    nq, nkv = B_HEADS * B_HEAD_DIM, B_KV_HEADS * B_HEAD_DIM
    q, k, v, qm = w[:, :nq], w[:, nq:nq + nkv], w[:, nq + nkv:nq + 2 * nkv], w[:, nq + 2 * nkv:]
    z = jnp.zeros((d, B_HEAD_DIM), w.dtype)

    def spread(t):
        cols = []
        for h in range(B_KV_HEADS):
            th = t[:, h * B_HEAD_DIM:(h + 1) * B_HEAD_DIM]
            cols += [th, z, z, th]
        return jnp.concatenate(cols, axis=1)

    return jnp.concatenate([q, spread(k), spread(v), qm], axis=1).astype(BF16)


def _dsa_kernel(q_ref, qi_ref, wi_ref, kf_ref, ct_ref, ki_ref, wuv_ref, o_ref,
                keys_ref, planes_ref, qbd_ref, qi2_ref, wrow_ref, acc_ref, sa_ref, sb_ref, *, topk):
    kc = KEY_CHUNK
    hw = A_HEADS * LANES
    gw = 2 * LANES
    i = pl.program_id(1)
    nch = (i * BLOCK + BLOCK + kc - 1) // kc

    qbd_ref[...] = jnp.zeros(qbd_ref.shape, BF16)
    for h in range(A_HEADS):
        rows = slice(h * BLOCK, (h + 1) * BLOCK)
        qbd_ref[h // 2, (h % 2) * BLOCK:(h % 2 + 1) * BLOCK, (h % 2) * LANES:(h % 2 + 1) * LANES] = (
            q_ref[:, h * LANES:(h + 1) * LANES])
        qi2_ref[rows, :] = qi_ref[:, h * LANES:(h + 1) * LANES]
    w_t = wi_ref[...].T
    for j in range(IDX_HEADS // 2):
        wrow_ref[0:1, j * LANES:(j + 1) * LANES] = w_t[2 * j:2 * j + 1, :]
        wrow_ref[1:2, j * LANES:(j + 1) * LANES] = w_t[2 * j + 1:2 * j + 2, :]

    qpos = i * BLOCK + lax.broadcasted_iota(I32, (kc, BLOCK), 1)
    krow = lax.broadcasted_iota(I32, (kc, BLOCK), 0)

    def score_chunk(c):
        off = pl.multiple_of(c * kc, kc)
        kk = ki_ref[c]
        acc = None
        for g in range(0, hw, gw):
            d = lax.dot_general(kk, qi2_ref[g:g + gw, :], _NT, preferred_element_type=F32)
            t = (jnp.maximum(d[0:kc], 0.0) * wrow_ref[0:1, g:g + gw]
                 + jnp.maximum(d[kc:2 * kc], 0.0) * wrow_ref[1:2, g:g + gw])
            for u in range(0, gw, LANES):
                acc = t[:, u:u + LANES] if acc is None else acc + t[:, u:u + LANES]
        bits = pltpu.bitcast(acc, I32)
        key = jnp.where(bits < 0, bits ^ 0x7FFFFFFF, bits)
        key = jnp.where(off + krow <= qpos, key, INT_MIN)
        keys_ref[pl.ds(off, kc), :] = key
        planes = _bit_transpose32([(key[8 * r:8 * r + 8, :] ^ INT_MIN) for r in range(kc // 8)])
        for p in range(32):
            planes_ref[c, p] = planes[p]

    @pl.when(i == 0)
    def _():
        planes_ref[...] = jnp.zeros(planes_ref.shape, I32)

    def score_body(c, carry):
        score_chunk(c)
        return carry

    _loop_k_per_trip(0, nch, score_body, 0, k=4)

    n_chunks = planes_ref.shape[0]

    def bit_body(p, carry):
        alive, need, res = carry
        ones = [a & planes_ref[c, p] for c, a in enumerate(alive)]
        cnt = ones[0] * 0
        for o in ones:
            cnt = cnt + lax.population_count(o)
        cnt = jnp.sum(cnt, axis=0, keepdims=True)
        take = cnt >= need
        alive = [jnp.where(take, o, a ^ o) for o, a in zip(ones, alive)]
        need = jnp.where(take, need, need - cnt)
        res = jnp.where(take, res | lax.shift_left(jnp.int32(1), 31 - p), res)
        return alive, need, res

    for g in range(0, hw, gw):
        sa_ref[0:kc, g:g + gw] = lax.dot_general(kf_ref[0:kc, g:g + gw], qbd_ref[g // gw], _NT,
                                                 preferred_element_type=F32)

    carry = ([jnp.where(c < nch, jnp.full((8, BLOCK), -1, I32), 0) for c in range(n_chunks)],
             jnp.full((1, BLOCK), topk, I32), jnp.zeros((1, BLOCK), I32))
    for p in range(32):
        carry = bit_body(p, carry)
    thr = jnp.maximum(carry[2] ^ INT_MIN, INT_MIN + 1)

    acc_ref[...] = jnp.zeros(acc_ref.shape, F32)
    bias0 = jnp.where(keys_ref[0:kc, :] >= thr, 0.0, NEG_INF)
    bias0 = jnp.concatenate([bias0] * (gw // LANES), axis=1)
    for g in range(0, hw, gw):
        t = sa_ref[0:kc, g:g + gw] + bias0
        sa_ref[0:kc, g:g + gw] = t
        sa_ref[kc:kc + 1, g:g + gw] = jnp.max(t, axis=0, keepdims=True)

    def logits(c, s_ref):
        off = pl.multiple_of(c * kc, kc)
        bias = jnp.where(keys_ref[pl.ds(off, kc), :] >= thr, 0.0, NEG_INF)
        bias = jnp.concatenate([bias] * (gw // LANES), axis=1)
        for g in range(0, hw, gw):
            t = lax.dot_general(kf_ref[pl.ds(off, kc), g:g + gw], qbd_ref[g // gw], _NT,
                                preferred_element_type=F32) + bias
            s_ref[0:kc, g:g + gw] = t
            s_ref[kc:kc + 1, g:g + gw] = jnp.max(t, axis=0, keepdims=True)

    def consume(s_ref, c, carry):
        m_prev, l_prev = carry
        ct = ct_ref[c]
        m_out, l_out = [], []
        for g in range(0, hw, gw):
            s = s_ref[0:kc, g:g + gw]
            m_new = jnp.maximum(m_prev[:, g:g + gw], s_ref[kc:kc + 1, g:g + gw])
            alpha = jnp.exp2(m_prev[:, g:g + gw] - m_new)
            p = jnp.exp2(s - m_new)
            l_out.append(alpha * l_prev[:, g:g + gw] + jnp.sum(p, axis=0, keepdims=True))
            m_out.append(m_new)
            acc_ref[:, g:g + gw] = alpha * acc_ref[:, g:g + gw] + jnp.dot(
                ct, p.astype(BF16), preferred_element_type=F32)
        return jnp.concatenate(m_out, axis=1), jnp.concatenate(l_out, axis=1)

    def pair_body(j, carry):
        c0 = 2 * j
        logits(c0 + 1, sb_ref)
        carry = consume(sa_ref, c0, carry)
        logits(jnp.minimum(c0 + 2, nch - 1), sa_ref)
        return consume(sb_ref, c0 + 1, carry)

    carry = _loop_k_per_trip(0, nch // 2, pair_body,
                             (jnp.full((1, hw), NEG_INF, F32), jnp.zeros((1, hw), F32)))
    _, l_fin = lax.fori_loop(0, nch % 2, lambda _, c: consume(sa_ref, nch - 1, c), carry)

    inv_l = 1.0 / l_fin
    for h in range(A_HEADS):
        cols = slice(h * LANES, (h + 1) * LANES)
        o_lat = (acc_ref[:, cols] * inv_l[:, cols]).T.astype(BF16)
        o_ref[:, cols] = jnp.dot(o_lat, wuv_ref[h], preferred_element_type=F32).astype(BF16)


def _dsa(q, qi, wi, kf, ct, ki, wuv):
    B, S, _ = q.shape
    topk = min(TOPK_MAX, S // 4)
    blk = lambda width: pl.BlockSpec((None, BLOCK, width), lambda b, i: (b, i, 0))
    seq = lambda width: pl.BlockSpec((None, S, width), lambda b, i: (b, 0, 0))
    const3 = lambda shape: pl.BlockSpec(shape, lambda b, i: (0, 0, 0))
    return pl.pallas_call(
        functools.partial(_dsa_kernel, topk=topk),
        grid=(B, S // BLOCK),
        in_specs=[blk(A_HEADS * A_QK_DIM), blk(IDX_HEADS * IDX_DIM), blk(LANES),
                  seq(A_HEADS * A_QK_DIM),
                  pl.BlockSpec((None, S // KEY_CHUNK, A_KV_RANK, KEY_CHUNK), lambda b, i: (b, 0, 0, 0)),
                  pl.BlockSpec((None, S // KEY_CHUNK, 2 * KEY_CHUNK, LANES), lambda b, i: (b, 0, 0, 0)),
                  const3(wuv.shape)],
        out_specs=blk(A_HEADS * LANES),
        out_shape=jax.ShapeDtypeStruct((B, S, A_HEADS * LANES), BF16),
        scratch_shapes=[
            pltpu.VMEM((S, BLOCK), I32),
            pltpu.VMEM((S // KEY_CHUNK, 32, 8, BLOCK), I32),
            pltpu.VMEM((A_HEADS // 2, 2 * BLOCK, 2 * LANES), BF16),
            pltpu.VMEM((A_HEADS * BLOCK, LANES), BF16),
            pltpu.VMEM((8, A_HEADS * LANES), F32),
            pltpu.VMEM((A_KV_RANK, A_HEADS * LANES), F32),
            pltpu.VMEM((KEY_CHUNK + 8, A_HEADS * LANES), F32),
            pltpu.VMEM((KEY_CHUNK + 8, A_HEADS * LANES), F32),
        ],
        compiler_params=_cparams(("parallel", "arbitrary")),
        name="dsa",
    )(q, qi, wi, kf, ct, ki, wuv)


def _window_bias(branches, nrel):
    rel = jnp.arange(nrel + 1, dtype=I32)[:, None, None]
    krow = jnp.arange(KEY_CHUNK, dtype=I32)[None, :, None]
    qcol = jnp.arange(KEY_CHUNK, dtype=I32)[None, None, :]
    dist = KEY_CHUNK * (nrel - 1 - rel) + qcol - krow
    mult = sum(((dist >= 0) & (dist <= window) & (dist % dil == 0)).astype(F32) for window, dil in branches)
    return jnp.where(mult > 0, jnp.log2(jnp.maximum(mult, 1.0)), NEG_INF)


def _wattn_kernel(q_ref, k_ref, vt_ref, bias_ref, o_ref, acc_ref, sa_ref, sb_ref, p_ref, *, nh, nrel):
    kc = qb = KEY_CHUNK
    last = k_ref.shape[0] // kc - 1
    top = pl.program_id(1)
    c_lo = jnp.maximum(top - (nrel - 1), 0)
    npairs = (top + 2 - c_lo) // 2

    acc_ref[...] = jnp.zeros(acc_ref.shape, F32)

    def logits(c, s_ref):
        off = pl.multiple_of(jnp.minimum(c, last) * kc, kc)
        bias = bias_ref[c - top + (nrel - 1)]
        for h in range(nh):
            cols = slice(h * LANES, (h + 1) * LANES)
            t = lax.dot_general(k_ref[pl.ds(off, kc), cols], q_ref[:, cols], _NT,
                                preferred_element_type=F32) + bias
            s_ref[0:kc, h * qb:(h + 1) * qb] = t
            s_ref[kc:kc + 1, h * qb:(h + 1) * qb] = jnp.max(t, axis=0, keepdims=True)

    def consume(s_ref, c, carry):
        m_prev, l_prev = carry
        m_new = jnp.maximum(m_prev, s_ref[kc:kc + 1, :])
        alpha = jnp.exp2(m_prev - m_new)
        p_ref[...] = jnp.exp2(s_ref[0:kc, :] - m_new).astype(BF16)
        l_new = alpha * l_prev + jnp.dot(jnp.ones((16, kc), BF16), p_ref[...], preferred_element_type=F32)[0:1, :]
        cd = jnp.minimum(c, last)
        for h in range(nh):
            cols = slice(h * qb, (h + 1) * qb)
            acc_ref[:, cols] = alpha[:, cols] * acc_ref[:, cols] + jnp.dot(
                vt_ref[cd, h * LANES:(h + 1) * LANES, :], p_ref[:, cols], preferred_element_type=F32)
        return m_new, l_new

    logits(c_lo, sa_ref)

    def pair_body(j, carry):
        c0 = c_lo + 2 * j
        logits(c0 + 1, sb_ref)
        carry = consume(sa_ref, c0, carry)
        logits(jnp.minimum(c0 + 2, c_lo + 2 * npairs - 1), sa_ref)
        return consume(sb_ref, c0 + 1, carry)

    _, l_fin = _loop_k_per_trip(0, npairs, pair_body,
                                (jnp.full((1, nh * qb), NEG_INF, F32), jnp.zeros((1, nh * qb), F32)))

    inv_l = 1.0 / l_fin
    for h in range(nh):
        cols = slice(h * qb, (h + 1) * qb)
        o_ref[:, h * LANES:(h + 1) * LANES] = (acc_ref[:, cols] * inv_l[:, cols]).T.astype(BF16)


def _swa_kernel(sink_ref, q_ref, kp_ref, kc_ref, vp_ref, vc_ref, o_ref, *, subheads, max_dist):
    n = pl.program_id(1)
    qi = lax.broadcasted_iota(I32, (BLOCK, 2 * BLOCK), 0)
    kj = lax.broadcasted_iota(I32, (BLOCK, 2 * BLOCK), 1)
    dist = BLOCK + qi - kj
    mask = (dist >= 0) & (dist <= max_dist) & ((kj >= BLOCK) | (n > 0))
    for j, subs in enumerate(subheads):
        qt = q_ref[:, j * LANES:(j + 1) * LANES]
        o_t = None
        for kt, vt, hidx in subs:
            kk = jnp.concatenate([kp_ref[:, kt * LANES:(kt + 1) * LANES],
                                  kc_ref[:, kt * LANES:(kt + 1) * LANES]], axis=0)
            vv = jnp.concatenate([vp_ref[:, vt * LANES:(vt + 1) * LANES],
                                  vc_ref[:, vt * LANES:(vt + 1) * LANES]], axis=0)
            s = lax.dot_general(qt, kk, _NT, preferred_element_type=F32)
            s = jnp.where(mask, s, NEG_INF)
            sk = sink_ref[hidx] * LOG2E
            m = jnp.maximum(jnp.max(s, axis=-1, keepdims=True), sk)
            p = jnp.exp2(s - m)
            l = jnp.sum(p, axis=-1, keepdims=True) + jnp.exp2(sk - m)
            o_s = jnp.dot(p.astype(BF16), vv, preferred_element_type=F32) / l
            o_t = o_s if o_t is None else o_t + o_s
        o_ref[:, j * LANES:(j + 1) * LANES] = o_t.astype(BF16)


def _swa(q, k, v, sinks, *, subheads, max_dist):
    B, S, wq = q.shape
    wk, wv = k.shape[-1], v.shape[-1]
    cur = lambda width: pl.BlockSpec((None, BLOCK, width), lambda b, i: (b, i, 0))
    prv = lambda width: pl.BlockSpec((None, BLOCK, width), lambda b, i: (b, jnp.maximum(i - 1, 0), 0))
    return pl.pallas_call(
        functools.partial(_swa_kernel, subheads=subheads, max_dist=max_dist),
        grid=(B, S // BLOCK),
        in_specs=[pl.BlockSpec(memory_space=pltpu.SMEM), cur(wq), prv(wk), cur(wk), prv(wv), cur(wv)],
        out_specs=cur(wq),
        out_shape=jax.ShapeDtypeStruct((B, S, wq), BF16),
        compiler_params=_cparams(("parallel", "arbitrary")),
        name="swa",
    )(sinks, q, k, k, v, v)


def _wattn(q, k, vt, branches):
    B, S, wq = q.shape
    widest = max(w for w, _ in branches)
    nrel = min(-(-widest // KEY_CHUNK) + 1, S // KEY_CHUNK)
    bias = _window_bias(branches, nrel)
    nh = wq // LANES
    blk = lambda width: pl.BlockSpec((None, KEY_CHUNK, width), lambda b, i: (b, i, 0))
    return pl.pallas_call(
        functools.partial(_wattn_kernel, nh=nh, nrel=nrel),
        grid=(B, S // KEY_CHUNK),
        in_specs=[blk(wq),
                  pl.BlockSpec((None, S, wq), lambda b, i: (b, 0, 0)),
                  pl.BlockSpec((None,) + vt.shape[1:], lambda b, i: (b, 0, 0, 0)),
                  pl.BlockSpec(bias.shape, lambda b, i: (0, 0, 0))],
        out_specs=blk(wq),
        out_shape=jax.ShapeDtypeStruct((B, S, wq), BF16),
        scratch_shapes=[pltpu.VMEM((LANES, nh * KEY_CHUNK), F32),
                        pltpu.VMEM((KEY_CHUNK + 8, nh * KEY_CHUNK), F32),
                        pltpu.VMEM((KEY_CHUNK + 8, nh * KEY_CHUNK), F32),
                        pltpu.VMEM((KEY_CHUNK, nh * KEY_CHUNK), BF16)],
        compiler_params=_cparams(("parallel", "arbitrary")),
        name="wattn",
    )(q, k, vt, bias)


def _memkv_kernel(mem_ref, g_ref, w_ref, k_ref, v_ref):
    hb = _rms(mem_ref[...], g_ref[...]).astype(BF16)
    y = jnp.dot(hb, w_ref[...], preferred_element_type=F32)
    lane = lax.broadcasted_iota(I32, (y.shape[0], LANES), 1)
    for t in range(MEM_WIDTH // LANES):
        for out_ref, base in ((k_ref, 0), (v_ref, MEM_WIDTH)):
            tile = y[:, base + t * LANES:base + (t + 1) * LANES]
            out_ref[:, (2 * t) * LANES:(2 * t + 1) * LANES] = jnp.where(lane < MEM_HEAD_DIM, tile, 0.0).astype(BF16)
            out_ref[:, (2 * t + 1) * LANES:(2 * t + 2) * LANES] = jnp.where(lane >= MEM_HEAD_DIM, tile, 0.0).astype(BF16)


def _memkv(mem, g_mem, w):
    B, M, D = mem.shape
    L = w.shape[0]
    out = jax.ShapeDtypeStruct((L, B, M, 2 * MEM_WIDTH), BF16)
    ospec = pl.BlockSpec((None, None, M, 2 * MEM_WIDTH), lambda l, b: (l, b, 0, 0))
    return pl.pallas_call(
        _memkv_kernel,
        grid=(L, B),
        in_specs=[pl.BlockSpec((None, M, D), lambda l, b: (b, 0, 0)),
                  pl.BlockSpec((1, D), lambda l, b: (0, 0)),
                  pl.BlockSpec((None, D, 2 * MEM_WIDTH), lambda l, b: (l, 0, 0))],
        out_specs=(ospec, ospec),
        out_shape=(out, out),
        compiler_params=_cparams(("parallel", "parallel")),
        name="memkv",
    )(mem, g_mem, w)


def _mixer_residual(x_ref, mix_ref, qm_ref, mk_ref, mv_ref, w_ref):
    mw = mix_ref.shape[-1]
    n_sub = 2 * MEM_WIDTH // LANES
    logits = [lax.dot_general(qm_ref[:, (u // 2) * LANES:(u // 2 + 1) * LANES], mk_ref[:, u * LANES:(u + 1) * LANES],
                              _NT, preferred_element_type=F32) for u in range(n_sub)]
    y = x_ref[...] + jnp.dot(mix_ref[...], w_ref[0:mw, :], preferred_element_type=F32)
    tiles = []
    for t in range(n_sub // 2):
        o_t = None
        for u in (2 * t, 2 * t + 1):
            p = jnp.exp2(logits[u] - jnp.max(logits[u], axis=-1, keepdims=True))
            l = jnp.sum(p, axis=-1, keepdims=True)
            o_s = jnp.dot(p.astype(BF16), mv_ref[:, u * LANES:(u + 1) * LANES], preferred_element_type=F32) / l
            o_t = o_s if o_t is None else o_t + o_s
        tiles.append(o_t.astype(BF16))
    return y + jnp.dot(jnp.concatenate(tiles, axis=1), w_ref[mw:mw + MEM_WIDTH, :], preferred_element_type=F32)


def _out_kernel(x_ref, mix_ref, qm_ref, mk_ref, mv_ref, w_ref, o_ref):
    o_ref[...] = _mixer_residual(x_ref, mix_ref, qm_ref, mk_ref, mv_ref, w_ref)


def _out_proj(x, mix, qm, mk, mv, w, tm):
    B, S, D = x.shape
    row = lambda width: pl.BlockSpec((None, tm, width), lambda b, i: (b, i, 0))
    mem = pl.BlockSpec((None, mk.shape[1], 2 * MEM_WIDTH), lambda b, i: (b, 0, 0))
    return pl.pallas_call(
        _out_kernel,
        grid=(B, S // tm),
        in_specs=[row(D), row(mix.shape[-1]), row(MEM_WIDTH), mem, mem,
                  pl.BlockSpec(w.shape, lambda b, i: (0, 0))],
        out_specs=row(D),
        out_shape=jax.ShapeDtypeStruct((B, S, D), F32),
        compiler_params=_cparams(("parallel", "parallel")),
        name="out_proj",
    )(x, mix, qm, mk, mv, w)


def _ffn_kernel(x_ref, g_ref, wup_ref, cw_ref, cb_ref, wdn_ref, gf_ref, o_ref, carry_ref, *, tm, cw, final):
    dff = wdn_ref.shape[0]
    first = pl.program_id(1) == 0
    x = x_ref[...]
    hb = _rms(x, g_ref[...]).astype(BF16)
    row = lax.broadcasted_iota(I32, (tm, cw), 0)
    acc = x

    def up(c0):
        return (jnp.dot(hb, wup_ref[:, c0:c0 + cw], preferred_element_type=F32),
                jnp.dot(hb, wup_ref[:, dff + c0:dff + c0 + cw], preferred_element_type=F32))

    nxt = up(0)
    for c0 in range(0, dff, cw):
        a, b = nxt
        if c0 + cw < dff:
            nxt = up(c0 + cw)
        prev = jnp.where(first, 0.0, carry_ref[:, c0:c0 + cw])
        p1, p2 = prev[7:8, :], prev[6:7, :]
        a1 = jnp.where(row == 0, p1, pltpu.roll(a, 1, 0))
        a2 = jnp.where(row == 0, p2, jnp.where(row == 1, p1, pltpu.roll(a, 2, 0)))
        carry_ref[:, c0:c0 + cw] = a[tm - 8:tm, :]
        w = cw_ref[:, c0:c0 + cw]
        conv = w[0:1, :] * a2 + w[1:2, :] * a1 + w[2:3, :] * a + cb_ref[:, c0:c0 + cw]
        gated = conv / (1.0 + jnp.exp(-conv)) * b
        acc = acc + jnp.dot(gated.astype(BF16), wdn_ref[c0:c0 + cw, :], preferred_element_type=F32)
    if final:
        acc = _rms(acc, gf_ref[...])
    o_ref[...] = acc


def _ffn(x, g, wup, cw, cb, wdn, gf, tm, final):
    B, S, D = x.shape
    dff = wdn.shape[0]
    row = pl.BlockSpec((None, tm, D), lambda b, i: (b, i, 0))
    const = lambda shape: pl.BlockSpec(shape, lambda b, i: (0, 0), pipeline_mode=pl.Buffered(1))
    return pl.pallas_call(
        functools.partial(_ffn_kernel, tm=tm, cw=2 * LANES, final=final),
        grid=(B, S // tm),
        in_specs=[row, const((1, D)), const(wup.shape), const(cw.shape), const((1, dff)), const(wdn.shape),
                  const((1, D))],
        out_specs=row,
        out_shape=jax.ShapeDtypeStruct((B, S, D), F32),
        scratch_shapes=[pltpu.VMEM((8, dff), F32)],
        compiler_params=_cparams(("arbitrary", "arbitrary")),
        name="ffn",
    )(x, g, wup, cw, cb, wdn, gf)


def _b_subheads():
    return tuple(((2 * (j // 2), 2 * (j // 2), 2 * j), (2 * (j // 2) + 1, 2 * (j // 2) + 1, 2 * j + 1))
                 for j in range(B_HEADS // 2))


def kernel(x, mem, positions, g_mix, g_ffn, g_mem, g_final, w_mem_kv, a_w_in, a_kv_norm, a_w_uk, a_w_uv, a_w_out,
           b_w_in, b_sinks, b_w_out, c_w_in, c_w_out, f_w_up, f_conv_w, f_conv_b, f_w_down):
    B, S, D = x.shape
    depth = g_mix.shape[0]
    tm = min(512, S)
    tab32 = _rope_table(positions, A_QK_DIM, A_ROPE_DIM)
    tab16 = _rope_table(positions, B_HEAD_DIM, B_HEAD_DIM // 4)
    mem_k, mem_v = _memkv(mem, g_mem.reshape(1, D), w_mem_kv.astype(BF16))
    conv_w = jnp.pad(f_conv_w, ((0, 0), (0, 8 - CONV_WIDTH), (0, 0)))
    for i in range(depth):
        kind, j = i % 3, i // 3
        g = g_mix[i].reshape(1, D)
        if kind == 0:
            wuk = jnp.pad(a_w_uk[j], ((0, 0), (0, 0), (A_ROPE_DIM, 0))).reshape(A_KV_RANK, -1).astype(BF16)
            wuv = jnp.transpose(a_w_uv[j], (1, 0, 2)).astype(BF16)
            q, kf, ct, qi, ki, wi, qm = _proj_a(x, g, _prep_a_w_in(a_w_in[j]), a_kv_norm[j].reshape(1, -1), wuk,
                                                tab32, tab16, tm)
            mix = _dsa(q, qi, wi, kf, ct, ki, wuv)
            w_out = a_w_out[j]
        elif kind == 1:
            nq, nkv = B_HEADS * B_HEAD_DIM, 4 * B_KV_HEADS * B_HEAD_DIM
            q, k, v, qm = _proj_qkv(x, g, _prep_b_w_in(b_w_in[j]), tab16, tm, B_HEAD_DIM, B_HEAD_DIM // 4,
                                    nq, nkv, nkv, False)
            mix = _swa(q, k, v, b_sinks[j], subheads=_b_subheads(), max_dist=B_WINDOW - 1)
            w_out = b_w_out[j]
        else:
            nq = C_HEADS * C_HEAD_DIM
            q, k, v, qm = _proj_qkv(x, g, c_w_in[j].astype(BF16), tab32, tm, C_HEAD_DIM, C_HEAD_DIM // 4,
                                    nq, nq, nq, True)
            mix = _wattn(q, k, v, C_BRANCHES)
            w_out = c_w_out[j]
        x = _out_proj(x, mix, qm, mem_k[i], mem_v[i], w_out.astype(BF16), tm)
        x = _ffn(x, g_ffn[i].reshape(1, D), f_w_up[i].astype(BF16), conv_w[i], f_conv_b[i].reshape(1, -1),
                 f_w_down[i].astype(BF16), g_final.reshape(1, D), min(256, S), i == depth - 1)
    return x
```

```python
import functools

import jax
import jax.numpy as jnp
import numpy as np
from jax import lax
from jax.experimental import pallas as pl
from jax.experimental.pallas import tpu as pltpu

F32 = jnp.float32
BF16 = jnp.bfloat16
I32 = jnp.int32

LANES = 128
BLOCK = 128
ROPE_THETA = 500000.0
EPS = 1e-6
NEG_INF = -1e30
LOG2E = 1.4426950408889634
INT_MIN = -(2**31)

A_HEADS = 8
A_QK_DIM = 128
A_ROPE_DIM = 32
A_KV_RANK = 256
IDX_HEADS = 16
IDX_DIM = 64
TOPK_MAX = 256
B_HEADS = 16
B_KV_HEADS = 4
B_HEAD_DIM = 64
B_WINDOW = 128
C_HEADS = 8
C_HEAD_DIM = 128
C_BRANCHES = ((128, 1), (512, 4), (2048, 16))
MEM_HEADS = 4
MEM_HEAD_DIM = 64
MEM_WIDTH = MEM_HEADS * MEM_HEAD_DIM
CONV_WIDTH = 3
KEY_CHUNK = 256

VMEM_LIMIT = 56 * 1024 * 1024

_NT = (((1,), (1,)), ((), ()))


def _cparams(sem):
    return pltpu.CompilerParams(dimension_semantics=sem, vmem_limit_bytes=VMEM_LIMIT)


def _rms(x, g):
    return x * lax.rsqrt(jnp.mean(x * x, axis=-1, keepdims=True) + EPS) * g


def _loop_k_per_trip(lo, hi, body, carry, k=2):
    if k == 1:
        return lax.fori_loop(lo, hi, body, carry)
    trips = (hi - lo) // k

    def several(t, c):
        for u in range(k):
            c = body(lo + k * t + u, c)
        return c

    carry = lax.fori_loop(0, trips, several, carry)
    return _loop_k_per_trip(lo + k * trips, hi, body, carry, k // 2)


def _bit_transpose32(words):
    a = list(words)
    j, m = 16, 0x0000FFFF
    while j:
        mask = jnp.int32(m - (1 << 32) if m >= (1 << 31) else m)
        for k in range(32):
            if k & j == 0:
                t = (a[k] ^ lax.shift_right_logical(a[k + j], j)) & mask
                a[k] = a[k] ^ t
                a[k + j] = a[k + j] ^ lax.shift_left(t, j)
        j >>= 1
        m = (m ^ (m << j)) & 0xFFFFFFFF
    return a


def _rope_tile(t, tab, half):
    c, sa, sb = tab[:, 0:LANES], tab[:, LANES:2 * LANES], tab[:, 2 * LANES:3 * LANES]
    return t * c + pltpu.roll(t, half, 1) * sa + pltpu.roll(t, LANES - half, 1) * sb


def _rope_table(positions, head_dim, rot):
    half = rot // 2
    inv = ROPE_THETA ** (-jnp.arange(0, rot, 2, dtype=F32) / rot)
    ang = positions.astype(F32)[..., None] * inv
    cs = jnp.concatenate([jnp.cos(ang), jnp.sin(ang)], axis=-1)
    sel = np.zeros((rot, 3 * LANES), np.float32)
    one = np.zeros((3 * LANES,), np.float32)
    for l in range(LANES):
        j = l % head_dim
        if j < half:
            sel[j, l] = 1.0
            sel[half + j, 2 * LANES + l] = -1.0
        elif j < rot:
            sel[j - half, l] = 1.0
            sel[j, LANES + l] = 1.0
        else:
            one[l] = 1.0
    return jnp.dot(cs, jnp.asarray(sel), precision=lax.Precision.HIGHEST) + jnp.asarray(one)


def _staggered(mm, jobs):
    nxt = mm(jobs[0][0], jobs[0][1])
    for n, (_, _, epilogue) in enumerate(jobs):
        y = nxt
        if n + 1 < len(jobs):
            nxt = mm(jobs[n + 1][0], jobs[n + 1][1])
        epilogue(y)


def _proj_a_kernel(x_ref, g_ref, w_ref, kvn_ref, wuk_ref, t32_ref, t16_ref,
                   q_ref, kf_ref, ct_ref, qi_ref, ki_ref, wi_ref, qm_ref, *, tm):
    hb = _rms(x_ref[...], g_ref[...]).astype(BF16)
    t32 = t32_ref[...]
    t16 = t16_ref[...]

    def mm(a, b):
        return jnp.dot(hb, w_ref[:, a:b], preferred_element_type=F32)

    def roped(ref, col, tab, half, scale):
        def epilogue(y):
            for u in range(y.shape[1] // LANES):
                t = _rope_tile(y[:, u * LANES:(u + 1) * LANES], tab, half)
                if scale != 1.0:
                    t = t * scale
                ref[:, col + u * LANES:col + (u + 1) * LANES] = t.astype(BF16)
        return epilogue

    latent_bf16 = []

    def latent(y):
        c = _rms(y, kvn_ref[...])
        latent_bf16.append(c.astype(BF16))
        for u in range(tm // KEY_CHUNK):
            ct_ref[u] = c[u * KEY_CHUNK:(u + 1) * KEY_CHUNK, :].T.astype(BF16)

    def full_keys(y):
        k_rope = _rope_tile(y, t32, A_ROPE_DIM // 2)
        k_nope = jnp.dot(latent_bf16[0], wuk_ref[...], preferred_element_type=F32)
        for h in range(A_HEADS):
            kf_ref[:, h * LANES:(h + 1) * LANES] = (k_nope[:, h * LANES:(h + 1) * LANES] + k_rope).astype(BF16)

    def index_keys(y):
        for u in range(2):
            t = _rope_tile(y[:, u * LANES:(u + 1) * LANES], t16, IDX_DIM // 8).astype(BF16)
            for v in range(tm // KEY_CHUNK):
                ki_ref[v, u * KEY_CHUNK:(u + 1) * KEY_CHUNK, :] = t[v * KEY_CHUNK:(v + 1) * KEY_CHUNK, :]

    def index_weights(y):
        wi_ref[...] = y * (IDX_HEADS * IDX_DIM) ** -0.5

    def mem_query(y):
        qm_ref[...] = (y * (MEM_HEAD_DIM ** -0.5 * LOG2E)).astype(BF16)

    jobs, o = [], 0
    for j in range(0, A_HEADS * A_QK_DIM, 2 * LANES):
        jobs.append((o + j, o + j + 2 * LANES, roped(q_ref, j, t32, A_ROPE_DIM // 2, A_QK_DIM ** -0.5 * LOG2E)))
    o += A_HEADS * A_QK_DIM
    jobs.append((o, o + A_KV_RANK, latent))
    o += A_KV_RANK
    jobs.append((o, o + LANES, full_keys))
    o += LANES
    for j in range(0, IDX_HEADS * IDX_DIM, 2 * LANES):
        jobs.append((o + j, o + j + 2 * LANES, roped(qi_ref, j, t16, IDX_DIM // 8, 1.0)))
    o += IDX_HEADS * IDX_DIM
    jobs.append((o, o + 2 * LANES, index_keys))
    o += 2 * LANES
    jobs.append((o, o + LANES, index_weights))
    o += LANES
    jobs.append((o, o + MEM_WIDTH, mem_query))
    _staggered(mm, jobs)


def _prep_a_w_in(w):
    d = w.shape[0]
    sizes = (A_HEADS * A_QK_DIM, A_KV_RANK, A_ROPE_DIM, IDX_HEADS * IDX_DIM, IDX_DIM, IDX_HEADS, MEM_WIDTH)
    offs = [0]
    for s in sizes:
        offs.append(offs[-1] + s)
    q, ckv, kr, qi, ki, wi, qm = [w[:, offs[i]:offs[i + 1]] for i in range(len(sizes))]
    z = lambda n: jnp.zeros((d, n), w.dtype)
    return jnp.concatenate([
        q, ckv, kr, z(LANES - A_ROPE_DIM), qi,
        ki, z(LANES - IDX_DIM), z(LANES - IDX_DIM), ki,
        wi, z(LANES - IDX_HEADS), qm], axis=1).astype(BF16)


def _proj_a(x, g, w, kvn, wuk, t32, t16, tm):
    B, S, D = x.shape
    n = w.shape[1]
    row = lambda width: pl.BlockSpec((None, tm, width), lambda b, i: (b, i, 0))
    const = lambda shape: pl.BlockSpec(shape, lambda b, i: (0,) * len(shape))
    out_shape = (
        jax.ShapeDtypeStruct((B, S, A_HEADS * A_QK_DIM), BF16),
        jax.ShapeDtypeStruct((B, S, A_HEADS * A_QK_DIM), BF16),
        jax.ShapeDtypeStruct((B, S // KEY_CHUNK, A_KV_RANK, KEY_CHUNK), BF16),
        jax.ShapeDtypeStruct((B, S, IDX_HEADS * IDX_DIM), BF16),
        jax.ShapeDtypeStruct((B, S // KEY_CHUNK, 2 * KEY_CHUNK, LANES), BF16),
        jax.ShapeDtypeStruct((B, S, LANES), F32),
        jax.ShapeDtypeStruct((B, S, MEM_WIDTH), BF16),
    )
    out_specs = (
        row(A_HEADS * A_QK_DIM), row(A_HEADS * A_QK_DIM),
        pl.BlockSpec((None, tm // KEY_CHUNK, A_KV_RANK, KEY_CHUNK), lambda b, i: (b, i, 0, 0)),
        row(IDX_HEADS * IDX_DIM),
        pl.BlockSpec((None, tm // KEY_CHUNK, 2 * KEY_CHUNK, LANES), lambda b, i: (b, i, 0, 0)),
        row(LANES), row(MEM_WIDTH),
    )
    return pl.pallas_call(
        functools.partial(_proj_a_kernel, tm=tm),
        grid=(B, S // tm),
        in_specs=[row(D), const((1, D)), const((D, n)), const((1, A_KV_RANK)), const(wuk.shape),
                  row(3 * LANES), row(3 * LANES)],
        out_specs=out_specs,
        out_shape=out_shape,
        compiler_params=_cparams(("parallel", "parallel")),
        name="proj_a",
    )(x, g, w, kvn, wuk, t32, t16)


def _proj_qkv_kernel(x_ref, g_ref, w_ref, tab_ref, q_ref, k_ref, v_ref, qm_ref, *,
                     head_dim, rot, nq, nk, nv, v_transposed):
    hb = _rms(x_ref[...], g_ref[...]).astype(BF16)
    tm = hb.shape[0]
    tab = tab_ref[...]

    def mm(a, b):
        return jnp.dot(hb, w_ref[:, a:b], preferred_element_type=F32)

    def roped(ref, col, scale):
        def epilogue(y):
            for u in range(2):
                t = _rope_tile(y[:, u * LANES:(u + 1) * LANES], tab, rot // 2)
                if scale != 1.0:
                    t = t * scale
                ref[:, col + u * LANES:col + (u + 1) * LANES] = t.astype(BF16)
        return epilogue

    def value(col):
        def epilogue(y):
            if v_transposed:
                for u in range(tm // KEY_CHUNK):
                    v_ref[u, col:col + 2 * LANES, :] = y[u * KEY_CHUNK:(u + 1) * KEY_CHUNK, :].T.astype(BF16)
            else:
                v_ref[:, col:col + 2 * LANES] = y.astype(BF16)
        return epilogue

    def mem_query(y):
        qm_ref[...] = (y * (MEM_HEAD_DIM ** -0.5 * LOG2E)).astype(BF16)

    jobs = [(j, j + 2 * LANES, roped(q_ref, j, head_dim ** -0.5 * LOG2E)) for j in range(0, nq, 2 * LANES)]
    jobs += [(nq + j, nq + j + 2 * LANES, roped(k_ref, j, 1.0)) for j in range(0, nk, 2 * LANES)]
    jobs += [(nq + nk + j, nq + nk + j + 2 * LANES, value(j)) for j in range(0, nv, 2 * LANES)]
    jobs.append((nq + nk + nv, nq + nk + nv + MEM_WIDTH, mem_query))
    _staggered(mm, jobs)


def _proj_qkv(x, g, w, tab, tm, head_dim, rot, nq, nk, nv, v_transposed):
    B, S, D = x.shape
    n = w.shape[1]
    row = lambda width: pl.BlockSpec((None, tm, width), lambda b, i: (b, i, 0))
    const = lambda shape: pl.BlockSpec(shape, lambda b, i: (0,) * len(shape))
    if v_transposed:
        v_spec = pl.BlockSpec((None, tm // KEY_CHUNK, nv, KEY_CHUNK), lambda b, i: (b, i, 0, 0))
        v_shape = jax.ShapeDtypeStruct((B, S // KEY_CHUNK, nv, KEY_CHUNK), BF16)
    else:
        v_spec, v_shape = row(nv), jax.ShapeDtypeStruct((B, S, nv), BF16)
    return pl.pallas_call(
        functools.partial(_proj_qkv_kernel, head_dim=head_dim, rot=rot, nq=nq, nk=nk, nv=nv,
                          v_transposed=v_transposed),
        grid=(B, S // tm),
        in_specs=[row(D), const((1, D)), const((D, n)), row(3 * LANES)],
        out_specs=(row(nq), row(nk), v_spec, row(MEM_WIDTH)),
        out_shape=(jax.ShapeDtypeStruct((B, S, nq), BF16), jax.ShapeDtypeStruct((B, S, nk), BF16), v_shape,
                   jax.ShapeDtypeStruct((B, S, MEM_WIDTH), BF16)),
        compiler_params=_cparams(("parallel", "parallel")),
        name="proj_qkv",
    )(x, g, w, tab)


def _prep_b_w_in(w):
    d = w.shape[0]
    nq, nkv = B_HEADS * B_HEAD_DIM, B_KV_HEADS * B_HEAD_DIM
    q, k, v, qm = w[:, :nq], w[:, nq:nq + nkv], w[:, nq + nkv:nq + 2 * nkv], w[:, nq + 2 * nkv:]
    z = jnp.zeros((d, B_HEAD_DIM), w.dtype)

    def spread(t):
        cols = []
        for h in range(B_KV_HEADS):
            th = t[:, h * B_HEAD_DIM:(h + 1) * B_HEAD_DIM]
            cols += [th, z, z, th]
        return jnp.concatenate(cols, axis=1)

    return jnp.concatenate([q, spread(k), spread(v), qm], axis=1).astype(BF16)


def _dsa_kernel(q_ref, qi_ref, wi_ref, kf_ref, ct_ref, ki_ref, wuv_ref, o_ref,
                keys_ref, planes_ref, qbd_ref, qi2_ref, wrow_ref, acc_ref, sa_ref, sb_ref, *, topk):
    kc = KEY_CHUNK
    hw = A_HEADS * LANES
    gw = 2 * LANES
    i = pl.program_id(1)
    nch = (i * BLOCK + BLOCK + kc - 1) // kc

    qbd_ref[...] = jnp.zeros(qbd_ref.shape, BF16)
    for h in range(A_HEADS):
        rows = slice(h * BLOCK, (h + 1) * BLOCK)
        qbd_ref[h // 2, (h % 2) * BLOCK:(h % 2 + 1) * BLOCK, (h % 2) * LANES:(h % 2 + 1) * LANES] = (
            q_ref[:, h * LANES:(h + 1) * LANES])
        qi2_ref[rows, :] = qi_ref[:, h * LANES:(h + 1) * LANES]
    w_t = wi_ref[...].T
    for j in range(IDX_HEADS // 2):
        wrow_ref[0:1, j * LANES:(j + 1) * LANES] = w_t[2 * j:2 * j + 1, :]
        wrow_ref[1:2, j * LANES:(j + 1) * LANES] = w_t[2 * j + 1:2 * j + 2, :]

    qpos = i * BLOCK + lax.broadcasted_iota(I32, (kc, BLOCK), 1)
    krow = lax.broadcasted_iota(I32, (kc, BLOCK), 0)

    def score_chunk(c):
        off = pl.multiple_of(c * kc, kc)
        kk = ki_ref[c]
        acc = None
        for g in range(0, hw, gw):
            d = lax.dot_general(kk, qi2_ref[g:g + gw, :], _NT, preferred_element_type=F32)
            t = (jnp.maximum(d[0:kc], 0.0) * wrow_ref[0:1, g:g + gw]
                 + jnp.maximum(d[kc:2 * kc], 0.0) * wrow_ref[1:2, g:g + gw])
            for u in range(0, gw, LANES):
                acc = t[:, u:u + LANES] if acc is None else acc + t[:, u:u + LANES]
        bits = pltpu.bitcast(acc, I32)
        key = jnp.where(bits < 0, bits ^ 0x7FFFFFFF, bits)
        key = jnp.where(off + krow <= qpos, key, INT_MIN)
        keys_ref[pl.ds(off, kc), :] = key
        planes = _bit_transpose32([(key[8 * r:8 * r + 8, :] ^ INT_MIN) for r in range(kc // 8)])
        for p in range(32):
            planes_ref[c, p] = planes[p]

    @pl.when(i == 0)
    def _():
        planes_ref[...] = jnp.zeros(planes_ref.shape, I32)

    def score_body(c, carry):
        score_chunk(c)
        return carry

    _loop_k_per_trip(0, nch, score_body, 0, k=4)

    n_chunks = planes_ref.shape[0]

    def bit_body(p, carry):
        alive, need, res = carry
        ones = [a & planes_ref[c, p] for c, a in enumerate(alive)]
        cnt = ones[0] * 0
        for o in ones:
            cnt = cnt + lax.population_count(o)
        cnt = jnp.sum(cnt, axis=0, keepdims=True)
        take = cnt >= need
        alive = [jnp.where(take, o, a ^ o) for o, a in zip(ones, alive)]
        need = jnp.where(take, need, need - cnt)
        res = jnp.where(take, res | lax.shift_left(jnp.int32(1), 31 - p), res)
        return alive, need, res

    for g in range(0, hw, gw):
        sa_ref[0:kc, g:g + gw] = lax.dot_general(kf_ref[0:kc, g:g + gw], qbd_ref[g // gw], _NT,
                                                 preferred_element_type=F32)

    carry = ([jnp.where(c < nch, jnp.full((8, BLOCK), -1, I32), 0) for c in range(n_chunks)],
             jnp.full((1, BLOCK), topk, I32), jnp.zeros((1, BLOCK), I32))
    for p in range(32):
        carry = bit_body(p, carry)
    thr = jnp.maximum(carry[2] ^ INT_MIN, INT_MIN + 1)

    acc_ref[...] = jnp.zeros(acc_ref.shape, F32)
    bias0 = jnp.where(keys_ref[0:kc, :] >= thr, 0.0, NEG_INF)
    bias0 = jnp.concatenate([bias0] * (gw // LANES), axis=1)
    for g in range(0, hw, gw):
        t = sa_ref[0:kc, g:g + gw] + bias0
        sa_ref[0:kc, g:g + gw] = t
        sa_ref[kc:kc + 1, g:g + gw] = jnp.max(t, axis=0, keepdims=True)

    def logits(c, s_ref):
        off = pl.multiple_of(c * kc, kc)
        bias = jnp.where(keys_ref[pl.ds(off, kc), :] >= thr, 0.0, NEG_INF)
        bias = jnp.concatenate([bias] * (gw // LANES), axis=1)
        for g in range(0, hw, gw):
            t = lax.dot_general(kf_ref[pl.ds(off, kc), g:g + gw], qbd_ref[g // gw], _NT,
                                preferred_element_type=F32) + bias
            s_ref[0:kc, g:g + gw] = t
            s_ref[kc:kc + 1, g:g + gw] = jnp.max(t, axis=0, keepdims=True)

    def consume(s_ref, c, carry):
        m_prev, l_prev = carry
        ct = ct_ref[c]
        m_out, l_out = [], []
        for g in range(0, hw, gw):
            s = s_ref[0:kc, g:g + gw]
            m_new = jnp.maximum(m_prev[:, g:g + gw], s_ref[kc:kc + 1, g:g + gw])
            alpha = jnp.exp2(m_prev[:, g:g + gw] - m_new)
            p = jnp.exp2(s - m_new)
            l_out.append(alpha * l_prev[:, g:g + gw] + jnp.sum(p, axis=0, keepdims=True))
            m_out.append(m_new)
            acc_ref[:, g:g + gw] = alpha * acc_ref[:, g:g + gw] + jnp.dot(
                ct, p.astype(BF16), preferred_element_type=F32)
        return jnp.concatenate(m_out, axis=1), jnp.concatenate(l_out, axis=1)

    def pair_body(j, carry):
        c0 = 2 * j
        logits(c0 + 1, sb_ref)
        carry = consume(sa_ref, c0, carry)
        logits(jnp.minimum(c0 + 2, nch - 1), sa_ref)
        return consume(sb_ref, c0 + 1, carry)

    carry = _loop_k_per_trip(0, nch // 2, pair_body,
                             (jnp.full((1, hw), NEG_INF, F32), jnp.zeros((1, hw), F32)))
    _, l_fin = lax.fori_loop(0, nch % 2, lambda _, c: consume(sa_ref, nch - 1, c), carry)

    inv_l = 1.0 / l_fin
    for h in range(A_HEADS):
        cols = slice(h * LANES, (h + 1) * LANES)
        o_lat = (acc_ref[:, cols] * inv_l[:, cols]).T.astype(BF16)
        o_ref[:, cols] = jnp.dot(o_lat, wuv_ref[h], preferred_element_type=F32).astype(BF16)


def _dsa(q, qi, wi, kf, ct, ki, wuv):
    B, S, _ = q.shape
    topk = min(TOPK_MAX, S // 4)
    blk = lambda width: pl.BlockSpec((None, BLOCK, width), lambda b, i: (b, i, 0))
    seq = lambda width: pl.BlockSpec((None, S, width), lambda b, i: (b, 0, 0))
    const3 = lambda shape: pl.BlockSpec(shape, lambda b, i: (0, 0, 0))
    return pl.pallas_call(
        functools.partial(_dsa_kernel, topk=topk),
        grid=(B, S // BLOCK),
        in_specs=[blk(A_HEADS * A_QK_DIM), blk(IDX_HEADS * IDX_DIM), blk(LANES),
                  seq(A_HEADS * A_QK_DIM),
                  pl.BlockSpec((None, S // KEY_CHUNK, A_KV_RANK, KEY_CHUNK), lambda b, i: (b, 0, 0, 0)),
                  pl.BlockSpec((None, S // KEY_CHUNK, 2 * KEY_CHUNK, LANES), lambda b, i: (b, 0, 0, 0)),
                  const3(wuv.shape)],
        out_specs=blk(A_HEADS * LANES),
        out_shape=jax.ShapeDtypeStruct((B, S, A_HEADS * LANES), BF16),
        scratch_shapes=[
            pltpu.VMEM((S, BLOCK), I32),
            pltpu.VMEM((S // KEY_CHUNK, 32, 8, BLOCK), I32),
            pltpu.VMEM((A_HEADS // 2, 2 * BLOCK, 2 * LANES), BF16),
            pltpu.VMEM((A_HEADS * BLOCK, LANES), BF16),
            pltpu.VMEM((8, A_HEADS * LANES), F32),
            pltpu.VMEM((A_KV_RANK, A_HEADS * LANES), F32),
            pltpu.VMEM((KEY_CHUNK + 8, A_HEADS * LANES), F32),
            pltpu.VMEM((KEY_CHUNK + 8, A_HEADS * LANES), F32),
        ],
        compiler_params=_cparams(("parallel", "arbitrary")),
        name="dsa",
    )(q, qi, wi, kf, ct, ki, wuv)


def _window_bias(branches, nrel):
    rel = jnp.arange(nrel, dtype=I32)[:, None, None]
    krow = jnp.arange(KEY_CHUNK, dtype=I32)[None, :, None]
    qcol = jnp.arange(KEY_CHUNK, dtype=I32)[None, None, :]
    dist = KEY_CHUNK * (nrel - 1 - rel) + qcol - krow
    mult = sum(((dist >= 0) & (dist <= window) & (dist % dil == 0)).astype(F32) for window, dil in branches)
    return jnp.where(mult > 0, jnp.log2(jnp.maximum(mult, 1.0)), NEG_INF)


def _wattn_kernel(q_ref, k_ref, vt_ref, bias_ref, o_ref, acc_ref, sa_ref, sb_ref, p_ref, *, nh, nrel):
    kc = qb = KEY_CHUNK
    top = pl.program_id(1)
    c_lo = jnp.maximum(top - (nrel - 1), 0)
    n = top + 1 - c_lo

    acc_ref[...] = jnp.zeros(acc_ref.shape, F32)

    def logits(c, s_ref):
        off = pl.multiple_of(c * kc, kc)
        bias = bias_ref[c - top + (nrel - 1)]
        for h in range(nh):
            cols = slice(h * LANES, (h + 1) * LANES)
            t = lax.dot_general(k_ref[pl.ds(off, kc), cols], q_ref[:, cols], _NT,
                                preferred_element_type=F32) + bias
            s_ref[0:kc, h * qb:(h + 1) * qb] = t
            s_ref[kc:kc + 1, h * qb:(h + 1) * qb] = jnp.max(t, axis=0, keepdims=True)

    def consume(s_ref, c, carry):
        m_prev, l_prev = carry
        m_new = jnp.maximum(m_prev, s_ref[kc:kc + 1, :])
        alpha = jnp.exp2(m_prev - m_new)
        p_ref[...] = jnp.exp2(s_ref[0:kc, :] - m_new).astype(BF16)
        l_new = alpha * l_prev + jnp.dot(jnp.ones((16, kc), BF16), p_ref[...], preferred_element_type=F32)[0:1, :]
        for h in range(nh):
            cols = slice(h * qb, (h + 1) * qb)
            acc_ref[:, cols] = alpha[:, cols] * acc_ref[:, cols] + jnp.dot(
                vt_ref[c, h * LANES:(h + 1) * LANES, :], p_ref[:, cols], preferred_element_type=F32)
        return m_new, l_new

    logits(c_lo, sa_ref)

    def pair_body(j, carry):
        c0 = c_lo + 2 * j
        logits(c0 + 1, sb_ref)
        carry = consume(sa_ref, c0, carry)
        logits(jnp.minimum(c0 + 2, top), sa_ref)
        return consume(sb_ref, c0 + 1, carry)

    carry = _loop_k_per_trip(0, n // 2, pair_body,
                             (jnp.full((1, nh * qb), NEG_INF, F32), jnp.zeros((1, nh * qb), F32)))
    _, l_fin = lax.fori_loop(0, n % 2, lambda _, c: consume(sa_ref, top, c), carry)

    inv_l = 1.0 / l_fin
    for h in range(nh):
        cols = slice(h * qb, (h + 1) * qb)
        o_ref[:, h * LANES:(h + 1) * LANES] = (acc_ref[:, cols] * inv_l[:, cols]).T.astype(BF16)


def _swa_kernel(sink_ref, q_ref, kp_ref, kc_ref, vp_ref, vc_ref, o_ref, *, subheads, max_dist):
    n = pl.program_id(1)
    qi = lax.broadcasted_iota(I32, (BLOCK, 2 * BLOCK), 0)
    kj = lax.broadcasted_iota(I32, (BLOCK, 2 * BLOCK), 1)
    dist = BLOCK + qi - kj
    mask = (dist >= 0) & (dist <= max_dist) & ((kj >= BLOCK) | (n > 0))
    for j, subs in enumerate(subheads):
        qt = q_ref[:, j * LANES:(j + 1) * LANES]
        o_t = None
        for kt, vt, hidx in subs:
            kk = jnp.concatenate([kp_ref[:, kt * LANES:(kt + 1) * LANES],
                                  kc_ref[:, kt * LANES:(kt + 1) * LANES]], axis=0)
            vv = jnp.concatenate([vp_ref[:, vt * LANES:(vt + 1) * LANES],
                                  vc_ref[:, vt * LANES:(vt + 1) * LANES]], axis=0)
            s = lax.dot_general(qt, kk, _NT, preferred_element_type=F32)
            s = jnp.where(mask, s, NEG_INF)
            sk = sink_ref[hidx] * LOG2E
            m = jnp.maximum(jnp.max(s, axis=-1, keepdims=True), sk)
            p = jnp.exp2(s - m)
            l = jnp.sum(p, axis=-1, keepdims=True) + jnp.exp2(sk - m)
            o_s = jnp.dot(p.astype(BF16), vv, preferred_element_type=F32) / l
            o_t = o_s if o_t is None else o_t + o_s
        o_ref[:, j * LANES:(j + 1) * LANES] = o_t.astype(BF16)


def _swa(q, k, v, sinks, *, subheads, max_dist):
    B, S, wq = q.shape
    wk, wv = k.shape[-1], v.shape[-1]
    cur = lambda width: pl.BlockSpec((None, BLOCK, width), lambda b, i: (b, i, 0))
    prv = lambda width: pl.BlockSpec((None, BLOCK, width), lambda b, i: (b, jnp.maximum(i - 1, 0), 0))
    return pl.pallas_call(
        functools.partial(_swa_kernel, subheads=subheads, max_dist=max_dist),
        grid=(B, S // BLOCK),
        in_specs=[pl.BlockSpec(memory_space=pltpu.SMEM), cur(wq), prv(wk), cur(wk), prv(wv), cur(wv)],
        out_specs=cur(wq),
        out_shape=jax.ShapeDtypeStruct((B, S, wq), BF16),
        compiler_params=_cparams(("parallel", "arbitrary")),
        name="swa",
    )(sinks, q, k, k, v, v)


def _wattn(q, k, vt, branches):
    B, S, wq = q.shape
    widest = max(w for w, _ in branches)
    nrel = min(-(-widest // KEY_CHUNK) + 1, S // KEY_CHUNK)
    bias = _window_bias(branches, nrel)
    nh = wq // LANES
    blk = lambda width: pl.BlockSpec((None, KEY_CHUNK, width), lambda b, i: (b, i, 0))
    return pl.pallas_call(
        functools.partial(_wattn_kernel, nh=nh, nrel=nrel),
        grid=(B, S // KEY_CHUNK),
        in_specs=[blk(wq),
                  pl.BlockSpec((None, S, wq), lambda b, i: (b, 0, 0)),
                  pl.BlockSpec((None,) + vt.shape[1:], lambda b, i: (b, 0, 0, 0)),
                  pl.BlockSpec(bias.shape, lambda b, i: (0, 0, 0))],
        out_specs=blk(wq),
        out_shape=jax.ShapeDtypeStruct((B, S, wq), BF16),
        scratch_shapes=[pltpu.VMEM((LANES, nh * KEY_CHUNK), F32),
                        pltpu.VMEM((KEY_CHUNK + 8, nh * KEY_CHUNK), F32),
                        pltpu.VMEM((KEY_CHUNK + 8, nh * KEY_CHUNK), F32),
                        pltpu.VMEM((KEY_CHUNK, nh * KEY_CHUNK), BF16)],
        compiler_params=_cparams(("parallel", "arbitrary")),
        name="wattn",
    )(q, k, vt, bias)


def _memkv_kernel(mem_ref, g_ref, w_ref, k_ref, v_ref):
    hb = _rms(mem_ref[...], g_ref[...]).astype(BF16)
    y = jnp.dot(hb, w_ref[...], preferred_element_type=F32)
    lane = lax.broadcasted_iota(I32, (y.shape[0], LANES), 1)
    for t in range(MEM_WIDTH // LANES):
        for out_ref, base in ((k_ref, 0), (v_ref, MEM_WIDTH)):
            tile = y[:, base + t * LANES:base + (t + 1) * LANES]
            out_ref[:, (2 * t) * LANES:(2 * t + 1) * LANES] = jnp.where(lane < MEM_HEAD_DIM, tile, 0.0).astype(BF16)
            out_ref[:, (2 * t + 1) * LANES:(2 * t + 2) * LANES] = jnp.where(lane >= MEM_HEAD_DIM, tile, 0.0).astype(BF16)


def _memkv(mem, g_mem, w):
    B, M, D = mem.shape
    L = w.shape[0]
    out = jax.ShapeDtypeStruct((L, B, M, 2 * MEM_WIDTH), BF16)
    ospec = pl.BlockSpec((None, None, M, 2 * MEM_WIDTH), lambda l, b: (l, b, 0, 0))
    return pl.pallas_call(
        _memkv_kernel,
        grid=(L, B),
        in_specs=[pl.BlockSpec((None, M, D), lambda l, b: (b, 0, 0)),
                  pl.BlockSpec((1, D), lambda l, b: (0, 0)),
                  pl.BlockSpec((None, D, 2 * MEM_WIDTH), lambda l, b: (l, 0, 0))],
        out_specs=(ospec, ospec),
        out_shape=(out, out),
        compiler_params=_cparams(("parallel", "parallel")),
        name="memkv",
    )(mem, g_mem, w)


def _mixer_residual(x_ref, mix_ref, qm_ref, mk_ref, mv_ref, w_ref):
    mw = mix_ref.shape[-1]
    n_sub = 2 * MEM_WIDTH // LANES
    logits = [lax.dot_general(qm_ref[:, (u // 2) * LANES:(u // 2 + 1) * LANES], mk_ref[:, u * LANES:(u + 1) * LANES],
                              _NT, preferred_element_type=F32) for u in range(n_sub)]
    y = x_ref[...] + jnp.dot(mix_ref[...], w_ref[0:mw, :], preferred_element_type=F32)
    tiles = []
    for t in range(n_sub // 2):
        o_t = None
        for u in (2 * t, 2 * t + 1):
            p = jnp.exp2(logits[u] - jnp.max(logits[u], axis=-1, keepdims=True))
            l = jnp.sum(p, axis=-1, keepdims=True)
            o_s = jnp.dot(p.astype(BF16), mv_ref[:, u * LANES:(u + 1) * LANES], preferred_element_type=F32) / l
            o_t = o_s if o_t is None else o_t + o_s
        tiles.append(o_t.astype(BF16))
    return y + jnp.dot(jnp.concatenate(tiles, axis=1), w_ref[mw:mw + MEM_WIDTH, :], preferred_element_type=F32)


def _out_kernel(x_ref, mix_ref, qm_ref, mk_ref, mv_ref, w_ref, o_ref):
    o_ref[...] = _mixer_residual(x_ref, mix_ref, qm_ref, mk_ref, mv_ref, w_ref)


def _out_proj(x, mix, qm, mk, mv, w, tm):
    B, S, D = x.shape
    row = lambda width: pl.BlockSpec((None, tm, width), lambda b, i: (b, i, 0))
    mem = pl.BlockSpec((None, mk.shape[1], 2 * MEM_WIDTH), lambda b, i: (b, 0, 0))
    return pl.pallas_call(
        _out_kernel,
        grid=(B, S // tm),
        in_specs=[row(D), row(mix.shape[-1]), row(MEM_WIDTH), mem, mem,
                  pl.BlockSpec(w.shape, lambda b, i: (0, 0))],
        out_specs=row(D),
        out_shape=jax.ShapeDtypeStruct((B, S, D), F32),
        compiler_params=_cparams(("parallel", "parallel")),
        name="out_proj",
    )(x, mix, qm, mk, mv, w)


def _ffn_kernel(x_ref, g_ref, wup_ref, cw_ref, cb_ref, wdn_ref, gf_ref, o_ref, carry_ref, *, tm, cw, final):
    dff = wdn_ref.shape[0]
    first = pl.program_id(1) == 0
    x = x_ref[...]
    hb = _rms(x, g_ref[...]).astype(BF16)
    row = lax.broadcasted_iota(I32, (tm, cw), 0)
    acc = x

    def up(c0):
        return (jnp.dot(hb, wup_ref[:, c0:c0 + cw], preferred_element_type=F32),
                jnp.dot(hb, wup_ref[:, dff + c0:dff + c0 + cw], preferred_element_type=F32))

    nxt = up(0)
    for c0 in range(0, dff, cw):
        a, b = nxt
        if c0 + cw < dff:
            nxt = up(c0 + cw)
        prev = jnp.where(first, 0.0, carry_ref[:, c0:c0 + cw])
        p1, p2 = prev[7:8, :], prev[6:7, :]
        a1 = jnp.where(row == 0, p1, pltpu.roll(a, 1, 0))
        a2 = jnp.where(row == 0, p2, jnp.where(row == 1, p1, pltpu.roll(a, 2, 0)))
        carry_ref[:, c0:c0 + cw] = a[tm - 8:tm, :]
        w = cw_ref[:, c0:c0 + cw]
        conv = w[0:1, :] * a2 + w[1:2, :] * a1 + w[2:3, :] * a + cb_ref[:, c0:c0 + cw]
        gated = conv / (1.0 + jnp.exp(-conv)) * b
        acc = acc + jnp.dot(gated.astype(BF16), wdn_ref[c0:c0 + cw, :], preferred_element_type=F32)
    if final:
        acc = _rms(acc, gf_ref[...])
    o_ref[...] = acc


def _ffn(x, g, wup, cw, cb, wdn, gf, tm, final):
    B, S, D = x.shape
    dff = wdn.shape[0]
    row = pl.BlockSpec((None, tm, D), lambda b, i: (b, i, 0))
    const = lambda shape: pl.BlockSpec(shape, lambda b, i: (0, 0), pipeline_mode=pl.Buffered(1))
    return pl.pallas_call(
        functools.partial(_ffn_kernel, tm=tm, cw=2 * LANES, final=final),
        grid=(B, S // tm),
        in_specs=[row, const((1, D)), const(wup.shape), const(cw.shape), const((1, dff)), const(wdn.shape),
                  const((1, D))],
        out_specs=row,
        out_shape=jax.ShapeDtypeStruct((B, S, D), F32),
        scratch_shapes=[pltpu.VMEM((8, dff), F32)],
        compiler_params=_cparams(("arbitrary", "arbitrary")),
        name="ffn",
    )(x, g, wup, cw, cb, wdn, gf)


def _b_subheads():
    return tuple(((2 * (j // 2), 2 * (j // 2), 2 * j), (2 * (j // 2) + 1, 2 * (j // 2) + 1, 2 * j + 1))
                 for j in range(B_HEADS // 2))


def kernel(x, mem, positions, g_mix, g_ffn, g_mem, g_final, w_mem_kv, a_w_in, a_kv_norm, a_w_uk, a_w_uv, a_w_out,
           b_w_in, b_sinks, b_w_out, c_w_in, c_w_out, f_w_up, f_conv_w, f_conv_b, f_w_down):
    B, S, D = x.shape
    depth = g_mix.shape[0]
    tm = min(512, S)
    tab32 = _rope_table(positions, A_QK_DIM, A_ROPE_DIM)
    tab16 = _rope_table(positions, B_HEAD_DIM, B_HEAD_DIM // 4)
    mem_k, mem_v = _memkv(mem, g_mem.reshape(1, D), w_mem_kv.astype(BF16))
    conv_w = jnp.pad(f_conv_w, ((0, 0), (0, 8 - CONV_WIDTH), (0, 0)))
    for i in range(depth):
        kind, j = i % 3, i // 3
        g = g_mix[i].reshape(1, D)
        if kind == 0:
            wuk = jnp.pad(a_w_uk[j], ((0, 0), (0, 0), (A_ROPE_DIM, 0))).reshape(A_KV_RANK, -1).astype(BF16)
            wuv = jnp.transpose(a_w_uv[j], (1, 0, 2)).astype(BF16)
            q, kf, ct, qi, ki, wi, qm = _proj_a(x, g, _prep_a_w_in(a_w_in[j]), a_kv_norm[j].reshape(1, -1), wuk,
                                                tab32, tab16, tm)
            mix = _dsa(q, qi, wi, kf, ct, ki, wuv)
            w_out = a_w_out[j]
        elif kind == 1:
            nq, nkv = B_HEADS * B_HEAD_DIM, 4 * B_KV_HEADS * B_HEAD_DIM
            q, k, v, qm = _proj_qkv(x, g, _prep_b_w_in(b_w_in[j]), tab16, tm, B_HEAD_DIM, B_HEAD_DIM // 4,
                                    nq, nkv, nkv, False)
            mix = _swa(q, k, v, b_sinks[j], subheads=_b_subheads(), max_dist=B_WINDOW - 1)
            w_out = b_w_out[j]
        else:
            nq = C_HEADS * C_HEAD_DIM
            q, k, v, qm = _proj_qkv(x, g, c_w_in[j].astype(BF16), tab32, tm, C_HEAD_DIM, C_HEAD_DIM // 4,
                                    nq, nq, nq, True)
            mix = _wattn(q, k, v, C_BRANCHES)
            w_out = c_w_out[j]
        x = _out_proj(x, mix, qm, mem_k[i], mem_v[i], w_out.astype(BF16), tm)
        x = _ffn(x, g_ffn[i].reshape(1, D), f_w_up[i].astype(BF16), conv_w[i], f_conv_b[i].reshape(1, -1),
                 f_w_down[i].astype(BF16), g_final.reshape(1, D), min(256, S), i == depth - 1)
    return x
```

```python
import functools

import jax
import jax.numpy as jnp
import numpy as np
from jax import lax
from jax.experimental import pallas as pl
from jax.experimental.pallas import tpu as pltpu

F32 = jnp.float32
BF16 = jnp.bfloat16
I32 = jnp.int32

LANES = 128
BLOCK = 128
ROPE_THETA = 500000.0
EPS = 1e-6
NEG_INF = -1e30
LOG2E = 1.4426950408889634
INT_MIN = -(2**31)

A_HEADS = 8
A_QK_DIM = 128
A_ROPE_DIM = 32
A_KV_RANK = 256
IDX_HEADS = 16
IDX_DIM = 64
TOPK_MAX = 256
B_HEADS = 16
B_KV_HEADS = 4
B_HEAD_DIM = 64
B_WINDOW = 128
C_HEADS = 8
C_HEAD_DIM = 128
C_BRANCHES = ((128, 1), (512, 4), (2048, 16))
MEM_HEADS = 4
MEM_HEAD_DIM = 64
MEM_WIDTH = MEM_HEADS * MEM_HEAD_DIM
CONV_WIDTH = 3
KEY_CHUNK = 256

VMEM_LIMIT = 56 * 1024 * 1024

_NT = (((1,), (1,)), ((), ()))


def _cparams(sem):
    return pltpu.CompilerParams(dimension_semantics=sem, vmem_limit_bytes=VMEM_LIMIT)


def _rms(x, g):
    return x * lax.rsqrt(jnp.mean(x * x, axis=-1, keepdims=True) + EPS) * g


def _loop_k_per_trip(lo, hi, body, carry, k=2):
    if k == 1:
        return lax.fori_loop(lo, hi, body, carry)
    trips = (hi - lo) // k

    def several(t, c):
        for u in range(k):
            c = body(lo + k * t + u, c)
        return c

    carry = lax.fori_loop(0, trips, several, carry)
    return _loop_k_per_trip(lo + k * trips, hi, body, carry, k // 2)


def _bit_transpose32(words):
    a = list(words)
    j, m = 16, 0x0000FFFF
    while j:
        mask = jnp.int32(m - (1 << 32) if m >= (1 << 31) else m)
        for k in range(32):
            if k & j == 0:
                t = (a[k] ^ lax.shift_right_logical(a[k + j], j)) & mask
                a[k] = a[k] ^ t
                a[k + j] = a[k + j] ^ lax.shift_left(t, j)
        j >>= 1
        m = (m ^ (m << j)) & 0xFFFFFFFF
    return a


def _rope_tile(t, tab, half):
    c, sa, sb = tab[:, 0:LANES], tab[:, LANES:2 * LANES], tab[:, 2 * LANES:3 * LANES]
    return t * c + pltpu.roll(t, half, 1) * sa + pltpu.roll(t, LANES - half, 1) * sb


def _rope_table(positions, head_dim, rot):
    half = rot // 2
    inv = ROPE_THETA ** (-jnp.arange(0, rot, 2, dtype=F32) / rot)
    ang = positions.astype(F32)[..., None] * inv
    cs = jnp.concatenate([jnp.cos(ang), jnp.sin(ang)], axis=-1)
    sel = np.zeros((rot, 3 * LANES), np.float32)
    one = np.zeros((3 * LANES,), np.float32)
    for l in range(LANES):
        j = l % head_dim
        if j < half:
            sel[j, l] = 1.0
            sel[half + j, 2 * LANES + l] = -1.0
        elif j < rot:
            sel[j - half, l] = 1.0
            sel[j, LANES + l] = 1.0
        else:
            one[l] = 1.0
    return jnp.dot(cs, jnp.asarray(sel), precision=lax.Precision.HIGHEST) + jnp.asarray(one)


def _staggered(mm, jobs):
    nxt = mm(jobs[0][0], jobs[0][1])
    for n, (_, _, epilogue) in enumerate(jobs):
        y = nxt
        if n + 1 < len(jobs):
            nxt = mm(jobs[n + 1][0], jobs[n + 1][1])
        epilogue(y)


def _proj_a_kernel(x_ref, g_ref, w_ref, kvn_ref, wuk_ref, t32_ref, t16_ref,
                   q_ref, kf_ref, ct_ref, qi_ref, ki_ref, wi_ref, qm_ref, *, tm):
    hb = _rms(x_ref[...], g_ref[...]).astype(BF16)
    t32 = t32_ref[...]
    t16 = t16_ref[...]

    def mm(a, b):
        return jnp.dot(hb, w_ref[:, a:b], preferred_element_type=F32)

    def roped(ref, col, tab, half, scale):
        def epilogue(y):
            for u in range(y.shape[1] // LANES):
                t = _rope_tile(y[:, u * LANES:(u + 1) * LANES], tab, half)
                if scale != 1.0:
                    t = t * scale
                ref[:, col + u * LANES:col + (u + 1) * LANES] = t.astype(BF16)
        return epilogue

    latent_bf16 = []

    def latent(y):
        c = _rms(y, kvn_ref[...])
        latent_bf16.append(c.astype(BF16))
        for u in range(tm // KEY_CHUNK):
            ct_ref[u] = c[u * KEY_CHUNK:(u + 1) * KEY_CHUNK, :].T.astype(BF16)

    def full_keys(y):
        k_rope = _rope_tile(y, t32, A_ROPE_DIM // 2)
        k_nope = jnp.dot(latent_bf16[0], wuk_ref[...], preferred_element_type=F32)
        for h in range(A_HEADS):
            kf_ref[:, h * LANES:(h + 1) * LANES] = (k_nope[:, h * LANES:(h + 1) * LANES] + k_rope).astype(BF16)

    def index_keys(y):
        for u in range(2):
            t = _rope_tile(y[:, u * LANES:(u + 1) * LANES], t16, IDX_DIM // 8).astype(BF16)
            for v in range(tm // KEY_CHUNK):
                ki_ref[v, u * KEY_CHUNK:(u + 1) * KEY_CHUNK, :] = t[v * KEY_CHUNK:(v + 1) * KEY_CHUNK, :]

    def index_weights(y):
        wi_ref[...] = y * (IDX_HEADS * IDX_DIM) ** -0.5

    def mem_query(y):
        qm_ref[...] = (y * (MEM_HEAD_DIM ** -0.5 * LOG2E)).astype(BF16)

    jobs, o = [], 0
    for j in range(0, A_HEADS * A_QK_DIM, 2 * LANES):
        jobs.append((o + j, o + j + 2 * LANES, roped(q_ref, j, t32, A_ROPE_DIM // 2, A_QK_DIM ** -0.5 * LOG2E)))
    o += A_HEADS * A_QK_DIM
    jobs.append((o, o + A_KV_RANK, latent))
    o += A_KV_RANK
    jobs.append((o, o + LANES, full_keys))
    o += LANES
    for j in range(0, IDX_HEADS * IDX_DIM, 2 * LANES):
        jobs.append((o + j, o + j + 2 * LANES, roped(qi_ref, j, t16, IDX_DIM // 8, 1.0)))
    o += IDX_HEADS * IDX_DIM
    jobs.append((o, o + 2 * LANES, index_keys))
    o += 2 * LANES
    jobs.append((o, o + LANES, index_weights))
    o += LANES
    jobs.append((o, o + MEM_WIDTH, mem_query))
    _staggered(mm, jobs)


def _prep_a_w_in(w):
    d = w.shape[0]
    sizes = (A_HEADS * A_QK_DIM, A_KV_RANK, A_ROPE_DIM, IDX_HEADS * IDX_DIM, IDX_DIM, IDX_HEADS, MEM_WIDTH)
    offs = [0]
    for s in sizes:
        offs.append(offs[-1] + s)
    q, ckv, kr, qi, ki, wi, qm = [w[:, offs[i]:offs[i + 1]] for i in range(len(sizes))]
    z = lambda n: jnp.zeros((d, n), w.dtype)
    return jnp.concatenate([
        q, ckv, kr, z(LANES - A_ROPE_DIM), qi,
        ki, z(LANES - IDX_DIM), z(LANES - IDX_DIM), ki,
        wi, z(LANES - IDX_HEADS), qm], axis=1).astype(BF16)


def _proj_a(x, g, w, kvn, wuk, t32, t16, tm):
    B, S, D = x.shape
    n = w.shape[1]
    row = lambda width: pl.BlockSpec((None, tm, width), lambda b, i: (b, i, 0))
    const = lambda shape: pl.BlockSpec(shape, lambda b, i: (0,) * len(shape))
    out_shape = (
        jax.ShapeDtypeStruct((B, S, A_HEADS * A_QK_DIM), BF16),
        jax.ShapeDtypeStruct((B, S, A_HEADS * A_QK_DIM), BF16),
        jax.ShapeDtypeStruct((B, S // KEY_CHUNK, A_KV_RANK, KEY_CHUNK), BF16),
        jax.ShapeDtypeStruct((B, S, IDX_HEADS * IDX_DIM), BF16),
        jax.ShapeDtypeStruct((B, S // KEY_CHUNK, 2 * KEY_CHUNK, LANES), BF16),
        jax.ShapeDtypeStruct((B, S, LANES), F32),
        jax.ShapeDtypeStruct((B, S, MEM_WIDTH), BF16),
    )
    out_specs = (
        row(A_HEADS * A_QK_DIM), row(A_HEADS * A_QK_DIM),
        pl.BlockSpec((None, tm // KEY_CHUNK, A_KV_RANK, KEY_CHUNK), lambda b, i: (b, i, 0, 0)),
        row(IDX_HEADS * IDX_DIM),
        pl.BlockSpec((None, tm // KEY_CHUNK, 2 * KEY_CHUNK, LANES), lambda b, i: (b, i, 0, 0)),
        row(LANES), row(MEM_WIDTH),
    )
    return pl.pallas_call(
        functools.partial(_proj_a_kernel, tm=tm),
        grid=(B, S // tm),
        in_specs=[row(D), const((1, D)), const((D, n)), const((1, A_KV_RANK)), const(wuk.shape),
                  row(3 * LANES), row(3 * LANES)],
        out_specs=out_specs,
        out_shape=out_shape,
        compiler_params=_cparams(("parallel", "parallel")),
        name="proj_a",
    )(x, g, w, kvn, wuk, t32, t16)


def _proj_qkv_kernel(x_ref, g_ref, w_ref, tab_ref, q_ref, k_ref, v_ref, qm_ref, *,
                     head_dim, rot, nq, nk, nv, v_transposed):
    hb = _rms(x_ref[...], g_ref[...]).astype(BF16)
    tm = hb.shape[0]
    tab = tab_ref[...]

    def mm(a, b):
        return jnp.dot(hb, w_ref[:, a:b], preferred_element_type=F32)

    def roped(ref, col, scale):
        def epilogue(y):
            for u in range(2):
                t = _rope_tile(y[:, u * LANES:(u + 1) * LANES], tab, rot // 2)
                if scale != 1.0:
                    t = t * scale
                ref[:, col + u * LANES:col + (u + 1) * LANES] = t.astype(BF16)
        return epilogue

    def value(col):
        def epilogue(y):
            if v_transposed:
                for u in range(tm // KEY_CHUNK):
                    v_ref[u, col:col + 2 * LANES, :] = y[u * KEY_CHUNK:(u + 1) * KEY_CHUNK, :].T.astype(BF16)
            else:
                v_ref[:, col:col + 2 * LANES] = y.astype(BF16)
        return epilogue

    def mem_query(y):
        qm_ref[...] = (y * (MEM_HEAD_DIM ** -0.5 * LOG2E)).astype(BF16)

    jobs = [(j, j + 2 * LANES, roped(q_ref, j, head_dim ** -0.5 * LOG2E)) for j in range(0, nq, 2 * LANES)]
    jobs += [(nq + j, nq + j + 2 * LANES, roped(k_ref, j, 1.0)) for j in range(0, nk, 2 * LANES)]
    jobs += [(nq + nk + j, nq + nk + j + 2 * LANES, value(j)) for j in range(0, nv, 2 * LANES)]
    jobs.append((nq + nk + nv, nq + nk + nv + MEM_WIDTH, mem_query))
    _staggered(mm, jobs)


def _proj_qkv(x, g, w, tab, tm, head_dim, rot, nq, nk, nv, v_transposed):
    B, S, D = x.shape
    n = w.shape[1]
    row = lambda width: pl.BlockSpec((None, tm, width), lambda b, i: (b, i, 0))
    const = lambda shape: pl.BlockSpec(shape, lambda b, i: (0,) * len(shape))
    if v_transposed:
        v_spec = pl.BlockSpec((None, tm // KEY_CHUNK, nv, KEY_CHUNK), lambda b, i: (b, i, 0, 0))
        v_shape = jax.ShapeDtypeStruct((B, S // KEY_CHUNK, nv, KEY_CHUNK), BF16)
    else:
        v_spec, v_shape = row(nv), jax.ShapeDtypeStruct((B, S, nv), BF16)
    return pl.pallas_call(
        functools.partial(_proj_qkv_kernel, head_dim=head_dim, rot=rot, nq=nq, nk=nk, nv=nv,
                          v_transposed=v_transposed),
        grid=(B, S // tm),
        in_specs=[row(D), const((1, D)), const((D, n)), row(3 * LANES)],
        out_specs=(row(nq), row(nk), v_spec, row(MEM_WIDTH)),
        out_shape=(jax.ShapeDtypeStruct((B, S, nq), BF16), jax.ShapeDtypeStruct((B, S, nk), BF16), v_shape,
                   jax.ShapeDtypeStruct((B, S, MEM_WIDTH), BF16)),
        compiler_params=_cparams(("parallel", "parallel")),
        name="proj_qkv",
    )(x, g, w, tab)


def _prep_b_w_in(w):
    d = w.shape[0]
    nq, nkv = B_HEADS * B_HEAD_DIM, B_KV_HEADS * B_HEAD_DIM
    q, k, v, qm = w[:, :nq], w[:, nq:nq + nkv], w[:, nq + nkv:nq + 2 * nkv], w[:, nq + 2 * nkv:]
    z = jnp.zeros((d, B_HEAD_DIM), w.dtype)

    def spread(t):
        cols = []
        for h in range(B_KV_HEADS):
            th = t[:, h * B_HEAD_DIM:(h + 1) * B_HEAD_DIM]
            cols += [th, z, z, th]
        return jnp.concatenate(cols, axis=1)

    return jnp.concatenate([q, spread(k), spread(v), qm], axis=1).astype(BF16)


def _dsa_kernel(q_ref, qi_ref, wi_ref, kf_ref, ct_ref, ki_ref, wuv_ref, o_ref,
                keys_ref, planes_ref, qbd_ref, qi2_ref, wrow_ref, acc_ref, sa_ref, sb_ref, *, topk):
    kc = KEY_CHUNK
    hw = A_HEADS * LANES
    gw = 2 * LANES
    i = pl.program_id(1)
    nch = (i * BLOCK + BLOCK + kc - 1) // kc

    qbd_ref[...] = jnp.zeros(qbd_ref.shape, BF16)
    for h in range(A_HEADS):
        rows = slice(h * BLOCK, (h + 1) * BLOCK)
        qbd_ref[h // 2, (h % 2) * BLOCK:(h % 2 + 1) * BLOCK, (h % 2) * LANES:(h % 2 + 1) * LANES] = (
            q_ref[:, h * LANES:(h + 1) * LANES])
        qi2_ref[rows, :] = qi_ref[:, h * LANES:(h + 1) * LANES]
    w_t = wi_ref[...].T
    for j in range(IDX_HEADS // 2):
        wrow_ref[0:1, j * LANES:(j + 1) * LANES] = w_t[2 * j:2 * j + 1, :]
        wrow_ref[1:2, j * LANES:(j + 1) * LANES] = w_t[2 * j + 1:2 * j + 2, :]

    qpos = i * BLOCK + lax.broadcasted_iota(I32, (kc, BLOCK), 1)
    krow = lax.broadcasted_iota(I32, (kc, BLOCK), 0)

    def score_chunk(c):
        off = pl.multiple_of(c * kc, kc)
        kk = ki_ref[c]
        acc = None
        for g in range(0, hw, gw):
            d = lax.dot_general(kk, qi2_ref[g:g + gw, :], _NT, preferred_element_type=F32)
            t = (jnp.maximum(d[0:kc], 0.0) * wrow_ref[0:1, g:g + gw]
                 + jnp.maximum(d[kc:2 * kc], 0.0) * wrow_ref[1:2, g:g + gw])
            for u in range(0, gw, LANES):
                acc = t[:, u:u + LANES] if acc is None else acc + t[:, u:u + LANES]
        bits = pltpu.bitcast(acc, I32)
        key = jnp.where(bits < 0, bits ^ 0x7FFFFFFF, bits)
        key = jnp.where(off + krow <= qpos, key, INT_MIN)
        keys_ref[pl.ds(off, kc), :] = key
        planes = _bit_transpose32([(key[8 * r:8 * r + 8, :] ^ INT_MIN) for r in range(kc // 8)])
        for p in range(32):
            planes_ref[c, p] = planes[p]

    @pl.when(i == 0)
    def _():
        planes_ref[...] = jnp.zeros(planes_ref.shape, I32)

    def score_body(c, carry):
        score_chunk(c)
        return carry

    _loop_k_per_trip(0, nch, score_body, 0, k=4)

    n_chunks = planes_ref.shape[0]

    def bit_body(p, carry):
        alive, need, res = carry
        ones = [a & planes_ref[c, p] for c, a in enumerate(alive)]
        cnt = ones[0] * 0
        for o in ones:
            cnt = cnt + lax.population_count(o)
        cnt = jnp.sum(cnt, axis=0, keepdims=True)
        take = cnt >= need
        alive = [jnp.where(take, o, a ^ o) for o, a in zip(ones, alive)]
        need = jnp.where(take, need, need - cnt)
        res = jnp.where(take, res | lax.shift_left(jnp.int32(1), 31 - p), res)
        return alive, need, res

    for g in range(0, hw, gw):
        sa_ref[0:kc, g:g + gw] = lax.dot_general(kf_ref[0:kc, g:g + gw], qbd_ref[g // gw], _NT,
                                                 preferred_element_type=F32)

    carry = ([jnp.where(c < nch, jnp.full((8, BLOCK), -1, I32), 0) for c in range(n_chunks)],
             jnp.full((1, BLOCK), topk, I32), jnp.zeros((1, BLOCK), I32))
    for p in range(32):
        carry = bit_body(p, carry)
    thr = jnp.maximum(carry[2] ^ INT_MIN, INT_MIN + 1)

    acc_ref[...] = jnp.zeros(acc_ref.shape, F32)
    bias0 = jnp.where(keys_ref[0:kc, :] >= thr, 0.0, NEG_INF)
    bias0 = jnp.concatenate([bias0] * (gw // LANES), axis=1)
    for g in range(0, hw, gw):
        t = sa_ref[0:kc, g:g + gw] + bias0
        sa_ref[0:kc, g:g + gw] = t
        sa_ref[kc:kc + 1, g:g + gw] = jnp.max(t, axis=0, keepdims=True)

    def logits(c, s_ref):
        off = pl.multiple_of(c * kc, kc)
        bias = jnp.where(keys_ref[pl.ds(off, kc), :] >= thr, 0.0, NEG_INF)
        bias = jnp.concatenate([bias] * (gw // LANES), axis=1)
        for g in range(0, hw, gw):
            t = lax.dot_general(kf_ref[pl.ds(off, kc), g:g + gw], qbd_ref[g // gw], _NT,
                                preferred_element_type=F32) + bias
            s_ref[0:kc, g:g + gw] = t
            s_ref[kc:kc + 1, g:g + gw] = jnp.max(t, axis=0, keepdims=True)

    def consume(s_ref, c, carry):
        m_prev, l_prev = carry
        ct = ct_ref[c]
        m_out, l_out = [], []
        for g in range(0, hw, gw):
            s = s_ref[0:kc, g:g + gw]
            m_new = jnp.maximum(m_prev[:, g:g + gw], s_ref[kc:kc + 1, g:g + gw])
            alpha = jnp.exp2(m_prev[:, g:g + gw] - m_new)
            p = jnp.exp2(s - m_new)
            l_out.append(alpha * l_prev[:, g:g + gw] + jnp.sum(p, axis=0, keepdims=True))
            m_out.append(m_new)
            acc_ref[:, g:g + gw] = alpha * acc_ref[:, g:g + gw] + jnp.dot(
                ct, p.astype(BF16), preferred_element_type=F32)
        return jnp.concatenate(m_out, axis=1), jnp.concatenate(l_out, axis=1)

    def pair_body(j, carry):
        c0 = 2 * j
        logits(c0 + 1, sb_ref)
        carry = consume(sa_ref, c0, carry)
        logits(jnp.minimum(c0 + 2, nch - 1), sa_ref)
        return consume(sb_ref, c0 + 1, carry)

    carry = _loop_k_per_trip(0, nch // 2, pair_body,
                             (jnp.full((1, hw), NEG_INF, F32), jnp.zeros((1, hw), F32)))
    _, l_fin = lax.fori_loop(0, nch % 2, lambda _, c: consume(sa_ref, nch - 1, c), carry)

    inv_l = 1.0 / l_fin
    for h in range(A_HEADS):
        cols = slice(h * LANES, (h + 1) * LANES)
        o_lat = (acc_ref[:, cols] * inv_l[:, cols]).T.astype(BF16)
        o_ref[:, cols] = jnp.dot(o_lat, wuv_ref[h], preferred_element_type=F32).astype(BF16)


def _dsa(q, qi, wi, kf, ct, ki, wuv):
    B, S, _ = q.shape
    topk = min(TOPK_MAX, S // 4)
    blk = lambda width: pl.BlockSpec((None, BLOCK, width), lambda b, i: (b, i, 0))
    seq = lambda width: pl.BlockSpec((None, S, width), lambda b, i: (b, 0, 0))
    const3 = lambda shape: pl.BlockSpec(shape, lambda b, i: (0, 0, 0))
    return pl.pallas_call(
        functools.partial(_dsa_kernel, topk=topk),
        grid=(B, S // BLOCK),
        in_specs=[blk(A_HEADS * A_QK_DIM), blk(IDX_HEADS * IDX_DIM), blk(LANES),
                  seq(A_HEADS * A_QK_DIM),
                  pl.BlockSpec((None, S // KEY_CHUNK, A_KV_RANK, KEY_CHUNK), lambda b, i: (b, 0, 0, 0)),
                  pl.BlockSpec((None, S // KEY_CHUNK, 2 * KEY_CHUNK, LANES), lambda b, i: (b, 0, 0, 0)),
                  const3(wuv.shape)],
        out_specs=blk(A_HEADS * LANES),
        out_shape=jax.ShapeDtypeStruct((B, S, A_HEADS * LANES), BF16),
        scratch_shapes=[
            pltpu.VMEM((S, BLOCK), I32),
            pltpu.VMEM((S // KEY_CHUNK, 32, 8, BLOCK), I32),
            pltpu.VMEM((A_HEADS // 2, 2 * BLOCK, 2 * LANES), BF16),
            pltpu.VMEM((A_HEADS * BLOCK, LANES), BF16),
            pltpu.VMEM((8, A_HEADS * LANES), F32),
            pltpu.VMEM((A_KV_RANK, A_HEADS * LANES), F32),
            pltpu.VMEM((KEY_CHUNK + 8, A_HEADS * LANES), F32),
            pltpu.VMEM((KEY_CHUNK + 8, A_HEADS * LANES), F32),
        ],
        compiler_params=_cparams(("parallel", "arbitrary")),
        name="dsa",
    )(q, qi, wi, kf, ct, ki, wuv)


def _window_bias(branches, nrel):
    rel = jnp.arange(nrel, dtype=I32)[:, None, None]
    krow = jnp.arange(KEY_CHUNK, dtype=I32)[None, :, None]
    qcol = jnp.arange(KEY_CHUNK, dtype=I32)[None, None, :]
    dist = KEY_CHUNK * (nrel - 1 - rel) + qcol - krow
    mult = sum(((dist >= 0) & (dist <= window) & (dist % dil == 0)).astype(F32) for window, dil in branches)
    return jnp.where(mult > 0, jnp.log2(jnp.maximum(mult, 1.0)), NEG_INF)


def _wattn_kernel(q_ref, k_ref, vt_ref, bias_ref, o_ref, acc_ref, sa_ref, sb_ref, p_ref, *, nh, nrel):
    kc = qb = KEY_CHUNK
    top = pl.program_id(1)
    c_lo = jnp.maximum(top - (nrel - 1), 0)
    n = top + 1 - c_lo

    acc_ref[...] = jnp.zeros(acc_ref.shape, F32)

    def logits(c, s_ref):
        off = pl.multiple_of(c * kc, kc)
        bias = bias_ref[c - top + (nrel - 1)]
        for h in range(nh):
            cols = slice(h * LANES, (h + 1) * LANES)
            t = lax.dot_general(k_ref[pl.ds(off, kc), cols], q_ref[:, cols], _NT,
                                preferred_element_type=F32) + bias
            s_ref[0:kc, h * qb:(h + 1) * qb] = t
            s_ref[kc:kc + 1, h * qb:(h + 1) * qb] = jnp.max(t, axis=0, keepdims=True)

    def consume(s_ref, c, carry):
        m_prev, l_prev = carry
        m_new = jnp.maximum(m_prev, s_ref[kc:kc + 1, :])
        alpha = jnp.exp2(m_prev - m_new)
        p_ref[...] = jnp.exp2(s_ref[0:kc, :] - m_new).astype(BF16)
        l_new = alpha * l_prev + jnp.dot(jnp.ones((16, kc), BF16), p_ref[...], preferred_element_type=F32)[0:1, :]
        for h in range(nh):
            cols = slice(h * qb, (h + 1) * qb)
            acc_ref[:, cols] = alpha[:, cols] * acc_ref[:, cols] + jnp.dot(
                vt_ref[c, h * LANES:(h + 1) * LANES, :], p_ref[:, cols], preferred_element_type=F32)
        return m_new, l_new

    logits(c_lo, sa_ref)

    def pair_body(j, carry):
        c0 = c_lo + 2 * j
        logits(c0 + 1, sb_ref)
        carry = consume(sa_ref, c0, carry)
        logits(jnp.minimum(c0 + 2, top), sa_ref)
        return consume(sb_ref, c0 + 1, carry)

    carry = _loop_k_per_trip(0, n // 2, pair_body,
                             (jnp.full((1, nh * qb), NEG_INF, F32), jnp.zeros((1, nh * qb), F32)))
    _, l_fin = lax.fori_loop(0, n % 2, lambda _, c: consume(sa_ref, top, c), carry)

    inv_l = 1.0 / l_fin
    for h in range(nh):
        cols = slice(h * qb, (h + 1) * qb)
        o_ref[:, h * LANES:(h + 1) * LANES] = (acc_ref[:, cols] * inv_l[:, cols]).T.astype(BF16)


def _swa_kernel(sink_ref, q_ref, kp_ref, kc_ref, vp_ref, vc_ref, o_ref, *, subheads, max_dist):
    n = pl.program_id(1)
    qi = lax.broadcasted_iota(I32, (BLOCK, 2 * BLOCK), 0)
    kj = lax.broadcasted_iota(I32, (BLOCK, 2 * BLOCK), 1)
    dist = BLOCK + qi - kj
    mask = (dist >= 0) & (dist <= max_dist) & ((kj >= BLOCK) | (n > 0))
    for j, subs in enumerate(subheads):
        qt = q_ref[:, j * LANES:(j + 1) * LANES]
        o_t = None
        for kt, vt, hidx in subs:
            kk = jnp.concatenate([kp_ref[:, kt * LANES:(kt + 1) * LANES],
                                  kc_ref[:, kt * LANES:(kt + 1) * LANES]], axis=0)
            vv = jnp.concatenate([vp_ref[:, vt * LANES:(vt + 1) * LANES],
                                  vc_ref[:, vt * LANES:(vt + 1) * LANES]], axis=0)
            s = lax.dot_general(qt, kk, _NT, preferred_element_type=F32)
            s = jnp.where(mask, s, NEG_INF)
            sk = sink_ref[hidx] * LOG2E
            m = jnp.maximum(jnp.max(s, axis=-1, keepdims=True), sk)
            p = jnp.exp2(s - m)
            l = jnp.sum(p, axis=-1, keepdims=True) + jnp.exp2(sk - m)
            o_s = jnp.dot(p.astype(BF16), vv, preferred_element_type=F32) / l
            o_t = o_s if o_t is None else o_t + o_s
        o_ref[:, j * LANES:(j + 1) * LANES] = o_t.astype(BF16)


def _swa(q, k, v, sinks, *, subheads, max_dist):
    B, S, wq = q.shape
    wk, wv = k.shape[-1], v.shape[-1]
    cur = lambda width: pl.BlockSpec((None, BLOCK, width), lambda b, i: (b, i, 0))
    prv = lambda width: pl.BlockSpec((None, BLOCK, width), lambda b, i: (b, jnp.maximum(i - 1, 0), 0))
    return pl.pallas_call(
        functools.partial(_swa_kernel, subheads=subheads, max_dist=max_dist),
        grid=(B, S // BLOCK),
        in_specs=[pl.BlockSpec(memory_space=pltpu.SMEM), cur(wq), prv(wk), cur(wk), prv(wv), cur(wv)],
        out_specs=cur(wq),
        out_shape=jax.ShapeDtypeStruct((B, S, wq), BF16),
        compiler_params=_cparams(("parallel", "arbitrary")),
        name="swa",
    )(sinks, q, k, k, v, v)


def _wattn(q, k, vt, branches):
    B, S, wq = q.shape
    widest = max(w for w, _ in branches)
    nrel = min(-(-widest // KEY_CHUNK) + 1, S // KEY_CHUNK)
    bias = _window_bias(branches, nrel)
    nh = wq // LANES
    blk = lambda width: pl.BlockSpec((None, KEY_CHUNK, width), lambda b, i: (b, i, 0))
    return pl.pallas_call(
        functools.partial(_wattn_kernel, nh=nh, nrel=nrel),
        grid=(B, S // KEY_CHUNK),
        in_specs=[blk(wq),
                  pl.BlockSpec((None, S, wq), lambda b, i: (b, 0, 0)),
                  pl.BlockSpec((None,) + vt.shape[1:], lambda b, i: (b, 0, 0, 0)),
                  pl.BlockSpec(bias.shape, lambda b, i: (0, 0, 0))],
        out_specs=blk(wq),
        out_shape=jax.ShapeDtypeStruct((B, S, wq), BF16),
        scratch_shapes=[pltpu.VMEM((LANES, nh * KEY_CHUNK), F32),
                        pltpu.VMEM((KEY_CHUNK + 8, nh * KEY_CHUNK), F32),
                        pltpu.VMEM((KEY_CHUNK + 8, nh * KEY_CHUNK), F32),
                        pltpu.VMEM((KEY_CHUNK, nh * KEY_CHUNK), BF16)],
        compiler_params=_cparams(("parallel", "arbitrary")),
        name="wattn",
    )(q, k, vt, bias)


def _memkv_kernel(mem_ref, g_ref, w_ref, k_ref, v_ref):
    hb = _rms(mem_ref[...], g_ref[...]).astype(BF16)
    y = jnp.dot(hb, w_ref[...], preferred_element_type=F32)
    lane = lax.broadcasted_iota(I32, (y.shape[0], LANES), 1)
    for t in range(MEM_WIDTH // LANES):
        for out_ref, base in ((k_ref, 0), (v_ref, MEM_WIDTH)):
            tile = y[:, base + t * LANES:base + (t + 1) * LANES]
            out_ref[:, (2 * t) * LANES:(2 * t + 1) * LANES] = jnp.where(lane < MEM_HEAD_DIM, tile, 0.0).astype(BF16)
            out_ref[:, (2 * t + 1) * LANES:(2 * t + 2) * LANES] = jnp.where(lane >= MEM_HEAD_DIM, tile, 0.0).astype(BF16)


def _memkv(mem, g_mem, w):
    B, M, D = mem.shape
    L = w.shape[0]
    out = jax.ShapeDtypeStruct((L, B, M, 2 * MEM_WIDTH), BF16)
    ospec = pl.BlockSpec((None, None, M, 2 * MEM_WIDTH), lambda l, b: (l, b, 0, 0))
    return pl.pallas_call(
        _memkv_kernel,
        grid=(L, B),
        in_specs=[pl.BlockSpec((None, M, D), lambda l, b: (b, 0, 0)),
                  pl.BlockSpec((1, D), lambda l, b: (0, 0)),
                  pl.BlockSpec((None, D, 2 * MEM_WIDTH), lambda l, b: (l, 0, 0))],
        out_specs=(ospec, ospec),
        out_shape=(out, out),
        compiler_params=_cparams(("parallel", "parallel")),
        name="memkv",
    )(mem, g_mem, w)


def _mixer_residual(x_ref, mix_ref, qm_ref, mk_ref, mv_ref, w_ref):
    mw = mix_ref.shape[-1]
    n_sub = 2 * MEM_WIDTH // LANES
    logits = [lax.dot_general(qm_ref[:, (u // 2) * LANES:(u // 2 + 1) * LANES], mk_ref[:, u * LANES:(u + 1) * LANES],
                              _NT, preferred_element_type=F32) for u in range(n_sub)]
    y = x_ref[...] + jnp.dot(mix_ref[...], w_ref[0:mw, :], preferred_element_type=F32)
    tiles = []
    for t in range(n_sub // 2):
        o_t = None
        for u in (2 * t, 2 * t + 1):
            p = jnp.exp2(logits[u] - jnp.max(logits[u], axis=-1, keepdims=True))
            l = jnp.sum(p, axis=-1, keepdims=True)
            o_s = jnp.dot(p.astype(BF16), mv_ref[:, u * LANES:(u + 1) * LANES], preferred_element_type=F32) / l
            o_t = o_s if o_t is None else o_t + o_s
        tiles.append(o_t.astype(BF16))
    return y + jnp.dot(jnp.concatenate(tiles, axis=1), w_ref[mw:mw + MEM_WIDTH, :], preferred_element_type=F32)


def _out_kernel(x_ref, mix_ref, qm_ref, mk_ref, mv_ref, w_ref, o_ref):
    o_ref[...] = _mixer_residual(x_ref, mix_ref, qm_ref, mk_ref, mv_ref, w_ref)


def _out_proj(x, mix, qm, mk, mv, w, tm):
    B, S, D = x.shape
    row = lambda width: pl.BlockSpec((None, tm, width), lambda b, i: (b, i, 0))
    mem = pl.BlockSpec((None, mk.shape[1], 2 * MEM_WIDTH), lambda b, i: (b, 0, 0))
    return pl.pallas_call(
        _out_kernel,
        grid=(B, S // tm),
        in_specs=[row(D), row(mix.shape[-1]), row(MEM_WIDTH), mem, mem,
                  pl.BlockSpec(w.shape, lambda b, i: (0, 0))],
        out_specs=row(D),
        out_shape=jax.ShapeDtypeStruct((B, S, D), F32),
        compiler_params=_cparams(("parallel", "parallel")),
        name="out_proj",
    )(x, mix, qm, mk, mv, w)


def _ffn_kernel(x_ref, g_ref, wup_ref, cw_ref, cb_ref, wdn_ref, gf_ref, o_ref, carry_ref, *, tm, cw, final):
    dff = wdn_ref.shape[0]
    hm = tm // 2
    first = pl.program_id(1) == 0
    row = lax.broadcasted_iota(I32, (hm, cw), 0)
    xs = [x_ref[0:hm, :], x_ref[hm:tm, :]]
    hbs = [_rms(x, g_ref[...]).astype(BF16) for x in xs]
    accs = list(xs)

    def up(hb, c0):
        return (jnp.dot(hb, wup_ref[:, c0:c0 + cw], preferred_element_type=F32),
                jnp.dot(hb, wup_ref[:, dff + c0:dff + c0 + cw], preferred_element_type=F32))

    def gate(a, b, prev, c0):
        p1, p2 = prev[7:8, :], prev[6:7, :]
        a1 = jnp.where(row == 0, p1, pltpu.roll(a, 1, 0))
        a2 = jnp.where(row == 0, p2, jnp.where(row == 1, p1, pltpu.roll(a, 2, 0)))
        w = cw_ref[:, c0:c0 + cw]
        conv = w[0:1, :] * a2 + w[1:2, :] * a1 + w[2:3, :] * a + cb_ref[:, c0:c0 + cw]
        return (conv / (1.0 + jnp.exp(-conv)) * b).astype(BF16)

    nxt = [up(hb, 0) for hb in hbs]
    for c0 in range(0, dff, cw):
        cur = nxt
        if c0 + cw < dff:
            nxt = [up(hb, c0 + cw) for hb in hbs]
        prev = jnp.where(first, 0.0, carry_ref[:, c0:c0 + cw])
        for k, (a, b) in enumerate(cur):
            accs[k] = accs[k] + jnp.dot(gate(a, b, prev, c0), wdn_ref[c0:c0 + cw, :], preferred_element_type=F32)
            prev = a[hm - 8:hm, :]
        carry_ref[:, c0:c0 + cw] = prev
    for k, acc in enumerate(accs):
        o_ref[k * hm:(k + 1) * hm, :] = _rms(acc, gf_ref[...]) if final else acc


def _ffn(x, g, wup, cw, cb, wdn, gf, tm, final):
    B, S, D = x.shape
    dff = wdn.shape[0]
    row = pl.BlockSpec((None, tm, D), lambda b, i: (b, i, 0))
    const = lambda shape: pl.BlockSpec(shape, lambda b, i: (0, 0), pipeline_mode=pl.Buffered(1))
    return pl.pallas_call(
        functools.partial(_ffn_kernel, tm=tm, cw=2 * LANES, final=final),
        grid=(B, S // tm),
        in_specs=[row, const((1, D)), const(wup.shape), const(cw.shape), const((1, dff)), const(wdn.shape),
                  const((1, D))],
        out_specs=row,
        out_shape=jax.ShapeDtypeStruct((B, S, D), F32),
        scratch_shapes=[pltpu.VMEM((8, dff), F32)],
        compiler_params=_cparams(("arbitrary", "arbitrary")),
        name="ffn",
    )(x, g, wup, cw, cb, wdn, gf)


def _b_subheads():
    return tuple(((2 * (j // 2), 2 * (j // 2), 2 * j), (2 * (j // 2) + 1, 2 * (j // 2) + 1, 2 * j + 1))
                 for j in range(B_HEADS // 2))


def kernel(x, mem, positions, g_mix, g_ffn, g_mem, g_final, w_mem_kv, a_w_in, a_kv_norm, a_w_uk, a_w_uv, a_w_out,
           b_w_in, b_sinks, b_w_out, c_w_in, c_w_out, f_w_up, f_conv_w, f_conv_b, f_w_down):
    B, S, D = x.shape
    depth = g_mix.shape[0]
    tm = min(512, S)
    tab32 = _rope_table(positions, A_QK_DIM, A_ROPE_DIM)
    tab16 = _rope_table(positions, B_HEAD_DIM, B_HEAD_DIM // 4)
    mem_k, mem_v = _memkv(mem, g_mem.reshape(1, D), w_mem_kv.astype(BF16))
    conv_w = jnp.pad(f_conv_w, ((0, 0), (0, 8 - CONV_WIDTH), (0, 0)))
    for i in range(depth):
        kind, j = i % 3, i // 3
        g = g_mix[i].reshape(1, D)
        if kind == 0:
            wuk = jnp.pad(a_w_uk[j], ((0, 0), (0, 0), (A_ROPE_DIM, 0))).reshape(A_KV_RANK, -1).astype(BF16)
            wuv = jnp.transpose(a_w_uv[j], (1, 0, 2)).astype(BF16)
            q, kf, ct, qi, ki, wi, qm = _proj_a(x, g, _prep_a_w_in(a_w_in[j]), a_kv_norm[j].reshape(1, -1), wuk,
                                                tab32, tab16, tm)
            mix = _dsa(q, qi, wi, kf, ct, ki, wuv)
            w_out = a_w_out[j]
        elif kind == 1:
            nq, nkv = B_HEADS * B_HEAD_DIM, 4 * B_KV_HEADS * B_HEAD_DIM
            q, k, v, qm = _proj_qkv(x, g, _prep_b_w_in(b_w_in[j]), tab16, tm, B_HEAD_DIM, B_HEAD_DIM // 4,
                                    nq, nkv, nkv, False)
            mix = _swa(q, k, v, b_sinks[j], subheads=_b_subheads(), max_dist=B_WINDOW - 1)
            w_out = b_w_out[j]
        else:
            nq = C_HEADS * C_HEAD_DIM
            q, k, v, qm = _proj_qkv(x, g, c_w_in[j].astype(BF16), tab32, tm, C_HEAD_DIM, C_HEAD_DIM // 4,
                                    nq, nq, nq, True)
            mix = _wattn(q, k, v, C_BRANCHES)
            w_out = c_w_out[j]
        x = _out_proj(x, mix, qm, mem_k[i], mem_v[i], w_out.astype(BF16), tm)
        x = _ffn(x, g_ffn[i].reshape(1, D), f_w_up[i].astype(BF16), conv_w[i], f_conv_b[i].reshape(1, -1),
                 f_w_down[i].astype(BF16), g_final.reshape(1, D), tm, i == depth - 1)
    return x
```

```python
import functools

import jax
import jax.numpy as jnp
import numpy as np
from jax import lax
from jax.experimental import pallas as pl
from jax.experimental.pallas import tpu as pltpu

F32 = jnp.float32
BF16 = jnp.bfloat16
I32 = jnp.int32

LANES = 128
BLOCK = 128
ROPE_THETA = 500000.0
EPS = 1e-6
NEG_INF = -1e30
LOG2E = 1.4426950408889634
INT_MIN = -(2**31)

A_HEADS = 8
A_QK_DIM = 128
A_ROPE_DIM = 32
A_KV_RANK = 256
IDX_HEADS = 16
IDX_DIM = 64
TOPK_MAX = 256
B_HEADS = 16
B_KV_HEADS = 4
B_HEAD_DIM = 64
B_WINDOW = 128
C_HEADS = 8
C_HEAD_DIM = 128
C_BRANCHES = ((128, 1), (512, 4), (2048, 16))
MEM_HEADS = 4
MEM_HEAD_DIM = 64
MEM_WIDTH = MEM_HEADS * MEM_HEAD_DIM
CONV_WIDTH = 3
KEY_CHUNK = 256

VMEM_LIMIT = 56 * 1024 * 1024

_NT = (((1,), (1,)), ((), ()))


def _cparams(sem):
    return pltpu.CompilerParams(dimension_semantics=sem, vmem_limit_bytes=VMEM_LIMIT)


def _rms(x, g):
    return x * lax.rsqrt(jnp.mean(x * x, axis=-1, keepdims=True) + EPS) * g


def _loop_k_per_trip(lo, hi, body, carry, k=2):
    if k == 1:
        return lax.fori_loop(lo, hi, body, carry)
    trips = (hi - lo) // k

    def several(t, c):
        for u in range(k):
            c = body(lo + k * t + u, c)
        return c

    carry = lax.fori_loop(0, trips, several, carry)
    return _loop_k_per_trip(lo + k * trips, hi, body, carry, k // 2)


def _bit_transpose32(words):
    a = list(words)
    j, m = 16, 0x0000FFFF
    while j:
        mask = jnp.int32(m - (1 << 32) if m >= (1 << 31) else m)
        for k in range(32):
            if k & j == 0:
                t = (a[k] ^ lax.shift_right_logical(a[k + j], j)) & mask
                a[k] = a[k] ^ t
                a[k + j] = a[k + j] ^ lax.shift_left(t, j)
        j >>= 1
        m = (m ^ (m << j)) & 0xFFFFFFFF
    return a


def _rope_tile(t, tab, half):
    c, sa, sb = tab[:, 0:LANES], tab[:, LANES:2 * LANES], tab[:, 2 * LANES:3 * LANES]
    return t * c + pltpu.roll(t, half, 1) * sa + pltpu.roll(t, LANES - half, 1) * sb


def _rope_table(positions, head_dim, rot):
    half = rot // 2
    inv = ROPE_THETA ** (-jnp.arange(0, rot, 2, dtype=F32) / rot)
    ang = positions.astype(F32)[..., None] * inv
    cs = jnp.concatenate([jnp.cos(ang), jnp.sin(ang)], axis=-1)
    sel = np.zeros((rot, 3 * LANES), np.float32)
    one = np.zeros((3 * LANES,), np.float32)
    for l in range(LANES):
        j = l % head_dim
        if j < half:
            sel[j, l] = 1.0
            sel[half + j, 2 * LANES + l] = -1.0
        elif j < rot:
            sel[j - half, l] = 1.0
            sel[j, LANES + l] = 1.0
        else:
            one[l] = 1.0
    return jnp.dot(cs, jnp.asarray(sel), precision=lax.Precision.HIGHEST) + jnp.asarray(one)


def _staggered(mm, jobs):
    nxt = mm(jobs[0][0], jobs[0][1])
    for n, (_, _, epilogue) in enumerate(jobs):
        y = nxt
        if n + 1 < len(jobs):
            nxt = mm(jobs[n + 1][0], jobs[n + 1][1])
        epilogue(y)


def _proj_a_kernel(x_ref, g_ref, w_ref, kvn_ref, wuk_ref, t32_ref, t16_ref,
                   q_ref, kf_ref, ct_ref, qi_ref, ki_ref, wi_ref, qm_ref, *, tm):
    hb = _rms(x_ref[...], g_ref[...]).astype(BF16)
    t32 = t32_ref[...]
    t16 = t16_ref[...]

    def mm(a, b):
        return jnp.dot(hb, w_ref[:, a:b], preferred_element_type=F32)

    def roped(ref, col, tab, half, scale):
        def epilogue(y):
            for u in range(y.shape[1] // LANES):
                t = _rope_tile(y[:, u * LANES:(u + 1) * LANES], tab, half)
                if scale != 1.0:
                    t = t * scale
                ref[:, col + u * LANES:col + (u + 1) * LANES] = t.astype(BF16)
        return epilogue

    latent_bf16 = []

    def latent(y):
        c = _rms(y, kvn_ref[...])
        latent_bf16.append(c.astype(BF16))
        for u in range(tm // KEY_CHUNK):
            ct_ref[u] = c[u * KEY_CHUNK:(u + 1) * KEY_CHUNK, :].T.astype(BF16)

    def full_keys(y):
        k_rope = _rope_tile(y, t32, A_ROPE_DIM // 2)
        k_nope = jnp.dot(latent_bf16[0], wuk_ref[...], preferred_element_type=F32)
        for h in range(A_HEADS):
            kf_ref[:, h * LANES:(h + 1) * LANES] = (k_nope[:, h * LANES:(h + 1) * LANES] + k_rope).astype(BF16)

    def index_keys(y):
        for u in range(2):
            t = _rope_tile(y[:, u * LANES:(u + 1) * LANES], t16, IDX_DIM // 8).astype(BF16)
            for v in range(tm // KEY_CHUNK):
                ki_ref[v, u * KEY_CHUNK:(u + 1) * KEY_CHUNK, :] = t[v * KEY_CHUNK:(v + 1) * KEY_CHUNK, :]

    def index_weights(y):
        wi_ref[...] = y * (IDX_HEADS * IDX_DIM) ** -0.5

    def mem_query(y):
        qm_ref[...] = (y * (MEM_HEAD_DIM ** -0.5 * LOG2E)).astype(BF16)

    jobs, o = [], 0
    for j in range(0, A_HEADS * A_QK_DIM, 2 * LANES):
        jobs.append((o + j, o + j + 2 * LANES, roped(q_ref, j, t32, A_ROPE_DIM // 2, A_QK_DIM ** -0.5 * LOG2E)))
    o += A_HEADS * A_QK_DIM
    jobs.append((o, o + A_KV_RANK, latent))
    o += A_KV_RANK
    jobs.append((o, o + LANES, full_keys))
    o += LANES
    for j in range(0, IDX_HEADS * IDX_DIM, 2 * LANES):
        jobs.append((o + j, o + j + 2 * LANES, roped(qi_ref, j, t16, IDX_DIM // 8, 1.0)))
    o += IDX_HEADS * IDX_DIM
    jobs.append((o, o + 2 * LANES, index_keys))
    o += 2 * LANES
    jobs.append((o, o + LANES, index_weights))
    o += LANES
    jobs.append((o, o + MEM_WIDTH, mem_query))
    _staggered(mm, jobs)


def _prep_a_w_in(w):
    d = w.shape[0]
    sizes = (A_HEADS * A_QK_DIM, A_KV_RANK, A_ROPE_DIM, IDX_HEADS * IDX_DIM, IDX_DIM, IDX_HEADS, MEM_WIDTH)
    offs = [0]
    for s in sizes:
        offs.append(offs[-1] + s)
    q, ckv, kr, qi, ki, wi, qm = [w[:, offs[i]:offs[i + 1]] for i in range(len(sizes))]
    z = lambda n: jnp.zeros((d, n), w.dtype)
    return jnp.concatenate([
        q, ckv, kr, z(LANES - A_ROPE_DIM), qi,
        ki, z(LANES - IDX_DIM), z(LANES - IDX_DIM), ki,
        wi, z(LANES - IDX_HEADS), qm], axis=1).astype(BF16)


def _proj_a(x, g, w, kvn, wuk, t32, t16, tm):
    B, S, D = x.shape
    n = w.shape[1]
    row = lambda width: pl.BlockSpec((None, tm, width), lambda b, i: (b, i, 0))
    const = lambda shape: pl.BlockSpec(shape, lambda b, i: (0,) * len(shape))
    out_shape = (
        jax.ShapeDtypeStruct((B, S, A_HEADS * A_QK_DIM), BF16),
        jax.ShapeDtypeStruct((B, S, A_HEADS * A_QK_DIM), BF16),
        jax.ShapeDtypeStruct((B, S // KEY_CHUNK, A_KV_RANK, KEY_CHUNK), BF16),
        jax.ShapeDtypeStruct((B, S, IDX_HEADS * IDX_DIM), BF16),
        jax.ShapeDtypeStruct((B, S // KEY_CHUNK, 2 * KEY_CHUNK, LANES), BF16),
        jax.ShapeDtypeStruct((B, S, LANES), F32),
        jax.ShapeDtypeStruct((B, S, MEM_WIDTH), BF16),
    )
    out_specs = (
        row(A_HEADS * A_QK_DIM), row(A_HEADS * A_QK_DIM),
        pl.BlockSpec((None, tm // KEY_CHUNK, A_KV_RANK, KEY_CHUNK), lambda b, i: (b, i, 0, 0)),
        row(IDX_HEADS * IDX_DIM),
        pl.BlockSpec((None, tm // KEY_CHUNK, 2 * KEY_CHUNK, LANES), lambda b, i: (b, i, 0, 0)),
        row(LANES), row(MEM_WIDTH),
    )
    return pl.pallas_call(
        functools.partial(_proj_a_kernel, tm=tm),
        grid=(B, S // tm),
        in_specs=[row(D), const((1, D)), const((D, n)), const((1, A_KV_RANK)), const(wuk.shape),
                  row(3 * LANES), row(3 * LANES)],
        out_specs=out_specs,
        out_shape=out_shape,
        compiler_params=_cparams(("parallel", "parallel")),
        name="proj_a",
    )(x, g, w, kvn, wuk, t32, t16)


def _proj_qkv_kernel(x_ref, g_ref, w_ref, tab_ref, q_ref, k_ref, v_ref, qm_ref, *,
                     head_dim, rot, nq, nk, nv, v_transposed):
    hb = _rms(x_ref[...], g_ref[...]).astype(BF16)
    tm = hb.shape[0]
    tab = tab_ref[...]

    def mm(a, b):
        return jnp.dot(hb, w_ref[:, a:b], preferred_element_type=F32)

    def roped(ref, col, scale):
        def epilogue(y):
            for u in range(2):
                t = _rope_tile(y[:, u * LANES:(u + 1) * LANES], tab, rot // 2)
                if scale != 1.0:
                    t = t * scale
                ref[:, col + u * LANES:col + (u + 1) * LANES] = t.astype(BF16)
        return epilogue

    def value(col):
        def epilogue(y):
            if v_transposed:
                for u in range(tm // KEY_CHUNK):
                    v_ref[u, col:col + 2 * LANES, :] = y[u * KEY_CHUNK:(u + 1) * KEY_CHUNK, :].T.astype(BF16)
            else:
                v_ref[:, col:col + 2 * LANES] = y.astype(BF16)
        return epilogue

    def mem_query(y):
        qm_ref[...] = (y * (MEM_HEAD_DIM ** -0.5 * LOG2E)).astype(BF16)

    jobs = [(j, j + 2 * LANES, roped(q_ref, j, head_dim ** -0.5 * LOG2E)) for j in range(0, nq, 2 * LANES)]
    jobs += [(nq + j, nq + j + 2 * LANES, roped(k_ref, j, 1.0)) for j in range(0, nk, 2 * LANES)]
    jobs += [(nq + nk + j, nq + nk + j + 2 * LANES, value(j)) for j in range(0, nv, 2 * LANES)]
    jobs.append((nq + nk + nv, nq + nk + nv + MEM_WIDTH, mem_query))
    _staggered(mm, jobs)


def _proj_qkv(x, g, w, tab, tm, head_dim, rot, nq, nk, nv, v_transposed):
    B, S, D = x.shape
    n = w.shape[1]
    row = lambda width: pl.BlockSpec((None, tm, width), lambda b, i: (b, i, 0))
    const = lambda shape: pl.BlockSpec(shape, lambda b, i: (0,) * len(shape))
    if v_transposed:
        v_spec = pl.BlockSpec((None, tm // KEY_CHUNK, nv, KEY_CHUNK), lambda b, i: (b, i, 0, 0))
        v_shape = jax.ShapeDtypeStruct((B, S // KEY_CHUNK, nv, KEY_CHUNK), BF16)
    else:
        v_spec, v_shape = row(nv), jax.ShapeDtypeStruct((B, S, nv), BF16)
    return pl.pallas_call(
        functools.partial(_proj_qkv_kernel, head_dim=head_dim, rot=rot, nq=nq, nk=nk, nv=nv,
                          v_transposed=v_transposed),
        grid=(B, S // tm),
        in_specs=[row(D), const((1, D)), const((D, n)), row(3 * LANES)],
        out_specs=(row(nq), row(nk), v_spec, row(MEM_WIDTH)),
        out_shape=(jax.ShapeDtypeStruct((B, S, nq), BF16), jax.ShapeDtypeStruct((B, S, nk), BF16), v_shape,
                   jax.ShapeDtypeStruct((B, S, MEM_WIDTH), BF16)),
        compiler_params=_cparams(("parallel", "parallel")),
        name="proj_qkv",
    )(x, g, w, tab)


def _prep_b_w_in(w):
    d = w.shape[0]
    nq, nkv = B_HEADS * B_HEAD_DIM, B_KV_HEADS * B_HEAD_DIM
    q, k, v, qm = w[:, :nq], w[:, nq:nq + nkv], w[:, nq + nkv:nq + 2 * nkv], w[:, nq + 2 * nkv:]
    z = jnp.zeros((d, B_HEAD_DIM), w.dtype)

    def spread(t):
        cols = []
        for h in range(B_KV_HEADS):
            th = t[:, h * B_HEAD_DIM:(h + 1) * B_HEAD_DIM]
            cols += [th, z, z, th]
        return jnp.concatenate(cols, axis=1)

    return jnp.concatenate([q, spread(k), spread(v), qm], axis=1).astype(BF16)


def _dsa_kernel(q_ref, qi_ref, wi_ref, kf_ref, ct_ref, ki_ref, wuv_ref, o_ref,
                keys_ref, planes_ref, qbd_ref, qi2_ref, wrow_ref, acc_ref, sa_ref, sb_ref, *, topk):
    kc = KEY_CHUNK
    hw = A_HEADS * LANES
    gw = 2 * LANES
    i = pl.program_id(1)
    nch = (i * BLOCK + BLOCK + kc - 1) // kc

    qbd_ref[...] = jnp.zeros(qbd_ref.shape, BF16)
    for h in range(A_HEADS):
        rows = slice(h * BLOCK, (h + 1) * BLOCK)
        qbd_ref[h // 2, (h % 2) * BLOCK:(h % 2 + 1) * BLOCK, (h % 2) * LANES:(h % 2 + 1) * LANES] = (
            q_ref[:, h * LANES:(h + 1) * LANES])
        qi2_ref[rows, :] = qi_ref[:, h * LANES:(h + 1) * LANES]
    w_t = wi_ref[...].T
    for j in range(IDX_HEADS // 2):
        wrow_ref[0:1, j * LANES:(j + 1) * LANES] = w_t[2 * j:2 * j + 1, :]
        wrow_ref[1:2, j * LANES:(j + 1) * LANES] = w_t[2 * j + 1:2 * j + 2, :]

    qpos = i * BLOCK + lax.broadcasted_iota(I32, (kc, BLOCK), 1)
    krow = lax.broadcasted_iota(I32, (kc, BLOCK), 0)

    def score_chunk(c):
        off = pl.multiple_of(c * kc, kc)
        kk = ki_ref[c]
        acc = None
        for g in range(0, hw, gw):
            d = lax.dot_general(kk, qi2_ref[g:g + gw, :], _NT, preferred_element_type=F32)
            t = (jnp.maximum(d[0:kc], 0.0) * wrow_ref[0:1, g:g + gw]
                 + jnp.maximum(d[kc:2 * kc], 0.0) * wrow_ref[1:2, g:g + gw])
            for u in range(0, gw, LANES):
                acc = t[:, u:u + LANES] if acc is None else acc + t[:, u:u + LANES]
        bits = pltpu.bitcast(acc, I32)
        key = jnp.where(bits < 0, bits ^ 0x7FFFFFFF, bits)
        key = jnp.where(off + krow <= qpos, key, INT_MIN)
        keys_ref[pl.ds(off, kc), :] = key
        planes = _bit_transpose32([(key[8 * r:8 * r + 8, :] ^ INT_MIN) for r in range(kc // 8)])
        for p in range(32):
            planes_ref[c, p] = planes[p]

    @pl.when(i == 0)
    def _():
        planes_ref[...] = jnp.zeros(planes_ref.shape, I32)

    def score_body(c, carry):
        score_chunk(c)
        return carry

    _loop_k_per_trip(0, nch, score_body, 0, k=8)

    n_chunks = planes_ref.shape[0]

    def bit_body(p, carry):
        alive, need, res = carry
        ones = [a & planes_ref[c, p] for c, a in enumerate(alive)]
        cnt = ones[0] * 0
        for o in ones:
            cnt = cnt + lax.population_count(o)
        cnt = jnp.sum(cnt, axis=0, keepdims=True)
        take = cnt >= need
        alive = [jnp.where(take, o, a ^ o) for o, a in zip(ones, alive)]
        need = jnp.where(take, need, need - cnt)
        res = jnp.where(take, res | lax.shift_left(jnp.int32(1), 31 - p), res)
        return alive, need, res

    for g in range(0, hw, gw):
        sa_ref[0:kc, g:g + gw] = lax.dot_general(kf_ref[0:kc, g:g + gw], qbd_ref[g // gw], _NT,
                                                 preferred_element_type=F32)

    carry = ([jnp.where(c < nch, jnp.full((8, BLOCK), -1, I32), 0) for c in range(n_chunks)],
             jnp.full((1, BLOCK), topk, I32), jnp.zeros((1, BLOCK), I32))
    for p in range(32):
        carry = bit_body(p, carry)
    thr = jnp.maximum(carry[2] ^ INT_MIN, INT_MIN + 1)

    acc_ref[...] = jnp.zeros(acc_ref.shape, F32)
    bias0 = jnp.where(keys_ref[0:kc, :] >= thr, 0.0, NEG_INF)
    bias0 = jnp.concatenate([bias0] * (gw // LANES), axis=1)
    for g in range(0, hw, gw):
        t = sa_ref[0:kc, g:g + gw] + bias0
        sa_ref[0:kc, g:g + gw] = t
        sa_ref[kc:kc + 1, g:g + gw] = jnp.max(t, axis=0, keepdims=True)

    def logits(c, s_ref):
        off = pl.multiple_of(c * kc, kc)
        bias = jnp.where(keys_ref[pl.ds(off, kc), :] >= thr, 0.0, NEG_INF)
        bias = jnp.concatenate([bias] * (gw // LANES), axis=1)
        for g in range(0, hw, gw):
            t = lax.dot_general(kf_ref[pl.ds(off, kc), g:g + gw], qbd_ref[g // gw], _NT,
                                preferred_element_type=F32) + bias
            s_ref[0:kc, g:g + gw] = t
            s_ref[kc:kc + 1, g:g + gw] = jnp.max(t, axis=0, keepdims=True)

    def consume(s_ref, c, carry):
        m_prev, l_prev = carry
        ct = ct_ref[c]
        m_out, l_out = [], []
        for g in range(0, hw, gw):
            s = s_ref[0:kc, g:g + gw]
            m_new = jnp.maximum(m_prev[:, g:g + gw], s_ref[kc:kc + 1, g:g + gw])
            alpha = jnp.exp2(m_prev[:, g:g + gw] - m_new)
            p = jnp.exp2(s - m_new)
            l_out.append(alpha * l_prev[:, g:g + gw] + jnp.sum(p, axis=0, keepdims=True))
            m_out.append(m_new)
            acc_ref[:, g:g + gw] = alpha * acc_ref[:, g:g + gw] + jnp.dot(
                ct, p.astype(BF16), preferred_element_type=F32)
        return jnp.concatenate(m_out, axis=1), jnp.concatenate(l_out, axis=1)

    def pair_body(j, carry):
        c0 = 2 * j
        logits(c0 + 1, sb_ref)
        carry = consume(sa_ref, c0, carry)
        logits(jnp.minimum(c0 + 2, nch - 1), sa_ref)
        return consume(sb_ref, c0 + 1, carry)

    carry = _loop_k_per_trip(0, nch // 2, pair_body,
                             (jnp.full((1, hw), NEG_INF, F32), jnp.zeros((1, hw), F32)), k=4)
    _, l_fin = lax.fori_loop(0, nch % 2, lambda _, c: consume(sa_ref, nch - 1, c), carry)

    inv_l = 1.0 / l_fin
    for h in range(A_HEADS):
        cols = slice(h * LANES, (h + 1) * LANES)
        o_lat = (acc_ref[:, cols] * inv_l[:, cols]).T.astype(BF16)
        o_ref[:, cols] = jnp.dot(o_lat, wuv_ref[h], preferred_element_type=F32).astype(BF16)


def _dsa(q, qi, wi, kf, ct, ki, wuv):
    B, S, _ = q.shape
    topk = min(TOPK_MAX, S // 4)
    blk = lambda width: pl.BlockSpec((None, BLOCK, width), lambda b, i: (b, i, 0))
    seq = lambda width: pl.BlockSpec((None, S, width), lambda b, i: (b, 0, 0))
    const3 = lambda shape: pl.BlockSpec(shape, lambda b, i: (0, 0, 0))
    return pl.pallas_call(
        functools.partial(_dsa_kernel, topk=topk),
        grid=(B, S // BLOCK),
        in_specs=[blk(A_HEADS * A_QK_DIM), blk(IDX_HEADS * IDX_DIM), blk(LANES),
                  seq(A_HEADS * A_QK_DIM),
                  pl.BlockSpec((None, S // KEY_CHUNK, A_KV_RANK, KEY_CHUNK), lambda b, i: (b, 0, 0, 0)),
                  pl.BlockSpec((None, S // KEY_CHUNK, 2 * KEY_CHUNK, LANES), lambda b, i: (b, 0, 0, 0)),
                  const3(wuv.shape)],
        out_specs=blk(A_HEADS * LANES),
        out_shape=jax.ShapeDtypeStruct((B, S, A_HEADS * LANES), BF16),
        scratch_shapes=[
            pltpu.VMEM((S, BLOCK), I32),
            pltpu.VMEM((S // KEY_CHUNK, 32, 8, BLOCK), I32),
            pltpu.VMEM((A_HEADS // 2, 2 * BLOCK, 2 * LANES), BF16),
            pltpu.VMEM((A_HEADS * BLOCK, LANES), BF16),
            pltpu.VMEM((8, A_HEADS * LANES), F32),
            pltpu.VMEM((A_KV_RANK, A_HEADS * LANES), F32),
            pltpu.VMEM((KEY_CHUNK + 8, A_HEADS * LANES), F32),
            pltpu.VMEM((KEY_CHUNK + 8, A_HEADS * LANES), F32),
        ],
        compiler_params=_cparams(("parallel", "arbitrary")),
        name="dsa",
    )(q, qi, wi, kf, ct, ki, wuv)


def _window_bias(branches, nrel):
    rel = jnp.arange(nrel, dtype=I32)[:, None, None]
    krow = jnp.arange(KEY_CHUNK, dtype=I32)[None, :, None]
    qcol = jnp.arange(KEY_CHUNK, dtype=I32)[None, None, :]
    dist = KEY_CHUNK * (nrel - 1 - rel) + qcol - krow
    mult = sum(((dist >= 0) & (dist <= window) & (dist % dil == 0)).astype(F32) for window, dil in branches)
    return jnp.where(mult > 0, jnp.log2(jnp.maximum(mult, 1.0)), NEG_INF)


def _wattn_kernel(q_ref, k_ref, vt_ref, bias_ref, o_ref, acc_ref, sa_ref, sb_ref, p_ref, *, nh, nrel):
    kc = qb = KEY_CHUNK
    top = pl.program_id(1)
    c_lo = jnp.maximum(top - (nrel - 1), 0)
    n = top + 1 - c_lo

    acc_ref[...] = jnp.zeros(acc_ref.shape, F32)

    def logits(c, s_ref):
        off = pl.multiple_of(c * kc, kc)
        bias = bias_ref[c - top + (nrel - 1)]
        for h in range(nh):
            cols = slice(h * LANES, (h + 1) * LANES)
            t = lax.dot_general(k_ref[pl.ds(off, kc), cols], q_ref[:, cols], _NT,
                                preferred_element_type=F32) + bias
            s_ref[0:kc, h * qb:(h + 1) * qb] = t
            s_ref[kc:kc + 1, h * qb:(h + 1) * qb] = jnp.max(t, axis=0, keepdims=True)

    def consume(s_ref, c, carry):
        m_prev, l_prev = carry
        m_new = jnp.maximum(m_prev, s_ref[kc:kc + 1, :])
        alpha = jnp.exp2(m_prev - m_new)
        p_ref[...] = jnp.exp2(s_ref[0:kc, :] - m_new).astype(BF16)
        l_new = alpha * l_prev + jnp.dot(jnp.ones((16, kc), BF16), p_ref[...], preferred_element_type=F32)[0:1, :]
        for h in range(nh):
            cols = slice(h * qb, (h + 1) * qb)
            acc_ref[:, cols] = alpha[:, cols] * acc_ref[:, cols] + jnp.dot(
                vt_ref[c, h * LANES:(h + 1) * LANES, :], p_ref[:, cols], preferred_element_type=F32)
        return m_new, l_new

    logits(c_lo, sa_ref)

    def pair_body(j, carry):
        c0 = c_lo + 2 * j
        logits(c0 + 1, sb_ref)
        carry = consume(sa_ref, c0, carry)
        logits(jnp.minimum(c0 + 2, top), sa_ref)
        return consume(sb_ref, c0 + 1, carry)

    carry = _loop_k_per_trip(0, n // 2, pair_body,
                             (jnp.full((1, nh * qb), NEG_INF, F32), jnp.zeros((1, nh * qb), F32)), k=4)
    _, l_fin = lax.fori_loop(0, n % 2, lambda _, c: consume(sa_ref, top, c), carry)

    inv_l = 1.0 / l_fin
    for h in range(nh):
        cols = slice(h * qb, (h + 1) * qb)
        o_ref[:, h * LANES:(h + 1) * LANES] = (acc_ref[:, cols] * inv_l[:, cols]).T.astype(BF16)


def _swa_kernel(sink_ref, q_ref, kp_ref, kc_ref, vp_ref, vc_ref, o_ref, *, subheads, max_dist):
    n = pl.program_id(1)
    qi = lax.broadcasted_iota(I32, (BLOCK, 2 * BLOCK), 0)
    kj = lax.broadcasted_iota(I32, (BLOCK, 2 * BLOCK), 1)
    dist = BLOCK + qi - kj
    mask = (dist >= 0) & (dist <= max_dist) & ((kj >= BLOCK) | (n > 0))
    for j, subs in enumerate(subheads):
        qt = q_ref[:, j * LANES:(j + 1) * LANES]
        o_t = None
        for kt, vt, hidx in subs:
            kk = jnp.concatenate([kp_ref[:, kt * LANES:(kt + 1) * LANES],
                                  kc_ref[:, kt * LANES:(kt + 1) * LANES]], axis=0)
            vv = jnp.concatenate([vp_ref[:, vt * LANES:(vt + 1) * LANES],
                                  vc_ref[:, vt * LANES:(vt + 1) * LANES]], axis=0)
            s = lax.dot_general(qt, kk, _NT, preferred_element_type=F32)
            s = jnp.where(mask, s, NEG_INF)
            sk = sink_ref[hidx] * LOG2E
            m = jnp.maximum(jnp.max(s, axis=-1, keepdims=True), sk)
            p = jnp.exp2(s - m)
            l = jnp.sum(p, axis=-1, keepdims=True) + jnp.exp2(sk - m)
            o_s = jnp.dot(p.astype(BF16), vv, preferred_element_type=F32) / l
            o_t = o_s if o_t is None else o_t + o_s
        o_ref[:, j * LANES:(j + 1) * LANES] = o_t.astype(BF16)


def _swa(q, k, v, sinks, *, subheads, max_dist):
    B, S, wq = q.shape
    wk, wv = k.shape[-1], v.shape[-1]
    cur = lambda width: pl.BlockSpec((None, BLOCK, width), lambda b, i: (b, i, 0))
    prv = lambda width: pl.BlockSpec((None, BLOCK, width), lambda b, i: (b, jnp.maximum(i - 1, 0), 0))
    return pl.pallas_call(
        functools.partial(_swa_kernel, subheads=subheads, max_dist=max_dist),
        grid=(B, S // BLOCK),
        in_specs=[pl.BlockSpec(memory_space=pltpu.SMEM), cur(wq), prv(wk), cur(wk), prv(wv), cur(wv)],
        out_specs=cur(wq),
        out_shape=jax.ShapeDtypeStruct((B, S, wq), BF16),
        compiler_params=_cparams(("parallel", "arbitrary")),
        name="swa",
    )(sinks, q, k, k, v, v)


def _wattn(q, k, vt, branches):
    B, S, wq = q.shape
    widest = max(w for w, _ in branches)
    nrel = min(-(-widest // KEY_CHUNK) + 1, S // KEY_CHUNK)
    bias = _window_bias(branches, nrel)
    nh = wq // LANES
    blk = lambda width: pl.BlockSpec((None, KEY_CHUNK, width), lambda b, i: (b, i, 0))
    return pl.pallas_call(
        functools.partial(_wattn_kernel, nh=nh, nrel=nrel),
        grid=(B, S // KEY_CHUNK),
        in_specs=[blk(wq),
                  pl.BlockSpec((None, S, wq), lambda b, i: (b, 0, 0)),
                  pl.BlockSpec((None,) + vt.shape[1:], lambda b, i: (b, 0, 0, 0)),
                  pl.BlockSpec(bias.shape, lambda b, i: (0, 0, 0))],
        out_specs=blk(wq),
        out_shape=jax.ShapeDtypeStruct((B, S, wq), BF16),
        scratch_shapes=[pltpu.VMEM((LANES, nh * KEY_CHUNK), F32),
                        pltpu.VMEM((KEY_CHUNK + 8, nh * KEY_CHUNK), F32),
                        pltpu.VMEM((KEY_CHUNK + 8, nh * KEY_CHUNK), F32),
                        pltpu.VMEM((KEY_CHUNK, nh * KEY_CHUNK), BF16)],
        compiler_params=_cparams(("parallel", "arbitrary")),
        name="wattn",
    )(q, k, vt, bias)


def _memkv_kernel(mem_ref, g_ref, w_ref, k_ref, v_ref):
    hb = _rms(mem_ref[...], g_ref[...]).astype(BF16)
    y = jnp.dot(hb, w_ref[...], preferred_element_type=F32)
    lane = lax.broadcasted_iota(I32, (y.shape[0], LANES), 1)
    for t in range(MEM_WIDTH // LANES):
        for out_ref, base in ((k_ref, 0), (v_ref, MEM_WIDTH)):
            tile = y[:, base + t * LANES:base + (t + 1) * LANES]
            out_ref[:, (2 * t) * LANES:(2 * t + 1) * LANES] = jnp.where(lane < MEM_HEAD_DIM, tile, 0.0).astype(BF16)
            out_ref[:, (2 * t + 1) * LANES:(2 * t + 2) * LANES] = jnp.where(lane >= MEM_HEAD_DIM, tile, 0.0).astype(BF16)


def _memkv(mem, g_mem, w):
    B, M, D = mem.shape
    L = w.shape[0]
    out = jax.ShapeDtypeStruct((L, B, M, 2 * MEM_WIDTH), BF16)
    ospec = pl.BlockSpec((None, None, M, 2 * MEM_WIDTH), lambda l, b: (l, b, 0, 0))
    return pl.pallas_call(
        _memkv_kernel,
        grid=(L, B),
        in_specs=[pl.BlockSpec((None, M, D), lambda l, b: (b, 0, 0)),
                  pl.BlockSpec((1, D), lambda l, b: (0, 0)),
                  pl.BlockSpec((None, D, 2 * MEM_WIDTH), lambda l, b: (l, 0, 0))],
        out_specs=(ospec, ospec),
        out_shape=(out, out),
        compiler_params=_cparams(("parallel", "parallel")),
        name="memkv",
    )(mem, g_mem, w)


def _mixer_residual(x_ref, mix_ref, qm_ref, mk_ref, mv_ref, w_ref):
    mw = mix_ref.shape[-1]
    n_sub = 2 * MEM_WIDTH // LANES
    logits = [lax.dot_general(qm_ref[:, (u // 2) * LANES:(u // 2 + 1) * LANES], mk_ref[:, u * LANES:(u + 1) * LANES],
                              _NT, preferred_element_type=F32) for u in range(n_sub)]
    y = x_ref[...] + jnp.dot(mix_ref[...], w_ref[0:mw, :], preferred_element_type=F32)
    tiles = []
    for t in range(n_sub // 2):
        o_t = None
        for u in (2 * t, 2 * t + 1):
            p = jnp.exp2(logits[u] - jnp.max(logits[u], axis=-1, keepdims=True))
            l = jnp.sum(p, axis=-1, keepdims=True)
            o_s = jnp.dot(p.astype(BF16), mv_ref[:, u * LANES:(u + 1) * LANES], preferred_element_type=F32) / l
            o_t = o_s if o_t is None else o_t + o_s
        tiles.append(o_t.astype(BF16))
    return y + jnp.dot(jnp.concatenate(tiles, axis=1), w_ref[mw:mw + MEM_WIDTH, :], preferred_element_type=F32)


def _out_kernel(x_ref, mix_ref, qm_ref, mk_ref, mv_ref, w_ref, o_ref):
    o_ref[...] = _mixer_residual(x_ref, mix_ref, qm_ref, mk_ref, mv_ref, w_ref)


def _out_proj(x, mix, qm, mk, mv, w, tm):
    B, S, D = x.shape
    row = lambda width: pl.BlockSpec((None, tm, width), lambda b, i: (b, i, 0))
    mem = pl.BlockSpec((None, mk.shape[1], 2 * MEM_WIDTH), lambda b, i: (b, 0, 0))
    return pl.pallas_call(
        _out_kernel,
        grid=(B, S // tm),
        in_specs=[row(D), row(mix.shape[-1]), row(MEM_WIDTH), mem, mem,
                  pl.BlockSpec(w.shape, lambda b, i: (0, 0))],
        out_specs=row(D),
        out_shape=jax.ShapeDtypeStruct((B, S, D), F32),
        compiler_params=_cparams(("parallel", "parallel")),
        name="out_proj",
    )(x, mix, qm, mk, mv, w)


def _ffn_kernel(x_ref, g_ref, wup_ref, cw_ref, cb_ref, wdn_ref, gf_ref, o_ref, carry_ref, *, tm, cw, final):
    dff = wdn_ref.shape[0]
    hm = min(tm, 2 * LANES)
    first = pl.program_id(1) == 0
    row = lax.broadcasted_iota(I32, (hm, cw), 0)
    xs = [x_ref[r:r + hm, :] for r in range(0, tm, hm)]
    hbs = [_rms(x, g_ref[...]).astype(BF16) for x in xs]
    accs = list(xs)

    def up(hb, c0):
        return (jnp.dot(hb, wup_ref[:, c0:c0 + cw], preferred_element_type=F32),
                jnp.dot(hb, wup_ref[:, dff + c0:dff + c0 + cw], preferred_element_type=F32))

    def gate(a, b, prev, c0):
        p1, p2 = prev[7:8, :], prev[6:7, :]
        a1 = jnp.where(row == 0, p1, pltpu.roll(a, 1, 0))
        a2 = jnp.where(row == 0, p2, jnp.where(row == 1, p1, pltpu.roll(a, 2, 0)))
        w = cw_ref[:, c0:c0 + cw]
        conv = w[0:1, :] * a2 + w[1:2, :] * a1 + w[2:3, :] * a + cb_ref[:, c0:c0 + cw]
        return (conv / (1.0 + jnp.exp(-conv)) * b).astype(BF16)

    nxt = [up(hb, 0) for hb in hbs]
    for c0 in range(0, dff, cw):
        cur = nxt
        if c0 + cw < dff:
            nxt = [up(hb, c0 + cw) for hb in hbs]
        prev = jnp.where(first, 0.0, carry_ref[:, c0:c0 + cw])
        for k, (a, b) in enumerate(cur):
            accs[k] = accs[k] + jnp.dot(gate(a, b, prev, c0), wdn_ref[c0:c0 + cw, :], preferred_element_type=F32)
            prev = a[hm - 8:hm, :]
        carry_ref[:, c0:c0 + cw] = prev
    for k, acc in enumerate(accs):
        o_ref[k * hm:(k + 1) * hm, :] = _rms(acc, gf_ref[...]) if final else acc


def _ffn(x, g, wup, cw, cb, wdn, gf, tm, final):
    B, S, D = x.shape
    dff = wdn.shape[0]
    row = pl.BlockSpec((None, tm, D), lambda b, i: (b, i, 0))
    const = lambda shape: pl.BlockSpec(shape, lambda b, i: (0, 0), pipeline_mode=pl.Buffered(1))
    return pl.pallas_call(
        functools.partial(_ffn_kernel, tm=tm, cw=2 * LANES, final=final),
        grid=(B, S // tm),
        in_specs=[row, const((1, D)), const(wup.shape), const(cw.shape), const((1, dff)), const(wdn.shape),
                  const((1, D))],
        out_specs=row,
        out_shape=jax.ShapeDtypeStruct((B, S, D), F32),
        scratch_shapes=[pltpu.VMEM((8, dff), F32)],
        compiler_params=_cparams(("arbitrary", "arbitrary")),
        name="ffn",
    )(x, g, wup, cw, cb, wdn, gf)


def _b_subheads():
    return tuple(((2 * (j // 2), 2 * (j // 2), 2 * j), (2 * (j // 2) + 1, 2 * (j // 2) + 1, 2 * j + 1))
                 for j in range(B_HEADS // 2))


def kernel(x, mem, positions, g_mix, g_ffn, g_mem, g_final, w_mem_kv, a_w_in, a_kv_norm, a_w_uk, a_w_uv, a_w_out,
           b_w_in, b_sinks, b_w_out, c_w_in, c_w_out, f_w_up, f_conv_w, f_conv_b, f_w_down):
    B, S, D = x.shape
    depth = g_mix.shape[0]
    tm = min(512, S)
    tab32 = _rope_table(positions, A_QK_DIM, A_ROPE_DIM)
    tab16 = _rope_table(positions, B_HEAD_DIM, B_HEAD_DIM // 4)
    mem_k, mem_v = _memkv(mem, g_mem.reshape(1, D), w_mem_kv.astype(BF16))
    conv_w = jnp.pad(f_conv_w, ((0, 0), (0, 8 - CONV_WIDTH), (0, 0)))
    for i in range(depth):
        kind, j = i % 3, i // 3
        g = g_mix[i].reshape(1, D)
        if kind == 0:
            wuk = jnp.pad(a_w_uk[j], ((0, 0), (0, 0), (A_ROPE_DIM, 0))).reshape(A_KV_RANK, -1).astype(BF16)
            wuv = jnp.transpose(a_w_uv[j], (1, 0, 2)).astype(BF16)
            q, kf, ct, qi, ki, wi, qm = _proj_a(x, g, _prep_a_w_in(a_w_in[j]), a_kv_norm[j].reshape(1, -1), wuk,
                                                tab32, tab16, tm)
            mix = _dsa(q, qi, wi, kf, ct, ki, wuv)
            w_out = a_w_out[j]
        elif kind == 1:
            nq, nkv = B_HEADS * B_HEAD_DIM, 4 * B_KV_HEADS * B_HEAD_DIM
            q, k, v, qm = _proj_qkv(x, g, _prep_b_w_in(b_w_in[j]), tab16, tm, B_HEAD_DIM, B_HEAD_DIM // 4,
                                    nq, nkv, nkv, False)
            mix = _swa(q, k, v, b_sinks[j], subheads=_b_subheads(), max_dist=B_WINDOW - 1)
            w_out = b_w_out[j]
        else:
            nq = C_HEADS * C_HEAD_DIM
            q, k, v, qm = _proj_qkv(x, g, c_w_in[j].astype(BF16), tab32, tm, C_HEAD_DIM, C_HEAD_DIM // 4,
                                    nq, nq, nq, True)
            mix = _wattn(q, k, v, C_BRANCHES)
            w_out = c_w_out[j]
        x = _out_proj(x, mix, qm, mem_k[i], mem_v[i], w_out.astype(BF16), tm)
        x = _ffn(x, g_ffn[i].reshape(1, D), f_w_up[i].astype(BF16), conv_w[i], f_conv_b[i].reshape(1, -1),
                 f_w_down[i].astype(BF16), g_final.reshape(1, D), tm, i == depth - 1)
    return x
```

```python
import functools

import jax
import jax.numpy as jnp
import numpy as np
from jax import lax
from jax.experimental import pallas as pl
from jax.experimental.pallas import tpu as pltpu

F32 = jnp.float32
BF16 = jnp.bfloat16
I32 = jnp.int32

LANES = 128
BLOCK = 128
ROPE_THETA = 500000.0
EPS = 1e-6
NEG_INF = -1e30
LOG2E = 1.4426950408889634
INT_MIN = -(2**31)

A_HEADS = 8
A_QK_DIM = 128
A_ROPE_DIM = 32
A_KV_RANK = 256
IDX_HEADS = 16
IDX_DIM = 64
TOPK_MAX = 256
B_HEADS = 16
B_KV_HEADS = 4
B_HEAD_DIM = 64
B_WINDOW = 128
C_HEADS = 8
C_HEAD_DIM = 128
C_BRANCHES = ((128, 1), (512, 4), (2048, 16))
MEM_HEADS = 4
MEM_HEAD_DIM = 64
MEM_WIDTH = MEM_HEADS * MEM_HEAD_DIM
CONV_WIDTH = 3
KEY_CHUNK = 256

VMEM_LIMIT = 56 * 1024 * 1024

_NT = (((1,), (1,)), ((), ()))


def _cparams(sem):
    return pltpu.CompilerParams(dimension_semantics=sem, vmem_limit_bytes=VMEM_LIMIT)


def _rms(x, g):
    return x * lax.rsqrt(jnp.mean(x * x, axis=-1, keepdims=True) + EPS) * g


def _loop_k_per_trip(lo, hi, body, carry, k=2):
    if k == 1:
        return lax.fori_loop(lo, hi, body, carry)
    trips = (hi - lo) // k

    def several(t, c):
        for u in range(k):
            c = body(lo + k * t + u, c)
        return c

    carry = lax.fori_loop(0, trips, several, carry)
    return _loop_k_per_trip(lo + k * trips, hi, body, carry, k // 2)


def _bit_transpose32(words):
    a = list(words)
    j, m = 16, 0x0000FFFF
    while j:
        mask = jnp.int32(m - (1 << 32) if m >= (1 << 31) else m)
        for k in range(32):
            if k & j == 0:
                t = (a[k] ^ lax.shift_right_logical(a[k + j], j)) & mask
                a[k] = a[k] ^ t
                a[k + j] = a[k + j] ^ lax.shift_left(t, j)
        j >>= 1
        m = (m ^ (m << j)) & 0xFFFFFFFF
    return a


def _rope_tile(t, tab, half):
    c, sa, sb = tab[:, 0:LANES], tab[:, LANES:2 * LANES], tab[:, 2 * LANES:3 * LANES]
    return t * c + pltpu.roll(t, half, 1) * sa + pltpu.roll(t, LANES - half, 1) * sb


def _rope_table(positions, head_dim, rot):
    half = rot // 2
    inv = ROPE_THETA ** (-jnp.arange(0, rot, 2, dtype=F32) / rot)
    ang = positions.astype(F32)[..., None] * inv
    cs = jnp.concatenate([jnp.cos(ang), jnp.sin(ang)], axis=-1)
    sel = np.zeros((rot, 3 * LANES), np.float32)
    one = np.zeros((3 * LANES,), np.float32)
    for l in range(LANES):
        j = l % head_dim
        if j < half:
            sel[j, l] = 1.0
            sel[half + j, 2 * LANES + l] = -1.0
        elif j < rot:
            sel[j - half, l] = 1.0
            sel[j, LANES + l] = 1.0
        else:
            one[l] = 1.0
    return jnp.dot(cs, jnp.asarray(sel), precision=lax.Precision.HIGHEST) + jnp.asarray(one)


def _staggered(hbs, w_ref, jobs):
    def mms(lo, hi):
        return [jnp.dot(hb, w_ref[:, lo:hi], preferred_element_type=F32) for hb in hbs]

    nxt = mms(jobs[0][0], jobs[0][1])
    for n, (_, _, epilogue) in enumerate(jobs):
        cur = nxt
        if n + 1 < len(jobs):
            nxt = mms(jobs[n + 1][0], jobs[n + 1][1])
        for part, y in enumerate(cur):
            epilogue(y, part)


def _proj_a_kernel(x_ref, g_ref, w_ref, kvn_ref, wuk_ref, t32_ref, t16_ref,
                   q_ref, kf_ref, ct_ref, qi_ref, ki_ref, wi_ref, qm_ref, *, tm):
    hm = KEY_CHUNK
    rows = [slice(r, r + hm) for r in range(0, tm, hm)]
    hbs = [_rms(x_ref[r, :], g_ref[...]).astype(BF16) for r in rows]

    def roped(ref, col, tab_ref, half, scale):
        def epilogue(y, part):
            tab = tab_ref[rows[part], :]
            for u in range(y.shape[1] // LANES):
                t = _rope_tile(y[:, u * LANES:(u + 1) * LANES], tab, half)
                if scale != 1.0:
                    t = t * scale
                ref[rows[part], col + u * LANES:col + (u + 1) * LANES] = t.astype(BF16)
        return epilogue

    latent_bf16 = {}

    def latent(y, part):
        c = _rms(y, kvn_ref[...])
        latent_bf16[part] = c.astype(BF16)
        ct_ref[part] = c.T.astype(BF16)

    def full_keys(y, part):
        k_rope = _rope_tile(y, t32_ref[rows[part], :], A_ROPE_DIM // 2)
        k_nope = jnp.dot(latent_bf16[part], wuk_ref[...], preferred_element_type=F32)
        for h in range(A_HEADS):
            kf_ref[rows[part], h * LANES:(h + 1) * LANES] = (
                k_nope[:, h * LANES:(h + 1) * LANES] + k_rope).astype(BF16)

    def index_keys(y, part):
        tab = t16_ref[rows[part], :]
        for u in range(2):
            ki_ref[part, u * hm:(u + 1) * hm, :] = _rope_tile(
                y[:, u * LANES:(u + 1) * LANES], tab, IDX_DIM // 8).astype(BF16)

    def index_weights(y, part):
        wi_ref[rows[part], :] = y * (IDX_HEADS * IDX_DIM) ** -0.5

    def mem_query(y, part):
        qm_ref[rows[part], :] = (y * (MEM_HEAD_DIM ** -0.5 * LOG2E)).astype(BF16)

    jobs, o = [], 0
    for j in range(0, A_HEADS * A_QK_DIM, 2 * LANES):
        jobs.append((o + j, o + j + 2 * LANES,
                     roped(q_ref, j, t32_ref, A_ROPE_DIM // 2, A_QK_DIM ** -0.5 * LOG2E)))
    o += A_HEADS * A_QK_DIM
    jobs.append((o, o + A_KV_RANK, latent))
    o += A_KV_RANK
    jobs.append((o, o + LANES, full_keys))
    o += LANES
    for j in range(0, IDX_HEADS * IDX_DIM, 2 * LANES):
        jobs.append((o + j, o + j + 2 * LANES, roped(qi_ref, j, t16_ref, IDX_DIM // 8, 1.0)))
    o += IDX_HEADS * IDX_DIM
    jobs.append((o, o + 2 * LANES, index_keys))
    o += 2 * LANES
    jobs.append((o, o + LANES, index_weights))
    o += LANES
    jobs.append((o, o + MEM_WIDTH, mem_query))
    _staggered(hbs, w_ref, jobs)


def _prep_a_w_in(w):
    d = w.shape[0]
    sizes = (A_HEADS * A_QK_DIM, A_KV_RANK, A_ROPE_DIM, IDX_HEADS * IDX_DIM, IDX_DIM, IDX_HEADS, MEM_WIDTH)
    offs = [0]
    for s in sizes:
        offs.append(offs[-1] + s)
    q, ckv, kr, qi, ki, wi, qm = [w[:, offs[i]:offs[i + 1]] for i in range(len(sizes))]
    z = lambda n: jnp.zeros((d, n), w.dtype)
    return jnp.concatenate([
        q, ckv, kr, z(LANES - A_ROPE_DIM), qi,
        ki, z(LANES - IDX_DIM), z(LANES - IDX_DIM), ki,
        wi, z(LANES - IDX_HEADS), qm], axis=1).astype(BF16)


def _proj_a(x, g, w, kvn, wuk, t32, t16, tm):
    B, S, D = x.shape
    n = w.shape[1]
    row = lambda width: pl.BlockSpec((None, tm, width), lambda b, i: (b, i, 0))
    const = lambda shape: pl.BlockSpec(shape, lambda b, i: (0,) * len(shape))
    out_shape = (
        jax.ShapeDtypeStruct((B, S, A_HEADS * A_QK_DIM), BF16),
        jax.ShapeDtypeStruct((B, S, A_HEADS * A_QK_DIM), BF16),
        jax.ShapeDtypeStruct((B, S // KEY_CHUNK, A_KV_RANK, KEY_CHUNK), BF16),
        jax.ShapeDtypeStruct((B, S, IDX_HEADS * IDX_DIM), BF16),
        jax.ShapeDtypeStruct((B, S // KEY_CHUNK, 2 * KEY_CHUNK, LANES), BF16),
        jax.ShapeDtypeStruct((B, S, LANES), F32),
        jax.ShapeDtypeStruct((B, S, MEM_WIDTH), BF16),
    )
    out_specs = (
        row(A_HEADS * A_QK_DIM), row(A_HEADS * A_QK_DIM),
        pl.BlockSpec((None, tm // KEY_CHUNK, A_KV_RANK, KEY_CHUNK), lambda b, i: (b, i, 0, 0)),
        row(IDX_HEADS * IDX_DIM),
        pl.BlockSpec((None, tm // KEY_CHUNK, 2 * KEY_CHUNK, LANES), lambda b, i: (b, i, 0, 0)),
        row(LANES), row(MEM_WIDTH),
    )
    return pl.pallas_call(
        functools.partial(_proj_a_kernel, tm=tm),
        grid=(B, S // tm),
        in_specs=[row(D), const((1, D)), const((D, n)), const((1, A_KV_RANK)), const(wuk.shape),
                  row(3 * LANES), row(3 * LANES)],
        out_specs=out_specs,
        out_shape=out_shape,
        compiler_params=_cparams(("parallel", "parallel")),
        name="proj_a",
    )(x, g, w, kvn, wuk, t32, t16)


def _proj_qkv_kernel(x_ref, g_ref, w_ref, tab_ref, q_ref, k_ref, v_ref, qm_ref, *,
                     head_dim, rot, nq, nk, nv, v_transposed):
    tm = x_ref.shape[0]
    hm = KEY_CHUNK
    rows = [slice(r, r + hm) for r in range(0, tm, hm)]
    hbs = [_rms(x_ref[r, :], g_ref[...]).astype(BF16) for r in rows]

    def roped(ref, col, scale):
        def epilogue(y, part):
            tab = tab_ref[rows[part], :]
            for u in range(2):
                t = _rope_tile(y[:, u * LANES:(u + 1) * LANES], tab, rot // 2)
                if scale != 1.0:
                    t = t * scale
                ref[rows[part], col + u * LANES:col + (u + 1) * LANES] = t.astype(BF16)
        return epilogue

    def value(col):
        def epilogue(y, part):
            if v_transposed:
                v_ref[part, col:col + 2 * LANES, :] = jnp.swapaxes(y[None], 1, 2)[0].astype(BF16)
            else:
                v_ref[rows[part], col:col + 2 * LANES] = y.astype(BF16)
        return epilogue

    def mem_query(y, part):
        qm_ref[rows[part], :] = (y * (MEM_HEAD_DIM ** -0.5 * LOG2E)).astype(BF16)

    jobs = [(j, j + 2 * LANES, roped(q_ref, j, head_dim ** -0.5 * LOG2E)) for j in range(0, nq, 2 * LANES)]
    jobs += [(nq + j, nq + j + 2 * LANES, roped(k_ref, j, 1.0)) for j in range(0, nk, 2 * LANES)]
    jobs += [(nq + nk + j, nq + nk + j + 2 * LANES, value(j)) for j in range(0, nv, 2 * LANES)]
    jobs.append((nq + nk + nv, nq + nk + nv + MEM_WIDTH, mem_query))
    _staggered(hbs, w_ref, jobs)


def _proj_qkv(x, g, w, tab, tm, head_dim, rot, nq, nk, nv, v_transposed):
    B, S, D = x.shape
    n = w.shape[1]
    row = lambda width: pl.BlockSpec((None, tm, width), lambda b, i: (b, i, 0))
    const = lambda shape: pl.BlockSpec(shape, lambda b, i: (0,) * len(shape))
    if v_transposed:
        v_spec = pl.BlockSpec((None, tm // KEY_CHUNK, nv, KEY_CHUNK), lambda b, i: (b, i, 0, 0))
        v_shape = jax.ShapeDtypeStruct((B, S // KEY_CHUNK, nv, KEY_CHUNK), BF16)
    else:
        v_spec, v_shape = row(nv), jax.ShapeDtypeStruct((B, S, nv), BF16)
    return pl.pallas_call(
        functools.partial(_proj_qkv_kernel, head_dim=head_dim, rot=rot, nq=nq, nk=nk, nv=nv,
                          v_transposed=v_transposed),
        grid=(B, S // tm),
        in_specs=[row(D), const((1, D)), const((D, n)), row(3 * LANES)],
        out_specs=(row(nq), row(nk), v_spec, row(MEM_WIDTH)),
        out_shape=(jax.ShapeDtypeStruct((B, S, nq), BF16), jax.ShapeDtypeStruct((B, S, nk), BF16), v_shape,
                   jax.ShapeDtypeStruct((B, S, MEM_WIDTH), BF16)),
        compiler_params=_cparams(("parallel", "parallel")),
        name="proj_qkv",
    )(x, g, w, tab)


def _prep_b_w_in(w):
    d = w.shape[0]
    nq, nkv = B_HEADS * B_HEAD_DIM, B_KV_HEADS * B_HEAD_DIM
    q, k, v, qm = w[:, :nq], w[:, nq:nq + nkv], w[:, nq + nkv:nq + 2 * nkv], w[:, nq + 2 * nkv:]
    z = jnp.zeros((d, B_HEAD_DIM), w.dtype)

    def spread(t):
        cols = []
        for h in range(B_KV_HEADS):
            th = t[:, h * B_HEAD_DIM:(h + 1) * B_HEAD_DIM]
            cols += [th, z, z, th]
        return jnp.concatenate(cols, axis=1)

    return jnp.concatenate([q, spread(k), spread(v), qm], axis=1).astype(BF16)


def _dsa_kernel(q_ref, qi_ref, wi_ref, kf_ref, ct_ref, ki_ref, wuv_ref, o_ref,
                keys_ref, planes_ref, qbd_ref, qi2_ref, wrow_ref, acc_ref, sa_ref, sb_ref, *, topk):
    kc = KEY_CHUNK
    hw = A_HEADS * LANES
    gw = 2 * LANES
    i = pl.program_id(1)
    nch = (i * BLOCK + BLOCK + kc - 1) // kc

    qbd_ref[...] = jnp.zeros(qbd_ref.shape, BF16)
    for h in range(A_HEADS):
        rows = slice(h * BLOCK, (h + 1) * BLOCK)
        qbd_ref[h // 2, (h % 2) * BLOCK:(h % 2 + 1) * BLOCK, (h % 2) * LANES:(h % 2 + 1) * LANES] = (
            q_ref[:, h * LANES:(h + 1) * LANES])
        qi2_ref[rows, :] = qi_ref[:, h * LANES:(h + 1) * LANES]
    w_t = wi_ref[...].T
    for j in range(IDX_HEADS // 2):
        wrow_ref[0:1, j * LANES:(j + 1) * LANES] = w_t[2 * j:2 * j + 1, :]
        wrow_ref[1:2, j * LANES:(j + 1) * LANES] = w_t[2 * j + 1:2 * j + 2, :]

    qpos = i * BLOCK + lax.broadcasted_iota(I32, (kc, BLOCK), 1)
    krow = lax.broadcasted_iota(I32, (kc, BLOCK), 0)

    def score_chunk(c):
        off = pl.multiple_of(c * kc, kc)
        kk = ki_ref[c]
        acc = None
        for g in range(0, hw, gw):
            d = lax.dot_general(kk, qi2_ref[g:g + gw, :], _NT, preferred_element_type=F32)
            t = (jnp.maximum(d[0:kc], 0.0) * wrow_ref[0:1, g:g + gw]
                 + jnp.maximum(d[kc:2 * kc], 0.0) * wrow_ref[1:2, g:g + gw])
            for u in range(0, gw, LANES):
                acc = t[:, u:u + LANES] if acc is None else acc + t[:, u:u + LANES]
        bits = pltpu.bitcast(acc, I32)
        key = jnp.where(bits < 0, bits ^ 0x7FFFFFFF, bits)
        key = jnp.where(off + krow <= qpos, key, INT_MIN)
        keys_ref[pl.ds(off, kc), :] = key
        planes = _bit_transpose32([(key[8 * r:8 * r + 8, :] ^ INT_MIN) for r in range(kc // 8)])
        for p in range(32):
            planes_ref[c, p] = planes[p]

    @pl.when(i == 0)
    def _():
        planes_ref[...] = jnp.zeros(planes_ref.shape, I32)

    def score_body(c, carry):
        score_chunk(c)
        return carry

    _loop_k_per_trip(0, nch, score_body, 0, k=8)

    n_chunks = planes_ref.shape[0]

    def bit_body(p, carry):
        alive, need, res = carry
        ones = [a & planes_ref[c, p] for c, a in enumerate(alive)]
        cnt = ones[0] * 0
        for o in ones:
            cnt = cnt + lax.population_count(o)
        cnt = jnp.sum(cnt, axis=0, keepdims=True)
        take = cnt >= need
        alive = [jnp.where(take, o, a ^ o) for o, a in zip(ones, alive)]
        need = jnp.where(take, need, need - cnt)
        res = jnp.where(take, res | lax.shift_left(jnp.int32(1), 31 - p), res)
        return alive, need, res

    for g in range(0, hw, gw):
        sa_ref[0:kc, g:g + gw] = lax.dot_general(kf_ref[0:kc, g:g + gw], qbd_ref[g // gw], _NT,
                                                 preferred_element_type=F32)

    carry = ([jnp.where(c < nch, jnp.full((8, BLOCK), -1, I32), 0) for c in range(n_chunks)],
             jnp.full((1, BLOCK), topk, I32), jnp.zeros((1, BLOCK), I32))
    for p in range(32):
        carry = bit_body(p, carry)
    thr = jnp.maximum(carry[2] ^ INT_MIN, INT_MIN + 1)

    acc_ref[...] = jnp.zeros(acc_ref.shape, F32)
    bias0 = jnp.where(keys_ref[0:kc, :] >= thr, 0.0, NEG_INF)
    bias0 = jnp.concatenate([bias0] * (gw // LANES), axis=1)
    for g in range(0, hw, gw):
        t = sa_ref[0:kc, g:g + gw] + bias0
        sa_ref[0:kc, g:g + gw] = t
        sa_ref[kc:kc + 1, g:g + gw] = jnp.max(t, axis=0, keepdims=True)

    def logits(c, s_ref):
        off = pl.multiple_of(c * kc, kc)
        bias = jnp.where(keys_ref[pl.ds(off, kc), :] >= thr, 0.0, NEG_INF)
        bias = jnp.concatenate([bias] * (gw // LANES), axis=1)
        for g in range(0, hw, gw):
            t = lax.dot_general(kf_ref[pl.ds(off, kc), g:g + gw], qbd_ref[g // gw], _NT,
                                preferred_element_type=F32) + bias
            s_ref[0:kc, g:g + gw] = t
            s_ref[kc:kc + 1, g:g + gw] = jnp.max(t, axis=0, keepdims=True)

    def consume(s_ref, c, carry):
        m_prev, l_prev = carry
        ct = ct_ref[c]
        m_out, l_out = [], []
        for g in range(0, hw, gw):
            s = s_ref[0:kc, g:g + gw]
            m_new = jnp.maximum(m_prev[:, g:g + gw], s_ref[kc:kc + 1, g:g + gw])
            alpha = jnp.exp2(m_prev[:, g:g + gw] - m_new)
            p = jnp.exp2(s - m_new)
            l_out.append(alpha * l_prev[:, g:g + gw] + jnp.sum(p, axis=0, keepdims=True))
            m_out.append(m_new)
            acc_ref[:, g:g + gw] = alpha * acc_ref[:, g:g + gw] + jnp.dot(
                ct, p.astype(BF16), preferred_element_type=F32)
        return jnp.concatenate(m_out, axis=1), jnp.concatenate(l_out, axis=1)

    def pair_body(j, carry):
        c0 = 2 * j
        logits(c0 + 1, sb_ref)
        carry = consume(sa_ref, c0, carry)
        logits(jnp.minimum(c0 + 2, nch - 1), sa_ref)
        return consume(sb_ref, c0 + 1, carry)

    carry = _loop_k_per_trip(0, nch // 2, pair_body,
                             (jnp.full((1, hw), NEG_INF, F32), jnp.zeros((1, hw), F32)), k=4)
    _, l_fin = lax.fori_loop(0, nch % 2, lambda _, c: consume(sa_ref, nch - 1, c), carry)

    inv_l = 1.0 / l_fin
    for h in range(A_HEADS):
        cols = slice(h * LANES, (h + 1) * LANES)
        o_lat = (acc_ref[:, cols] * inv_l[:, cols]).T.astype(BF16)
        o_ref[:, cols] = jnp.dot(o_lat, wuv_ref[h], preferred_element_type=F32).astype(BF16)


def _dsa(q, qi, wi, kf, ct, ki, wuv):
    B, S, _ = q.shape
    topk = min(TOPK_MAX, S // 4)
    blk = lambda width: pl.BlockSpec((None, BLOCK, width), lambda b, i: (b, i, 0))
    seq = lambda width: pl.BlockSpec((None, S, width), lambda b, i: (b, 0, 0))
    const3 = lambda shape: pl.BlockSpec(shape, lambda b, i: (0, 0, 0))
    return pl.pallas_call(
        functools.partial(_dsa_kernel, topk=topk),
        grid=(B, S // BLOCK),
        in_specs=[blk(A_HEADS * A_QK_DIM), blk(IDX_HEADS * IDX_DIM), blk(LANES),
                  seq(A_HEADS * A_QK_DIM),
                  pl.BlockSpec((None, S // KEY_CHUNK, A_KV_RANK, KEY_CHUNK), lambda b, i: (b, 0, 0, 0)),
                  pl.BlockSpec((None, S // KEY_CHUNK, 2 * KEY_CHUNK, LANES), lambda b, i: (b, 0, 0, 0)),
                  const3(wuv.shape)],
        out_specs=blk(A_HEADS * LANES),
        out_shape=jax.ShapeDtypeStruct((B, S, A_HEADS * LANES), BF16),
        scratch_shapes=[
            pltpu.VMEM((S, BLOCK), I32),
            pltpu.VMEM((S // KEY_CHUNK, 32, 8, BLOCK), I32),
            pltpu.VMEM((A_HEADS // 2, 2 * BLOCK, 2 * LANES), BF16),
            pltpu.VMEM((A_HEADS * BLOCK, LANES), BF16),
            pltpu.VMEM((8, A_HEADS * LANES), F32),
            pltpu.VMEM((A_KV_RANK, A_HEADS * LANES), F32),
            pltpu.VMEM((KEY_CHUNK + 8, A_HEADS * LANES), F32),
            pltpu.VMEM((KEY_CHUNK + 8, A_HEADS * LANES), F32),
        ],
        compiler_params=_cparams(("parallel", "arbitrary")),
        name="dsa",
    )(q, qi, wi, kf, ct, ki, wuv)


def _window_bias(branches, nrel):
    rel = jnp.arange(nrel, dtype=I32)[:, None, None]
    krow = jnp.arange(KEY_CHUNK, dtype=I32)[None, :, None]
    qcol = jnp.arange(KEY_CHUNK, dtype=I32)[None, None, :]
    dist = KEY_CHUNK * (nrel - 1 - rel) + qcol - krow
    mult = sum(((dist >= 0) & (dist <= window) & (dist % dil == 0)).astype(F32) for window, dil in branches)
    return jnp.where(mult > 0, jnp.log2(jnp.maximum(mult, 1.0)), NEG_INF)


def _wattn_kernel(q_ref, k_ref, vt_ref, bias_ref, o_ref, acc_ref, sa_ref, sb_ref, p_ref, *, nh, nrel):
    kc = qb = KEY_CHUNK
    top = pl.program_id(1)
    c_lo = jnp.maximum(top - (nrel - 1), 0)
    n = top + 1 - c_lo

    acc_ref[...] = jnp.zeros(acc_ref.shape, F32)

    def logits(c, s_ref):
        off = pl.multiple_of(c * kc, kc)
        bias = bias_ref[c - top + (nrel - 1)]
        for h in range(nh):
            cols = slice(h * LANES, (h + 1) * LANES)
            t = lax.dot_general(k_ref[pl.ds(off, kc), cols], q_ref[:, cols], _NT,
                                preferred_element_type=F32) + bias
            s_ref[0:kc, h * qb:(h + 1) * qb] = t
            s_ref[kc:kc + 1, h * qb:(h + 1) * qb] = jnp.max(t, axis=0, keepdims=True)

    def consume(s_ref, c, carry):
        m_prev, l_prev = carry
        m_new = jnp.maximum(m_prev, s_ref[kc:kc + 1, :])
        alpha = jnp.exp2(m_prev - m_new)
        p_ref[...] = jnp.exp2(s_ref[0:kc, :] - m_new).astype(BF16)
        l_new = alpha * l_prev + jnp.dot(jnp.ones((16, kc), BF16), p_ref[...], preferred_element_type=F32)[0:1, :]
        for h in range(nh):
            cols = slice(h * qb, (h + 1) * qb)
            acc_ref[:, cols] = alpha[:, cols] * acc_ref[:, cols] + jnp.dot(
                vt_ref[c, h * LANES:(h + 1) * LANES, :], p_ref[:, cols], preferred_element_type=F32)
        return m_new, l_new

    logits(c_lo, sa_ref)

    def pair_body(j, carry):
        c0 = c_lo + 2 * j
        logits(c0 + 1, sb_ref)
        carry = consume(sa_ref, c0, carry)
        logits(jnp.minimum(c0 + 2, top), sa_ref)
        return consume(sb_ref, c0 + 1, carry)

    carry = _loop_k_per_trip(0, n // 2, pair_body,
                             (jnp.full((1, nh * qb), NEG_INF, F32), jnp.zeros((1, nh * qb), F32)), k=4)
    _, l_fin = lax.fori_loop(0, n % 2, lambda _, c: consume(sa_ref, top, c), carry)

    inv_l = 1.0 / l_fin
    for h in range(nh):
        cols = slice(h * qb, (h + 1) * qb)
        o_ref[:, h * LANES:(h + 1) * LANES] = (acc_ref[:, cols] * inv_l[:, cols]).T.astype(BF16)


def _swa_kernel(sink_ref, q_ref, kp_ref, kc_ref, vp_ref, vc_ref, o_ref, *, subheads, max_dist):
    n = pl.program_id(1)
    qi = lax.broadcasted_iota(I32, (BLOCK, 2 * BLOCK), 0)
    kj = lax.broadcasted_iota(I32, (BLOCK, 2 * BLOCK), 1)
    dist = BLOCK + qi - kj
    mask = (dist >= 0) & (dist <= max_dist) & ((kj >= BLOCK) | (n > 0))
    for j, subs in enumerate(subheads):
        qt = q_ref[:, j * LANES:(j + 1) * LANES]
        o_t = None
        for kt, vt, hidx in subs:
            kk = jnp.concatenate([kp_ref[:, kt * LANES:(kt + 1) * LANES],
                                  kc_ref[:, kt * LANES:(kt + 1) * LANES]], axis=0)
            vv = jnp.concatenate([vp_ref[:, vt * LANES:(vt + 1) * LANES],
                                  vc_ref[:, vt * LANES:(vt + 1) * LANES]], axis=0)
            s = lax.dot_general(qt, kk, _NT, preferred_element_type=F32)
            s = jnp.where(mask, s, NEG_INF)
            sk = sink_ref[hidx] * LOG2E
            m = jnp.maximum(jnp.max(s, axis=-1, keepdims=True), sk)
            p = jnp.exp2(s - m)
            l = jnp.sum(p, axis=-1, keepdims=True) + jnp.exp2(sk - m)
            o_s = jnp.dot(p.astype(BF16), vv, preferred_element_type=F32) / l
            o_t = o_s if o_t is None else o_t + o_s
        o_ref[:, j * LANES:(j + 1) * LANES] = o_t.astype(BF16)


def _swa(q, k, v, sinks, *, subheads, max_dist):
    B, S, wq = q.shape
    wk, wv = k.shape[-1], v.shape[-1]
    cur = lambda width: pl.BlockSpec((None, BLOCK, width), lambda b, i: (b, i, 0))
    prv = lambda width: pl.BlockSpec((None, BLOCK, width), lambda b, i: (b, jnp.maximum(i - 1, 0), 0))
    return pl.pallas_call(
        functools.partial(_swa_kernel, subheads=subheads, max_dist=max_dist),
        grid=(B, S // BLOCK),
        in_specs=[pl.BlockSpec(memory_space=pltpu.SMEM), cur(wq), prv(wk), cur(wk), prv(wv), cur(wv)],
        out_specs=cur(wq),
        out_shape=jax.ShapeDtypeStruct((B, S, wq), BF16),
        compiler_params=_cparams(("parallel", "arbitrary")),
        name="swa",
    )(sinks, q, k, k, v, v)


def _wattn(q, k, vt, branches):
    B, S, wq = q.shape
    widest = max(w for w, _ in branches)
    nrel = min(-(-widest // KEY_CHUNK) + 1, S // KEY_CHUNK)
    bias = _window_bias(branches, nrel)
    nh = wq // LANES
    blk = lambda width: pl.BlockSpec((None, KEY_CHUNK, width), lambda b, i: (b, i, 0))
    return pl.pallas_call(
        functools.partial(_wattn_kernel, nh=nh, nrel=nrel),
        grid=(B, S // KEY_CHUNK),
        in_specs=[blk(wq),
                  pl.BlockSpec((None, S, wq), lambda b, i: (b, 0, 0)),
                  pl.BlockSpec((None,) + vt.shape[1:], lambda b, i: (b, 0, 0, 0)),
                  pl.BlockSpec(bias.shape, lambda b, i: (0, 0, 0))],
        out_specs=blk(wq),
        out_shape=jax.ShapeDtypeStruct((B, S, wq), BF16),
        scratch_shapes=[pltpu.VMEM((LANES, nh * KEY_CHUNK), F32),
                        pltpu.VMEM((KEY_CHUNK + 8, nh * KEY_CHUNK), F32),
                        pltpu.VMEM((KEY_CHUNK + 8, nh * KEY_CHUNK), F32),
                        pltpu.VMEM((KEY_CHUNK, nh * KEY_CHUNK), BF16)],
        compiler_params=_cparams(("parallel", "arbitrary")),
        name="wattn",
    )(q, k, vt, bias)


def _memkv_kernel(mem_ref, g_ref, w_ref, k_ref, v_ref):
    hb = _rms(mem_ref[...], g_ref[...]).astype(BF16)
    y = jnp.dot(hb, w_ref[...], preferred_element_type=F32)
    lane = lax.broadcasted_iota(I32, (y.shape[0], LANES), 1)
    for t in range(MEM_WIDTH // LANES):
        for out_ref, base in ((k_ref, 0), (v_ref, MEM_WIDTH)):
            tile = y[:, base + t * LANES:base + (t + 1) * LANES]
            out_ref[:, (2 * t) * LANES:(2 * t + 1) * LANES] = jnp.where(lane < MEM_HEAD_DIM, tile, 0.0).astype(BF16)
            out_ref[:, (2 * t + 1) * LANES:(2 * t + 2) * LANES] = jnp.where(lane >= MEM_HEAD_DIM, tile, 0.0).astype(BF16)


def _memkv(mem, g_mem, w):
    B, M, D = mem.shape
    L = w.shape[0]
    out = jax.ShapeDtypeStruct((L, B, M, 2 * MEM_WIDTH), BF16)
    ospec = pl.BlockSpec((None, None, M, 2 * MEM_WIDTH), lambda l, b: (l, b, 0, 0))
    return pl.pallas_call(
        _memkv_kernel,
        grid=(L, B),
        in_specs=[pl.BlockSpec((None, M, D), lambda l, b: (b, 0, 0)),
                  pl.BlockSpec((1, D), lambda l, b: (0, 0)),
                  pl.BlockSpec((None, D, 2 * MEM_WIDTH), lambda l, b: (l, 0, 0))],
        out_specs=(ospec, ospec),
        out_shape=(out, out),
        compiler_params=_cparams(("parallel", "parallel")),
        name="memkv",
    )(mem, g_mem, w)


def _mixer_residual(x_ref, mix_ref, qm_ref, mk_ref, mv_ref, w_ref):
    mw = mix_ref.shape[-1]
    n_sub = 2 * MEM_WIDTH // LANES
    logits = [lax.dot_general(qm_ref[:, (u // 2) * LANES:(u // 2 + 1) * LANES], mk_ref[:, u * LANES:(u + 1) * LANES],
                              _NT, preferred_element_type=F32) for u in range(n_sub)]
    y = x_ref[...] + jnp.dot(mix_ref[...], w_ref[0:mw, :], preferred_element_type=F32)
    tiles = []
    for t in range(n_sub // 2):
        o_t = None
        for u in (2 * t, 2 * t + 1):
            p = jnp.exp2(logits[u] - jnp.max(logits[u], axis=-1, keepdims=True))
            l = jnp.sum(p, axis=-1, keepdims=True)
            o_s = jnp.dot(p.astype(BF16), mv_ref[:, u * LANES:(u + 1) * LANES], preferred_element_type=F32) / l
            o_t = o_s if o_t is None else o_t + o_s
        tiles.append(o_t.astype(BF16))
    return y + jnp.dot(jnp.concatenate(tiles, axis=1), w_ref[mw:mw + MEM_WIDTH, :], preferred_element_type=F32)


def _out_kernel(x_ref, mix_ref, qm_ref, mk_ref, mv_ref, w_ref, o_ref):
    o_ref[...] = _mixer_residual(x_ref, mix_ref, qm_ref, mk_ref, mv_ref, w_ref)


def _out_proj(x, mix, qm, mk, mv, w, tm):
    B, S, D = x.shape
    row = lambda width: pl.BlockSpec((None, tm, width), lambda b, i: (b, i, 0))
    mem = pl.BlockSpec((None, mk.shape[1], 2 * MEM_WIDTH), lambda b, i: (b, 0, 0))
    return pl.pallas_call(
        _out_kernel,
        grid=(B, S // tm),
        in_specs=[row(D), row(mix.shape[-1]), row(MEM_WIDTH), mem, mem,
                  pl.BlockSpec(w.shape, lambda b, i: (0, 0))],
        out_specs=row(D),
        out_shape=jax.ShapeDtypeStruct((B, S, D), F32),
        compiler_params=_cparams(("parallel", "parallel")),
        name="out_proj",
    )(x, mix, qm, mk, mv, w)


def _ffn_kernel(x_ref, g_ref, wup_ref, cw_ref, cb_ref, wdn_ref, gf_ref, o_ref, carry_ref, *, tm, cw, final):
    dff = wdn_ref.shape[0]
    hm = min(tm, 2 * LANES)
    first = pl.program_id(1) == 0
    row = lax.broadcasted_iota(I32, (hm, cw), 0)
    xs = [x_ref[r:r + hm, :] for r in range(0, tm, hm)]
    hbs = [_rms(x, g_ref[...]).astype(BF16) for x in xs]
    accs = list(xs)

    def up(hb, c0):
        return (jnp.dot(hb, wup_ref[:, c0:c0 + cw], preferred_element_type=F32),
                jnp.dot(hb, wup_ref[:, dff + c0:dff + c0 + cw], preferred_element_type=F32))

    def gate(a, b, prev, c0):
        p1, p2 = prev[7:8, :], prev[6:7, :]
        a1 = jnp.where(row == 0, p1, pltpu.roll(a, 1, 0))
        a2 = jnp.where(row == 0, p2, jnp.where(row == 1, p1, pltpu.roll(a, 2, 0)))
        w = cw_ref[:, c0:c0 + cw]
        conv = w[0:1, :] * a2 + w[1:2, :] * a1 + w[2:3, :] * a + cb_ref[:, c0:c0 + cw]
        return (conv / (1.0 + jnp.exp(-conv)) * b).astype(BF16)

    nxt = [up(hb, 0) for hb in hbs]
    for c0 in range(0, dff, cw):
        cur = nxt
        if c0 + cw < dff:
            nxt = [up(hb, c0 + cw) for hb in hbs]
        prev = jnp.where(first, 0.0, carry_ref[:, c0:c0 + cw])
        for k, (a, b) in enumerate(cur):
            accs[k] = accs[k] + jnp.dot(gate(a, b, prev, c0), wdn_ref[c0:c0 + cw, :], preferred_element_type=F32)
            prev = a[hm - 8:hm, :]
        carry_ref[:, c0:c0 + cw] = prev
    for k, acc in enumerate(accs):
        o_ref[k * hm:(k + 1) * hm, :] = _rms(acc, gf_ref[...]) if final else acc


def _ffn(x, g, wup, cw, cb, wdn, gf, tm, final):
    B, S, D = x.shape
    dff = wdn.shape[0]
    row = pl.BlockSpec((None, tm, D), lambda b, i: (b, i, 0))
    const = lambda shape: pl.BlockSpec(shape, lambda b, i: (0, 0), pipeline_mode=pl.Buffered(1))
    return pl.pallas_call(
        functools.partial(_ffn_kernel, tm=tm, cw=2 * LANES, final=final),
        grid=(B, S // tm),
        in_specs=[row, const((1, D)), const(wup.shape), const(cw.shape), const((1, dff)), const(wdn.shape),
                  const((1, D))],
        out_specs=row,
        out_shape=jax.ShapeDtypeStruct((B, S, D), F32),
        scratch_shapes=[pltpu.VMEM((8, dff), F32)],
        compiler_params=_cparams(("arbitrary", "arbitrary")),
        name="ffn",
    )(x, g, wup, cw, cb, wdn, gf)


def _b_subheads():
    return tuple(((2 * (j // 2), 2 * (j // 2), 2 * j), (2 * (j // 2) + 1, 2 * (j // 2) + 1, 2 * j + 1))
                 for j in range(B_HEADS // 2))


def kernel(x, mem, positions, g_mix, g_ffn, g_mem, g_final, w_mem_kv, a_w_in, a_kv_norm, a_w_uk, a_w_uv, a_w_out,
           b_w_in, b_sinks, b_w_out, c_w_in, c_w_out, f_w_up, f_conv_w, f_conv_b, f_w_down):
    B, S, D = x.shape
    depth = g_mix.shape[0]
    tm = min(512, S)
    tab32 = _rope_table(positions, A_QK_DIM, A_ROPE_DIM)
    tab16 = _rope_table(positions, B_HEAD_DIM, B_HEAD_DIM // 4)
    mem_k, mem_v = _memkv(mem, g_mem.reshape(1, D), w_mem_kv.astype(BF16))
    conv_w = jnp.pad(f_conv_w, ((0, 0), (0, 8 - CONV_WIDTH), (0, 0)))
    for i in range(depth):
        kind, j = i % 3, i // 3
        g = g_mix[i].reshape(1, D)
        if kind == 0:
            wuk = jnp.pad(a_w_uk[j], ((0, 0), (0, 0), (A_ROPE_DIM, 0))).reshape(A_KV_RANK, -1).astype(BF16)
            wuv = jnp.transpose(a_w_uv[j], (1, 0, 2)).astype(BF16)
            q, kf, ct, qi, ki, wi, qm = _proj_a(x, g, _prep_a_w_in(a_w_in[j]), a_kv_norm[j].reshape(1, -1), wuk,
                                                tab32, tab16, tm)
            mix = _dsa(q, qi, wi, kf, ct, ki, wuv)
            w_out = a_w_out[j]
        elif kind == 1:
            nq, nkv = B_HEADS * B_HEAD_DIM, 4 * B_KV_HEADS * B_HEAD_DIM
            q, k, v, qm = _proj_qkv(x, g, _prep_b_w_in(b_w_in[j]), tab16, tm, B_HEAD_DIM, B_HEAD_DIM // 4,
                                    nq, nkv, nkv, False)
            mix = _swa(q, k, v, b_sinks[j], subheads=_b_subheads(), max_dist=B_WINDOW - 1)
            w_out = b_w_out[j]
        else:
            nq = C_HEADS * C_HEAD_DIM
            q, k, v, qm = _proj_qkv(x, g, c_w_in[j].astype(BF16), tab32, tm, C_HEAD_DIM, C_HEAD_DIM // 4,
                                    nq, nq, nq, True)
            mix = _wattn(q, k, v, C_BRANCHES)
            w_out = c_w_out[j]
        x = _out_proj(x, mix, qm, mem_k[i], mem_v[i], w_out.astype(BF16), tm)
        x = _ffn(x, g_ffn[i].reshape(1, D), f_w_up[i].astype(BF16), conv_w[i], f_conv_b[i].reshape(1, -1),
                 f_w_down[i].astype(BF16), g_final.reshape(1, D), tm, i == depth - 1)
    return x
```

```python
import functools

import jax
import jax.numpy as jnp
import numpy as np
from jax import lax
from jax.experimental import pallas as pl
from jax.experimental.pallas import tpu as pltpu

F32 = jnp.float32
BF16 = jnp.bfloat16
I32 = jnp.int32

LANES = 128
BLOCK = 128
ROPE_THETA = 500000.0
EPS = 1e-6
NEG_INF = -1e30
LOG2E = 1.4426950408889634
INT_MIN = -(2**31)

A_HEADS = 8
A_QK_DIM = 128
A_ROPE_DIM = 32
A_KV_RANK = 256
IDX_HEADS = 16
IDX_DIM = 64
TOPK_MAX = 256
B_HEADS = 16
B_KV_HEADS = 4
B_HEAD_DIM = 64
B_WINDOW = 128
C_HEADS = 8
C_HEAD_DIM = 128
C_BRANCHES = ((128, 1), (512, 4), (2048, 16))
MEM_HEADS = 4
MEM_HEAD_DIM = 64
MEM_WIDTH = MEM_HEADS * MEM_HEAD_DIM
CONV_WIDTH = 3
KEY_CHUNK = 256

VMEM_LIMIT = 56 * 1024 * 1024

_NT = (((1,), (1,)), ((), ()))


def _cparams(sem):
    return pltpu.CompilerParams(dimension_semantics=sem, vmem_limit_bytes=VMEM_LIMIT)


def _rms(x, g):
    return x * lax.rsqrt(jnp.mean(x * x, axis=-1, keepdims=True) + EPS) * g


def _loop_k_per_trip(lo, hi, body, carry, k=2):
    if k == 1:
        return lax.fori_loop(lo, hi, body, carry)
    trips = (hi - lo) // k

    def several(t, c):
        for u in range(k):
            c = body(lo + k * t + u, c)
        return c

    carry = lax.fori_loop(0, trips, several, carry)
    return _loop_k_per_trip(lo + k * trips, hi, body, carry, k // 2)


def _bit_transpose32(words):
    a = list(words)
    j, m = 16, 0x0000FFFF
    while j:
        mask = jnp.int32(m - (1 << 32) if m >= (1 << 31) else m)
        for k in range(32):
            if k & j == 0:
                t = (a[k] ^ lax.shift_right_logical(a[k + j], j)) & mask
                a[k] = a[k] ^ t
                a[k + j] = a[k + j] ^ lax.shift_left(t, j)
        j >>= 1
        m = (m ^ (m << j)) & 0xFFFFFFFF
    return a


def _rope_tile(t, tab, half):
    c, sa, sb = tab[:, 0:LANES], tab[:, LANES:2 * LANES], tab[:, 2 * LANES:3 * LANES]
    return t * c + pltpu.roll(t, half, 1) * sa + pltpu.roll(t, LANES - half, 1) * sb


def _rope_table(positions, head_dim, rot):
    half = rot // 2
    inv = ROPE_THETA ** (-jnp.arange(0, rot, 2, dtype=F32) / rot)
    ang = positions.astype(F32)[..., None] * inv
    cs = jnp.concatenate([jnp.cos(ang), jnp.sin(ang)], axis=-1)
    sel = np.zeros((rot, 3 * LANES), np.float32)
    one = np.zeros((3 * LANES,), np.float32)
    for l in range(LANES):
        j = l % head_dim
        if j < half:
            sel[j, l] = 1.0
            sel[half + j, 2 * LANES + l] = -1.0
        elif j < rot:
            sel[j - half, l] = 1.0
            sel[j, LANES + l] = 1.0
        else:
            one[l] = 1.0
    return jnp.dot(cs, jnp.asarray(sel), precision=lax.Precision.HIGHEST) + jnp.asarray(one)


def _staggered(hbs, w_ref, jobs):
    def mms(lo, hi):
        return [jnp.dot(hb, w_ref[:, lo:hi], preferred_element_type=F32) for hb in hbs]

    nxt = mms(jobs[0][0], jobs[0][1])
    for n, (_, _, epilogue) in enumerate(jobs):
        cur = nxt
        if n + 1 < len(jobs):
            nxt = mms(jobs[n + 1][0], jobs[n + 1][1])
        for part, y in enumerate(cur):
            epilogue(y, part)


def _proj_a_kernel(x_ref, g_ref, w_ref, kvn_ref, wuk_ref, t32_ref, t16_ref,
                   q_ref, kf_ref, ct_ref, qi_ref, ki_ref, wi_ref, qm_ref, *, tm):
    hm = KEY_CHUNK
    rows = [slice(r, r + hm) for r in range(0, tm, hm)]
    hbs = [_rms(x_ref[r, :], g_ref[...]).astype(BF16) for r in rows]

    def roped(ref, col, tab_ref, half, scale):
        def epilogue(y, part):
            tab = tab_ref[rows[part], :]
            for u in range(y.shape[1] // LANES):
                t = _rope_tile(y[:, u * LANES:(u + 1) * LANES], tab, half)
                if scale != 1.0:
                    t = t * scale
                ref[rows[part], col + u * LANES:col + (u + 1) * LANES] = t.astype(BF16)
        return epilogue

    latent_bf16 = {}

    def latent(y, part):
        c = _rms(y, kvn_ref[...])
        latent_bf16[part] = c.astype(BF16)
        ct_ref[part] = c.T.astype(BF16)

    def full_keys(y, part):
        k_rope = _rope_tile(y, t32_ref[rows[part], :], A_ROPE_DIM // 2)
        k_nope = jnp.dot(latent_bf16[part], wuk_ref[...], preferred_element_type=F32)
        for h in range(A_HEADS):
            kf_ref[rows[part], h * LANES:(h + 1) * LANES] = (
                k_nope[:, h * LANES:(h + 1) * LANES] + k_rope).astype(BF16)

    def index_keys(y, part):
        tab = t16_ref[rows[part], :]
        for u in range(2):
            ki_ref[part, u * hm:(u + 1) * hm, :] = _rope_tile(
                y[:, u * LANES:(u + 1) * LANES], tab, IDX_DIM // 8).astype(BF16)

    def index_weights(y, part):
        wi_ref[rows[part], :] = y * (IDX_HEADS * IDX_DIM) ** -0.5

    def mem_query(y, part):
        qm_ref[rows[part], :] = (y * (MEM_HEAD_DIM ** -0.5 * LOG2E)).astype(BF16)

    jobs, o = [], 0
    for j in range(0, A_HEADS * A_QK_DIM, 2 * LANES):
        jobs.append((o + j, o + j + 2 * LANES,
                     roped(q_ref, j, t32_ref, A_ROPE_DIM // 2, A_QK_DIM ** -0.5 * LOG2E)))
    o += A_HEADS * A_QK_DIM
    jobs.append((o, o + A_KV_RANK, latent))
    o += A_KV_RANK
    jobs.append((o, o + LANES, full_keys))
    o += LANES
    for j in range(0, IDX_HEADS * IDX_DIM, 2 * LANES):
        jobs.append((o + j, o + j + 2 * LANES, roped(qi_ref, j, t16_ref, IDX_DIM // 8, 1.0)))
    o += IDX_HEADS * IDX_DIM
    jobs.append((o, o + 2 * LANES, index_keys))
    o += 2 * LANES
    jobs.append((o, o + LANES, index_weights))
    o += LANES
    jobs.append((o, o + MEM_WIDTH, mem_query))
    _staggered(hbs, w_ref, jobs)


def _prep_a_w_in(w):
    d = w.shape[0]
    sizes = (A_HEADS * A_QK_DIM, A_KV_RANK, A_ROPE_DIM, IDX_HEADS * IDX_DIM, IDX_DIM, IDX_HEADS, MEM_WIDTH)
    offs = [0]
    for s in sizes:
        offs.append(offs[-1] + s)
    q, ckv, kr, qi, ki, wi, qm = [w[:, offs[i]:offs[i + 1]] for i in range(len(sizes))]
    z = lambda n: jnp.zeros((d, n), w.dtype)
    return jnp.concatenate([
        q, ckv, kr, z(LANES - A_ROPE_DIM), qi,
        ki, z(LANES - IDX_DIM), z(LANES - IDX_DIM), ki,
        wi, z(LANES - IDX_HEADS), qm], axis=1).astype(BF16)


def _proj_a(x, g, w, kvn, wuk, t32, t16, tm):
    B, S, D = x.shape
    n = w.shape[1]
    row = lambda width: pl.BlockSpec((None, tm, width), lambda b, i: (b, i, 0))
    const = lambda shape: pl.BlockSpec(shape, lambda b, i: (0,) * len(shape))
    out_shape = (
        jax.ShapeDtypeStruct((B, S, A_HEADS * A_QK_DIM), BF16),
        jax.ShapeDtypeStruct((B, S, A_HEADS * A_QK_DIM), BF16),
        jax.ShapeDtypeStruct((B, S // KEY_CHUNK, A_KV_RANK, KEY_CHUNK), BF16),
        jax.ShapeDtypeStruct((B, S, IDX_HEADS * IDX_DIM), BF16),
        jax.ShapeDtypeStruct((B, S // KEY_CHUNK, 2 * KEY_CHUNK, LANES), BF16),
        jax.ShapeDtypeStruct((B, S, LANES), F32),
        jax.ShapeDtypeStruct((B, S, MEM_WIDTH), BF16),
    )
    out_specs = (
        row(A_HEADS * A_QK_DIM), row(A_HEADS * A_QK_DIM),
        pl.BlockSpec((None, tm // KEY_CHUNK, A_KV_RANK, KEY_CHUNK), lambda b, i: (b, i, 0, 0)),
        row(IDX_HEADS * IDX_DIM),
        pl.BlockSpec((None, tm // KEY_CHUNK, 2 * KEY_CHUNK, LANES), lambda b, i: (b, i, 0, 0)),
        row(LANES), row(MEM_WIDTH),
    )
    return pl.pallas_call(
        functools.partial(_proj_a_kernel, tm=tm),
        grid=(B, S // tm),
        in_specs=[row(D), const((1, D)), const((D, n)), const((1, A_KV_RANK)), const(wuk.shape),
                  row(3 * LANES), row(3 * LANES)],
        out_specs=out_specs,
        out_shape=out_shape,
        compiler_params=_cparams(("parallel", "parallel")),
        name="proj_a",
    )(x, g, w, kvn, wuk, t32, t16)


def _proj_qkv_kernel(x_ref, g_ref, w_ref, tab_ref, q_ref, k_ref, v_ref, qm_ref, *,
                     head_dim, rot, nq, nk, nv, v_transposed):
    tm = x_ref.shape[0]
    hm = KEY_CHUNK
    rows = [slice(r, r + hm) for r in range(0, tm, hm)]
    hbs = [_rms(x_ref[r, :], g_ref[...]).astype(BF16) for r in rows]

    def roped(ref, col, scale):
        def epilogue(y, part):
            tab = tab_ref[rows[part], :]
            for u in range(2):
                t = _rope_tile(y[:, u * LANES:(u + 1) * LANES], tab, rot // 2)
                if scale != 1.0:
                    t = t * scale
                ref[rows[part], col + u * LANES:col + (u + 1) * LANES] = t.astype(BF16)
        return epilogue

    def value(col):
        def epilogue(y, part):
            if v_transposed:
                v_ref[part, col:col + 2 * LANES, :] = jnp.swapaxes(y[None], 1, 2)[0].astype(BF16)
            else:
                v_ref[rows[part], col:col + 2 * LANES] = y.astype(BF16)
        return epilogue

    def mem_query(y, part):
        qm_ref[rows[part], :] = (y * (MEM_HEAD_DIM ** -0.5 * LOG2E)).astype(BF16)

    jobs = [(j, j + 2 * LANES, roped(q_ref, j, head_dim ** -0.5 * LOG2E)) for j in range(0, nq, 2 * LANES)]
    jobs += [(nq + j, nq + j + 2 * LANES, roped(k_ref, j, 1.0)) for j in range(0, nk, 2 * LANES)]
    jobs += [(nq + nk + j, nq + nk + j + 2 * LANES, value(j)) for j in range(0, nv, 2 * LANES)]
    jobs.append((nq + nk + nv, nq + nk + nv + MEM_WIDTH, mem_query))
    _staggered(hbs, w_ref, jobs)


def _proj_qkv(x, g, w, tab, tm, head_dim, rot, nq, nk, nv, v_transposed):
    B, S, D = x.shape
    n = w.shape[1]
    row = lambda width: pl.BlockSpec((None, tm, width), lambda b, i: (b, i, 0))
    const = lambda shape: pl.BlockSpec(shape, lambda b, i: (0,) * len(shape))
    if v_transposed:
        v_spec = pl.BlockSpec((None, tm // KEY_CHUNK, nv, KEY_CHUNK), lambda b, i: (b, i, 0, 0))
        v_shape = jax.ShapeDtypeStruct((B, S // KEY_CHUNK, nv, KEY_CHUNK), BF16)
    else:
        v_spec, v_shape = row(nv), jax.ShapeDtypeStruct((B, S, nv), BF16)
    return pl.pallas_call(
        functools.partial(_proj_qkv_kernel, head_dim=head_dim, rot=rot, nq=nq, nk=nk, nv=nv,
                          v_transposed=v_transposed),
        grid=(B, S // tm),
        in_specs=[row(D), const((1, D)), const((D, n)), row(3 * LANES)],
        out_specs=(row(nq), row(nk), v_spec, row(MEM_WIDTH)),
        out_shape=(jax.ShapeDtypeStruct((B, S, nq), BF16), jax.ShapeDtypeStruct((B, S, nk), BF16), v_shape,
                   jax.ShapeDtypeStruct((B, S, MEM_WIDTH), BF16)),
        compiler_params=_cparams(("parallel", "parallel")),
        name="proj_qkv",
    )(x, g, w, tab)


def _prep_b_w_in(w):
    d = w.shape[0]
    nq, nkv = B_HEADS * B_HEAD_DIM, B_KV_HEADS * B_HEAD_DIM
    q, k, v, qm = w[:, :nq], w[:, nq:nq + nkv], w[:, nq + nkv:nq + 2 * nkv], w[:, nq + 2 * nkv:]
    z = jnp.zeros((d, B_HEAD_DIM), w.dtype)

    def spread(t):
        cols = []
        for h in range(B_KV_HEADS):
            th = t[:, h * B_HEAD_DIM:(h + 1) * B_HEAD_DIM]
            cols += [th, z, z, th]
        return jnp.concatenate(cols, axis=1)

    return jnp.concatenate([q, spread(k), spread(v), qm], axis=1).astype(BF16)


DSA_BLOCK = 256


def _dsa_kernel(q_ref, qi_ref, wi_ref, kf_ref, ct_ref, ki_ref, wuv_ref, o_ref,
                keys_ref, planes_ref, qbd_ref, qi2_ref, wrow_ref, acc_ref, sa_ref, sb_ref, *, topk):
    kc = KEY_CHUNK
    qb = DSA_BLOCK
    hw = A_HEADS * qb
    gw = 2 * qb
    i = pl.program_id(1)
    nch = (i * qb + qb + kc - 1) // kc

    qbd_ref[...] = jnp.zeros(qbd_ref.shape, BF16)
    for h in range(A_HEADS):
        qbd_ref[h // 2, (h % 2) * qb:(h % 2 + 1) * qb, (h % 2) * LANES:(h % 2 + 1) * LANES] = (
            q_ref[:, h * LANES:(h + 1) * LANES])
        qi2_ref[h * qb:(h + 1) * qb, :] = qi_ref[:, h * LANES:(h + 1) * LANES]
    w_t = wi_ref[...].T
    for j in range(IDX_HEADS // 2):
        wrow_ref[0:1, j * qb:(j + 1) * qb] = w_t[2 * j:2 * j + 1, :]
        wrow_ref[1:2, j * qb:(j + 1) * qb] = w_t[2 * j + 1:2 * j + 2, :]

    qpos = i * qb + lax.broadcasted_iota(I32, (kc, qb), 1)
    krow = lax.broadcasted_iota(I32, (kc, qb), 0)

    def score_chunk(c):
        off = pl.multiple_of(c * kc, kc)
        kk = ki_ref[c]
        acc = None
        for g in range(0, hw, qb):
            d = lax.dot_general(kk, qi2_ref[g:g + qb, :], _NT, preferred_element_type=F32)
            t = (jnp.maximum(d[0:kc], 0.0) * wrow_ref[0:1, g:g + qb]
                 + jnp.maximum(d[kc:2 * kc], 0.0) * wrow_ref[1:2, g:g + qb])
            acc = t if acc is None else acc + t
        bits = pltpu.bitcast(acc, I32)
        key = jnp.where(bits < 0, bits ^ 0x7FFFFFFF, bits)
        key = jnp.where(off + krow <= qpos, key, INT_MIN)
        keys_ref[pl.ds(off, kc), :] = key
        planes = _bit_transpose32([(key[8 * r:8 * r + 8, :] ^ INT_MIN) for r in range(kc // 8)])
        for p in range(32):
            planes_ref[c, p] = planes[p]

    @pl.when(i == 0)
    def _():
        planes_ref[...] = jnp.zeros(planes_ref.shape, I32)

    def score_body(c, carry):
        score_chunk(c)
        return carry

    _loop_k_per_trip(0, nch, score_body, 0, k=4)

    n_chunks = planes_ref.shape[0]

    def bit_body(p, carry):
        alive, need, res = carry
        ones = [a & planes_ref[c, p] for c, a in enumerate(alive)]
        cnt = ones[0] * 0
        for o in ones:
            cnt = cnt + lax.population_count(o)
        cnt = jnp.sum(cnt, axis=0, keepdims=True)
        take = cnt >= need
        alive = [jnp.where(take, o, a ^ o) for o, a in zip(ones, alive)]
        need = jnp.where(take, need, need - cnt)
        res = jnp.where(take, res | lax.shift_left(jnp.int32(1), 31 - p), res)
        return alive, need, res

    for g in range(0, hw, gw):
        sa_ref[0:kc, g:g + gw] = lax.dot_general(kf_ref[0:kc, (g // gw) * 2 * LANES:(g // gw + 1) * 2 * LANES],
                                                 qbd_ref[g // gw], _NT, preferred_element_type=F32)

    carry = ([jnp.where(c < nch, jnp.full((8, qb), -1, I32), 0) for c in range(n_chunks)],
             jnp.full((1, qb), topk, I32), jnp.zeros((1, qb), I32))
    for p in range(32):
        carry = bit_body(p, carry)
    thr = jnp.maximum(carry[2] ^ INT_MIN, INT_MIN + 1)

    acc_ref[...] = jnp.zeros(acc_ref.shape, F32)
    bias0 = jnp.where(keys_ref[0:kc, :] >= thr, 0.0, NEG_INF)
    bias0 = jnp.concatenate([bias0, bias0], axis=1)
    for g in range(0, hw, gw):
        t = sa_ref[0:kc, g:g + gw] + bias0
        sa_ref[0:kc, g:g + gw] = t
        sa_ref[kc:kc + 1, g:g + gw] = jnp.max(t, axis=0, keepdims=True)

    def logits(c, s_ref):
        off = pl.multiple_of(c * kc, kc)
        bias = jnp.where(keys_ref[pl.ds(off, kc), :] >= thr, 0.0, NEG_INF)
        bias = jnp.concatenate([bias, bias], axis=1)
        for g in range(0, hw, gw):
            t = lax.dot_general(kf_ref[pl.ds(off, kc), (g // gw) * 2 * LANES:(g // gw + 1) * 2 * LANES],
                                qbd_ref[g // gw], _NT, preferred_element_type=F32) + bias
            s_ref[0:kc, g:g + gw] = t
            s_ref[kc:kc + 1, g:g + gw] = jnp.max(t, axis=0, keepdims=True)

    def consume(s_ref, c, carry):
        m_prev, l_prev = carry
        ct = ct_ref[c]
        m_out, l_out = [], []
        for g in range(0, hw, gw):
            s = s_ref[0:kc, g:g + gw]
            m_new = jnp.maximum(m_prev[:, g:g + gw], s_ref[kc:kc + 1, g:g + gw])
            alpha = jnp.exp2(m_prev[:, g:g + gw] - m_new)
            p = jnp.exp2(s - m_new)
            l_out.append(alpha * l_prev[:, g:g + gw] + jnp.sum(p, axis=0, keepdims=True))
            m_out.append(m_new)
            acc_ref[:, g:g + gw] = alpha * acc_ref[:, g:g + gw] + jnp.dot(
                ct, p.astype(BF16), preferred_element_type=F32)
        return jnp.concatenate(m_out, axis=1), jnp.concatenate(l_out, axis=1)

    def pair_body(j, carry):
        c0 = 2 * j
        logits(c0 + 1, sb_ref)
        carry = consume(sa_ref, c0, carry)
        logits(jnp.minimum(c0 + 2, nch - 1), sa_ref)
        return consume(sb_ref, c0 + 1, carry)

    carry = _loop_k_per_trip(0, nch // 2, pair_body,
                             (jnp.full((1, hw), NEG_INF, F32), jnp.zeros((1, hw), F32)), k=2)
    _, l_fin = lax.fori_loop(0, nch % 2, lambda _, c: consume(sa_ref, nch - 1, c), carry)

    inv_l = 1.0 / l_fin
    for h in range(A_HEADS):
        cols = slice(h * qb, (h + 1) * qb)
        o_lat = (acc_ref[:, cols] * inv_l[:, cols]).T.astype(BF16)
        o_ref[:, h * LANES:(h + 1) * LANES] = jnp.dot(o_lat, wuv_ref[h], preferred_element_type=F32).astype(BF16)


def _dsa(q, qi, wi, kf, ct, ki, wuv):
    B, S, _ = q.shape
    topk = min(TOPK_MAX, S // 4)
    qb = DSA_BLOCK
    blk = lambda width: pl.BlockSpec((None, qb, width), lambda b, i: (b, i, 0))
    seq = lambda width: pl.BlockSpec((None, S, width), lambda b, i: (b, 0, 0))
    const3 = lambda shape: pl.BlockSpec(shape, lambda b, i: (0, 0, 0))
    return pl.pallas_call(
        functools.partial(_dsa_kernel, topk=topk),
        grid=(B, S // qb),
        in_specs=[blk(A_HEADS * A_QK_DIM), blk(IDX_HEADS * IDX_DIM), blk(LANES),
                  seq(A_HEADS * A_QK_DIM),
                  pl.BlockSpec((None, S // KEY_CHUNK, A_KV_RANK, KEY_CHUNK), lambda b, i: (b, 0, 0, 0)),
                  pl.BlockSpec((None, S // KEY_CHUNK, 2 * KEY_CHUNK, LANES), lambda b, i: (b, 0, 0, 0)),
                  const3(wuv.shape)],
        out_specs=blk(A_HEADS * LANES),
        out_shape=jax.ShapeDtypeStruct((B, S, A_HEADS * LANES), BF16),
        scratch_shapes=[
            pltpu.VMEM((S, qb), I32),
            pltpu.VMEM((S // KEY_CHUNK, 32, 8, qb), I32),
            pltpu.VMEM((A_HEADS // 2, 2 * qb, 2 * LANES), BF16),
            pltpu.VMEM((A_HEADS * qb, LANES), BF16),
            pltpu.VMEM((8, A_HEADS * qb), F32),
            pltpu.VMEM((A_KV_RANK, A_HEADS * qb), F32),
            pltpu.VMEM((KEY_CHUNK + 8, A_HEADS * qb), F32),
            pltpu.VMEM((KEY_CHUNK + 8, A_HEADS * qb), F32),
        ],
        compiler_params=_cparams(("parallel", "arbitrary")),
        name="dsa",
    )(q, qi, wi, kf, ct, ki, wuv)


def _window_bias(branches, nrel):
    rel = jnp.arange(nrel, dtype=I32)[:, None, None]
    krow = jnp.arange(KEY_CHUNK, dtype=I32)[None, :, None]
    qcol = jnp.arange(KEY_CHUNK, dtype=I32)[None, None, :]
    dist = KEY_CHUNK * (nrel - 1 - rel) + qcol - krow
    mult = sum(((dist >= 0) & (dist <= window) & (dist % dil == 0)).astype(F32) for window, dil in branches)
    return jnp.where(mult > 0, jnp.log2(jnp.maximum(mult, 1.0)), NEG_INF)


def _wattn_kernel(q_ref, k_ref, vt_ref, bias_ref, o_ref, acc_ref, sa_ref, sb_ref, p_ref, *, nh, nrel):
    kc = qb = KEY_CHUNK
    top = pl.program_id(1)
    c_lo = jnp.maximum(top - (nrel - 1), 0)
    n = top + 1 - c_lo

    acc_ref[...] = jnp.zeros(acc_ref.shape, F32)

    def logits(c, s_ref):
        off = pl.multiple_of(c * kc, kc)
        bias = bias_ref[c - top + (nrel - 1)]
        for h in range(nh):
            cols = slice(h * LANES, (h + 1) * LANES)
            t = lax.dot_general(k_ref[pl.ds(off, kc), cols], q_ref[:, cols], _NT,
                                preferred_element_type=F32) + bias
            s_ref[0:kc, h * qb:(h + 1) * qb] = t
            s_ref[kc:kc + 1, h * qb:(h + 1) * qb] = jnp.max(t, axis=0, keepdims=True)

    def consume(s_ref, c, carry):
        m_prev, l_prev = carry
        m_new = jnp.maximum(m_prev, s_ref[kc:kc + 1, :])
        alpha = jnp.exp2(m_prev - m_new)
        p_ref[...] = jnp.exp2(s_ref[0:kc, :] - m_new).astype(BF16)
        l_new = alpha * l_prev + jnp.dot(jnp.ones((16, kc), BF16), p_ref[...], preferred_element_type=F32)[0:1, :]
        for h in range(nh):
            cols = slice(h * qb, (h + 1) * qb)
            acc_ref[:, cols] = alpha[:, cols] * acc_ref[:, cols] + jnp.dot(
                vt_ref[c, h * LANES:(h + 1) * LANES, :], p_ref[:, cols], preferred_element_type=F32)
        return m_new, l_new

    logits(c_lo, sa_ref)

    def pair_body(j, carry):
        c0 = c_lo + 2 * j
        logits(c0 + 1, sb_ref)
        carry = consume(sa_ref, c0, carry)
        logits(jnp.minimum(c0 + 2, top), sa_ref)
        return consume(sb_ref, c0 + 1, carry)

    carry = _loop_k_per_trip(0, n // 2, pair_body,
                             (jnp.full((1, nh * qb), NEG_INF, F32), jnp.zeros((1, nh * qb), F32)), k=4)
    _, l_fin = lax.fori_loop(0, n % 2, lambda _, c: consume(sa_ref, top, c), carry)

    inv_l = 1.0 / l_fin
    for h in range(nh):
        cols = slice(h * qb, (h + 1) * qb)
        o_ref[:, h * LANES:(h + 1) * LANES] = (acc_ref[:, cols] * inv_l[:, cols]).T.astype(BF16)


def _swa_kernel(sink_ref, q_ref, kp_ref, kc_ref, vp_ref, vc_ref, o_ref, *, subheads, max_dist):
    n = pl.program_id(1)
    qi = lax.broadcasted_iota(I32, (BLOCK, 2 * BLOCK), 0)
    kj = lax.broadcasted_iota(I32, (BLOCK, 2 * BLOCK), 1)
    dist = BLOCK + qi - kj
    mask = (dist >= 0) & (dist <= max_dist) & ((kj >= BLOCK) | (n > 0))
    for j, subs in enumerate(subheads):
        qt = q_ref[:, j * LANES:(j + 1) * LANES]
        o_t = None
        for kt, vt, hidx in subs:
            kk = jnp.concatenate([kp_ref[:, kt * LANES:(kt + 1) * LANES],
                                  kc_ref[:, kt * LANES:(kt + 1) * LANES]], axis=0)
            vv = jnp.concatenate([vp_ref[:, vt * LANES:(vt + 1) * LANES],
                                  vc_ref[:, vt * LANES:(vt + 1) * LANES]], axis=0)
            s = lax.dot_general(qt, kk, _NT, preferred_element_type=F32)
            s = jnp.where(mask, s, NEG_INF)
            sk = sink_ref[hidx] * LOG2E
            m = jnp.maximum(jnp.max(s, axis=-1, keepdims=True), sk)
            p = jnp.exp2(s - m)
            l = jnp.sum(p, axis=-1, keepdims=True) + jnp.exp2(sk - m)
            o_s = jnp.dot(p.astype(BF16), vv, preferred_element_type=F32) / l
            o_t = o_s if o_t is None else o_t + o_s
        o_ref[:, j * LANES:(j + 1) * LANES] = o_t.astype(BF16)


def _swa(q, k, v, sinks, *, subheads, max_dist):
    B, S, wq = q.shape
    wk, wv = k.shape[-1], v.shape[-1]
    cur = lambda width: pl.BlockSpec((None, BLOCK, width), lambda b, i: (b, i, 0))
    prv = lambda width: pl.BlockSpec((None, BLOCK, width), lambda b, i: (b, jnp.maximum(i - 1, 0), 0))
    return pl.pallas_call(
        functools.partial(_swa_kernel, subheads=subheads, max_dist=max_dist),
        grid=(B, S // BLOCK),
        in_specs=[pl.BlockSpec(memory_space=pltpu.SMEM), cur(wq), prv(wk), cur(wk), prv(wv), cur(wv)],
        out_specs=cur(wq),
        out_shape=jax.ShapeDtypeStruct((B, S, wq), BF16),
        compiler_params=_cparams(("parallel", "arbitrary")),
        name="swa",
    )(sinks, q, k, k, v, v)


def _wattn(q, k, vt, branches):
    B, S, wq = q.shape
    widest = max(w for w, _ in branches)
    nrel = min(-(-widest // KEY_CHUNK) + 1, S // KEY_CHUNK)
    bias = _window_bias(branches, nrel)
    nh = wq // LANES
    blk = lambda width: pl.BlockSpec((None, KEY_CHUNK, width), lambda b, i: (b, i, 0))
    return pl.pallas_call(
        functools.partial(_wattn_kernel, nh=nh, nrel=nrel),
        grid=(B, S // KEY_CHUNK),
        in_specs=[blk(wq),
                  pl.BlockSpec((None, S, wq), lambda b, i: (b, 0, 0)),
                  pl.BlockSpec((None,) + vt.shape[1:], lambda b, i: (b, 0, 0, 0)),
                  pl.BlockSpec(bias.shape, lambda b, i: (0, 0, 0))],
        out_specs=blk(wq),
        out_shape=jax.ShapeDtypeStruct((B, S, wq), BF16),
        scratch_shapes=[pltpu.VMEM((LANES, nh * KEY_CHUNK), F32),
                        pltpu.VMEM((KEY_CHUNK + 8, nh * KEY_CHUNK), F32),
                        pltpu.VMEM((KEY_CHUNK + 8, nh * KEY_CHUNK), F32),
                        pltpu.VMEM((KEY_CHUNK, nh * KEY_CHUNK), BF16)],
        compiler_params=_cparams(("parallel", "arbitrary")),
        name="wattn",
    )(q, k, vt, bias)


def _memkv_kernel(mem_ref, g_ref, w_ref, k_ref, v_ref):
    hb = _rms(mem_ref[...], g_ref[...]).astype(BF16)
    y = jnp.dot(hb, w_ref[...], preferred_element_type=F32)
    lane = lax.broadcasted_iota(I32, (y.shape[0], LANES), 1)
    for t in range(MEM_WIDTH // LANES):
        for out_ref, base in ((k_ref, 0), (v_ref, MEM_WIDTH)):
            tile = y[:, base + t * LANES:base + (t + 1) * LANES]
            out_ref[:, (2 * t) * LANES:(2 * t + 1) * LANES] = jnp.where(lane < MEM_HEAD_DIM, tile, 0.0).astype(BF16)
            out_ref[:, (2 * t + 1) * LANES:(2 * t + 2) * LANES] = jnp.where(lane >= MEM_HEAD_DIM, tile, 0.0).astype(BF16)


def _memkv(mem, g_mem, w):
    B, M, D = mem.shape
    L = w.shape[0]
    out = jax.ShapeDtypeStruct((L, B, M, 2 * MEM_WIDTH), BF16)
    ospec = pl.BlockSpec((None, None, M, 2 * MEM_WIDTH), lambda l, b: (l, b, 0, 0))
    return pl.pallas_call(
        _memkv_kernel,
        grid=(L, B),
        in_specs=[pl.BlockSpec((None, M, D), lambda l, b: (b, 0, 0)),
                  pl.BlockSpec((1, D), lambda l, b: (0, 0)),
                  pl.BlockSpec((None, D, 2 * MEM_WIDTH), lambda l, b: (l, 0, 0))],
        out_specs=(ospec, ospec),
        out_shape=(out, out),
        compiler_params=_cparams(("parallel", "parallel")),
        name="memkv",
    )(mem, g_mem, w)


def _mixer_residual(x_ref, mix_ref, qm_ref, mk_ref, mv_ref, w_ref):
    mw = mix_ref.shape[-1]
    n_sub = 2 * MEM_WIDTH // LANES
    logits = [lax.dot_general(qm_ref[:, (u // 2) * LANES:(u // 2 + 1) * LANES], mk_ref[:, u * LANES:(u + 1) * LANES],
                              _NT, preferred_element_type=F32) for u in range(n_sub)]
    y = x_ref[...] + jnp.dot(mix_ref[...], w_ref[0:mw, :], preferred_element_type=F32)
    tiles = []
    for t in range(n_sub // 2):
        o_t = None
        for u in (2 * t, 2 * t + 1):
            p = jnp.exp2(logits[u] - jnp.max(logits[u], axis=-1, keepdims=True))
            l = jnp.sum(p, axis=-1, keepdims=True)
            o_s = jnp.dot(p.astype(BF16), mv_ref[:, u * LANES:(u + 1) * LANES], preferred_element_type=F32) / l
            o_t = o_s if o_t is None else o_t + o_s
        tiles.append(o_t.astype(BF16))
    return y + jnp.dot(jnp.concatenate(tiles, axis=1), w_ref[mw:mw + MEM_WIDTH, :], preferred_element_type=F32)


def _out_kernel(x_ref, mix_ref, qm_ref, mk_ref, mv_ref, w_ref, o_ref):
    o_ref[...] = _mixer_residual(x_ref, mix_ref, qm_ref, mk_ref, mv_ref, w_ref)


def _out_proj(x, mix, qm, mk, mv, w, tm):
    B, S, D = x.shape
    row = lambda width: pl.BlockSpec((None, tm, width), lambda b, i: (b, i, 0))
    mem = pl.BlockSpec((None, mk.shape[1], 2 * MEM_WIDTH), lambda b, i: (b, 0, 0))
    return pl.pallas_call(
        _out_kernel,
        grid=(B, S // tm),
        in_specs=[row(D), row(mix.shape[-1]), row(MEM_WIDTH), mem, mem,
                  pl.BlockSpec(w.shape, lambda b, i: (0, 0))],
        out_specs=row(D),
        out_shape=jax.ShapeDtypeStruct((B, S, D), F32),
        compiler_params=_cparams(("parallel", "parallel")),
        name="out_proj",
    )(x, mix, qm, mk, mv, w)


def _ffn_kernel(x_ref, g_ref, wup_ref, cw_ref, cb_ref, wdn_ref, gf_ref, o_ref, carry_ref, *, tm, cw, final):
    dff = wdn_ref.shape[0]
    hm = min(tm, 2 * LANES)
    first = pl.program_id(1) == 0
    row = lax.broadcasted_iota(I32, (hm, cw), 0)
    xs = [x_ref[r:r + hm, :] for r in range(0, tm, hm)]
    hbs = [_rms(x, g_ref[...]).astype(BF16) for x in xs]
    accs = list(xs)

    def up(hb, c0):
        return (jnp.dot(hb, wup_ref[:, c0:c0 + cw], preferred_element_type=F32),
                jnp.dot(hb, wup_ref[:, dff + c0:dff + c0 + cw], preferred_element_type=F32))

    def gate(a, b, prev, c0):
        p1, p2 = prev[7:8, :], prev[6:7, :]
        a1 = jnp.where(row == 0, p1, pltpu.roll(a, 1, 0))
        a2 = jnp.where(row == 0, p2, jnp.where(row == 1, p1, pltpu.roll(a, 2, 0)))
        w = cw_ref[:, c0:c0 + cw]
        conv = w[0:1, :] * a2 + w[1:2, :] * a1 + w[2:3, :] * a + cb_ref[:, c0:c0 + cw]
        return (conv / (1.0 + jnp.exp(-conv)) * b).astype(BF16)

    nxt = [up(hb, 0) for hb in hbs]
    for c0 in range(0, dff, cw):
        cur = nxt
        if c0 + cw < dff:
            nxt = [up(hb, c0 + cw) for hb in hbs]
        prev = jnp.where(first, 0.0, carry_ref[:, c0:c0 + cw])
        for k, (a, b) in enumerate(cur):
            accs[k] = accs[k] + jnp.dot(gate(a, b, prev, c0), wdn_ref[c0:c0 + cw, :], preferred_element_type=F32)
            prev = a[hm - 8:hm, :]
        carry_ref[:, c0:c0 + cw] = prev
    for k, acc in enumerate(accs):
        o_ref[k * hm:(k + 1) * hm, :] = _rms(acc, gf_ref[...]) if final else acc


def _ffn(x, g, wup, cw, cb, wdn, gf, tm, final):
    B, S, D = x.shape
    dff = wdn.shape[0]
    row = pl.BlockSpec((None, tm, D), lambda b, i: (b, i, 0))
    const = lambda shape: pl.BlockSpec(shape, lambda b, i: (0, 0), pipeline_mode=pl.Buffered(1))
    return pl.pallas_call(
        functools.partial(_ffn_kernel, tm=tm, cw=2 * LANES, final=final),
        grid=(B, S // tm),
        in_specs=[row, const((1, D)), const(wup.shape), const(cw.shape), const((1, dff)), const(wdn.shape),
                  const((1, D))],
        out_specs=row,
        out_shape=jax.ShapeDtypeStruct((B, S, D), F32),
        scratch_shapes=[pltpu.VMEM((8, dff), F32)],
        compiler_params=_cparams(("arbitrary", "arbitrary")),
        name="ffn",
    )(x, g, wup, cw, cb, wdn, gf)


def _b_subheads():
    return tuple(((2 * (j // 2), 2 * (j // 2), 2 * j), (2 * (j // 2) + 1, 2 * (j // 2) + 1, 2 * j + 1))
                 for j in range(B_HEADS // 2))


def kernel(x, mem, positions, g_mix, g_ffn, g_mem, g_final, w_mem_kv, a_w_in, a_kv_norm, a_w_uk, a_w_uv, a_w_out,
           b_w_in, b_sinks, b_w_out, c_w_in, c_w_out, f_w_up, f_conv_w, f_conv_b, f_w_down):
    B, S, D = x.shape
    depth = g_mix.shape[0]
    tm = min(512, S)
    tab32 = _rope_table(positions, A_QK_DIM, A_ROPE_DIM)
    tab16 = _rope_table(positions, B_HEAD_DIM, B_HEAD_DIM // 4)
    mem_k, mem_v = _memkv(mem, g_mem.reshape(1, D), w_mem_kv.astype(BF16))
    conv_w = jnp.pad(f_conv_w, ((0, 0), (0, 8 - CONV_WIDTH), (0, 0)))
    for i in range(depth):
        kind, j = i % 3, i // 3
        g = g_mix[i].reshape(1, D)
        if kind == 0:
            wuk = jnp.pad(a_w_uk[j], ((0, 0), (0, 0), (A_ROPE_DIM, 0))).reshape(A_KV_RANK, -1).astype(BF16)
            wuv = jnp.transpose(a_w_uv[j], (1, 0, 2)).astype(BF16)
            q, kf, ct, qi, ki, wi, qm = _proj_a(x, g, _prep_a_w_in(a_w_in[j]), a_kv_norm[j].reshape(1, -1), wuk,
                                                tab32, tab16, tm)
            mix = _dsa(q, qi, wi, kf, ct, ki, wuv)
            w_out = a_w_out[j]
        elif kind == 1:
            nq, nkv = B_HEADS * B_HEAD_DIM, 4 * B_KV_HEADS * B_HEAD_DIM
            q, k, v, qm = _proj_qkv(x, g, _prep_b_w_in(b_w_in[j]), tab16, tm, B_HEAD_DIM, B_HEAD_DIM // 4,
                                    nq, nkv, nkv, False)
            mix = _swa(q, k, v, b_sinks[j], subheads=_b_subheads(), max_dist=B_WINDOW - 1)
            w_out = b_w_out[j]
        else:
            nq = C_HEADS * C_HEAD_DIM
            q, k, v, qm = _proj_qkv(x, g, c_w_in[j].astype(BF16), tab32, tm, C_HEAD_DIM, C_HEAD_DIM // 4,
                                    nq, nq, nq, True)
            mix = _wattn(q, k, v, C_BRANCHES)
            w_out = c_w_out[j]
        x = _out_proj(x, mix, qm, mem_k[i], mem_v[i], w_out.astype(BF16), tm)
        x = _ffn(x, g_ffn[i].reshape(1, D), f_w_up[i].astype(BF16), conv_w[i], f_conv_b[i].reshape(1, -1),
                 f_w_down[i].astype(BF16), g_final.reshape(1, D), tm, i == depth - 1)
    return x
```

```python
import functools

import jax
import jax.numpy as jnp
import numpy as np
from jax import lax
from jax.experimental import pallas as pl
from jax.experimental.pallas import tpu as pltpu

F32 = jnp.float32
BF16 = jnp.bfloat16
I32 = jnp.int32

LANES = 128
BLOCK = 128
ROPE_THETA = 500000.0
EPS = 1e-6
NEG_INF = -1e30
LOG2E = 1.4426950408889634
INT_MIN = -(2**31)

A_HEADS = 8
A_QK_DIM = 128
A_ROPE_DIM = 32
A_KV_RANK = 256
IDX_HEADS = 16
IDX_DIM = 64
TOPK_MAX = 256
B_HEADS = 16
B_KV_HEADS = 4
B_HEAD_DIM = 64
B_WINDOW = 128
C_HEADS = 8
C_HEAD_DIM = 128
C_BRANCHES = ((128, 1), (512, 4), (2048, 16))
MEM_HEADS = 4
MEM_HEAD_DIM = 64
MEM_WIDTH = MEM_HEADS * MEM_HEAD_DIM
CONV_WIDTH = 3
KEY_CHUNK = 256

VMEM_LIMIT = 56 * 1024 * 1024

_NT = (((1,), (1,)), ((), ()))


def _cparams(sem):
    return pltpu.CompilerParams(dimension_semantics=sem, vmem_limit_bytes=VMEM_LIMIT)


def _rms(x, g):
    return x * lax.rsqrt(jnp.mean(x * x, axis=-1, keepdims=True) + EPS) * g


def _loop_k_per_trip(lo, hi, body, carry, k=2):
    if k == 1:
        return lax.fori_loop(lo, hi, body, carry)
    trips = (hi - lo) // k

    def several(t, c):
        for u in range(k):
            c = body(lo + k * t + u, c)
        return c

    carry = lax.fori_loop(0, trips, several, carry)
    return _loop_k_per_trip(lo + k * trips, hi, body, carry, k // 2)


def _bit_transpose32(words):
    a = list(words)
    j, m = 16, 0x0000FFFF
    while j:
        mask = jnp.int32(m - (1 << 32) if m >= (1 << 31) else m)
        for k in range(32):
            if k & j == 0:
                t = (a[k] ^ lax.shift_right_logical(a[k + j], j)) & mask
                a[k] = a[k] ^ t
                a[k + j] = a[k + j] ^ lax.shift_left(t, j)
        j >>= 1
        m = (m ^ (m << j)) & 0xFFFFFFFF
    return a


def _rope_tile(t, tab, half):
    c, sa, sb = tab[:, 0:LANES], tab[:, LANES:2 * LANES], tab[:, 2 * LANES:3 * LANES]
    return t * c + pltpu.roll(t, half, 1) * sa + pltpu.roll(t, LANES - half, 1) * sb


def _rope_table(positions, head_dim, rot):
    half = rot // 2
    inv = ROPE_THETA ** (-jnp.arange(0, rot, 2, dtype=F32) / rot)
    ang = positions.astype(F32)[..., None] * inv
    cs = jnp.concatenate([jnp.cos(ang), jnp.sin(ang)], axis=-1)
    sel = np.zeros((rot, 3 * LANES), np.float32)
    one = np.zeros((3 * LANES,), np.float32)
    for l in range(LANES):
        j = l % head_dim
        if j < half:
            sel[j, l] = 1.0
            sel[half + j, 2 * LANES + l] = -1.0
        elif j < rot:
            sel[j - half, l] = 1.0
            sel[j, LANES + l] = 1.0
        else:
            one[l] = 1.0
    return jnp.dot(cs, jnp.asarray(sel), precision=lax.Precision.HIGHEST) + jnp.asarray(one)


def _staggered(hbs, w_ref, jobs):
    def mms(lo, hi):
        return [jnp.dot(hb, w_ref[:, lo:hi], preferred_element_type=F32) for hb in hbs]

    nxt = mms(jobs[0][0], jobs[0][1])
    for n, (_, _, epilogue) in enumerate(jobs):
        cur = nxt
        if n + 1 < len(jobs):
            nxt = mms(jobs[n + 1][0], jobs[n + 1][1])
        for part, y in enumerate(cur):
            epilogue(y, part)


def _proj_a_kernel(x_ref, g_ref, w_ref, kvn_ref, wuk_ref, t32_ref, t16_ref,
                   q_ref, kf_ref, ct_ref, qi_ref, ki_ref, wi_ref, qm_ref, *, tm):
    hm = KEY_CHUNK
    rows = [slice(r, r + hm) for r in range(0, tm, hm)]
    hbs = [_rms(x_ref[r, :], g_ref[...]).astype(BF16) for r in rows]

    def roped(ref, col, tab_ref, half, scale):
        def epilogue(y, part):
            tab = tab_ref[rows[part], :]
            for u in range(y.shape[1] // LANES):
                t = _rope_tile(y[:, u * LANES:(u + 1) * LANES], tab, half)
                if scale != 1.0:
                    t = t * scale
                ref[rows[part], col + u * LANES:col + (u + 1) * LANES] = t.astype(BF16)
        return epilogue

    latent_bf16 = {}

    def latent(y, part):
        c = _rms(y, kvn_ref[...])
        latent_bf16[part] = c.astype(BF16)
        ct_ref[part] = c.T.astype(BF16)

    def full_keys(y, part):
        k_rope = _rope_tile(y, t32_ref[rows[part], :], A_ROPE_DIM // 2)
        k_nope = jnp.dot(latent_bf16[part], wuk_ref[...], preferred_element_type=F32)
        for h in range(A_HEADS):
            kf_ref[rows[part], h * LANES:(h + 1) * LANES] = (
                k_nope[:, h * LANES:(h + 1) * LANES] + k_rope).astype(BF16)

    def index_keys(y, part):
        tab = t16_ref[rows[part], :]
        for u in range(2):
            ki_ref[part, u * hm:(u + 1) * hm, :] = _rope_tile(
                y[:, u * LANES:(u + 1) * LANES], tab, IDX_DIM // 8).astype(BF16)

    def index_weights(y, part):
        wi_ref[rows[part], :] = y * (IDX_HEADS * IDX_DIM) ** -0.5

    def mem_query(y, part):
        qm_ref[rows[part], :] = (y * (MEM_HEAD_DIM ** -0.5 * LOG2E)).astype(BF16)

    jobs, o = [], 0
    for j in range(0, A_HEADS * A_QK_DIM, 2 * LANES):
        jobs.append((o + j, o + j + 2 * LANES,
                     roped(q_ref, j, t32_ref, A_ROPE_DIM // 2, A_QK_DIM ** -0.5 * LOG2E)))
    o += A_HEADS * A_QK_DIM
    jobs.append((o, o + A_KV_RANK, latent))
    o += A_KV_RANK
    jobs.append((o, o + LANES, full_keys))
    o += LANES
    for j in range(0, IDX_HEADS * IDX_DIM, 2 * LANES):
        jobs.append((o + j, o + j + 2 * LANES, roped(qi_ref, j, t16_ref, IDX_DIM // 8, 1.0)))
    o += IDX_HEADS * IDX_DIM
    jobs.append((o, o + 2 * LANES, index_keys))
    o += 2 * LANES
    jobs.append((o, o + LANES, index_weights))
    o += LANES
    jobs.append((o, o + MEM_WIDTH, mem_query))
    _staggered(hbs, w_ref, jobs)


def _prep_a_w_in(w):
    d = w.shape[0]
    sizes = (A_HEADS * A_QK_DIM, A_KV_RANK, A_ROPE_DIM, IDX_HEADS * IDX_DIM, IDX_DIM, IDX_HEADS, MEM_WIDTH)
    offs = [0]
    for s in sizes:
        offs.append(offs[-1] + s)
    q, ckv, kr, qi, ki, wi, qm = [w[:, offs[i]:offs[i + 1]] for i in range(len(sizes))]
    z = lambda n: jnp.zeros((d, n), w.dtype)
    return jnp.concatenate([
        q, ckv, kr, z(LANES - A_ROPE_DIM), qi,
        ki, z(LANES - IDX_DIM), z(LANES - IDX_DIM), ki,
        wi, z(LANES - IDX_HEADS), qm], axis=1).astype(BF16)


def _proj_a(x, g, w, kvn, wuk, t32, t16, tm):
    B, S, D = x.shape
    n = w.shape[1]
    row = lambda width: pl.BlockSpec((None, tm, width), lambda b, i: (b, i, 0))
    const = lambda shape: pl.BlockSpec(shape, lambda b, i: (0,) * len(shape))
    out_shape = (
        jax.ShapeDtypeStruct((B, S, A_HEADS * A_QK_DIM), BF16),
        jax.ShapeDtypeStruct((B, S, A_HEADS * A_QK_DIM), BF16),
        jax.ShapeDtypeStruct((B, S // KEY_CHUNK, A_KV_RANK, KEY_CHUNK), BF16),
        jax.ShapeDtypeStruct((B, S, IDX_HEADS * IDX_DIM), BF16),
        jax.ShapeDtypeStruct((B, S // KEY_CHUNK, 2 * KEY_CHUNK, LANES), BF16),
        jax.ShapeDtypeStruct((B, S, LANES), F32),
        jax.ShapeDtypeStruct((B, S, MEM_WIDTH), BF16),
    )
    out_specs = (
        row(A_HEADS * A_QK_DIM), row(A_HEADS * A_QK_DIM),
        pl.BlockSpec((None, tm // KEY_CHUNK, A_KV_RANK, KEY_CHUNK), lambda b, i: (b, i, 0, 0)),
        row(IDX_HEADS * IDX_DIM),
        pl.BlockSpec((None, tm // KEY_CHUNK, 2 * KEY_CHUNK, LANES), lambda b, i: (b, i, 0, 0)),
        row(LANES), row(MEM_WIDTH),
    )
    return pl.pallas_call(
        functools.partial(_proj_a_kernel, tm=tm),
        grid=(B, S // tm),
        in_specs=[row(D), const((1, D)), const((D, n)), const((1, A_KV_RANK)), const(wuk.shape),
                  row(3 * LANES), row(3 * LANES)],
        out_specs=out_specs,
        out_shape=out_shape,
        compiler_params=_cparams(("parallel", "parallel")),
        name="proj_a",
    )(x, g, w, kvn, wuk, t32, t16)


def _proj_qkv_kernel(x_ref, g_ref, w_ref, tab_ref, q_ref, k_ref, v_ref, qm_ref, *,
                     head_dim, rot, nq, nk, nv, v_transposed):
    tm = x_ref.shape[0]
    hm = KEY_CHUNK
    rows = [slice(r, r + hm) for r in range(0, tm, hm)]
    hbs = [_rms(x_ref[r, :], g_ref[...]).astype(BF16) for r in rows]

    def roped(ref, col, scale):
        def epilogue(y, part):
            tab = tab_ref[rows[part], :]
            for u in range(2):
                t = _rope_tile(y[:, u * LANES:(u + 1) * LANES], tab, rot // 2)
                if scale != 1.0:
                    t = t * scale
                ref[rows[part], col + u * LANES:col + (u + 1) * LANES] = t.astype(BF16)
        return epilogue

    def value(col):
        def epilogue(y, part):
            if v_transposed:
                v_ref[part, col:col + 2 * LANES, :] = jnp.swapaxes(y[None], 1, 2)[0].astype(BF16)
            else:
                v_ref[rows[part], col:col + 2 * LANES] = y.astype(BF16)
        return epilogue

    def mem_query(y, part):
        qm_ref[rows[part], :] = (y * (MEM_HEAD_DIM ** -0.5 * LOG2E)).astype(BF16)

    jobs = [(j, j + 2 * LANES, roped(q_ref, j, head_dim ** -0.5 * LOG2E)) for j in range(0, nq, 2 * LANES)]
    jobs += [(nq + j, nq + j + 2 * LANES, roped(k_ref, j, 1.0)) for j in range(0, nk, 2 * LANES)]
    jobs += [(nq + nk + j, nq + nk + j + 2 * LANES, value(j)) for j in range(0, nv, 2 * LANES)]
    jobs.append((nq + nk + nv, nq + nk + nv + MEM_WIDTH, mem_query))
    _staggered(hbs, w_ref, jobs)


def _proj_qkv(x, g, w, tab, tm, head_dim, rot, nq, nk, nv, v_transposed):
    B, S, D = x.shape
    n = w.shape[1]
    row = lambda width: pl.BlockSpec((None, tm, width), lambda b, i: (b, i, 0))
    const = lambda shape: pl.BlockSpec(shape, lambda b, i: (0,) * len(shape))
    if v_transposed:
        v_spec = pl.BlockSpec((None, tm // KEY_CHUNK, nv, KEY_CHUNK), lambda b, i: (b, i, 0, 0))
        v_shape = jax.ShapeDtypeStruct((B, S // KEY_CHUNK, nv, KEY_CHUNK), BF16)
    else:
        v_spec, v_shape = row(nv), jax.ShapeDtypeStruct((B, S, nv), BF16)
    return pl.pallas_call(
        functools.partial(_proj_qkv_kernel, head_dim=head_dim, rot=rot, nq=nq, nk=nk, nv=nv,
                          v_transposed=v_transposed),
        grid=(B, S // tm),
        in_specs=[row(D), const((1, D)), const((D, n)), row(3 * LANES)],
        out_specs=(row(nq), row(nk), v_spec, row(MEM_WIDTH)),
        out_shape=(jax.ShapeDtypeStruct((B, S, nq), BF16), jax.ShapeDtypeStruct((B, S, nk), BF16), v_shape,
                   jax.ShapeDtypeStruct((B, S, MEM_WIDTH), BF16)),
        compiler_params=_cparams(("parallel", "parallel")),
        name="proj_qkv",
    )(x, g, w, tab)


def _prep_b_w_in(w):
    d = w.shape[0]
    nq, nkv = B_HEADS * B_HEAD_DIM, B_KV_HEADS * B_HEAD_DIM
    q, k, v, qm = w[:, :nq], w[:, nq:nq + nkv], w[:, nq + nkv:nq + 2 * nkv], w[:, nq + 2 * nkv:]
    z = jnp.zeros((d, B_HEAD_DIM), w.dtype)

    def spread(t):
        cols = []
        for h in range(B_KV_HEADS):
            th = t[:, h * B_HEAD_DIM:(h + 1) * B_HEAD_DIM]
            cols += [th, z, z, th]
        return jnp.concatenate(cols, axis=1)

    return jnp.concatenate([q, spread(k), spread(v), qm], axis=1).astype(BF16)


DSA_BLOCK = 256


def _dsa_kernel(q_ref, qi_ref, wi_ref, kf_ref, ct_ref, ki_ref, wuv_ref, o_ref,
                keys_ref, planes_ref, qbd_ref, qi2_ref, wrow_ref, acc_ref, sa_ref, sb_ref, *, topk):
    kc = KEY_CHUNK
    qb = DSA_BLOCK
    hw = A_HEADS * qb
    gw = 2 * qb
    i = pl.program_id(1)
    nch = (i * qb + qb + kc - 1) // kc

    qbd_ref[...] = jnp.zeros(qbd_ref.shape, BF16)
    for h in range(A_HEADS):
        qbd_ref[h // 2, (h % 2) * qb:(h % 2 + 1) * qb, (h % 2) * LANES:(h % 2 + 1) * LANES] = (
            q_ref[:, h * LANES:(h + 1) * LANES])
        qi2_ref[h * qb:(h + 1) * qb, :] = qi_ref[:, h * LANES:(h + 1) * LANES]
    w_t = wi_ref[...].T
    for j in range(IDX_HEADS // 2):
        wrow_ref[0:1, j * qb:(j + 1) * qb] = w_t[2 * j:2 * j + 1, :]
        wrow_ref[1:2, j * qb:(j + 1) * qb] = w_t[2 * j + 1:2 * j + 2, :]

    qpos = i * qb + lax.broadcasted_iota(I32, (kc, qb), 1)
    krow = lax.broadcasted_iota(I32, (kc, qb), 0)

    def score_chunk(c):
        off = pl.multiple_of(c * kc, kc)
        kk = ki_ref[c]
        acc = None
        for g in range(0, hw, qb):
            d = lax.dot_general(kk, qi2_ref[g:g + qb, :], _NT, preferred_element_type=F32)
            t = (jnp.maximum(d[0:kc], 0.0) * wrow_ref[0:1, g:g + qb]
                 + jnp.maximum(d[kc:2 * kc], 0.0) * wrow_ref[1:2, g:g + qb])
            acc = t if acc is None else acc + t
        bits = pltpu.bitcast(acc, I32)
        key = jnp.where(bits < 0, bits ^ 0x7FFFFFFF, bits)
        key = jnp.where(off + krow <= qpos, key, INT_MIN)
        keys_ref[pl.ds(off, kc), :] = key
        planes = _bit_transpose32([(key[8 * r:8 * r + 8, :] ^ INT_MIN) for r in range(kc // 8)])
        for p in range(32):
            planes_ref[c, p] = planes[p]

    @pl.when(i == 0)
    def _():
        planes_ref[...] = jnp.zeros(planes_ref.shape, I32)

    def score_body(c, carry):
        score_chunk(c)
        return carry

    _loop_k_per_trip(0, nch, score_body, 0, k=4)

    n_chunks = planes_ref.shape[0]

    def bit_body(p, carry):
        alive, need, res = carry
        ones = [a & planes_ref[c, p] for c, a in enumerate(alive)]
        cnt = ones[0] * 0
        for o in ones:
            cnt = cnt + lax.population_count(o)
        cnt = jnp.sum(cnt, axis=0, keepdims=True)
        take = cnt >= need
        alive = [jnp.where(take, o, a ^ o) for o, a in zip(ones, alive)]
        need = jnp.where(take, need, need - cnt)
        res = jnp.where(take, res | lax.shift_left(jnp.int32(1), 31 - p), res)
        return alive, need, res

    for g in range(0, hw, gw):
        sa_ref[0:kc, g:g + gw] = lax.dot_general(kf_ref[0:kc, (g // gw) * 2 * LANES:(g // gw + 1) * 2 * LANES],
                                                 qbd_ref[g // gw], _NT, preferred_element_type=F32)

    carry = ([jnp.where(c < nch, jnp.full((8, qb), -1, I32), 0) for c in range(n_chunks)],
             jnp.full((1, qb), topk, I32), jnp.zeros((1, qb), I32))
    for p in range(32):
        carry = bit_body(p, carry)
    thr = jnp.maximum(carry[2] ^ INT_MIN, INT_MIN + 1)

    acc_ref[...] = jnp.zeros(acc_ref.shape, F32)
    bias0 = jnp.where(keys_ref[0:kc, :] >= thr, 0.0, NEG_INF)
    bias0 = jnp.concatenate([bias0, bias0], axis=1)
    for g in range(0, hw, gw):
        t = sa_ref[0:kc, g:g + gw] + bias0
        sa_ref[0:kc, g:g + gw] = t
        sa_ref[kc:kc + 1, g:g + gw] = jnp.max(t, axis=0, keepdims=True)

    def logits(c, s_ref):
        off = pl.multiple_of(c * kc, kc)
        bias = jnp.where(keys_ref[pl.ds(off, kc), :] >= thr, 0.0, NEG_INF)
        bias = jnp.concatenate([bias, bias], axis=1)
        for g in range(0, hw, gw):
            t = lax.dot_general(kf_ref[pl.ds(off, kc), (g // gw) * 2 * LANES:(g // gw + 1) * 2 * LANES],
                                qbd_ref[g // gw], _NT, preferred_element_type=F32) + bias
            s_ref[0:kc, g:g + gw] = t
            s_ref[kc:kc + 1, g:g + gw] = jnp.max(t, axis=0, keepdims=True)

    def consume(s_ref, c, carry):
        m_prev, l_prev = carry
        ct = ct_ref[c]
        m_out, l_out = [], []
        for g in range(0, hw, gw):
            s = s_ref[0:kc, g:g + gw]
            m_new = jnp.maximum(m_prev[:, g:g + gw], s_ref[kc:kc + 1, g:g + gw])
            alpha = jnp.exp2(m_prev[:, g:g + gw] - m_new)
            p = jnp.exp2(s - m_new)
            l_out.append(alpha * l_prev[:, g:g + gw] + jnp.sum(p, axis=0, keepdims=True))
            m_out.append(m_new)
            acc_ref[:, g:g + gw] = alpha * acc_ref[:, g:g + gw] + jnp.dot(
                ct, p.astype(BF16), preferred_element_type=F32)
        return jnp.concatenate(m_out, axis=1), jnp.concatenate(l_out, axis=1)

    def pair_body(j, carry):
        c0 = 2 * j
        logits(c0 + 1, sb_ref)
        carry = consume(sa_ref, c0, carry)
        logits(jnp.minimum(c0 + 2, nch - 1), sa_ref)
        return consume(sb_ref, c0 + 1, carry)

    carry = _loop_k_per_trip(0, nch // 2, pair_body,
                             (jnp.full((1, hw), NEG_INF, F32), jnp.zeros((1, hw), F32)), k=2)
    _, l_fin = lax.fori_loop(0, nch % 2, lambda _, c: consume(sa_ref, nch - 1, c), carry)

    inv_l = 1.0 / l_fin
    for h in range(A_HEADS):
        cols = slice(h * qb, (h + 1) * qb)
        o_lat = (acc_ref[:, cols] * inv_l[:, cols]).T.astype(BF16)
        o_ref[:, h * LANES:(h + 1) * LANES] = jnp.dot(o_lat, wuv_ref[h], preferred_element_type=F32).astype(BF16)


def _dsa(q, qi, wi, kf, ct, ki, wuv):
    B, S, _ = q.shape
    topk = min(TOPK_MAX, S // 4)
    qb = DSA_BLOCK
    blk = lambda width: pl.BlockSpec((None, qb, width), lambda b, i: (b, i, 0))
    seq = lambda width: pl.BlockSpec((None, S, width), lambda b, i: (b, 0, 0))
    const3 = lambda shape: pl.BlockSpec(shape, lambda b, i: (0, 0, 0))
    return pl.pallas_call(
        functools.partial(_dsa_kernel, topk=topk),
        grid=(B, S // qb),
        in_specs=[blk(A_HEADS * A_QK_DIM), blk(IDX_HEADS * IDX_DIM), blk(LANES),
                  seq(A_HEADS * A_QK_DIM),
                  pl.BlockSpec((None, S // KEY_CHUNK, A_KV_RANK, KEY_CHUNK), lambda b, i: (b, 0, 0, 0)),
                  pl.BlockSpec((None, S // KEY_CHUNK, 2 * KEY_CHUNK, LANES), lambda b, i: (b, 0, 0, 0)),
                  const3(wuv.shape)],
        out_specs=blk(A_HEADS * LANES),
        out_shape=jax.ShapeDtypeStruct((B, S, A_HEADS * LANES), BF16),
        scratch_shapes=[
            pltpu.VMEM((S, qb), I32),
            pltpu.VMEM((S // KEY_CHUNK, 32, 8, qb), I32),
            pltpu.VMEM((A_HEADS // 2, 2 * qb, 2 * LANES), BF16),
            pltpu.VMEM((A_HEADS * qb, LANES), BF16),
            pltpu.VMEM((8, A_HEADS * qb), F32),
            pltpu.VMEM((A_KV_RANK, A_HEADS * qb), F32),
            pltpu.VMEM((KEY_CHUNK + 8, A_HEADS * qb), F32),
            pltpu.VMEM((KEY_CHUNK + 8, A_HEADS * qb), F32),
        ],
        compiler_params=_cparams(("parallel", "arbitrary")),
        name="dsa",
    )(q, qi, wi, kf, ct, ki, wuv)


def _window_bias(branches, nrel):
    rel = jnp.arange(nrel, dtype=I32)[:, None, None]
    krow = jnp.arange(KEY_CHUNK, dtype=I32)[None, :, None]
    qcol = jnp.arange(KEY_CHUNK, dtype=I32)[None, None, :]
    dist = KEY_CHUNK * (nrel - 1 - rel) + qcol - krow
    mult = sum(((dist >= 0) & (dist <= window) & (dist % dil == 0)).astype(F32) for window, dil in branches)
    return jnp.where(mult > 0, jnp.log2(jnp.maximum(mult, 1.0)), NEG_INF)


def _wattn_kernel(q_ref, k_ref, vt_ref, bias_ref, o_ref, acc_ref, sa_ref, sb_ref, p_ref, *, nh, nrel):
    kc = qb = KEY_CHUNK
    top = pl.program_id(1)
    c_lo = jnp.maximum(top - (nrel - 1), 0)
    n = top + 1 - c_lo

    acc_ref[...] = jnp.zeros(acc_ref.shape, F32)

    def logits(c, s_ref):
        off = pl.multiple_of(c * kc, kc)
        bias = bias_ref[c - top + (nrel - 1)]
        for h in range(nh):
            cols = slice(h * LANES, (h + 1) * LANES)
            t = lax.dot_general(k_ref[pl.ds(off, kc), cols], q_ref[:, cols], _NT,
                                preferred_element_type=F32) + bias
            s_ref[0:kc, h * qb:(h + 1) * qb] = t
            s_ref[kc:kc + 1, h * qb:(h + 1) * qb] = jnp.max(t, axis=0, keepdims=True)

    def consume(s_ref, c, carry):
        m_prev, l_prev = carry
        m_new = jnp.maximum(m_prev, s_ref[kc:kc + 1, :])
        alpha = jnp.exp2(m_prev - m_new)
        p_ref[...] = jnp.exp2(s_ref[0:kc, :] - m_new).astype(BF16)
        l_new = alpha * l_prev + jnp.dot(jnp.ones((16, kc), BF16), p_ref[...], preferred_element_type=F32)[0:1, :]
        for h in range(nh):
            cols = slice(h * qb, (h + 1) * qb)
            acc_ref[:, cols] = alpha[:, cols] * acc_ref[:, cols] + jnp.dot(
                vt_ref[c, h * LANES:(h + 1) * LANES, :], p_ref[:, cols], preferred_element_type=F32)
        return m_new, l_new

    logits(c_lo, sa_ref)

    def pair_body(j, carry):
        c0 = c_lo + 2 * j
        logits(c0 + 1, sb_ref)
        carry = consume(sa_ref, c0, carry)
        logits(jnp.minimum(c0 + 2, top), sa_ref)
        return consume(sb_ref, c0 + 1, carry)

    carry = _loop_k_per_trip(0, n // 2, pair_body,
                             (jnp.full((1, nh * qb), NEG_INF, F32), jnp.zeros((1, nh * qb), F32)), k=4)
    _, l_fin = lax.fori_loop(0, n % 2, lambda _, c: consume(sa_ref, top, c), carry)

    inv_l = 1.0 / l_fin
    for h in range(nh):
        cols = slice(h * qb, (h + 1) * qb)
        o_ref[:, h * LANES:(h + 1) * LANES] = (acc_ref[:, cols] * inv_l[:, cols]).T.astype(BF16)


def _swa_kernel(sink_ref, q_ref, kp_ref, kc_ref, vp_ref, vc_ref, o_ref, *, subheads, max_dist):
    n = pl.program_id(1)
    qi = lax.broadcasted_iota(I32, (BLOCK, 2 * BLOCK), 0)
    kj = lax.broadcasted_iota(I32, (BLOCK, 2 * BLOCK), 1)
    dist = BLOCK + qi - kj
    mask = (dist >= 0) & (dist <= max_dist) & ((kj >= BLOCK) | (n > 0))
    for j, subs in enumerate(subheads):
        qt = q_ref[:, j * LANES:(j + 1) * LANES]
        o_t = None
        for kt, vt, hidx in subs:
            kk = jnp.concatenate([kp_ref[:, kt * LANES:(kt + 1) * LANES],
                                  kc_ref[:, kt * LANES:(kt + 1) * LANES]], axis=0)
            vv = jnp.concatenate([vp_ref[:, vt * LANES:(vt + 1) * LANES],
                                  vc_ref[:, vt * LANES:(vt + 1) * LANES]], axis=0)
            s = lax.dot_general(qt, kk, _NT, preferred_element_type=F32)
            s = jnp.where(mask, s, NEG_INF)
            sk = sink_ref[hidx] * LOG2E
            m = jnp.maximum(jnp.max(s, axis=-1, keepdims=True), sk)
            p = jnp.exp2(s - m)
            l = jnp.sum(p, axis=-1, keepdims=True) + jnp.exp2(sk - m)
            o_s = jnp.dot(p.astype(BF16), vv, preferred_element_type=F32) / l
            o_t = o_s if o_t is None else o_t + o_s
        o_ref[:, j * LANES:(j + 1) * LANES] = o_t.astype(BF16)


def _swa(q, k, v, sinks, *, subheads, max_dist):
    B, S, wq = q.shape
    wk, wv = k.shape[-1], v.shape[-1]
    cur = lambda width: pl.BlockSpec((None, BLOCK, width), lambda b, i: (b, i, 0))
    prv = lambda width: pl.BlockSpec((None, BLOCK, width), lambda b, i: (b, jnp.maximum(i - 1, 0), 0))
    return pl.pallas_call(
        functools.partial(_swa_kernel, subheads=subheads, max_dist=max_dist),
        grid=(B, S // BLOCK),
        in_specs=[pl.BlockSpec(memory_space=pltpu.SMEM), cur(wq), prv(wk), cur(wk), prv(wv), cur(wv)],
        out_specs=cur(wq),
        out_shape=jax.ShapeDtypeStruct((B, S, wq), BF16),
        compiler_params=_cparams(("parallel", "arbitrary")),
        name="swa",
    )(sinks, q, k, k, v, v)


def _wattn(q, k, vt, branches):
    B, S, wq = q.shape
    widest = max(w for w, _ in branches)
    nrel = min(-(-widest // KEY_CHUNK) + 1, S // KEY_CHUNK)
    bias = _window_bias(branches, nrel)
    nh = wq // LANES
    blk = lambda width: pl.BlockSpec((None, KEY_CHUNK, width), lambda b, i: (b, i, 0))
    return pl.pallas_call(
        functools.partial(_wattn_kernel, nh=nh, nrel=nrel),
        grid=(B, S // KEY_CHUNK),
        in_specs=[blk(wq),
                  pl.BlockSpec((None, S, wq), lambda b, i: (b, 0, 0)),
                  pl.BlockSpec((None,) + vt.shape[1:], lambda b, i: (b, 0, 0, 0)),
                  pl.BlockSpec(bias.shape, lambda b, i: (0, 0, 0))],
        out_specs=blk(wq),
        out_shape=jax.ShapeDtypeStruct((B, S, wq), BF16),
        scratch_shapes=[pltpu.VMEM((LANES, nh * KEY_CHUNK), F32),
                        pltpu.VMEM((KEY_CHUNK + 8, nh * KEY_CHUNK), F32),
                        pltpu.VMEM((KEY_CHUNK + 8, nh * KEY_CHUNK), F32),
                        pltpu.VMEM((KEY_CHUNK, nh * KEY_CHUNK), BF16)],
        compiler_params=_cparams(("parallel", "arbitrary")),
        name="wattn",
    )(q, k, vt, bias)


def _memkv_kernel(mem_ref, g_ref, w_ref, k_ref, v_ref):
    hb = _rms(mem_ref[...], g_ref[...]).astype(BF16)
    y = jnp.dot(hb, w_ref[...], preferred_element_type=F32)
    lane = lax.broadcasted_iota(I32, (y.shape[0], LANES), 1)
    for t in range(MEM_WIDTH // LANES):
        for out_ref, base in ((k_ref, 0), (v_ref, MEM_WIDTH)):
            tile = y[:, base + t * LANES:base + (t + 1) * LANES]
            out_ref[:, (2 * t) * LANES:(2 * t + 1) * LANES] = jnp.where(lane < MEM_HEAD_DIM, tile, 0.0).astype(BF16)
            out_ref[:, (2 * t + 1) * LANES:(2 * t + 2) * LANES] = jnp.where(lane >= MEM_HEAD_DIM, tile, 0.0).astype(BF16)


def _memkv(mem, g_mem, w):
    B, M, D = mem.shape
    L = w.shape[0]
    out = jax.ShapeDtypeStruct((L, B, M, 2 * MEM_WIDTH), BF16)
    ospec = pl.BlockSpec((None, None, M, 2 * MEM_WIDTH), lambda l, b: (l, b, 0, 0))
    return pl.pallas_call(
        _memkv_kernel,
        grid=(L, B),
        in_specs=[pl.BlockSpec((None, M, D), lambda l, b: (b, 0, 0)),
                  pl.BlockSpec((1, D), lambda l, b: (0, 0)),
                  pl.BlockSpec((None, D, 2 * MEM_WIDTH), lambda l, b: (l, 0, 0))],
        out_specs=(ospec, ospec),
        out_shape=(out, out),
        compiler_params=_cparams(("parallel", "parallel")),
        name="memkv",
    )(mem, g_mem, w)


def _mixer_residual(x_ref, mix_ref, qm_ref, mk_ref, mv_ref, w_ref):
    mw = mix_ref.shape[-1]
    n_sub = 2 * MEM_WIDTH // LANES
    logits = [lax.dot_general(qm_ref[:, (u // 2) * LANES:(u // 2 + 1) * LANES], mk_ref[:, u * LANES:(u + 1) * LANES],
                              _NT, preferred_element_type=F32) for u in range(n_sub)]
    y = x_ref[...] + jnp.dot(mix_ref[...], w_ref[0:mw, :], preferred_element_type=F32)
    tiles = []
    for t in range(n_sub // 2):
        o_t = None
        for u in (2 * t, 2 * t + 1):
            p = jnp.exp2(logits[u] - jnp.max(logits[u], axis=-1, keepdims=True))
            l = jnp.sum(p, axis=-1, keepdims=True)
            o_s = jnp.dot(p.astype(BF16), mv_ref[:, u * LANES:(u + 1) * LANES], preferred_element_type=F32) / l
            o_t = o_s if o_t is None else o_t + o_s
        tiles.append(o_t.astype(BF16))
    return y + jnp.dot(jnp.concatenate(tiles, axis=1), w_ref[mw:mw + MEM_WIDTH, :], preferred_element_type=F32)


def _out_kernel(x_ref, mix_ref, qm_ref, mk_ref, mv_ref, w_ref, o_ref):
    o_ref[...] = _mixer_residual(x_ref, mix_ref, qm_ref, mk_ref, mv_ref, w_ref)


def _out_proj(x, mix, qm, mk, mv, w, tm):
    B, S, D = x.shape
    row = lambda width: pl.BlockSpec((None, tm, width), lambda b, i: (b, i, 0))
    mem = pl.BlockSpec((None, mk.shape[1], 2 * MEM_WIDTH), lambda b, i: (b, 0, 0))
    return pl.pallas_call(
        _out_kernel,
        grid=(B, S // tm),
        in_specs=[row(D), row(mix.shape[-1]), row(MEM_WIDTH), mem, mem,
                  pl.BlockSpec(w.shape, lambda b, i: (0, 0))],
        out_specs=row(D),
        out_shape=jax.ShapeDtypeStruct((B, S, D), F32),
        compiler_params=_cparams(("parallel", "parallel")),
        name="out_proj",
    )(x, mix, qm, mk, mv, w)


def _ffn_kernel(x_ref, g_ref, wup_ref, cw_ref, cb_ref, wdn_ref, gf_ref, o_ref, carry_ref, *, tm, cw, final):
    dff = wdn_ref.shape[0]
    hm = min(tm, 2 * LANES)

    @pl.when(pl.program_id(1) == 0)
    def _():
        carry_ref[...] = jnp.zeros(carry_ref.shape, F32)

    row = lax.broadcasted_iota(I32, (hm, cw), 0)
    xs = [x_ref[r:r + hm, :] for r in range(0, tm, hm)]
    hbs = [_rms(x, g_ref[...]).astype(BF16) for x in xs]
    accs = list(xs)

    def up(hb, c0):
        return (jnp.dot(hb, wup_ref[:, c0:c0 + cw], preferred_element_type=F32),
                jnp.dot(hb, wup_ref[:, dff + c0:dff + c0 + cw], preferred_element_type=F32))

    def gate(a, b, prev, c0):
        p1, p2 = prev[7:8, :], prev[6:7, :]
        a1 = jnp.where(row == 0, p1, pltpu.roll(a, 1, 0))
        a2 = jnp.where(row == 0, p2, jnp.where(row == 1, p1, pltpu.roll(a, 2, 0)))
        w = cw_ref[:, c0:c0 + cw]
        conv = w[0:1, :] * a2 + w[1:2, :] * a1 + w[2:3, :] * a + cb_ref[:, c0:c0 + cw]
        return (conv / (1.0 + jnp.exp(-conv)) * b).astype(BF16)

    nxt = [up(hb, 0) for hb in hbs]
    for c0 in range(0, dff, cw):
        cur = nxt
        if c0 + cw < dff:
            nxt = [up(hb, c0 + cw) for hb in hbs]
        prev = carry_ref[:, c0:c0 + cw]
        for k, (a, b) in enumerate(cur):
            accs[k] = accs[k] + jnp.dot(gate(a, b, prev, c0), wdn_ref[c0:c0 + cw, :], preferred_element_type=F32)
            prev = a[hm - 8:hm, :]
        carry_ref[:, c0:c0 + cw] = prev
    for k, acc in enumerate(accs):
        o_ref[k * hm:(k + 1) * hm, :] = _rms(acc, gf_ref[...]) if final else acc


def _ffn(x, g, wup, cw, cb, wdn, gf, tm, final):
    B, S, D = x.shape
    dff = wdn.shape[0]
    row = pl.BlockSpec((None, tm, D), lambda b, i: (b, i, 0))
    const = lambda shape: pl.BlockSpec(shape, lambda b, i: (0, 0), pipeline_mode=pl.Buffered(1))
    return pl.pallas_call(
        functools.partial(_ffn_kernel, tm=tm, cw=2 * LANES, final=final),
        grid=(B, S // tm),
        in_specs=[row, const((1, D)), const(wup.shape), const(cw.shape), const((1, dff)), const(wdn.shape),
                  const((1, D))],
        out_specs=row,
        out_shape=jax.ShapeDtypeStruct((B, S, D), F32),
        scratch_shapes=[pltpu.VMEM((8, dff), F32)],
        compiler_params=_cparams(("arbitrary", "arbitrary")),
        name="ffn",
    )(x, g, wup, cw, cb, wdn, gf)


def _b_subheads():
    return tuple(((2 * (j // 2), 2 * (j // 2), 2 * j), (2 * (j // 2) + 1, 2 * (j // 2) + 1, 2 * j + 1))
                 for j in range(B_HEADS // 2))


def kernel(x, mem, positions, g_mix, g_ffn, g_mem, g_final, w_mem_kv, a_w_in, a_kv_norm, a_w_uk, a_w_uv, a_w_out,
           b_w_in, b_sinks, b_w_out, c_w_in, c_w_out, f_w_up, f_conv_w, f_conv_b, f_w_down):
    B, S, D = x.shape
    depth = g_mix.shape[0]
    tm = min(512, S)
    tab32 = _rope_table(positions, A_QK_DIM, A_ROPE_DIM)
    tab16 = _rope_table(positions, B_HEAD_DIM, B_HEAD_DIM // 4)
    mem_k, mem_v = _memkv(mem, g_mem.reshape(1, D), w_mem_kv.astype(BF16))
    conv_w = jnp.pad(f_conv_w, ((0, 0), (0, 8 - CONV_WIDTH), (0, 0)))
    for i in range(depth):
        kind, j = i % 3, i // 3
        g = g_mix[i].reshape(1, D)
        if kind == 0:
            wuk = jnp.pad(a_w_uk[j], ((0, 0), (0, 0), (A_ROPE_DIM, 0))).reshape(A_KV_RANK, -1).astype(BF16)
            wuv = jnp.transpose(a_w_uv[j], (1, 0, 2)).astype(BF16)
            q, kf, ct, qi, ki, wi, qm = _proj_a(x, g, _prep_a_w_in(a_w_in[j]), a_kv_norm[j].reshape(1, -1), wuk,
                                                tab32, tab16, tm)
            mix = _dsa(q, qi, wi, kf, ct, ki, wuv)
            w_out = a_w_out[j]
        elif kind == 1:
            nq, nkv = B_HEADS * B_HEAD_DIM, 4 * B_KV_HEADS * B_HEAD_DIM
            q, k, v, qm = _proj_qkv(x, g, _prep_b_w_in(b_w_in[j]), tab16, tm, B_HEAD_DIM, B_HEAD_DIM // 4,
                                    nq, nkv, nkv, False)
            mix = _swa(q, k, v, b_sinks[j], subheads=_b_subheads(), max_dist=B_WINDOW - 1)
            w_out = b_w_out[j]
        else:
            nq = C_HEADS * C_HEAD_DIM
            q, k, v, qm = _proj_qkv(x, g, c_w_in[j].astype(BF16), tab32, tm, C_HEAD_DIM, C_HEAD_DIM // 4,
                                    nq, nq, nq, True)
            mix = _wattn(q, k, v, C_BRANCHES)
            w_out = c_w_out[j]
        x = _out_proj(x, mix, qm, mem_k[i], mem_v[i], w_out.astype(BF16), tm)
        x = _ffn(x, g_ffn[i].reshape(1, D), f_w_up[i].astype(BF16), conv_w[i], f_conv_b[i].reshape(1, -1),
                 f_w_down[i].astype(BF16), g_final.reshape(1, D), tm, i == depth - 1)
    return x
```

```python
import functools

import jax
import jax.numpy as jnp
import numpy as np
from jax import lax
from jax.experimental import pallas as pl
from jax.experimental.pallas import tpu as pltpu

F32 = jnp.float32
BF16 = jnp.bfloat16
I32 = jnp.int32

LANES = 128
BLOCK = 128
ROPE_THETA = 500000.0
EPS = 1e-6
NEG_INF = -1e30
LOG2E = 1.4426950408889634
INT_MIN = -(2**31)

A_HEADS = 8
A_QK_DIM = 128
A_ROPE_DIM = 32
A_KV_RANK = 256
IDX_HEADS = 16
IDX_DIM = 64
TOPK_MAX = 256
B_HEADS = 16
B_KV_HEADS = 4
B_HEAD_DIM = 64
B_WINDOW = 128
C_HEADS = 8
C_HEAD_DIM = 128
C_BRANCHES = ((128, 1), (512, 4), (2048, 16))
MEM_HEADS = 4
MEM_HEAD_DIM = 64
MEM_WIDTH = MEM_HEADS * MEM_HEAD_DIM
CONV_WIDTH = 3
KEY_CHUNK = 256

VMEM_LIMIT = 56 * 1024 * 1024

_NT = (((1,), (1,)), ((), ()))


def _cparams(sem):
    return pltpu.CompilerParams(dimension_semantics=sem, vmem_limit_bytes=VMEM_LIMIT)


def _rms(x, g):
    return x * lax.rsqrt(jnp.mean(x * x, axis=-1, keepdims=True) + EPS) * g


def _loop_k_per_trip(lo, hi, body, carry, k=2):
    if k == 1:
        return lax.fori_loop(lo, hi, body, carry)
    trips = (hi - lo) // k

    def several(t, c):
        for u in range(k):
            c = body(lo + k * t + u, c)
        return c

    carry = lax.fori_loop(0, trips, several, carry)
    return _loop_k_per_trip(lo + k * trips, hi, body, carry, k // 2)


def _bit_transpose32(words):
    a = list(words)
    j, m = 16, 0x0000FFFF
    while j:
        mask = jnp.int32(m - (1 << 32) if m >= (1 << 31) else m)
        for k in range(32):
            if k & j == 0:
                t = (a[k] ^ lax.shift_right_logical(a[k + j], j)) & mask
                a[k] = a[k] ^ t
                a[k + j] = a[k + j] ^ lax.shift_left(t, j)
        j >>= 1
        m = (m ^ (m << j)) & 0xFFFFFFFF
    return a


def _rope_tile(t, tab, half):
    c, sa, sb = tab[:, 0:LANES], tab[:, LANES:2 * LANES], tab[:, 2 * LANES:3 * LANES]
    return t * c + pltpu.roll(t, half, 1) * sa + pltpu.roll(t, LANES - half, 1) * sb


def _rope_table(positions, head_dim, rot):
    half = rot // 2
    inv = ROPE_THETA ** (-jnp.arange(0, rot, 2, dtype=F32) / rot)
    ang = positions.astype(F32)[..., None] * inv
    cs = jnp.concatenate([jnp.cos(ang), jnp.sin(ang)], axis=-1)
    sel = np.zeros((rot, 3 * LANES), np.float32)
    one = np.zeros((3 * LANES,), np.float32)
    for l in range(LANES):
        j = l % head_dim
        if j < half:
            sel[j, l] = 1.0
            sel[half + j, 2 * LANES + l] = -1.0
        elif j < rot:
            sel[j - half, l] = 1.0
            sel[j, LANES + l] = 1.0
        else:
            one[l] = 1.0
    return jnp.dot(cs, jnp.asarray(sel), precision=lax.Precision.HIGHEST) + jnp.asarray(one)


def _staggered(hbs, w_ref, jobs):
    def mms(lo, hi):
        return [jnp.dot(hb, w_ref[:, lo:hi], preferred_element_type=F32) for hb in hbs]

    nxt = mms(jobs[0][0], jobs[0][1])
    for n, (_, _, epilogue) in enumerate(jobs):
        cur = nxt
        if n + 1 < len(jobs):
            nxt = mms(jobs[n + 1][0], jobs[n + 1][1])
        for part, y in enumerate(cur):
            epilogue(y, part)


def _proj_a_kernel(x_ref, g_ref, w_ref, kvn_ref, wuk_ref, t32_ref, t16_ref,
                   q_ref, kf_ref, ct_ref, qi_ref, ki_ref, wi_ref, qm_ref, *, tm):
    hm = KEY_CHUNK
    rows = [slice(r, r + hm) for r in range(0, tm, hm)]
    hbs = [_rms(x_ref[r, :], g_ref[...]).astype(BF16) for r in rows]

    def roped(ref, col, tab_ref, half, scale):
        def epilogue(y, part):
            tab = tab_ref[rows[part], :]
            for u in range(y.shape[1] // LANES):
                t = _rope_tile(y[:, u * LANES:(u + 1) * LANES], tab, half)
                if scale != 1.0:
                    t = t * scale
                ref[rows[part], col + u * LANES:col + (u + 1) * LANES] = t.astype(BF16)
        return epilogue

    latent_bf16 = {}

    def latent(y, part):
        c = _rms(y, kvn_ref[...])
        latent_bf16[part] = c.astype(BF16)
        ct_ref[part] = c.T.astype(BF16)

    def full_keys(y, part):
        k_rope = _rope_tile(y, t32_ref[rows[part], :], A_ROPE_DIM // 2)
        k_nope = jnp.dot(latent_bf16[part], wuk_ref[...], preferred_element_type=F32)
        for h in range(A_HEADS):
            kf_ref[rows[part], h * LANES:(h + 1) * LANES] = (
                k_nope[:, h * LANES:(h + 1) * LANES] + k_rope).astype(BF16)

    def index_keys(y, part):
        tab = t16_ref[rows[part], :]
        for u in range(2):
            ki_ref[part, u * hm:(u + 1) * hm, :] = _rope_tile(
                y[:, u * LANES:(u + 1) * LANES], tab, IDX_DIM // 8).astype(BF16)

    def index_weights(y, part):
        wi_ref[rows[part], :] = y * (IDX_HEADS * IDX_DIM) ** -0.5

    def mem_query(y, part):
        qm_ref[rows[part], :] = (y * (MEM_HEAD_DIM ** -0.5 * LOG2E)).astype(BF16)

    jobs, o = [], 0
    for j in range(0, A_HEADS * A_QK_DIM, 2 * LANES):
        jobs.append((o + j, o + j + 2 * LANES,
                     roped(q_ref, j, t32_ref, A_ROPE_DIM // 2, A_QK_DIM ** -0.5 * LOG2E)))
    o += A_HEADS * A_QK_DIM
    jobs.append((o, o + A_KV_RANK, latent))
    o += A_KV_RANK
    jobs.append((o, o + LANES, full_keys))
    o += LANES
    for j in range(0, IDX_HEADS * IDX_DIM, 2 * LANES):
        jobs.append((o + j, o + j + 2 * LANES, roped(qi_ref, j, t16_ref, IDX_DIM // 8, 1.0)))
    o += IDX_HEADS * IDX_DIM
    jobs.append((o, o + 2 * LANES, index_keys))
    o += 2 * LANES
    jobs.append((o, o + LANES, index_weights))
    o += LANES
    jobs.append((o, o + MEM_WIDTH, mem_query))
    _staggered(hbs, w_ref, jobs)


def _prep_a_w_in(w):
    d = w.shape[0]
    sizes = (A_HEADS * A_QK_DIM, A_KV_RANK, A_ROPE_DIM, IDX_HEADS * IDX_DIM, IDX_DIM, IDX_HEADS, MEM_WIDTH)
    offs = [0]
    for s in sizes:
        offs.append(offs[-1] + s)
    q, ckv, kr, qi, ki, wi, qm = [w[:, offs[i]:offs[i + 1]] for i in range(len(sizes))]
    z = lambda n: jnp.zeros((d, n), w.dtype)
    return jnp.concatenate([
        q, ckv, kr, z(LANES - A_ROPE_DIM), qi,
        ki, z(LANES - IDX_DIM), z(LANES - IDX_DIM), ki,
        wi, z(LANES - IDX_HEADS), qm], axis=1).astype(BF16)


def _proj_a(x, g, w, kvn, wuk, t32, t16, tm):
    B, S, D = x.shape
    n = w.shape[1]
    row = lambda width: pl.BlockSpec((None, tm, width), lambda b, i: (b, i, 0))
    const = lambda shape: pl.BlockSpec(shape, lambda b, i: (0,) * len(shape))
    out_shape = (
        jax.ShapeDtypeStruct((B, S, A_HEADS * A_QK_DIM), BF16),
        jax.ShapeDtypeStruct((B, S, A_HEADS * A_QK_DIM), BF16),
        jax.ShapeDtypeStruct((B, S // KEY_CHUNK, A_KV_RANK, KEY_CHUNK), BF16),
        jax.ShapeDtypeStruct((B, S, IDX_HEADS * IDX_DIM), BF16),
        jax.ShapeDtypeStruct((B, S // KEY_CHUNK, 2 * KEY_CHUNK, LANES), BF16),
        jax.ShapeDtypeStruct((B, S, LANES), F32),
        jax.ShapeDtypeStruct((B, S, MEM_WIDTH), BF16),
    )
    out_specs = (
        row(A_HEADS * A_QK_DIM), row(A_HEADS * A_QK_DIM),
        pl.BlockSpec((None, tm // KEY_CHUNK, A_KV_RANK, KEY_CHUNK), lambda b, i: (b, i, 0, 0)),
        row(IDX_HEADS * IDX_DIM),
        pl.BlockSpec((None, tm // KEY_CHUNK, 2 * KEY_CHUNK, LANES), lambda b, i: (b, i, 0, 0)),
        row(LANES), row(MEM_WIDTH),
    )
    return pl.pallas_call(
        functools.partial(_proj_a_kernel, tm=tm),
        grid=(B, S // tm),
        in_specs=[row(D), const((1, D)), const((D, n)), const((1, A_KV_RANK)), const(wuk.shape),
                  row(3 * LANES), row(3 * LANES)],
        out_specs=out_specs,
        out_shape=out_shape,
        compiler_params=_cparams(("parallel", "parallel")),
        name="proj_a",
    )(x, g, w, kvn, wuk, t32, t16)


def _proj_qkv_kernel(x_ref, g_ref, w_ref, tab_ref, q_ref, k_ref, v_ref, qm_ref, *,
                     head_dim, rot, nq, nk, nv, v_transposed):
    tm = x_ref.shape[0]
    hm = KEY_CHUNK
    rows = [slice(r, r + hm) for r in range(0, tm, hm)]
    hbs = [_rms(x_ref[r, :], g_ref[...]).astype(BF16) for r in rows]

    def roped(ref, col, scale):
        def epilogue(y, part):
            tab = tab_ref[rows[part], :]
            for u in range(2):
                t = _rope_tile(y[:, u * LANES:(u + 1) * LANES], tab, rot // 2)
                if scale != 1.0:
                    t = t * scale
                ref[rows[part], col + u * LANES:col + (u + 1) * LANES] = t.astype(BF16)
        return epilogue

    def value(col):
        def epilogue(y, part):
            if v_transposed:
                v_ref[part, col:col + 2 * LANES, :] = jnp.swapaxes(y[None], 1, 2)[0].astype(BF16)
            else:
                v_ref[rows[part], col:col + 2 * LANES] = y.astype(BF16)
        return epilogue

    def mem_query(y, part):
        qm_ref[rows[part], :] = (y * (MEM_HEAD_DIM ** -0.5 * LOG2E)).astype(BF16)

    jobs = [(j, j + 2 * LANES, roped(q_ref, j, head_dim ** -0.5 * LOG2E)) for j in range(0, nq, 2 * LANES)]
    jobs += [(nq + j, nq + j + 2 * LANES, roped(k_ref, j, 1.0)) for j in range(0, nk, 2 * LANES)]
    jobs += [(nq + nk + j, nq + nk + j + 2 * LANES, value(j)) for j in range(0, nv, 2 * LANES)]
    jobs.append((nq + nk + nv, nq + nk + nv + MEM_WIDTH, mem_query))
    _staggered(hbs, w_ref, jobs)


def _proj_qkv(x, g, w, tab, tm, head_dim, rot, nq, nk, nv, v_transposed):
    B, S, D = x.shape
    n = w.shape[1]
    row = lambda width: pl.BlockSpec((None, tm, width), lambda b, i: (b, i, 0))
    const = lambda shape: pl.BlockSpec(shape, lambda b, i: (0,) * len(shape))
    if v_transposed:
        v_spec = pl.BlockSpec((None, tm // KEY_CHUNK, nv, KEY_CHUNK), lambda b, i: (b, i, 0, 0))
        v_shape = jax.ShapeDtypeStruct((B, S // KEY_CHUNK, nv, KEY_CHUNK), BF16)
    else:
        v_spec, v_shape = row(nv), jax.ShapeDtypeStruct((B, S, nv), BF16)
    return pl.pallas_call(
        functools.partial(_proj_qkv_kernel, head_dim=head_dim, rot=rot, nq=nq, nk=nk, nv=nv,
                          v_transposed=v_transposed),
        grid=(B, S // tm),
        in_specs=[row(D), const((1, D)), const((D, n)), row(3 * LANES)],
        out_specs=(row(nq), row(nk), v_spec, row(MEM_WIDTH)),
        out_shape=(jax.ShapeDtypeStruct((B, S, nq), BF16), jax.ShapeDtypeStruct((B, S, nk), BF16), v_shape,
                   jax.ShapeDtypeStruct((B, S, MEM_WIDTH), BF16)),
        compiler_params=_cparams(("parallel", "parallel")),
        name="proj_qkv",
    )(x, g, w, tab)


def _prep_b_w_in(w):
    d = w.shape[0]
    nq, nkv = B_HEADS * B_HEAD_DIM, B_KV_HEADS * B_HEAD_DIM
    q, k, v, qm = w[:, :nq], w[:, nq:nq + nkv], w[:, nq + nkv:nq + 2 * nkv], w[:, nq + 2 * nkv:]
    z = jnp.zeros((d, B_HEAD_DIM), w.dtype)

    def spread(t):
        cols = []
        for h in range(B_KV_HEADS):
            th = t[:, h * B_HEAD_DIM:(h + 1) * B_HEAD_DIM]
            cols += [th, z, z, th]
        return jnp.concatenate(cols, axis=1)

    return jnp.concatenate([q, spread(k), spread(v), qm], axis=1).astype(BF16)


DSA_BLOCK = 256


def _dsa_kernel(q_ref, qi_ref, wi_ref, kf_ref, ct_ref, ki_ref, wuv_ref, o_ref,
                keys_ref, planes_ref, qbd_ref, qi2_ref, wrow_ref, acc_ref, sa_ref, sb_ref, *, topk):
    kc = KEY_CHUNK
    qb = DSA_BLOCK
    hw = A_HEADS * qb
    gw = 2 * qb
    i = pl.program_id(1)
    nch = (i * qb + qb + kc - 1) // kc

    qbd_ref[...] = jnp.zeros(qbd_ref.shape, BF16)
    for h in range(A_HEADS):
        qbd_ref[h // 2, (h % 2) * qb:(h % 2 + 1) * qb, (h % 2) * LANES:(h % 2 + 1) * LANES] = (
            q_ref[:, h * LANES:(h + 1) * LANES])
        qi2_ref[h * qb:(h + 1) * qb, :] = qi_ref[:, h * LANES:(h + 1) * LANES]
    w_t = wi_ref[...].T
    for j in range(IDX_HEADS // 2):
        wrow_ref[0:1, j * qb:(j + 1) * qb] = w_t[2 * j:2 * j + 1, :]
        wrow_ref[1:2, j * qb:(j + 1) * qb] = w_t[2 * j + 1:2 * j + 2, :]

    qpos = i * qb + lax.broadcasted_iota(I32, (kc, qb), 1)
    krow = lax.broadcasted_iota(I32, (kc, qb), 0)

    def score_chunk(c):
        off = pl.multiple_of(c * kc, kc)
        kk = ki_ref[c]
        acc = None
        for g in range(0, hw, qb):
            d = lax.dot_general(kk, qi2_ref[g:g + qb, :], _NT, preferred_element_type=F32)
            t = (jnp.maximum(d[0:kc], 0.0) * wrow_ref[0:1, g:g + qb]
                 + jnp.maximum(d[kc:2 * kc], 0.0) * wrow_ref[1:2, g:g + qb])
            acc = t if acc is None else acc + t
        bits = pltpu.bitcast(acc, I32)
        key = jnp.where(bits < 0, bits ^ 0x7FFFFFFF, bits)
        key = jnp.where(off + krow <= qpos, key, INT_MIN)
        keys_ref[pl.ds(off, kc), :] = key
        planes = _bit_transpose32([(key[8 * r:8 * r + 8, :] ^ INT_MIN) for r in range(kc // 8)])
        for p in range(32):
            planes_ref[c, p] = planes[p]

    @pl.when(i == 0)
    def _():
        planes_ref[...] = jnp.zeros(planes_ref.shape, I32)

    def score_body(c, carry):
        score_chunk(c)
        return carry

    _loop_k_per_trip(0, nch, score_body, 0, k=4)

    n_chunks = planes_ref.shape[0]

    def bit_body(p, carry):
        alive, need, res = carry
        ones = [a & planes_ref[c, p] for c, a in enumerate(alive)]
        cnt = ones[0] * 0
        for o in ones:
            cnt = cnt + lax.population_count(o)
        cnt = jnp.sum(cnt, axis=0, keepdims=True)
        take = cnt >= need
        alive = [jnp.where(take, o, a ^ o) for o, a in zip(ones, alive)]
        need = jnp.where(take, need, need - cnt)
        res = jnp.where(take, res | lax.shift_left(jnp.int32(1), 31 - p), res)
        return alive, need, res

    for g in range(0, hw, gw):
        sa_ref[0:kc, g:g + gw] = lax.dot_general(kf_ref[0:kc, (g // gw) * 2 * LANES:(g // gw + 1) * 2 * LANES],
                                                 qbd_ref[g // gw], _NT, preferred_element_type=F32)

    carry = ([jnp.where(c < nch, jnp.full((8, qb), -1, I32), 0) for c in range(n_chunks)],
             jnp.full((1, qb), topk, I32), jnp.zeros((1, qb), I32))
    for p in range(32):
        carry = bit_body(p, carry)
    thr = jnp.maximum(carry[2] ^ INT_MIN, INT_MIN + 1)

    acc_ref[...] = jnp.zeros(acc_ref.shape, F32)
    bias0 = jnp.where(keys_ref[0:kc, :] >= thr, 0.0, NEG_INF)
    bias0 = jnp.concatenate([bias0, bias0], axis=1)
    for g in range(0, hw, gw):
        t = sa_ref[0:kc, g:g + gw] + bias0
        sa_ref[0:kc, g:g + gw] = t
        sa_ref[kc:kc + 1, g:g + gw] = jnp.max(t, axis=0, keepdims=True)

    def logits(c, s_ref):
        off = pl.multiple_of(c * kc, kc)
        bias = jnp.where(keys_ref[pl.ds(off, kc), :] >= thr, 0.0, NEG_INF)
        bias = jnp.concatenate([bias, bias], axis=1)
        for g in range(0, hw, gw):
            t = lax.dot_general(kf_ref[pl.ds(off, kc), (g // gw) * 2 * LANES:(g // gw + 1) * 2 * LANES],
                                qbd_ref[g // gw], _NT, preferred_element_type=F32) + bias
            s_ref[0:kc, g:g + gw] = t
            s_ref[kc:kc + 1, g:g + gw] = jnp.max(t, axis=0, keepdims=True)

    def consume(s_ref, c, carry):
        m_prev, l_prev = carry
        ct = ct_ref[c]
        m_out, l_out = [], []
        for g in range(0, hw, gw):
            s = s_ref[0:kc, g:g + gw]
            m_new = jnp.maximum(m_prev[:, g:g + gw], s_ref[kc:kc + 1, g:g + gw])
            alpha = jnp.exp2(m_prev[:, g:g + gw] - m_new)
            p = jnp.exp2(s - m_new)
            l_out.append(alpha * l_prev[:, g:g + gw] + jnp.sum(p, axis=0, keepdims=True))
            m_out.append(m_new)
            acc_ref[:, g:g + gw] = alpha * acc_ref[:, g:g + gw] + jnp.dot(
                ct, p.astype(BF16), preferred_element_type=F32)
        return jnp.concatenate(m_out, axis=1), jnp.concatenate(l_out, axis=1)

    def pair_body(j, carry):
        c0 = 2 * j
        logits(c0 + 1, sb_ref)
        carry = consume(sa_ref, c0, carry)
        logits(jnp.minimum(c0 + 2, nch - 1), sa_ref)
        return consume(sb_ref, c0 + 1, carry)

    carry = _loop_k_per_trip(0, nch // 2, pair_body,
                             (jnp.full((1, hw), NEG_INF, F32), jnp.zeros((1, hw), F32)), k=2)
    _, l_fin = lax.fori_loop(0, nch % 2, lambda _, c: consume(sa_ref, nch - 1, c), carry)

    inv_l = 1.0 / l_fin
    for h in range(A_HEADS):
        cols = slice(h * qb, (h + 1) * qb)
        o_lat = (acc_ref[:, cols] * inv_l[:, cols]).T.astype(BF16)
        o_ref[:, h * LANES:(h + 1) * LANES] = jnp.dot(o_lat, wuv_ref[h], preferred_element_type=F32).astype(BF16)


def _dsa(q, qi, wi, kf, ct, ki, wuv):
    B, S, _ = q.shape
    topk = min(TOPK_MAX, S // 4)
    qb = DSA_BLOCK
    blk = lambda width: pl.BlockSpec((None, qb, width), lambda b, i: (b, i, 0))
    seq = lambda width: pl.BlockSpec((None, S, width), lambda b, i: (b, 0, 0))
    const3 = lambda shape: pl.BlockSpec(shape, lambda b, i: (0, 0, 0))
    return pl.pallas_call(
        functools.partial(_dsa_kernel, topk=topk),
        grid=(B, S // qb),
        in_specs=[blk(A_HEADS * A_QK_DIM), blk(IDX_HEADS * IDX_DIM), blk(LANES),
                  seq(A_HEADS * A_QK_DIM),
                  pl.BlockSpec((None, S // KEY_CHUNK, A_KV_RANK, KEY_CHUNK), lambda b, i: (b, 0, 0, 0)),
                  pl.BlockSpec((None, S // KEY_CHUNK, 2 * KEY_CHUNK, LANES), lambda b, i: (b, 0, 0, 0)),
                  const3(wuv.shape)],
        out_specs=blk(A_HEADS * LANES),
        out_shape=jax.ShapeDtypeStruct((B, S, A_HEADS * LANES), BF16),
        scratch_shapes=[
            pltpu.VMEM((S, qb), I32),
            pltpu.VMEM((S // KEY_CHUNK, 32, 8, qb), I32),
            pltpu.VMEM((A_HEADS // 2, 2 * qb, 2 * LANES), BF16),
            pltpu.VMEM((A_HEADS * qb, LANES), BF16),
            pltpu.VMEM((8, A_HEADS * qb), F32),
            pltpu.VMEM((A_KV_RANK, A_HEADS * qb), F32),
            pltpu.VMEM((KEY_CHUNK + 8, A_HEADS * qb), F32),
            pltpu.VMEM((KEY_CHUNK + 8, A_HEADS * qb), F32),
        ],
        compiler_params=_cparams(("parallel", "arbitrary")),
        name="dsa",
    )(q, qi, wi, kf, ct, ki, wuv)


def _window_bias(branches, nrel):
    rel = jnp.arange(nrel, dtype=I32)[:, None, None]
    krow = jnp.arange(KEY_CHUNK, dtype=I32)[None, :, None]
    qcol = jnp.arange(KEY_CHUNK, dtype=I32)[None, None, :]
    dist = KEY_CHUNK * (nrel - 1 - rel) + qcol - krow
    mult = sum(((dist >= 0) & (dist <= window) & (dist % dil == 0)).astype(F32) for window, dil in branches)
    return jnp.where(mult > 0, jnp.log2(jnp.maximum(mult, 1.0)), NEG_INF)


def _wattn_kernel(q_ref, k_ref, vt_ref, bias_ref, o_ref, acc_ref, sa_ref, sb_ref, p_ref, *, nh, nrel):
    kc = qb = KEY_CHUNK
    top = pl.program_id(1)
    c_lo = jnp.maximum(top - (nrel - 1), 0)
    n = top + 1 - c_lo

    acc_ref[...] = jnp.zeros(acc_ref.shape, F32)

    def logits(c, s_ref):
        off = pl.multiple_of(c * kc, kc)
        bias = bias_ref[c - top + (nrel - 1)]
        for h in range(nh):
            cols = slice(h * LANES, (h + 1) * LANES)
            t = lax.dot_general(k_ref[pl.ds(off, kc), cols], q_ref[:, cols], _NT,
                                preferred_element_type=F32) + bias
            s_ref[0:kc, h * qb:(h + 1) * qb] = t
            s_ref[kc:kc + 1, h * qb:(h + 1) * qb] = jnp.max(t, axis=0, keepdims=True)

    def consume(s_ref, c, carry):
        m_prev, l_prev = carry
        m_new = jnp.maximum(m_prev, s_ref[kc:kc + 1, :])
        alpha = jnp.exp2(m_prev - m_new)
        p_ref[...] = jnp.exp2(s_ref[0:kc, :] - m_new).astype(BF16)
        l_new = alpha * l_prev + jnp.dot(jnp.ones((16, kc), BF16), p_ref[...], preferred_element_type=F32)[0:1, :]
        for h in range(nh):
            cols = slice(h * qb, (h + 1) * qb)
            acc_ref[:, cols] = alpha[:, cols] * acc_ref[:, cols] + jnp.dot(
                vt_ref[c, h * LANES:(h + 1) * LANES, :], p_ref[:, cols], preferred_element_type=F32)
        return m_new, l_new

    logits(c_lo, sa_ref)

    def pair_body(j, carry):
        c0 = c_lo + 2 * j
        logits(c0 + 1, sb_ref)
        carry = consume(sa_ref, c0, carry)
        logits(jnp.minimum(c0 + 2, top), sa_ref)
        return consume(sb_ref, c0 + 1, carry)

    carry = _loop_k_per_trip(0, n // 2, pair_body,
                             (jnp.full((1, nh * qb), NEG_INF, F32), jnp.zeros((1, nh * qb), F32)), k=4)
    _, l_fin = lax.fori_loop(0, n % 2, lambda _, c: consume(sa_ref, top, c), carry)

    inv_l = 1.0 / l_fin
    for h in range(nh):
        cols = slice(h * qb, (h + 1) * qb)
        o_ref[:, h * LANES:(h + 1) * LANES] = (acc_ref[:, cols] * inv_l[:, cols]).T.astype(BF16)


def _swa_kernel(sink_ref, q_ref, kp_ref, kc_ref, vp_ref, vc_ref, o_ref, *, subheads, max_dist):
    n = pl.program_id(1)
    qi = lax.broadcasted_iota(I32, (BLOCK, 2 * BLOCK), 0)
    kj = lax.broadcasted_iota(I32, (BLOCK, 2 * BLOCK), 1)
    dist = BLOCK + qi - kj
    mask = (dist >= 0) & (dist <= max_dist) & ((kj >= BLOCK) | (n > 0))
    flat = [(j, kt, vt, hidx) for j, subs in enumerate(subheads) for kt, vt, hidx in subs]

    def logits(j, kt):
        kk = jnp.concatenate([kp_ref[:, kt * LANES:(kt + 1) * LANES],
                              kc_ref[:, kt * LANES:(kt + 1) * LANES]], axis=0)
        return lax.dot_general(q_ref[:, j * LANES:(j + 1) * LANES], kk, _NT, preferred_element_type=F32)

    tiles = [None] * len(subheads)
    nxt = logits(flat[0][0], flat[0][1])
    for e, (j, kt, vt, hidx) in enumerate(flat):
        s = jnp.where(mask, nxt, NEG_INF)
        if e + 1 < len(flat):
            nxt = logits(flat[e + 1][0], flat[e + 1][1])
        vv = jnp.concatenate([vp_ref[:, vt * LANES:(vt + 1) * LANES],
                              vc_ref[:, vt * LANES:(vt + 1) * LANES]], axis=0)
        sk = sink_ref[hidx] * LOG2E
        m = jnp.maximum(jnp.max(s, axis=-1, keepdims=True), sk)
        p = jnp.exp2(s - m)
        l = jnp.sum(p, axis=-1, keepdims=True) + jnp.exp2(sk - m)
        o_s = jnp.dot(p.astype(BF16), vv, preferred_element_type=F32) / l
        tiles[j] = o_s if tiles[j] is None else tiles[j] + o_s
    for j, o_t in enumerate(tiles):
        o_ref[:, j * LANES:(j + 1) * LANES] = o_t.astype(BF16)


def _swa(q, k, v, sinks, *, subheads, max_dist):
    B, S, wq = q.shape
    wk, wv = k.shape[-1], v.shape[-1]
    cur = lambda width: pl.BlockSpec((None, BLOCK, width), lambda b, i: (b, i, 0))
    prv = lambda width: pl.BlockSpec((None, BLOCK, width), lambda b, i: (b, jnp.maximum(i - 1, 0), 0))
    return pl.pallas_call(
        functools.partial(_swa_kernel, subheads=subheads, max_dist=max_dist),
        grid=(B, S // BLOCK),
        in_specs=[pl.BlockSpec(memory_space=pltpu.SMEM), cur(wq), prv(wk), cur(wk), prv(wv), cur(wv)],
        out_specs=cur(wq),
        out_shape=jax.ShapeDtypeStruct((B, S, wq), BF16),
        compiler_params=_cparams(("parallel", "arbitrary")),
        name="swa",
    )(sinks, q, k, k, v, v)


def _wattn(q, k, vt, branches):
    B, S, wq = q.shape
    widest = max(w for w, _ in branches)
    nrel = min(-(-widest // KEY_CHUNK) + 1, S // KEY_CHUNK)
    bias = _window_bias(branches, nrel)
    nh = wq // LANES
    blk = lambda width: pl.BlockSpec((None, KEY_CHUNK, width), lambda b, i: (b, i, 0))
    return pl.pallas_call(
        functools.partial(_wattn_kernel, nh=nh, nrel=nrel),
        grid=(B, S // KEY_CHUNK),
        in_specs=[blk(wq),
                  pl.BlockSpec((None, S, wq), lambda b, i: (b, 0, 0)),
                  pl.BlockSpec((None,) + vt.shape[1:], lambda b, i: (b, 0, 0, 0)),
                  pl.BlockSpec(bias.shape, lambda b, i: (0, 0, 0))],
        out_specs=blk(wq),
        out_shape=jax.ShapeDtypeStruct((B, S, wq), BF16),
        scratch_shapes=[pltpu.VMEM((LANES, nh * KEY_CHUNK), F32),
                        pltpu.VMEM((KEY_CHUNK + 8, nh * KEY_CHUNK), F32),
                        pltpu.VMEM((KEY_CHUNK + 8, nh * KEY_CHUNK), F32),
                        pltpu.VMEM((KEY_CHUNK, nh * KEY_CHUNK), BF16)],
        compiler_params=_cparams(("parallel", "arbitrary")),
        name="wattn",
    )(q, k, vt, bias)


def _memkv_kernel(mem_ref, g_ref, w_ref, k_ref, v_ref):
    hb = _rms(mem_ref[...], g_ref[...]).astype(BF16)
    y = jnp.dot(hb, w_ref[...], preferred_element_type=F32)
    lane = lax.broadcasted_iota(I32, (y.shape[0], LANES), 1)
    for t in range(MEM_WIDTH // LANES):
        for out_ref, base in ((k_ref, 0), (v_ref, MEM_WIDTH)):
            tile = y[:, base + t * LANES:base + (t + 1) * LANES]
            out_ref[:, (2 * t) * LANES:(2 * t + 1) * LANES] = jnp.where(lane < MEM_HEAD_DIM, tile, 0.0).astype(BF16)
            out_ref[:, (2 * t + 1) * LANES:(2 * t + 2) * LANES] = jnp.where(lane >= MEM_HEAD_DIM, tile, 0.0).astype(BF16)


def _memkv(mem, g_mem, w):
    B, M, D = mem.shape
    L = w.shape[0]
    out = jax.ShapeDtypeStruct((L, B, M, 2 * MEM_WIDTH), BF16)
    ospec = pl.BlockSpec((None, None, M, 2 * MEM_WIDTH), lambda l, b: (l, b, 0, 0))
    return pl.pallas_call(
        _memkv_kernel,
        grid=(L, B),
        in_specs=[pl.BlockSpec((None, M, D), lambda l, b: (b, 0, 0)),
                  pl.BlockSpec((1, D), lambda l, b: (0, 0)),
                  pl.BlockSpec((None, D, 2 * MEM_WIDTH), lambda l, b: (l, 0, 0))],
        out_specs=(ospec, ospec),
        out_shape=(out, out),
        compiler_params=_cparams(("parallel", "parallel")),
        name="memkv",
    )(mem, g_mem, w)


def _mixer_residual(x_ref, mix_ref, qm_ref, mk_ref, mv_ref, w_ref):
    mw = mix_ref.shape[-1]
    n_sub = 2 * MEM_WIDTH // LANES
    logits = [lax.dot_general(qm_ref[:, (u // 2) * LANES:(u // 2 + 1) * LANES], mk_ref[:, u * LANES:(u + 1) * LANES],
                              _NT, preferred_element_type=F32) for u in range(n_sub)]
    y = x_ref[...] + jnp.dot(mix_ref[...], w_ref[0:mw, :], preferred_element_type=F32)
    tiles = []
    for t in range(n_sub // 2):
        o_t = None
        for u in (2 * t, 2 * t + 1):
            p = jnp.exp2(logits[u] - jnp.max(logits[u], axis=-1, keepdims=True))
            l = jnp.sum(p, axis=-1, keepdims=True)
            o_s = jnp.dot(p.astype(BF16), mv_ref[:, u * LANES:(u + 1) * LANES], preferred_element_type=F32) / l
            o_t = o_s if o_t is None else o_t + o_s
        tiles.append(o_t.astype(BF16))
    return y + jnp.dot(jnp.concatenate(tiles, axis=1), w_ref[mw:mw + MEM_WIDTH, :], preferred_element_type=F32)


def _out_kernel(x_ref, mix_ref, qm_ref, mk_ref, mv_ref, w_ref, o_ref):
    o_ref[...] = _mixer_residual(x_ref, mix_ref, qm_ref, mk_ref, mv_ref, w_ref)


def _out_proj(x, mix, qm, mk, mv, w, tm):
    B, S, D = x.shape
    row = lambda width: pl.BlockSpec((None, tm, width), lambda b, i: (b, i, 0))
    mem = pl.BlockSpec((None, mk.shape[1], 2 * MEM_WIDTH), lambda b, i: (b, 0, 0))
    return pl.pallas_call(
        _out_kernel,
        grid=(B, S // tm),
        in_specs=[row(D), row(mix.shape[-1]), row(MEM_WIDTH), mem, mem,
                  pl.BlockSpec(w.shape, lambda b, i: (0, 0))],
        out_specs=row(D),
        out_shape=jax.ShapeDtypeStruct((B, S, D), F32),
        compiler_params=_cparams(("parallel", "parallel")),
        name="out_proj",
    )(x, mix, qm, mk, mv, w)


def _ffn_kernel(x_ref, g_ref, wup_ref, cw_ref, cb_ref, wdn_ref, gf_ref, o_ref, carry_ref, *, tm, cw, final):
    dff = wdn_ref.shape[0]
    hm = min(tm, 2 * LANES)

    @pl.when(pl.program_id(1) == 0)
    def _():
        carry_ref[...] = jnp.zeros(carry_ref.shape, F32)

    row = lax.broadcasted_iota(I32, (hm, cw), 0)
    xs = [x_ref[r:r + hm, :] for r in range(0, tm, hm)]
    hbs = [_rms(x, g_ref[...]).astype(BF16) for x in xs]
    accs = list(xs)

    def up(hb, c0):
        return (jnp.dot(hb, wup_ref[:, c0:c0 + cw], preferred_element_type=F32),
                jnp.dot(hb, wup_ref[:, dff + c0:dff + c0 + cw], preferred_element_type=F32))

    def gate(a, b, prev, c0):
        p1, p2 = prev[7:8, :], prev[6:7, :]
        a1 = jnp.where(row == 0, p1, pltpu.roll(a, 1, 0))
        a2 = jnp.where(row == 0, p2, jnp.where(row == 1, p1, pltpu.roll(a, 2, 0)))
        w = cw_ref[:, c0:c0 + cw]
        conv = w[0:1, :] * a2 + w[1:2, :] * a1 + w[2:3, :] * a + cb_ref[:, c0:c0 + cw]
        return (conv / (1.0 + jnp.exp(-conv)) * b).astype(BF16)

    nxt = [up(hb, 0) for hb in hbs]
    for c0 in range(0, dff, cw):
        cur = nxt
        if c0 + cw < dff:
            nxt = [up(hb, c0 + cw) for hb in hbs]
        prev = carry_ref[:, c0:c0 + cw]
        for k, (a, b) in enumerate(cur):
            accs[k] = accs[k] + jnp.dot(gate(a, b, prev, c0), wdn_ref[c0:c0 + cw, :], preferred_element_type=F32)
            prev = a[hm - 8:hm, :]
        carry_ref[:, c0:c0 + cw] = prev
    for k, acc in enumerate(accs):
        o_ref[k * hm:(k + 1) * hm, :] = _rms(acc, gf_ref[...]) if final else acc


def _ffn(x, g, wup, cw, cb, wdn, gf, tm, final):
    B, S, D = x.shape
    dff = wdn.shape[0]
    row = pl.BlockSpec((None, tm, D), lambda b, i: (b, i, 0))
    const = lambda shape: pl.BlockSpec(shape, lambda b, i: (0, 0), pipeline_mode=pl.Buffered(1))
    return pl.pallas_call(
        functools.partial(_ffn_kernel, tm=tm, cw=2 * LANES, final=final),
        grid=(B, S // tm),
        in_specs=[row, const((1, D)), const(wup.shape), const(cw.shape), const((1, dff)), const(wdn.shape),
                  const((1, D))],
        out_specs=row,
        out_shape=jax.ShapeDtypeStruct((B, S, D), F32),
        scratch_shapes=[pltpu.VMEM((8, dff), F32)],
        compiler_params=_cparams(("arbitrary", "arbitrary")),
        name="ffn",
    )(x, g, wup, cw, cb, wdn, gf)


def _b_subheads():
    return tuple(((2 * (j // 2), 2 * (j // 2), 2 * j), (2 * (j // 2) + 1, 2 * (j // 2) + 1, 2 * j + 1))
                 for j in range(B_HEADS // 2))


def kernel(x, mem, positions, g_mix, g_ffn, g_mem, g_final, w_mem_kv, a_w_in, a_kv_norm, a_w_uk, a_w_uv, a_w_out,
           b_w_in, b_sinks, b_w_out, c_w_in, c_w_out, f_w_up, f_conv_w, f_conv_b, f_w_down):
    B, S, D = x.shape
    depth = g_mix.shape[0]
    tm = min(512, S)
    tab32 = _rope_table(positions, A_QK_DIM, A_ROPE_DIM)
    tab16 = _rope_table(positions, B_HEAD_DIM, B_HEAD_DIM // 4)
    mem_k, mem_v = _memkv(mem, g_mem.reshape(1, D), w_mem_kv.astype(BF16))
    conv_w = jnp.pad(f_conv_w, ((0, 0), (0, 8 - CONV_WIDTH), (0, 0)))
    for i in range(depth):
        kind, j = i % 3, i // 3
        g = g_mix[i].reshape(1, D)
        if kind == 0:
            wuk = jnp.pad(a_w_uk[j], ((0, 0), (0, 0), (A_ROPE_DIM, 0))).reshape(A_KV_RANK, -1).astype(BF16)
            wuv = jnp.transpose(a_w_uv[j], (1, 0, 2)).astype(BF16)
            q, kf, ct, qi, ki, wi, qm = _proj_a(x, g, _prep_a_w_in(a_w_in[j]), a_kv_norm[j].reshape(1, -1), wuk,
                                                tab32, tab16, tm)
            mix = _dsa(q, qi, wi, kf, ct, ki, wuv)
            w_out = a_w_out[j]
        elif kind == 1:
            nq, nkv = B_HEADS * B_HEAD_DIM, 4 * B_KV_HEADS * B_HEAD_DIM
            q, k, v, qm = _proj_qkv(x, g, _prep_b_w_in(b_w_in[j]), tab16, tm, B_HEAD_DIM, B_HEAD_DIM // 4,
                                    nq, nkv, nkv, False)
            mix = _swa(q, k, v, b_sinks[j], subheads=_b_subheads(), max_dist=B_WINDOW - 1)
            w_out = b_w_out[j]
        else:
            nq = C_HEADS * C_HEAD_DIM
            q, k, v, qm = _proj_qkv(x, g, c_w_in[j].astype(BF16), tab32, tm, C_HEAD_DIM, C_HEAD_DIM // 4,
                                    nq, nq, nq, True)
            mix = _wattn(q, k, v, C_BRANCHES)
            w_out = c_w_out[j]
        x = _out_proj(x, mix, qm, mem_k[i], mem_v[i], w_out.astype(BF16), tm)
        x = _ffn(x, g_ffn[i].reshape(1, D), f_w_up[i].astype(BF16), conv_w[i], f_conv_b[i].reshape(1, -1),
                 f_w_down[i].astype(BF16), g_final.reshape(1, D), tm, i == depth - 1)
    return x
```

```python
import functools

import jax
import jax.numpy as jnp
import numpy as np
from jax import lax
from jax.experimental import pallas as pl
from jax.experimental.pallas import tpu as pltpu

F32 = jnp.float32
BF16 = jnp.bfloat16
I32 = jnp.int32

LANES = 128
BLOCK = 128
ROPE_THETA = 500000.0
EPS = 1e-6
NEG_INF = -1e30
LOG2E = 1.4426950408889634
INT_MIN = -(2**31)

A_HEADS = 8
A_QK_DIM = 128
A_ROPE_DIM = 32
A_KV_RANK = 256
IDX_HEADS = 16
IDX_DIM = 64
TOPK_MAX = 256
B_HEADS = 16
B_KV_HEADS = 4
B_HEAD_DIM = 64
B_WINDOW = 128
C_HEADS = 8
C_HEAD_DIM = 128
C_BRANCHES = ((128, 1), (512, 4), (2048, 16))
MEM_HEADS = 4
MEM_HEAD_DIM = 64
MEM_WIDTH = MEM_HEADS * MEM_HEAD_DIM
CONV_WIDTH = 3
KEY_CHUNK = 256

VMEM_LIMIT = 56 * 1024 * 1024

_NT = (((1,), (1,)), ((), ()))


def _cparams(sem):
    return pltpu.CompilerParams(dimension_semantics=sem, vmem_limit_bytes=VMEM_LIMIT)


def _rms(x, g):
    return x * lax.rsqrt(jnp.mean(x * x, axis=-1, keepdims=True) + EPS) * g


def _loop_k_per_trip(lo, hi, body, carry, k=2):
    if k == 1:
        return lax.fori_loop(lo, hi, body, carry)
    trips = (hi - lo) // k

    def several(t, c):
        for u in range(k):
            c = body(lo + k * t + u, c)
        return c

    carry = lax.fori_loop(0, trips, several, carry)
    return _loop_k_per_trip(lo + k * trips, hi, body, carry, k // 2)


def _bit_transpose32(words):
    a = list(words)
    j, m = 16, 0x0000FFFF
    while j:
        mask = jnp.int32(m - (1 << 32) if m >= (1 << 31) else m)
        for k in range(32):
            if k & j == 0:
                t = (a[k] ^ lax.shift_right_logical(a[k + j], j)) & mask
                a[k] = a[k] ^ t
                a[k + j] = a[k + j] ^ lax.shift_left(t, j)
        j >>= 1
        m = (m ^ (m << j)) & 0xFFFFFFFF
    return a


def _rope_tile(t, tab, half):
    c, sa, sb = tab[:, 0:LANES], tab[:, LANES:2 * LANES], tab[:, 2 * LANES:3 * LANES]
    return t * c + pltpu.roll(t, half, 1) * sa + pltpu.roll(t, LANES - half, 1) * sb


def _rope_table(positions, head_dim, rot):
    half = rot // 2
    inv = ROPE_THETA ** (-jnp.arange(0, rot, 2, dtype=F32) / rot)
    ang = positions.astype(F32)[..., None] * inv
    cs = jnp.concatenate([jnp.cos(ang), jnp.sin(ang)], axis=-1)
    sel = np.zeros((rot, 3 * LANES), np.float32)
    one = np.zeros((3 * LANES,), np.float32)
    for l in range(LANES):
        j = l % head_dim
        if j < half:
            sel[j, l] = 1.0
            sel[half + j, 2 * LANES + l] = -1.0
        elif j < rot:
            sel[j - half, l] = 1.0
            sel[j, LANES + l] = 1.0
        else:
            one[l] = 1.0
    return jnp.dot(cs, jnp.asarray(sel), precision=lax.Precision.HIGHEST) + jnp.asarray(one)


def _staggered(hbs, w_ref, jobs):
    def mms(lo, hi):
        return [jnp.dot(hb, w_ref[:, lo:hi], preferred_element_type=F32) for hb in hbs]

    nxt = mms(jobs[0][0], jobs[0][1])
    for n, (_, _, epilogue) in enumerate(jobs):
        cur = nxt
        if n + 1 < len(jobs):
            nxt = mms(jobs[n + 1][0], jobs[n + 1][1])
        for part, y in enumerate(cur):
            epilogue(y, part)


def _proj_a_kernel(x_ref, g_ref, w_ref, kvn_ref, wuk_ref, t32_ref, t16_ref,
                   q_ref, kf_ref, ct_ref, qi_ref, ki_ref, wi_ref, qm_ref, *, tm):
    hm = KEY_CHUNK
    rows = [slice(r, r + hm) for r in range(0, tm, hm)]
    hbs = [_rms(x_ref[r, :], g_ref[...]).astype(BF16) for r in rows]

    def roped(ref, col, tab_ref, half, scale):
        def epilogue(y, part):
            tab = tab_ref[rows[part], :]
            for u in range(y.shape[1] // LANES):
                t = _rope_tile(y[:, u * LANES:(u + 1) * LANES], tab, half)
                if scale != 1.0:
                    t = t * scale
                ref[rows[part], col + u * LANES:col + (u + 1) * LANES] = t.astype(BF16)
        return epilogue

    latent_bf16 = {}

    def latent(y, part):
        c = _rms(y, kvn_ref[...])
        latent_bf16[part] = c.astype(BF16)
        ct_ref[part] = c.T.astype(BF16)

    def full_keys(y, part):
        k_rope = _rope_tile(y, t32_ref[rows[part], :], A_ROPE_DIM // 2)
        k_nope = jnp.dot(latent_bf16[part], wuk_ref[...], preferred_element_type=F32)
        for h in range(A_HEADS):
            kf_ref[rows[part], h * LANES:(h + 1) * LANES] = (
                k_nope[:, h * LANES:(h + 1) * LANES] + k_rope).astype(BF16)

    def index_keys(y, part):
        tab = t16_ref[rows[part], :]
        for u in range(2):
            ki_ref[part, u * hm:(u + 1) * hm, :] = _rope_tile(
                y[:, u * LANES:(u + 1) * LANES], tab, IDX_DIM // 8).astype(BF16)

    def index_weights(y, part):
        wi_ref[rows[part], :] = y * (IDX_HEADS * IDX_DIM) ** -0.5

    def mem_query(y, part):
        qm_ref[rows[part], :] = (y * (MEM_HEAD_DIM ** -0.5 * LOG2E)).astype(BF16)

    jobs, o = [], 0
    for j in range(0, A_HEADS * A_QK_DIM, 2 * LANES):
        jobs.append((o + j, o + j + 2 * LANES,
                     roped(q_ref, j, t32_ref, A_ROPE_DIM // 2, A_QK_DIM ** -0.5 * LOG2E)))
    o += A_HEADS * A_QK_DIM
    jobs.append((o, o + A_KV_RANK, latent))
    o += A_KV_RANK
    jobs.append((o, o + LANES, full_keys))
    o += LANES
    for j in range(0, IDX_HEADS * IDX_DIM, 2 * LANES):
        jobs.append((o + j, o + j + 2 * LANES, roped(qi_ref, j, t16_ref, IDX_DIM // 8, 1.0)))
    o += IDX_HEADS * IDX_DIM
    jobs.append((o, o + 2 * LANES, index_keys))
    o += 2 * LANES
    jobs.append((o, o + LANES, index_weights))
    o += LANES
    jobs.append((o, o + MEM_WIDTH, mem_query))
    _staggered(hbs, w_ref, jobs)


def _prep_a_w_in(w):
    d = w.shape[0]
    sizes = (A_HEADS * A_QK_DIM, A_KV_RANK, A_ROPE_DIM, IDX_HEADS * IDX_DIM, IDX_DIM, IDX_HEADS, MEM_WIDTH)
    offs = [0]
    for s in sizes:
        offs.append(offs[-1] + s)
    q, ckv, kr, qi, ki, wi, qm = [w[:, offs[i]:offs[i + 1]] for i in range(len(sizes))]
    z = lambda n: jnp.zeros((d, n), w.dtype)
    return jnp.concatenate([
        q, ckv, kr, z(LANES - A_ROPE_DIM), qi,
        ki, z(LANES - IDX_DIM), z(LANES - IDX_DIM), ki,
        wi, z(LANES - IDX_HEADS), qm], axis=1).astype(BF16)


def _proj_a(x, g, w, kvn, wuk, t32, t16, tm):
    B, S, D = x.shape
    n = w.shape[1]
    row = lambda width: pl.BlockSpec((None, tm, width), lambda b, i: (b, i, 0))
    const = lambda shape: pl.BlockSpec(shape, lambda b, i: (0,) * len(shape))
    out_shape = (
        jax.ShapeDtypeStruct((B, S, A_HEADS * A_QK_DIM), BF16),
        jax.ShapeDtypeStruct((B, S, A_HEADS * A_QK_DIM), BF16),
        jax.ShapeDtypeStruct((B, S // KEY_CHUNK, A_KV_RANK, KEY_CHUNK), BF16),
        jax.ShapeDtypeStruct((B, S, IDX_HEADS * IDX_DIM), BF16),
        jax.ShapeDtypeStruct((B, S // KEY_CHUNK, 2 * KEY_CHUNK, LANES), BF16),
        jax.ShapeDtypeStruct((B, S, LANES), F32),
        jax.ShapeDtypeStruct((B, S, MEM_WIDTH), BF16),
    )
    out_specs = (
        row(A_HEADS * A_QK_DIM), row(A_HEADS * A_QK_DIM),
        pl.BlockSpec((None, tm // KEY_CHUNK, A_KV_RANK, KEY_CHUNK), lambda b, i: (b, i, 0, 0)),
        row(IDX_HEADS * IDX_DIM),
        pl.BlockSpec((None, tm // KEY_CHUNK, 2 * KEY_CHUNK, LANES), lambda b, i: (b, i, 0, 0)),
        row(LANES), row(MEM_WIDTH),
    )
    return pl.pallas_call(
        functools.partial(_proj_a_kernel, tm=tm),
        grid=(B, S // tm),
        in_specs=[row(D), const((1, D)), const((D, n)), const((1, A_KV_RANK)), const(wuk.shape),
                  row(3 * LANES), row(3 * LANES)],
        out_specs=out_specs,
        out_shape=out_shape,
        compiler_params=_cparams(("parallel", "parallel")),
        name="proj_a",
    )(x, g, w, kvn, wuk, t32, t16)


def _proj_qkv_kernel(x_ref, g_ref, w_ref, tab_ref, q_ref, k_ref, v_ref, qm_ref, *,
                     head_dim, rot, nq, nk, nv, v_transposed):
    tm = x_ref.shape[0]
    hm = KEY_CHUNK
    rows = [slice(r, r + hm) for r in range(0, tm, hm)]
    hbs = [_rms(x_ref[r, :], g_ref[...]).astype(BF16) for r in rows]

    def roped(ref, col, scale):
        def epilogue(y, part):
            tab = tab_ref[rows[part], :]
            for u in range(2):
                t = _rope_tile(y[:, u * LANES:(u + 1) * LANES], tab, rot // 2)
                if scale != 1.0:
                    t = t * scale
                ref[rows[part], col + u * LANES:col + (u + 1) * LANES] = t.astype(BF16)
        return epilogue

    def value(col):
        def epilogue(y, part):
            if v_transposed:
                v_ref[part, col:col + 2 * LANES, :] = jnp.swapaxes(y[None], 1, 2)[0].astype(BF16)
            else:
                v_ref[rows[part], col:col + 2 * LANES] = y.astype(BF16)
        return epilogue

    def mem_query(y, part):
        qm_ref[rows[part], :] = (y * (MEM_HEAD_DIM ** -0.5 * LOG2E)).astype(BF16)

    jobs = [(j, j + 2 * LANES, roped(q_ref, j, head_dim ** -0.5 * LOG2E)) for j in range(0, nq, 2 * LANES)]
    jobs += [(nq + j, nq + j + 2 * LANES, roped(k_ref, j, 1.0)) for j in range(0, nk, 2 * LANES)]
    jobs += [(nq + nk + j, nq + nk + j + 2 * LANES, value(j)) for j in range(0, nv, 2 * LANES)]
    jobs.append((nq + nk + nv, nq + nk + nv + MEM_WIDTH, mem_query))
    _staggered(hbs, w_ref, jobs)


def _proj_qkv(x, g, w, tab, tm, head_dim, rot, nq, nk, nv, v_transposed):
    B, S, D = x.shape
    n = w.shape[1]
    row = lambda width: pl.BlockSpec((None, tm, width), lambda b, i: (b, i, 0))
    const = lambda shape: pl.BlockSpec(shape, lambda b, i: (0,) * len(shape))
    if v_transposed:
        v_spec = pl.BlockSpec((None, tm // KEY_CHUNK, nv, KEY_CHUNK), lambda b, i: (b, i, 0, 0))
        v_shape = jax.ShapeDtypeStruct((B, S // KEY_CHUNK, nv, KEY_CHUNK), BF16)
    else:
        v_spec, v_shape = row(nv), jax.ShapeDtypeStruct((B, S, nv), BF16)
    return pl.pallas_call(
        functools.partial(_proj_qkv_kernel, head_dim=head_dim, rot=rot, nq=nq, nk=nk, nv=nv,
                          v_transposed=v_transposed),
        grid=(B, S // tm),
        in_specs=[row(D), const((1, D)), const((D, n)), row(3 * LANES)],
        out_specs=(row(nq), row(nk), v_spec, row(MEM_WIDTH)),
        out_shape=(jax.ShapeDtypeStruct((B, S, nq), BF16), jax.ShapeDtypeStruct((B, S, nk), BF16), v_shape,
                   jax.ShapeDtypeStruct((B, S, MEM_WIDTH), BF16)),
        compiler_params=_cparams(("parallel", "parallel")),
        name="proj_qkv",
    )(x, g, w, tab)


def _prep_b_w_in(w):
    d = w.shape[0]
    nq, nkv = B_HEADS * B_HEAD_DIM, B_KV_HEADS * B_HEAD_DIM
    q, k, v, qm = w[:, :nq], w[:, nq:nq + nkv], w[:, nq + nkv:nq + 2 * nkv], w[:, nq + 2 * nkv:]
    z = jnp.zeros((d, B_HEAD_DIM), w.dtype)

    def spread(t):
        cols = []
        for h in range(B_KV_HEADS):
            th = t[:, h * B_HEAD_DIM:(h + 1) * B_HEAD_DIM]
            cols += [th, z, z, th]
        return jnp.concatenate(cols, axis=1)

    return jnp.concatenate([q, spread(k), spread(v), qm], axis=1).astype(BF16)


DSA_BLOCK = 256


def _dsa_kernel(q_ref, qi_ref, wi_ref, kf_ref, ct_ref, ki_ref, wuv_ref, o_ref,
                keys_ref, planes_ref, qbd_ref, qi2_ref, wrow_ref, acc_ref, sa_ref, sb_ref, *, topk):
    kc = KEY_CHUNK
    qb = DSA_BLOCK
    hw = A_HEADS * qb
    gw = 2 * qb
    i = pl.program_id(1)
    nch = (i * qb + qb + kc - 1) // kc

    @pl.when(i == 0)
    def _():
        qbd_ref[...] = jnp.zeros(qbd_ref.shape, BF16)

    for h in range(A_HEADS):
        qbd_ref[h // 2, (h % 2) * qb:(h % 2 + 1) * qb, (h % 2) * LANES:(h % 2 + 1) * LANES] = (
            q_ref[:, h * LANES:(h + 1) * LANES])
        qi2_ref[h * qb:(h + 1) * qb, :] = qi_ref[:, h * LANES:(h + 1) * LANES]
    w_t = wi_ref[...].T
    for j in range(IDX_HEADS // 2):
        wrow_ref[0:1, j * qb:(j + 1) * qb] = w_t[2 * j:2 * j + 1, :]
        wrow_ref[1:2, j * qb:(j + 1) * qb] = w_t[2 * j + 1:2 * j + 2, :]

    qpos = i * qb + lax.broadcasted_iota(I32, (kc, qb), 1)
    krow = lax.broadcasted_iota(I32, (kc, qb), 0)

    def score_chunk(c):
        off = pl.multiple_of(c * kc, kc)
        kk = ki_ref[c]
        acc = None
        for g in range(0, hw, qb):
            d = lax.dot_general(kk, qi2_ref[g:g + qb, :], _NT, preferred_element_type=F32)
            t = (jnp.maximum(d[0:kc], 0.0) * wrow_ref[0:1, g:g + qb]
                 + jnp.maximum(d[kc:2 * kc], 0.0) * wrow_ref[1:2, g:g + qb])
            acc = t if acc is None else acc + t
        bits = pltpu.bitcast(acc, I32)
        key = jnp.where(bits < 0, bits ^ 0x7FFFFFFF, bits)
        key = jnp.where(off + krow <= qpos, key, INT_MIN)
        keys_ref[pl.ds(off, kc), :] = key
        planes = _bit_transpose32([(key[8 * r:8 * r + 8, :] ^ INT_MIN) for r in range(kc // 8)])
        for p in range(32):
            planes_ref[c, p] = planes[p]

    @pl.when(i == 0)
    def _():
        planes_ref[...] = jnp.zeros(planes_ref.shape, I32)

    def score_body(c, carry):
        score_chunk(c)
        return carry

    _loop_k_per_trip(0, nch, score_body, 0, k=4)

    n_chunks = planes_ref.shape[0]

    def bit_body(p, carry):
        alive, need, res = carry
        ones = [a & planes_ref[c, p] for c, a in enumerate(alive)]
        cnt = ones[0] * 0
        for o in ones:
            cnt = cnt + lax.population_count(o)
        cnt = jnp.sum(cnt, axis=0, keepdims=True)
        take = cnt >= need
        alive = [jnp.where(take, o, a ^ o) for o, a in zip(ones, alive)]
        need = jnp.where(take, need, need - cnt)
        res = jnp.where(take, res | lax.shift_left(jnp.int32(1), 31 - p), res)
        return alive, need, res

    for g in range(0, hw, gw):
        sa_ref[0:kc, g:g + gw] = lax.dot_general(kf_ref[0:kc, (g // gw) * 2 * LANES:(g // gw + 1) * 2 * LANES],
                                                 qbd_ref[g // gw], _NT, preferred_element_type=F32)

    carry = ([jnp.where(c < nch, jnp.full((8, qb), -1, I32), 0) for c in range(n_chunks)],
             jnp.full((1, qb), topk, I32), jnp.zeros((1, qb), I32))
    for p in range(32):
        carry = bit_body(p, carry)
    thr = jnp.maximum(carry[2] ^ INT_MIN, INT_MIN + 1)

    acc_ref[...] = jnp.zeros(acc_ref.shape, F32)
    bias0 = jnp.where(keys_ref[0:kc, :] >= thr, 0.0, NEG_INF)
    bias0 = jnp.concatenate([bias0, bias0], axis=1)
    for g in range(0, hw, gw):
        t = sa_ref[0:kc, g:g + gw] + bias0
        sa_ref[0:kc, g:g + gw] = t
        sa_ref[kc:kc + 1, g:g + gw] = jnp.max(t, axis=0, keepdims=True)

    def logits(c, s_ref):
        off = pl.multiple_of(c * kc, kc)
        bias = jnp.where(keys_ref[pl.ds(off, kc), :] >= thr, 0.0, NEG_INF)
        bias = jnp.concatenate([bias, bias], axis=1)
        for g in range(0, hw, gw):
            t = lax.dot_general(kf_ref[pl.ds(off, kc), (g // gw) * 2 * LANES:(g // gw + 1) * 2 * LANES],
                                qbd_ref[g // gw], _NT, preferred_element_type=F32) + bias
            s_ref[0:kc, g:g + gw] = t
            s_ref[kc:kc + 1, g:g + gw] = jnp.max(t, axis=0, keepdims=True)

    def consume(s_ref, c, carry):
        m_prev, l_prev = carry
        ct = ct_ref[c]
        m_out, l_out = [], []
        for g in range(0, hw, gw):
            s = s_ref[0:kc, g:g + gw]
            m_new = jnp.maximum(m_prev[:, g:g + gw], s_ref[kc:kc + 1, g:g + gw])
            alpha = jnp.exp2(m_prev[:, g:g + gw] - m_new)
            p = jnp.exp2(s - m_new)
            l_out.append(alpha * l_prev[:, g:g + gw] + jnp.sum(p, axis=0, keepdims=True))
            m_out.append(m_new)
            acc_ref[:, g:g + gw] = alpha * acc_ref[:, g:g + gw] + jnp.dot(
                ct, p.astype(BF16), preferred_element_type=F32)
        return jnp.concatenate(m_out, axis=1), jnp.concatenate(l_out, axis=1)

    def pair_body(j, carry):
        c0 = 2 * j
        logits(c0 + 1, sb_ref)
        carry = consume(sa_ref, c0, carry)
        logits(jnp.minimum(c0 + 2, nch - 1), sa_ref)
        return consume(sb_ref, c0 + 1, carry)

    carry = _loop_k_per_trip(0, nch // 2, pair_body,
                             (jnp.full((1, hw), NEG_INF, F32), jnp.zeros((1, hw), F32)), k=2)
    _, l_fin = lax.fori_loop(0, nch % 2, lambda _, c: consume(sa_ref, nch - 1, c), carry)

    inv_l = 1.0 / l_fin
    for h in range(A_HEADS):
        cols = slice(h * qb, (h + 1) * qb)
        out_t = jnp.dot(wuv_ref[h], (acc_ref[:, cols] * inv_l[:, cols]).astype(BF16), preferred_element_type=F32)
        o_ref[:, h * LANES:(h + 1) * LANES] = jnp.swapaxes(out_t[None], 1, 2)[0].astype(BF16)


def _dsa(q, qi, wi, kf, ct, ki, wuv):
    B, S, _ = q.shape
    topk = min(TOPK_MAX, S // 4)
    qb = DSA_BLOCK
    blk = lambda width: pl.BlockSpec((None, qb, width), lambda b, i: (b, i, 0))
    seq = lambda width: pl.BlockSpec((None, S, width), lambda b, i: (b, 0, 0))
    const3 = lambda shape: pl.BlockSpec(shape, lambda b, i: (0, 0, 0))
    return pl.pallas_call(
        functools.partial(_dsa_kernel, topk=topk),
        grid=(B, S // qb),
        in_specs=[blk(A_HEADS * A_QK_DIM), blk(IDX_HEADS * IDX_DIM), blk(LANES),
                  seq(A_HEADS * A_QK_DIM),
                  pl.BlockSpec((None, S // KEY_CHUNK, A_KV_RANK, KEY_CHUNK), lambda b, i: (b, 0, 0, 0)),
                  pl.BlockSpec((None, S // KEY_CHUNK, 2 * KEY_CHUNK, LANES), lambda b, i: (b, 0, 0, 0)),
                  const3(wuv.shape)],
        out_specs=blk(A_HEADS * LANES),
        out_shape=jax.ShapeDtypeStruct((B, S, A_HEADS * LANES), BF16),
        scratch_shapes=[
            pltpu.VMEM((S, qb), I32),
            pltpu.VMEM((S // KEY_CHUNK, 32, 8, qb), I32),
            pltpu.VMEM((A_HEADS // 2, 2 * qb, 2 * LANES), BF16),
            pltpu.VMEM((A_HEADS * qb, LANES), BF16),
            pltpu.VMEM((8, A_HEADS * qb), F32),
            pltpu.VMEM((A_KV_RANK, A_HEADS * qb), F32),
            pltpu.VMEM((KEY_CHUNK + 8, A_HEADS * qb), F32),
            pltpu.VMEM((KEY_CHUNK + 8, A_HEADS * qb), F32),
        ],
        compiler_params=_cparams(("parallel", "arbitrary")),
        name="dsa",
    )(q, qi, wi, kf, ct, ki, wuv)


def _window_bias(branches, nrel):
    rel = jnp.arange(nrel, dtype=I32)[:, None, None]
    krow = jnp.arange(KEY_CHUNK, dtype=I32)[None, :, None]
    qcol = jnp.arange(KEY_CHUNK, dtype=I32)[None, None, :]
    dist = KEY_CHUNK * (nrel - 1 - rel) + qcol - krow
    mult = sum(((dist >= 0) & (dist <= window) & (dist % dil == 0)).astype(F32) for window, dil in branches)
    return jnp.where(mult > 0, jnp.log2(jnp.maximum(mult, 1.0)), NEG_INF)


def _wattn_kernel(q_ref, k_ref, vt_ref, bias_ref, o_ref, acc_ref, sa_ref, sb_ref, p_ref, *, nh, nrel):
    kc = qb = KEY_CHUNK
    top = pl.program_id(1)
    c_lo = jnp.maximum(top - (nrel - 1), 0)
    n = top + 1 - c_lo

    acc_ref[...] = jnp.zeros(acc_ref.shape, F32)

    def logits(c, s_ref):
        off = pl.multiple_of(c * kc, kc)
        bias = bias_ref[c - top + (nrel - 1)]
        for h in range(nh):
            cols = slice(h * LANES, (h + 1) * LANES)
            t = lax.dot_general(k_ref[pl.ds(off, kc), cols], q_ref[:, cols], _NT,
                                preferred_element_type=F32) + bias
            s_ref[0:kc, h * qb:(h + 1) * qb] = t
            s_ref[kc:kc + 1, h * qb:(h + 1) * qb] = jnp.max(t, axis=0, keepdims=True)

    def consume(s_ref, c, carry):
        m_prev, l_prev = carry
        m_new = jnp.maximum(m_prev, s_ref[kc:kc + 1, :])
        alpha = jnp.exp2(m_prev - m_new)
        p_ref[...] = jnp.exp2(s_ref[0:kc, :] - m_new).astype(BF16)
        l_new = alpha * l_prev + jnp.dot(jnp.ones((16, kc), BF16), p_ref[...], preferred_element_type=F32)[0:1, :]
        for h in range(nh):
            cols = slice(h * qb, (h + 1) * qb)
            acc_ref[:, cols] = alpha[:, cols] * acc_ref[:, cols] + jnp.dot(
                vt_ref[c, h * LANES:(h + 1) * LANES, :], p_ref[:, cols], preferred_element_type=F32)
        return m_new, l_new

    logits(c_lo, sa_ref)

    def pair_body(j, carry):
        c0 = c_lo + 2 * j
        logits(c0 + 1, sb_ref)
        carry = consume(sa_ref, c0, carry)
        logits(jnp.minimum(c0 + 2, top), sa_ref)
        return consume(sb_ref, c0 + 1, carry)

    carry = _loop_k_per_trip(0, n // 2, pair_body,
                             (jnp.full((1, nh * qb), NEG_INF, F32), jnp.zeros((1, nh * qb), F32)), k=4)
    _, l_fin = lax.fori_loop(0, n % 2, lambda _, c: consume(sa_ref, top, c), carry)

    inv_l = 1.0 / l_fin
    for h in range(nh):
        cols = slice(h * qb, (h + 1) * qb)
        o_ref[:, h * LANES:(h + 1) * LANES] = (acc_ref[:, cols] * inv_l[:, cols]).T.astype(BF16)


def _swa_kernel(sink_ref, q_ref, kp_ref, kc_ref, vp_ref, vc_ref, o_ref, *, subheads, max_dist):
    n = pl.program_id(1)
    qi = lax.broadcasted_iota(I32, (BLOCK, 2 * BLOCK), 0)
    kj = lax.broadcasted_iota(I32, (BLOCK, 2 * BLOCK), 1)
    dist = BLOCK + qi - kj
    mask = (dist >= 0) & (dist <= max_dist) & ((kj >= BLOCK) | (n > 0))
    flat = [(j, kt, vt, hidx) for j, subs in enumerate(subheads) for kt, vt, hidx in subs]

    def logits(j, kt):
        kk = jnp.concatenate([kp_ref[:, kt * LANES:(kt + 1) * LANES],
                              kc_ref[:, kt * LANES:(kt + 1) * LANES]], axis=0)
        return lax.dot_general(q_ref[:, j * LANES:(j + 1) * LANES], kk, _NT, preferred_element_type=F32)

    tiles = [None] * len(subheads)
    nxt = logits(flat[0][0], flat[0][1])
    for e, (j, kt, vt, hidx) in enumerate(flat):
        s = jnp.where(mask, nxt, NEG_INF)
        if e + 1 < len(flat):
            nxt = logits(flat[e + 1][0], flat[e + 1][1])
        vv = jnp.concatenate([vp_ref[:, vt * LANES:(vt + 1) * LANES],
                              vc_ref[:, vt * LANES:(vt + 1) * LANES]], axis=0)
        sk = sink_ref[hidx] * LOG2E
        m = jnp.maximum(jnp.max(s, axis=-1, keepdims=True), sk)
        p = jnp.exp2(s - m)
        l = jnp.sum(p, axis=-1, keepdims=True) + jnp.exp2(sk - m)
        o_s = jnp.dot(p.astype(BF16), vv, preferred_element_type=F32) / l
        tiles[j] = o_s if tiles[j] is None else tiles[j] + o_s
    for j, o_t in enumerate(tiles):
        o_ref[:, j * LANES:(j + 1) * LANES] = o_t.astype(BF16)


def _swa(q, k, v, sinks, *, subheads, max_dist):
    B, S, wq = q.shape
    wk, wv = k.shape[-1], v.shape[-1]
    cur = lambda width: pl.BlockSpec((None, BLOCK, width), lambda b, i: (b, i, 0))
    prv = lambda width: pl.BlockSpec((None, BLOCK, width), lambda b, i: (b, jnp.maximum(i - 1, 0), 0))
    return pl.pallas_call(
        functools.partial(_swa_kernel, subheads=subheads, max_dist=max_dist),
        grid=(B, S // BLOCK),
        in_specs=[pl.BlockSpec(memory_space=pltpu.SMEM), cur(wq), prv(wk), cur(wk), prv(wv), cur(wv)],
        out_specs=cur(wq),
        out_shape=jax.ShapeDtypeStruct((B, S, wq), BF16),
        compiler_params=_cparams(("parallel", "arbitrary")),
        name="swa",
    )(sinks, q, k, k, v, v)


def _wattn(q, k, vt, branches):
    B, S, wq = q.shape
    widest = max(w for w, _ in branches)
    nrel = min(-(-widest // KEY_CHUNK) + 1, S // KEY_CHUNK)
    bias = _window_bias(branches, nrel)
    nh = wq // LANES
    blk = lambda width: pl.BlockSpec((None, KEY_CHUNK, width), lambda b, i: (b, i, 0))
    return pl.pallas_call(
        functools.partial(_wattn_kernel, nh=nh, nrel=nrel),
        grid=(B, S // KEY_CHUNK),
        in_specs=[blk(wq),
                  pl.BlockSpec((None, S, wq), lambda b, i: (b, 0, 0)),
                  pl.BlockSpec((None,) + vt.shape[1:], lambda b, i: (b, 0, 0, 0)),
                  pl.BlockSpec(bias.shape, lambda b, i: (0, 0, 0))],
        out_specs=blk(wq),
        out_shape=jax.ShapeDtypeStruct((B, S, wq), BF16),
        scratch_shapes=[pltpu.VMEM((LANES, nh * KEY_CHUNK), F32),
                        pltpu.VMEM((KEY_CHUNK + 8, nh * KEY_CHUNK), F32),
                        pltpu.VMEM((KEY_CHUNK + 8, nh * KEY_CHUNK), F32),
                        pltpu.VMEM((KEY_CHUNK, nh * KEY_CHUNK), BF16)],
        compiler_params=_cparams(("parallel", "arbitrary")),
        name="wattn",
    )(q, k, vt, bias)


def _memkv_kernel(mem_ref, g_ref, w_ref, k_ref, v_ref):
    hb = _rms(mem_ref[...], g_ref[...]).astype(BF16)
    y = jnp.dot(hb, w_ref[...], preferred_element_type=F32)
    lane = lax.broadcasted_iota(I32, (y.shape[0], LANES), 1)
    for t in range(MEM_WIDTH // LANES):
        for out_ref, base in ((k_ref, 0), (v_ref, MEM_WIDTH)):
            tile = y[:, base + t * LANES:base + (t + 1) * LANES]
            out_ref[:, (2 * t) * LANES:(2 * t + 1) * LANES] = jnp.where(lane < MEM_HEAD_DIM, tile, 0.0).astype(BF16)
            out_ref[:, (2 * t + 1) * LANES:(2 * t + 2) * LANES] = jnp.where(lane >= MEM_HEAD_DIM, tile, 0.0).astype(BF16)


def _memkv(mem, g_mem, w):
    B, M, D = mem.shape
    L = w.shape[0]
    out = jax.ShapeDtypeStruct((L, B, M, 2 * MEM_WIDTH), BF16)
    ospec = pl.BlockSpec((None, None, M, 2 * MEM_WIDTH), lambda l, b: (l, b, 0, 0))
    return pl.pallas_call(
        _memkv_kernel,
        grid=(L, B),
        in_specs=[pl.BlockSpec((None, M, D), lambda l, b: (b, 0, 0)),
                  pl.BlockSpec((1, D), lambda l, b: (0, 0)),
                  pl.BlockSpec((None, D, 2 * MEM_WIDTH), lambda l, b: (l, 0, 0))],
        out_specs=(ospec, ospec),
        out_shape=(out, out),
        compiler_params=_cparams(("parallel", "parallel")),
        name="memkv",
    )(mem, g_mem, w)


def _mixer_residual(x_ref, mix_ref, qm_ref, mk_ref, mv_ref, w_ref):
    mw = mix_ref.shape[-1]
    n_sub = 2 * MEM_WIDTH // LANES
    logits = [lax.dot_general(qm_ref[:, (u // 2) * LANES:(u // 2 + 1) * LANES], mk_ref[:, u * LANES:(u + 1) * LANES],
                              _NT, preferred_element_type=F32) for u in range(n_sub)]
    y = x_ref[...] + jnp.dot(mix_ref[...], w_ref[0:mw, :], preferred_element_type=F32)
    tiles = []
    for t in range(n_sub // 2):
        o_t = None
        for u in (2 * t, 2 * t + 1):
            p = jnp.exp2(logits[u] - jnp.max(logits[u], axis=-1, keepdims=True))
            l = jnp.sum(p, axis=-1, keepdims=True)
            o_s = jnp.dot(p.astype(BF16), mv_ref[:, u * LANES:(u + 1) * LANES], preferred_element_type=F32) / l
            o_t = o_s if o_t is None else o_t + o_s
        tiles.append(o_t.astype(BF16))
    return y + jnp.dot(jnp.concatenate(tiles, axis=1), w_ref[mw:mw + MEM_WIDTH, :], preferred_element_type=F32)


def _out_kernel(x_ref, mix_ref, qm_ref, mk_ref, mv_ref, w_ref, o_ref):
    o_ref[...] = _mixer_residual(x_ref, mix_ref, qm_ref, mk_ref, mv_ref, w_ref)


def _out_proj(x, mix, qm, mk, mv, w, tm):
    B, S, D = x.shape
    row = lambda width: pl.BlockSpec((None, tm, width), lambda b, i: (b, i, 0))
    mem = pl.BlockSpec((None, mk.shape[1], 2 * MEM_WIDTH), lambda b, i: (b, 0, 0))
    return pl.pallas_call(
        _out_kernel,
        grid=(B, S // tm),
        in_specs=[row(D), row(mix.shape[-1]), row(MEM_WIDTH), mem, mem,
                  pl.BlockSpec(w.shape, lambda b, i: (0, 0))],
        out_specs=row(D),
        out_shape=jax.ShapeDtypeStruct((B, S, D), F32),
        compiler_params=_cparams(("parallel", "parallel")),
        name="out_proj",
    )(x, mix, qm, mk, mv, w)


def _ffn_kernel(x_ref, g_ref, wup_ref, cw_ref, cb_ref, wdn_ref, gf_ref, o_ref, carry_ref, *, tm, cw, final):
    dff = wdn_ref.shape[0]
    hm = min(tm, 2 * LANES)

    @pl.when(pl.program_id(1) == 0)
    def _():
        carry_ref[...] = jnp.zeros(carry_ref.shape, F32)

    row = lax.broadcasted_iota(I32, (hm, cw), 0)
    xs = [x_ref[r:r + hm, :] for r in range(0, tm, hm)]
    hbs = [_rms(x, g_ref[...]).astype(BF16) for x in xs]
    accs = list(xs)

    def up(hb, c0):
        return (jnp.dot(hb, wup_ref[:, c0:c0 + cw], preferred_element_type=F32),
                jnp.dot(hb, wup_ref[:, dff + c0:dff + c0 + cw], preferred_element_type=F32))

    def gate(a, b, prev, c0):
        p1, p2 = prev[7:8, :], prev[6:7, :]
        a1 = jnp.where(row == 0, p1, pltpu.roll(a, 1, 0))
        a2 = jnp.where(row == 0, p2, jnp.where(row == 1, p1, pltpu.roll(a, 2, 0)))
        w = cw_ref[:, c0:c0 + cw]
        conv = w[0:1, :] * a2 + w[1:2, :] * a1 + w[2:3, :] * a + cb_ref[:, c0:c0 + cw]
        return (conv / (1.0 + jnp.exp(-conv)) * b).astype(BF16)

    nxt = [up(hb, 0) for hb in hbs]
    for c0 in range(0, dff, cw):
        cur = nxt
        if c0 + cw < dff:
            nxt = [up(hb, c0 + cw) for hb in hbs]
        prev = carry_ref[:, c0:c0 + cw]
        for k, (a, b) in enumerate(cur):
            accs[k] = accs[k] + jnp.dot(gate(a, b, prev, c0), wdn_ref[c0:c0 + cw, :], preferred_element_type=F32)
            prev = a[hm - 8:hm, :]
        carry_ref[:, c0:c0 + cw] = prev
    for k, acc in enumerate(accs):
        o_ref[k * hm:(k + 1) * hm, :] = _rms(acc, gf_ref[...]) if final else acc


def _ffn(x, g, wup, cw, cb, wdn, gf, tm, final):
    B, S, D = x.shape
    dff = wdn.shape[0]
    row = pl.BlockSpec((None, tm, D), lambda b, i: (b, i, 0))
    const = lambda shape: pl.BlockSpec(shape, lambda b, i: (0, 0), pipeline_mode=pl.Buffered(1))
    return pl.pallas_call(
        functools.partial(_ffn_kernel, tm=tm, cw=2 * LANES, final=final),
        grid=(B, S // tm),
        in_specs=[row, const((1, D)), const(wup.shape), const(cw.shape), const((1, dff)), const(wdn.shape),
                  const((1, D))],
        out_specs=row,
        out_shape=jax.ShapeDtypeStruct((B, S, D), F32),
        scratch_shapes=[pltpu.VMEM((8, dff), F32)],
        compiler_params=_cparams(("arbitrary", "arbitrary")),
        name="ffn",
    )(x, g, wup, cw, cb, wdn, gf)


def _b_subheads():
    return tuple(((2 * (j // 2), 2 * (j // 2), 2 * j), (2 * (j // 2) + 1, 2 * (j // 2) + 1, 2 * j + 1))
                 for j in range(B_HEADS // 2))


def kernel(x, mem, positions, g_mix, g_ffn, g_mem, g_final, w_mem_kv, a_w_in, a_kv_norm, a_w_uk, a_w_uv, a_w_out,
           b_w_in, b_sinks, b_w_out, c_w_in, c_w_out, f_w_up, f_conv_w, f_conv_b, f_w_down):
    B, S, D = x.shape
    depth = g_mix.shape[0]
    tm = min(512, S)
    tab32 = _rope_table(positions, A_QK_DIM, A_ROPE_DIM)
    tab16 = _rope_table(positions, B_HEAD_DIM, B_HEAD_DIM // 4)
    mem_k, mem_v = _memkv(mem, g_mem.reshape(1, D), w_mem_kv.astype(BF16))
    conv_w = jnp.pad(f_conv_w, ((0, 0), (0, 8 - CONV_WIDTH), (0, 0)))
    for i in range(depth):
        kind, j = i % 3, i // 3
        g = g_mix[i].reshape(1, D)
        if kind == 0:
            wuk = jnp.pad(a_w_uk[j], ((0, 0), (0, 0), (A_ROPE_DIM, 0))).reshape(A_KV_RANK, -1).astype(BF16)
            wuv = jnp.transpose(a_w_uv[j], (1, 2, 0)).astype(BF16)
            q, kf, ct, qi, ki, wi, qm = _proj_a(x, g, _prep_a_w_in(a_w_in[j]), a_kv_norm[j].reshape(1, -1), wuk,
                                                tab32, tab16, tm)
            mix = _dsa(q, qi, wi, kf, ct, ki, wuv)
            w_out = a_w_out[j]
        elif kind == 1:
            nq, nkv = B_HEADS * B_HEAD_DIM, 4 * B_KV_HEADS * B_HEAD_DIM
            q, k, v, qm = _proj_qkv(x, g, _prep_b_w_in(b_w_in[j]), tab16, tm, B_HEAD_DIM, B_HEAD_DIM // 4,
                                    nq, nkv, nkv, False)
            mix = _swa(q, k, v, b_sinks[j], subheads=_b_subheads(), max_dist=B_WINDOW - 1)
            w_out = b_w_out[j]
        else:
            nq = C_HEADS * C_HEAD_DIM
            q, k, v, qm = _proj_qkv(x, g, c_w_in[j].astype(BF16), tab32, tm, C_HEAD_DIM, C_HEAD_DIM // 4,
                                    nq, nq, nq, True)
            mix = _wattn(q, k, v, C_BRANCHES)
            w_out = c_w_out[j]
        x = _out_proj(x, mix, qm, mem_k[i], mem_v[i], w_out.astype(BF16), tm)
        x = _ffn(x, g_ffn[i].reshape(1, D), f_w_up[i].astype(BF16), conv_w[i], f_conv_b[i].reshape(1, -1),
                 f_w_down[i].astype(BF16), g_final.reshape(1, D), tm, i == depth - 1)
    return x
```

```python
import functools

import jax
import jax.numpy as jnp
import numpy as np
from jax import lax
from jax.experimental import pallas as pl
from jax.experimental.pallas import tpu as pltpu

F32 = jnp.float32
BF16 = jnp.bfloat16
I32 = jnp.int32

LANES = 128
BLOCK = 128
ROPE_THETA = 500000.0
EPS = 1e-6
NEG_INF = -1e30
LOG2E = 1.4426950408889634
INT_MIN = -(2**31)

A_HEADS = 8
A_QK_DIM = 128
A_ROPE_DIM = 32
A_KV_RANK = 256
IDX_HEADS = 16
IDX_DIM = 64
TOPK_MAX = 256
B_HEADS = 16
B_KV_HEADS = 4
B_HEAD_DIM = 64
B_WINDOW = 128
C_HEADS = 8
C_HEAD_DIM = 128
C_BRANCHES = ((128, 1), (512, 4), (2048, 16))
MEM_HEADS = 4
MEM_HEAD_DIM = 64
MEM_WIDTH = MEM_HEADS * MEM_HEAD_DIM
CONV_WIDTH = 3
KEY_CHUNK = 256

VMEM_LIMIT = 56 * 1024 * 1024

_NT = (((1,), (1,)), ((), ()))


def _cparams(sem):
    return pltpu.CompilerParams(dimension_semantics=sem, vmem_limit_bytes=VMEM_LIMIT)


def _rms(x, g):
    return x * lax.rsqrt(jnp.mean(x * x, axis=-1, keepdims=True) + EPS) * g


def _loop_k_per_trip(lo, hi, body, carry, k=2):
    if k == 1:
        return lax.fori_loop(lo, hi, body, carry)
    trips = (hi - lo) // k

    def several(t, c):
        for u in range(k):
            c = body(lo + k * t + u, c)
        return c

    carry = lax.fori_loop(0, trips, several, carry)
    return _loop_k_per_trip(lo + k * trips, hi, body, carry, k // 2)


def _bit_transpose32(words):
    a = list(words)
    j, m = 16, 0x0000FFFF
    while j:
        mask = jnp.int32(m - (1 << 32) if m >= (1 << 31) else m)
        for k in range(32):
            if k & j == 0:
                t = (a[k] ^ lax.shift_right_logical(a[k + j], j)) & mask
                a[k] = a[k] ^ t
                a[k + j] = a[k + j] ^ lax.shift_left(t, j)
        j >>= 1
        m = (m ^ (m << j)) & 0xFFFFFFFF
    return a


def _rope_tile(t, tab, half):
    c, sa, sb = tab[:, 0:LANES], tab[:, LANES:2 * LANES], tab[:, 2 * LANES:3 * LANES]
    return t * c + pltpu.roll(t, half, 1) * sa + pltpu.roll(t, LANES - half, 1) * sb


def _rope_table(positions, head_dim, rot):
    half = rot // 2
    inv = ROPE_THETA ** (-jnp.arange(0, rot, 2, dtype=F32) / rot)
    ang = positions.astype(F32)[..., None] * inv
    cs = jnp.concatenate([jnp.cos(ang), jnp.sin(ang)], axis=-1)
    sel = np.zeros((rot, 3 * LANES), np.float32)
    one = np.zeros((3 * LANES,), np.float32)
    for l in range(LANES):
        j = l % head_dim
        if j < half:
            sel[j, l] = 1.0
            sel[half + j, 2 * LANES + l] = -1.0
        elif j < rot:
            sel[j - half, l] = 1.0
            sel[j, LANES + l] = 1.0
        else:
            one[l] = 1.0
    return jnp.dot(cs, jnp.asarray(sel), precision=lax.Precision.HIGHEST) + jnp.asarray(one)


def _staggered(hbs, w_ref, jobs):
    def mms(lo, hi):
        return [jnp.dot(hb, w_ref[:, lo:hi], preferred_element_type=F32) for hb in hbs]

    nxt = mms(jobs[0][0], jobs[0][1])
    for n, (_, _, epilogue) in enumerate(jobs):
        cur = nxt
        if n + 1 < len(jobs):
            nxt = mms(jobs[n + 1][0], jobs[n + 1][1])
        for part, y in enumerate(cur):
            epilogue(y, part)


def _proj_a_kernel(x_ref, g_ref, w_ref, kvn_ref, wuk_ref, t32_ref, t16_ref,
                   q_ref, kf_ref, ct_ref, qi_ref, ki_ref, wi_ref, qm_ref, *, tm):
    hm = KEY_CHUNK
    rows = [slice(r, r + hm) for r in range(0, tm, hm)]
    hbs = [_rms(x_ref[r, :], g_ref[...]).astype(BF16) for r in rows]

    def roped(ref, col, tab_ref, half, scale):
        def epilogue(y, part):
            tab = tab_ref[rows[part], :]
            for u in range(y.shape[1] // LANES):
                t = _rope_tile(y[:, u * LANES:(u + 1) * LANES], tab, half)
                if scale != 1.0:
                    t = t * scale
                ref[rows[part], col + u * LANES:col + (u + 1) * LANES] = t.astype(BF16)
        return epilogue

    latent_bf16 = {}

    def latent(y, part):
        c = _rms(y, kvn_ref[...])
        latent_bf16[part] = c.astype(BF16)
        ct_ref[part] = c.T.astype(BF16)

    def full_keys(y, part):
        k_rope = _rope_tile(y, t32_ref[rows[part], :], A_ROPE_DIM // 2)
        k_nope = jnp.dot(latent_bf16[part], wuk_ref[...], preferred_element_type=F32)
        for h in range(A_HEADS):
            kf_ref[rows[part], h * LANES:(h + 1) * LANES] = (
                k_nope[:, h * LANES:(h + 1) * LANES] + k_rope).astype(BF16)

    def index_keys(y, part):
        tab = t16_ref[rows[part], :]
        for u in range(2):
            ki_ref[part, u * hm:(u + 1) * hm, :] = _rope_tile(
                y[:, u * LANES:(u + 1) * LANES], tab, IDX_DIM // 8).astype(BF16)

    def index_weights(y, part):
        wi_ref[rows[part], :] = y * (IDX_HEADS * IDX_DIM) ** -0.5

    def mem_query(y, part):
        qm_ref[rows[part], :] = (y * (MEM_HEAD_DIM ** -0.5 * LOG2E)).astype(BF16)

    jobs, o = [], 0
    for j in range(0, A_HEADS * A_QK_DIM, 2 * LANES):
        jobs.append((o + j, o + j + 2 * LANES,
                     roped(q_ref, j, t32_ref, A_ROPE_DIM // 2, A_QK_DIM ** -0.5 * LOG2E)))
    o += A_HEADS * A_QK_DIM
    jobs.append((o, o + A_KV_RANK, latent))
    o += A_KV_RANK
    jobs.append((o, o + LANES, full_keys))
    o += LANES
    for j in range(0, IDX_HEADS * IDX_DIM, 2 * LANES):
        jobs.append((o + j, o + j + 2 * LANES, roped(qi_ref, j, t16_ref, IDX_DIM // 8, 1.0)))
    o += IDX_HEADS * IDX_DIM
    jobs.append((o, o + 2 * LANES, index_keys))
    o += 2 * LANES
    jobs.append((o, o + LANES, index_weights))
    o += LANES
    jobs.append((o, o + MEM_WIDTH, mem_query))
    _staggered(hbs, w_ref, jobs)


def _prep_a_w_in(w):
    d = w.shape[0]
    sizes = (A_HEADS * A_QK_DIM, A_KV_RANK, A_ROPE_DIM, IDX_HEADS * IDX_DIM, IDX_DIM, IDX_HEADS, MEM_WIDTH)
    offs = [0]
    for s in sizes:
        offs.append(offs[-1] + s)
    q, ckv, kr, qi, ki, wi, qm = [w[:, offs[i]:offs[i + 1]] for i in range(len(sizes))]
    z = lambda n: jnp.zeros((d, n), w.dtype)
    return jnp.concatenate([
        q, ckv, kr, z(LANES - A_ROPE_DIM), qi,
        ki, z(LANES - IDX_DIM), z(LANES - IDX_DIM), ki,
        wi, z(LANES - IDX_HEADS), qm], axis=1).astype(BF16)


def _proj_a(x, g, w, kvn, wuk, t32, t16, tm):
    B, S, D = x.shape
    n = w.shape[1]
    row = lambda width: pl.BlockSpec((None, tm, width), lambda b, i: (b, i, 0))
    const = lambda shape: pl.BlockSpec(shape, lambda b, i: (0,) * len(shape))
    out_shape = (
        jax.ShapeDtypeStruct((B, S, A_HEADS * A_QK_DIM), BF16),
        jax.ShapeDtypeStruct((B, S, A_HEADS * A_QK_DIM), BF16),
        jax.ShapeDtypeStruct((B, S // KEY_CHUNK, A_KV_RANK, KEY_CHUNK), BF16),
        jax.ShapeDtypeStruct((B, S, IDX_HEADS * IDX_DIM), BF16),
        jax.ShapeDtypeStruct((B, S // KEY_CHUNK, 2 * KEY_CHUNK, LANES), BF16),
        jax.ShapeDtypeStruct((B, S, LANES), F32),
        jax.ShapeDtypeStruct((B, S, MEM_WIDTH), BF16),
    )
    out_specs = (
        row(A_HEADS * A_QK_DIM), row(A_HEADS * A_QK_DIM),
        pl.BlockSpec((None, tm // KEY_CHUNK, A_KV_RANK, KEY_CHUNK), lambda b, i: (b, i, 0, 0)),
        row(IDX_HEADS * IDX_DIM),
        pl.BlockSpec((None, tm // KEY_CHUNK, 2 * KEY_CHUNK, LANES), lambda b, i: (b, i, 0, 0)),
        row(LANES), row(MEM_WIDTH),
    )
    return pl.pallas_call(
        functools.partial(_proj_a_kernel, tm=tm),
        grid=(B, S // tm),
        in_specs=[row(D), const((1, D)), const((D, n)), const((1, A_KV_RANK)), const(wuk.shape),
                  row(3 * LANES), row(3 * LANES)],
        out_specs=out_specs,
        out_shape=out_shape,
        compiler_params=_cparams(("parallel", "parallel")),
        name="proj_a",
    )(x, g, w, kvn, wuk, t32, t16)


def _proj_qkv_kernel(x_ref, g_ref, w_ref, tab_ref, q_ref, k_ref, v_ref, qm_ref, *,
                     head_dim, rot, nq, nk, nv, v_transposed):
    tm = x_ref.shape[0]
    hm = KEY_CHUNK
    rows = [slice(r, r + hm) for r in range(0, tm, hm)]
    hbs = [_rms(x_ref[r, :], g_ref[...]).astype(BF16) for r in rows]

    def roped(ref, col, scale):
        def epilogue(y, part):
            tab = tab_ref[rows[part], :]
            for u in range(2):
                t = _rope_tile(y[:, u * LANES:(u + 1) * LANES], tab, rot // 2)
                if scale != 1.0:
                    t = t * scale
                ref[rows[part], col + u * LANES:col + (u + 1) * LANES] = t.astype(BF16)
        return epilogue

    def value(col):
        def epilogue(y, part):
            if v_transposed:
                v_ref[part, col:col + 2 * LANES, :] = jnp.swapaxes(y[None], 1, 2)[0].astype(BF16)
            else:
                v_ref[rows[part], col:col + 2 * LANES] = y.astype(BF16)
        return epilogue

    def mem_query(y, part):
        qm_ref[rows[part], :] = (y * (MEM_HEAD_DIM ** -0.5 * LOG2E)).astype(BF16)

    jobs = [(j, j + 2 * LANES, roped(q_ref, j, head_dim ** -0.5 * LOG2E)) for j in range(0, nq, 2 * LANES)]
    jobs += [(nq + j, nq + j + 2 * LANES, roped(k_ref, j, 1.0)) for j in range(0, nk, 2 * LANES)]
    jobs += [(nq + nk + j, nq + nk + j + 2 * LANES, value(j)) for j in range(0, nv, 2 * LANES)]
    jobs.append((nq + nk + nv, nq + nk + nv + MEM_WIDTH, mem_query))
    _staggered(hbs, w_ref, jobs)


def _proj_qkv(x, g, w, tab, tm, head_dim, rot, nq, nk, nv, v_transposed):
    B, S, D = x.shape
    n = w.shape[1]
    row = lambda width: pl.BlockSpec((None, tm, width), lambda b, i: (b, i, 0))
    const = lambda shape: pl.BlockSpec(shape, lambda b, i: (0,) * len(shape))
    if v_transposed:
        v_spec = pl.BlockSpec((None, tm // KEY_CHUNK, nv, KEY_CHUNK), lambda b, i: (b, i, 0, 0))
        v_shape = jax.ShapeDtypeStruct((B, S // KEY_CHUNK, nv, KEY_CHUNK), BF16)
    else:
        v_spec, v_shape = row(nv), jax.ShapeDtypeStruct((B, S, nv), BF16)
    return pl.pallas_call(
        functools.partial(_proj_qkv_kernel, head_dim=head_dim, rot=rot, nq=nq, nk=nk, nv=nv,
                          v_transposed=v_transposed),
        grid=(B, S // tm),
        in_specs=[row(D), const((1, D)), const((D, n)), row(3 * LANES)],
        out_specs=(row(nq), row(nk), v_spec, row(MEM_WIDTH)),
        out_shape=(jax.ShapeDtypeStruct((B, S, nq), BF16), jax.ShapeDtypeStruct((B, S, nk), BF16), v_shape,
                   jax.ShapeDtypeStruct((B, S, MEM_WIDTH), BF16)),
        compiler_params=_cparams(("parallel", "parallel")),
        name="proj_qkv",
    )(x, g, w, tab)


def _prep_b_w_in(w):
    d = w.shape[0]
    nq, nkv = B_HEADS * B_HEAD_DIM, B_KV_HEADS * B_HEAD_DIM
    q, k, v, qm = w[:, :nq], w[:, nq:nq + nkv], w[:, nq + nkv:nq + 2 * nkv], w[:, nq + 2 * nkv:]
    z = jnp.zeros((d, B_HEAD_DIM), w.dtype)

    def spread(t):
        cols = []
        for h in range(B_KV_HEADS):
            th = t[:, h * B_HEAD_DIM:(h + 1) * B_HEAD_DIM]
            cols += [th, z, z, th]
        return jnp.concatenate(cols, axis=1)

    return jnp.concatenate([q, spread(k), spread(v), qm], axis=1).astype(BF16)


DSA_BLOCK = 256


def _dsa_kernel(q_ref, qi_ref, wi_ref, kf_ref, ct_ref, ki_ref, wuv_ref, o_ref,
                keys_ref, planes_ref, qbd_ref, qi2_ref, wrow_ref, acc_ref, sa_ref, sb_ref, *, topk):
    kc = KEY_CHUNK
    qb = DSA_BLOCK
    hw = A_HEADS * qb
    gw = 2 * qb
    i = pl.program_id(1)
    nch = (i * qb + qb + kc - 1) // kc

    @pl.when(i == 0)
    def _():
        qbd_ref[...] = jnp.zeros(qbd_ref.shape, BF16)

    for h in range(A_HEADS):
        qbd_ref[h // 2, (h % 2) * qb:(h % 2 + 1) * qb, (h % 2) * LANES:(h % 2 + 1) * LANES] = (
            q_ref[:, h * LANES:(h + 1) * LANES])
        qi2_ref[h * qb:(h + 1) * qb, :] = qi_ref[:, h * LANES:(h + 1) * LANES]
    w_t = wi_ref[...].T
    for j in range(IDX_HEADS // 2):
        wrow_ref[0:1, j * qb:(j + 1) * qb] = w_t[2 * j:2 * j + 1, :]
        wrow_ref[1:2, j * qb:(j + 1) * qb] = w_t[2 * j + 1:2 * j + 2, :]

    qpos = i * qb + lax.broadcasted_iota(I32, (kc, qb), 1)
    krow = lax.broadcasted_iota(I32, (kc, qb), 0)

    def score_chunk(c):
        off = pl.multiple_of(c * kc, kc)
        kk = ki_ref[c]
        acc = None
        for g in range(0, hw, qb):
            d = lax.dot_general(kk, qi2_ref[g:g + qb, :], _NT, preferred_element_type=F32)
            t = (jnp.maximum(d[0:kc], 0.0) * wrow_ref[0:1, g:g + qb]
                 + jnp.maximum(d[kc:2 * kc], 0.0) * wrow_ref[1:2, g:g + qb])
            acc = t if acc is None else acc + t
        bits = pltpu.bitcast(acc, I32)
        key = jnp.where(bits < 0, bits ^ 0x7FFFFFFF, bits)
        key = jnp.where(off + krow <= qpos, key, INT_MIN)
        keys_ref[pl.ds(off, kc), :] = key
        planes = _bit_transpose32([(key[8 * r:8 * r + 8, :] ^ INT_MIN) for r in range(kc // 8)])
        for p in range(32):
            planes_ref[c, p] = planes[p]

    @pl.when(i == 0)
    def _():
        planes_ref[...] = jnp.zeros(planes_ref.shape, I32)

    def score_body(c, carry):
        score_chunk(c)
        return carry

    _loop_k_per_trip(0, nch, score_body, 0, k=4)

    n_chunks = planes_ref.shape[0]

    def bit_body(p, carry):
        alive, need, res = carry
        ones = [a & planes_ref[c, p] for c, a in enumerate(alive)]
        cnt = ones[0] * 0
        for o in ones:
            cnt = cnt + lax.population_count(o)
        cnt = jnp.sum(cnt, axis=0, keepdims=True)
        take = cnt >= need
        alive = [jnp.where(take, o, a ^ o) for o, a in zip(ones, alive)]
        need = jnp.where(take, need, need - cnt)
        res = jnp.where(take, res | lax.shift_left(jnp.int32(1), 31 - p), res)
        return alive, need, res

    for g in range(0, hw, gw):
        sa_ref[0:kc, g:g + gw] = lax.dot_general(kf_ref[0:kc, (g // gw) * 2 * LANES:(g // gw + 1) * 2 * LANES],
                                                 qbd_ref[g // gw], _NT, preferred_element_type=F32)

    carry = ([jnp.where(c < nch, jnp.full((8, qb), -1, I32), 0) for c in range(n_chunks)],
             jnp.full((1, qb), topk, I32), jnp.zeros((1, qb), I32))
    for p in range(32):
        carry = bit_body(p, carry)
    alive, need, res = carry
    thr = jnp.maximum(res ^ INT_MIN, INT_MIN + 1)

    tied = lax.population_count(alive[0])
    for a in alive[1:]:
        tied = tied + lax.population_count(a)
    tied = jnp.sum(tied, axis=0, keepdims=True)
    surplus = jnp.where((tied > need) & (res != 0), 1, 0)

    @pl.when(jnp.max(surplus) > 0)
    def _():
        seq_len = keys_ref.shape[0]
        nbits = (seq_len - 1).bit_length()
        sub = lax.broadcasted_iota(I32, (8, qb), 0)
        pos_planes = [_bit_transpose32([(seq_len - 1 - c * kc - 8 * r) - sub for r in range(kc // 8)])[32 - nbits:]
                      for c in range(n_chunks)]
        state = (alive, need, jnp.zeros((1, qb), I32))
        for j in range(nbits):
            alive_j, need_j, kept = state
            ones = [a & pos_planes[c][j] for c, a in enumerate(alive_j)]
            cnt = lax.population_count(ones[0])
            for o in ones[1:]:
                cnt = cnt + lax.population_count(o)
            cnt = jnp.sum(cnt, axis=0, keepdims=True)
            take = cnt >= need_j
            state = ([jnp.where(take, o, a ^ o) for o, a in zip(ones, alive_j)],
                     jnp.where(take, need_j, need_j - cnt),
                     jnp.where(take, kept | (1 << (nbits - 1 - j)), kept))
        last_kept = state[2]

        def demote(c, carry_):
            off = pl.multiple_of(c * kc, kc)
            key = keys_ref[pl.ds(off, kc), :]
            inv_pos = (seq_len - 1 - off) - krow
            keys_ref[pl.ds(off, kc), :] = jnp.where((key == thr) & (inv_pos < last_kept), key - 1, key)
            return carry_

        lax.fori_loop(0, nch, demote, 0)

    acc_ref[...] = jnp.zeros(acc_ref.shape, F32)
    bias0 = jnp.where(keys_ref[0:kc, :] >= thr, 0.0, NEG_INF)
    bias0 = jnp.concatenate([bias0, bias0], axis=1)
    for g in range(0, hw, gw):
        t = sa_ref[0:kc, g:g + gw] + bias0
        sa_ref[0:kc, g:g + gw] = t
        sa_ref[kc:kc + 1, g:g + gw] = jnp.max(t, axis=0, keepdims=True)

    def logits(c, s_ref):
        off = pl.multiple_of(c * kc, kc)
        bias = jnp.where(keys_ref[pl.ds(off, kc), :] >= thr, 0.0, NEG_INF)
        bias = jnp.concatenate([bias, bias], axis=1)
        for g in range(0, hw, gw):
            t = lax.dot_general(kf_ref[pl.ds(off, kc), (g // gw) * 2 * LANES:(g // gw + 1) * 2 * LANES],
                                qbd_ref[g // gw], _NT, preferred_element_type=F32) + bias
            s_ref[0:kc, g:g + gw] = t
            s_ref[kc:kc + 1, g:g + gw] = jnp.max(t, axis=0, keepdims=True)

    def consume(s_ref, c, carry):
        m_prev, l_prev = carry
        ct = ct_ref[c]
        m_out, l_out = [], []
        for g in range(0, hw, gw):
            s = s_ref[0:kc, g:g + gw]
            m_new = jnp.maximum(m_prev[:, g:g + gw], s_ref[kc:kc + 1, g:g + gw])
            alpha = jnp.exp2(m_prev[:, g:g + gw] - m_new)
            p = jnp.exp2(s - m_new)
            l_out.append(alpha * l_prev[:, g:g + gw] + jnp.sum(p, axis=0, keepdims=True))
            m_out.append(m_new)
            acc_ref[:, g:g + gw] = alpha * acc_ref[:, g:g + gw] + jnp.dot(
                ct, p.astype(BF16), preferred_element_type=F32)
        return jnp.concatenate(m_out, axis=1), jnp.concatenate(l_out, axis=1)

    def pair_body(j, carry):
        c0 = 2 * j
        logits(c0 + 1, sb_ref)
        carry = consume(sa_ref, c0, carry)
        logits(jnp.minimum(c0 + 2, nch - 1), sa_ref)
        return consume(sb_ref, c0 + 1, carry)

    carry = _loop_k_per_trip(0, nch // 2, pair_body,
                             (jnp.full((1, hw), NEG_INF, F32), jnp.zeros((1, hw), F32)), k=2)
    _, l_fin = lax.fori_loop(0, nch % 2, lambda _, c: consume(sa_ref, nch - 1, c), carry)

    inv_l = 1.0 / l_fin
    for h in range(A_HEADS):
        cols = slice(h * qb, (h + 1) * qb)
        out_t = jnp.dot(wuv_ref[h], (acc_ref[:, cols] * inv_l[:, cols]).astype(BF16), preferred_element_type=F32)
        o_ref[:, h * LANES:(h + 1) * LANES] = jnp.swapaxes(out_t[None], 1, 2)[0].astype(BF16)


def _dsa(q, qi, wi, kf, ct, ki, wuv):
    B, S, _ = q.shape
    topk = min(TOPK_MAX, S // 4)
    qb = DSA_BLOCK
    blk = lambda width: pl.BlockSpec((None, qb, width), lambda b, i: (b, i, 0))
    seq = lambda width: pl.BlockSpec((None, S, width), lambda b, i: (b, 0, 0))
    const3 = lambda shape: pl.BlockSpec(shape, lambda b, i: (0, 0, 0))
    return pl.pallas_call(
        functools.partial(_dsa_kernel, topk=topk),
        grid=(B, S // qb),
        in_specs=[blk(A_HEADS * A_QK_DIM), blk(IDX_HEADS * IDX_DIM), blk(LANES),
                  seq(A_HEADS * A_QK_DIM),
                  pl.BlockSpec((None, S // KEY_CHUNK, A_KV_RANK, KEY_CHUNK), lambda b, i: (b, 0, 0, 0)),
                  pl.BlockSpec((None, S // KEY_CHUNK, 2 * KEY_CHUNK, LANES), lambda b, i: (b, 0, 0, 0)),
                  const3(wuv.shape)],
        out_specs=blk(A_HEADS * LANES),
        out_shape=jax.ShapeDtypeStruct((B, S, A_HEADS * LANES), BF16),
        scratch_shapes=[
            pltpu.VMEM((S, qb), I32),
            pltpu.VMEM((S // KEY_CHUNK, 32, 8, qb), I32),
            pltpu.VMEM((A_HEADS // 2, 2 * qb, 2 * LANES), BF16),
            pltpu.VMEM((A_HEADS * qb, LANES), BF16),
            pltpu.VMEM((8, A_HEADS * qb), F32),
            pltpu.VMEM((A_KV_RANK, A_HEADS * qb), F32),
            pltpu.VMEM((KEY_CHUNK + 8, A_HEADS * qb), F32),
            pltpu.VMEM((KEY_CHUNK + 8, A_HEADS * qb), F32),
        ],
        compiler_params=_cparams(("parallel", "arbitrary")),
        name="dsa",
    )(q, qi, wi, kf, ct, ki, wuv)


def _window_bias(branches, nrel):
    rel = jnp.arange(nrel, dtype=I32)[:, None, None]
    krow = jnp.arange(KEY_CHUNK, dtype=I32)[None, :, None]
    qcol = jnp.arange(KEY_CHUNK, dtype=I32)[None, None, :]
    dist = KEY_CHUNK * (nrel - 1 - rel) + qcol - krow
    mult = sum(((dist >= 0) & (dist <= window) & (dist % dil == 0)).astype(F32) for window, dil in branches)
    return jnp.where(mult > 0, jnp.log2(jnp.maximum(mult, 1.0)), NEG_INF)


def _wattn_kernel(q_ref, k_ref, vt_ref, bias_ref, o_ref, acc_ref, sa_ref, sb_ref, p_ref, *, nh, nrel):
    kc = qb = KEY_CHUNK
    top = pl.program_id(1)
    c_lo = jnp.maximum(top - (nrel - 1), 0)
    n = top + 1 - c_lo

    acc_ref[...] = jnp.zeros(acc_ref.shape, F32)

    def logits(c, s_ref):
        off = pl.multiple_of(c * kc, kc)
        bias = bias_ref[c - top + (nrel - 1)]
        for h in range(nh):
            cols = slice(h * LANES, (h + 1) * LANES)
            t = lax.dot_general(k_ref[pl.ds(off, kc), cols], q_ref[:, cols], _NT,
                                preferred_element_type=F32) + bias
            s_ref[0:kc, h * qb:(h + 1) * qb] = t
            s_ref[kc:kc + 1, h * qb:(h + 1) * qb] = jnp.max(t, axis=0, keepdims=True)

    def consume(s_ref, c, carry):
        m_prev, l_prev = carry
        m_new = jnp.maximum(m_prev, s_ref[kc:kc + 1, :])
        alpha = jnp.exp2(m_prev - m_new)
        p_ref[...] = jnp.exp2(s_ref[0:kc, :] - m_new).astype(BF16)
        l_new = alpha * l_prev + jnp.dot(jnp.ones((16, kc), BF16), p_ref[...], preferred_element_type=F32)[0:1, :]
        for h in range(nh):
            cols = slice(h * qb, (h + 1) * qb)
            acc_ref[:, cols] = alpha[:, cols] * acc_ref[:, cols] + jnp.dot(
                vt_ref[c, h * LANES:(h + 1) * LANES, :], p_ref[:, cols], preferred_element_type=F32)
        return m_new, l_new

    logits(c_lo, sa_ref)

    def pair_body(j, carry):
        c0 = c_lo + 2 * j
        logits(c0 + 1, sb_ref)
        carry = consume(sa_ref, c0, carry)
        logits(jnp.minimum(c0 + 2, top), sa_ref)
        return consume(sb_ref, c0 + 1, carry)

    carry = _loop_k_per_trip(0, n // 2, pair_body,
                             (jnp.full((1, nh * qb), NEG_INF, F32), jnp.zeros((1, nh * qb), F32)), k=4)
    _, l_fin = lax.fori_loop(0, n % 2, lambda _, c: consume(sa_ref, top, c), carry)

    inv_l = 1.0 / l_fin
    for h in range(nh):
        cols = slice(h * qb, (h + 1) * qb)
        o_ref[:, h * LANES:(h + 1) * LANES] = (acc_ref[:, cols] * inv_l[:, cols]).T.astype(BF16)


def _swa_kernel(sink_ref, q_ref, kp_ref, kc_ref, vp_ref, vc_ref, o_ref, *, subheads, max_dist):
    n = pl.program_id(1)
    qi = lax.broadcasted_iota(I32, (BLOCK, 2 * BLOCK), 0)
    kj = lax.broadcasted_iota(I32, (BLOCK, 2 * BLOCK), 1)
    dist = BLOCK + qi - kj
    mask = (dist >= 0) & (dist <= max_dist) & ((kj >= BLOCK) | (n > 0))
    flat = [(j, kt, vt, hidx) for j, subs in enumerate(subheads) for kt, vt, hidx in subs]

    def logits(j, kt):
        kk = jnp.concatenate([kp_ref[:, kt * LANES:(kt + 1) * LANES],
                              kc_ref[:, kt * LANES:(kt + 1) * LANES]], axis=0)
        return lax.dot_general(q_ref[:, j * LANES:(j + 1) * LANES], kk, _NT, preferred_element_type=F32)

    tiles = [None] * len(subheads)
    nxt = logits(flat[0][0], flat[0][1])
    for e, (j, kt, vt, hidx) in enumerate(flat):
        s = jnp.where(mask, nxt, NEG_INF)
        if e + 1 < len(flat):
            nxt = logits(flat[e + 1][0], flat[e + 1][1])
        vv = jnp.concatenate([vp_ref[:, vt * LANES:(vt + 1) * LANES],
                              vc_ref[:, vt * LANES:(vt + 1) * LANES]], axis=0)
        sk = sink_ref[hidx] * LOG2E
        m = jnp.maximum(jnp.max(s, axis=-1, keepdims=True), sk)
        p = jnp.exp2(s - m)
        l = jnp.sum(p, axis=-1, keepdims=True) + jnp.exp2(sk - m)
        o_s = jnp.dot(p.astype(BF16), vv, preferred_element_type=F32) / l
        tiles[j] = o_s if tiles[j] is None else tiles[j] + o_s
    for j, o_t in enumerate(tiles):
        o_ref[:, j * LANES:(j + 1) * LANES] = o_t.astype(BF16)


def _swa(q, k, v, sinks, *, subheads, max_dist):
    B, S, wq = q.shape
    wk, wv = k.shape[-1], v.shape[-1]
    cur = lambda width: pl.BlockSpec((None, BLOCK, width), lambda b, i: (b, i, 0))
    prv = lambda width: pl.BlockSpec((None, BLOCK, width), lambda b, i: (b, jnp.maximum(i - 1, 0), 0))
    return pl.pallas_call(
        functools.partial(_swa_kernel, subheads=subheads, max_dist=max_dist),
        grid=(B, S // BLOCK),
        in_specs=[pl.BlockSpec(memory_space=pltpu.SMEM), cur(wq), prv(wk), cur(wk), prv(wv), cur(wv)],
        out_specs=cur(wq),
        out_shape=jax.ShapeDtypeStruct((B, S, wq), BF16),
        compiler_params=_cparams(("parallel", "arbitrary")),
        name="swa",
    )(sinks, q, k, k, v, v)


def _wattn(q, k, vt, branches):
    B, S, wq = q.shape
    widest = max(w for w, _ in branches)
    nrel = min(-(-widest // KEY_CHUNK) + 1, S // KEY_CHUNK)
    bias = _window_bias(branches, nrel)
    nh = wq // LANES
    blk = lambda width: pl.BlockSpec((None, KEY_CHUNK, width), lambda b, i: (b, i, 0))
    return pl.pallas_call(
        functools.partial(_wattn_kernel, nh=nh, nrel=nrel),
        grid=(B, S // KEY_CHUNK),
        in_specs=[blk(wq),
                  pl.BlockSpec((None, S, wq), lambda b, i: (b, 0, 0)),
                  pl.BlockSpec((None,) + vt.shape[1:], lambda b, i: (b, 0, 0, 0)),
                  pl.BlockSpec(bias.shape, lambda b, i: (0, 0, 0))],
        out_specs=blk(wq),
        out_shape=jax.ShapeDtypeStruct((B, S, wq), BF16),
        scratch_shapes=[pltpu.VMEM((LANES, nh * KEY_CHUNK), F32),
                        pltpu.VMEM((KEY_CHUNK + 8, nh * KEY_CHUNK), F32),
                        pltpu.VMEM((KEY_CHUNK + 8, nh * KEY_CHUNK), F32),
                        pltpu.VMEM((KEY_CHUNK, nh * KEY_CHUNK), BF16)],
        compiler_params=_cparams(("parallel", "arbitrary")),
        name="wattn",
    )(q, k, vt, bias)


def _memkv_kernel(mem_ref, g_ref, w_ref, k_ref, v_ref):
    hb = _rms(mem_ref[...], g_ref[...]).astype(BF16)
    y = jnp.dot(hb, w_ref[...], preferred_element_type=F32)
    lane = lax.broadcasted_iota(I32, (y.shape[0], LANES), 1)
    for t in range(MEM_WIDTH // LANES):
        for out_ref, base in ((k_ref, 0), (v_ref, MEM_WIDTH)):
            tile = y[:, base + t * LANES:base + (t + 1) * LANES]
            out_ref[:, (2 * t) * LANES:(2 * t + 1) * LANES] = jnp.where(lane < MEM_HEAD_DIM, tile, 0.0).astype(BF16)
            out_ref[:, (2 * t + 1) * LANES:(2 * t + 2) * LANES] = jnp.where(lane >= MEM_HEAD_DIM, tile, 0.0).astype(BF16)


def _memkv(mem, g_mem, w):
    B, M, D = mem.shape
    L = w.shape[0]
    out = jax.ShapeDtypeStruct((L, B, M, 2 * MEM_WIDTH), BF16)
    ospec = pl.BlockSpec((None, None, M, 2 * MEM_WIDTH), lambda l, b: (l, b, 0, 0))
    return pl.pallas_call(
        _memkv_kernel,
        grid=(L, B),
        in_specs=[pl.BlockSpec((None, M, D), lambda l, b: (b, 0, 0)),
                  pl.BlockSpec((1, D), lambda l, b: (0, 0)),
                  pl.BlockSpec((None, D, 2 * MEM_WIDTH), lambda l, b: (l, 0, 0))],
        out_specs=(ospec, ospec),
        out_shape=(out, out),
        compiler_params=_cparams(("parallel", "parallel")),
        name="memkv",
    )(mem, g_mem, w)


def _mixer_residual(x_ref, mix_ref, qm_ref, mk_ref, mv_ref, w_ref):
    mw = mix_ref.shape[-1]
    n_sub = 2 * MEM_WIDTH // LANES
    logits = [lax.dot_general(qm_ref[:, (u // 2) * LANES:(u // 2 + 1) * LANES], mk_ref[:, u * LANES:(u + 1) * LANES],
                              _NT, preferred_element_type=F32) for u in range(n_sub)]
    y = x_ref[...] + jnp.dot(mix_ref[...], w_ref[0:mw, :], preferred_element_type=F32)
    tiles = []
    for t in range(n_sub // 2):
        o_t = None
        for u in (2 * t, 2 * t + 1):
            p = jnp.exp2(logits[u] - jnp.max(logits[u], axis=-1, keepdims=True))
            l = jnp.sum(p, axis=-1, keepdims=True)
            o_s = jnp.dot(p.astype(BF16), mv_ref[:, u * LANES:(u + 1) * LANES], preferred_element_type=F32) / l
            o_t = o_s if o_t is None else o_t + o_s
        tiles.append(o_t.astype(BF16))
    return y + jnp.dot(jnp.concatenate(tiles, axis=1), w_ref[mw:mw + MEM_WIDTH, :], preferred_element_type=F32)


def _out_kernel(x_ref, mix_ref, qm_ref, mk_ref, mv_ref, w_ref, o_ref):
    o_ref[...] = _mixer_residual(x_ref, mix_ref, qm_ref, mk_ref, mv_ref, w_ref)


def _out_proj(x, mix, qm, mk, mv, w, tm):
    B, S, D = x.shape
    row = lambda width: pl.BlockSpec((None, tm, width), lambda b, i: (b, i, 0))
    mem = pl.BlockSpec((None, mk.shape[1], 2 * MEM_WIDTH), lambda b, i: (b, 0, 0))
    return pl.pallas_call(
        _out_kernel,
        grid=(B, S // tm),
        in_specs=[row(D), row(mix.shape[-1]), row(MEM_WIDTH), mem, mem,
                  pl.BlockSpec(w.shape, lambda b, i: (0, 0))],
        out_specs=row(D),
        out_shape=jax.ShapeDtypeStruct((B, S, D), F32),
        compiler_params=_cparams(("parallel", "parallel")),
        name="out_proj",
    )(x, mix, qm, mk, mv, w)


def _ffn_kernel(x_ref, g_ref, wup_ref, cw_ref, cb_ref, wdn_ref, gf_ref, o_ref, carry_ref, *, tm, cw, final):
    dff = wdn_ref.shape[0]
    hm = min(tm, 2 * LANES)

    @pl.when(pl.program_id(1) == 0)
    def _():
        carry_ref[...] = jnp.zeros(carry_ref.shape, F32)

    row = lax.broadcasted_iota(I32, (hm, cw), 0)
    xs = [x_ref[r:r + hm, :] for r in range(0, tm, hm)]
    hbs = [_rms(x, g_ref[...]).astype(BF16) for x in xs]
    accs = list(xs)

    def up(hb, c0):
        return (jnp.dot(hb, wup_ref[:, c0:c0 + cw], preferred_element_type=F32),
                jnp.dot(hb, wup_ref[:, dff + c0:dff + c0 + cw], preferred_element_type=F32))

    def gate(a, b, prev, c0):
        p1, p2 = prev[7:8, :], prev[6:7, :]
        a1 = jnp.where(row == 0, p1, pltpu.roll(a, 1, 0))
        a2 = jnp.where(row == 0, p2, jnp.where(row == 1, p1, pltpu.roll(a, 2, 0)))
        w = cw_ref[:, c0:c0 + cw]
        conv = w[0:1, :] * a2 + w[1:2, :] * a1 + w[2:3, :] * a + cb_ref[:, c0:c0 + cw]
        return (conv / (1.0 + jnp.exp(-conv)) * b).astype(BF16)

    nxt = [up(hb, 0) for hb in hbs]
    for c0 in range(0, dff, cw):
        cur = nxt
        if c0 + cw < dff:
            nxt = [up(hb, c0 + cw) for hb in hbs]
        prev = carry_ref[:, c0:c0 + cw]
        for k, (a, b) in enumerate(cur):
            accs[k] = accs[k] + jnp.dot(gate(a, b, prev, c0), wdn_ref[c0:c0 + cw, :], preferred_element_type=F32)
            prev = a[hm - 8:hm, :]
        carry_ref[:, c0:c0 + cw] = prev
    for k, acc in enumerate(accs):
        o_ref[k * hm:(k + 1) * hm, :] = _rms(acc, gf_ref[...]) if final else acc


def _ffn(x, g, wup, cw, cb, wdn, gf, tm, final):
    B, S, D = x.shape
    dff = wdn.shape[0]
    row = pl.BlockSpec((None, tm, D), lambda b, i: (b, i, 0))
    const = lambda shape: pl.BlockSpec(shape, lambda b, i: (0, 0), pipeline_mode=pl.Buffered(1))
    return pl.pallas_call(
        functools.partial(_ffn_kernel, tm=tm, cw=2 * LANES, final=final),
        grid=(B, S // tm),
        in_specs=[row, const((1, D)), const(wup.shape), const(cw.shape), const((1, dff)), const(wdn.shape),
                  const((1, D))],
        out_specs=row,
        out_shape=jax.ShapeDtypeStruct((B, S, D), F32),
        scratch_shapes=[pltpu.VMEM((8, dff), F32)],
        compiler_params=_cparams(("arbitrary", "arbitrary")),
        name="ffn",
    )(x, g, wup, cw, cb, wdn, gf)


def _b_subheads():
    return tuple(((2 * (j // 2), 2 * (j // 2), 2 * j), (2 * (j // 2) + 1, 2 * (j // 2) + 1, 2 * j + 1))
                 for j in range(B_HEADS // 2))


def kernel(x, mem, positions, g_mix, g_ffn, g_mem, g_final, w_mem_kv, a_w_in, a_kv_norm, a_w_uk, a_w_uv, a_w_out,
           b_w_in, b_sinks, b_w_out, c_w_in, c_w_out, f_w_up, f_conv_w, f_conv_b, f_w_down):
    B, S, D = x.shape
    depth = g_mix.shape[0]
    tm = min(512, S)
    tab32 = _rope_table(positions, A_QK_DIM, A_ROPE_DIM)
    tab16 = _rope_table(positions, B_HEAD_DIM, B_HEAD_DIM // 4)
    mem_k, mem_v = _memkv(mem, g_mem.reshape(1, D), w_mem_kv.astype(BF16))
    conv_w = jnp.pad(f_conv_w, ((0, 0), (0, 8 - CONV_WIDTH), (0, 0)))
    for i in range(depth):
        kind, j = i % 3, i // 3
        g = g_mix[i].reshape(1, D)
        if kind == 0:
            wuk = jnp.pad(a_w_uk[j], ((0, 0), (0, 0), (A_ROPE_DIM, 0))).reshape(A_KV_RANK, -1).astype(BF16)
            wuv = jnp.transpose(a_w_uv[j], (1, 2, 0)).astype(BF16)
            q, kf, ct, qi, ki, wi, qm = _proj_a(x, g, _prep_a_w_in(a_w_in[j]), a_kv_norm[j].reshape(1, -1), wuk,
                                                tab32, tab16, tm)
            mix = _dsa(q, qi, wi, kf, ct, ki, wuv)
            w_out = a_w_out[j]
        elif kind == 1:
            nq, nkv = B_HEADS * B_HEAD_DIM, 4 * B_KV_HEADS * B_HEAD_DIM
            q, k, v, qm = _proj_qkv(x, g, _prep_b_w_in(b_w_in[j]), tab16, tm, B_HEAD_DIM, B_HEAD_DIM // 4,
                                    nq, nkv, nkv, False)
            mix = _swa(q, k, v, b_sinks[j], subheads=_b_subheads(), max_dist=B_WINDOW - 1)
            w_out = b_w_out[j]
        else:
            nq = C_HEADS * C_HEAD_DIM
            q, k, v, qm = _proj_qkv(x, g, c_w_in[j].astype(BF16), tab32, tm, C_HEAD_DIM, C_HEAD_DIM // 4,
                                    nq, nq, nq, True)
            mix = _wattn(q, k, v, C_BRANCHES)
            w_out = c_w_out[j]
        x = _out_proj(x, mix, qm, mem_k[i], mem_v[i], w_out.astype(BF16), tm)
        x = _ffn(x, g_ffn[i].reshape(1, D), f_w_up[i].astype(BF16), conv_w[i], f_conv_b[i].reshape(1, -1),
                 f_w_down[i].astype(BF16), g_final.reshape(1, D), tm, i == depth - 1)
    return x
```

```python
import functools

import jax
import jax.numpy as jnp
import numpy as np
from jax import lax
from jax.experimental import pallas as pl
from jax.experimental.pallas import tpu as pltpu

F32 = jnp.float32
BF16 = jnp.bfloat16
I32 = jnp.int32

LANES = 128
BLOCK = 128
ROPE_THETA = 500000.0
EPS = 1e-6
NEG_INF = -1e30
LOG2E = 1.4426950408889634
INT_MIN = -(2**31)

A_HEADS = 8
A_QK_DIM = 128
A_ROPE_DIM = 32
A_KV_RANK = 256
IDX_HEADS = 16
IDX_DIM = 64
TOPK_MAX = 256
B_HEADS = 16
B_KV_HEADS = 4
B_HEAD_DIM = 64
B_WINDOW = 128
C_HEADS = 8
C_HEAD_DIM = 128
C_BRANCHES = ((128, 1), (512, 4), (2048, 16))
MEM_HEADS = 4
MEM_HEAD_DIM = 64
MEM_WIDTH = MEM_HEADS * MEM_HEAD_DIM
CONV_WIDTH = 3
KEY_CHUNK = 256

VMEM_LIMIT = 56 * 1024 * 1024

_NT = (((1,), (1,)), ((), ()))


def _cparams(sem):
    return pltpu.CompilerParams(dimension_semantics=sem, vmem_limit_bytes=VMEM_LIMIT)


def _rms(x, g):
    return x * lax.rsqrt(jnp.mean(x * x, axis=-1, keepdims=True) + EPS) * g


def _loop_k_per_trip(lo, hi, body, carry, k=2):
    if k == 1:
        return lax.fori_loop(lo, hi, body, carry)
    trips = (hi - lo) // k

    def several(t, c):
        for u in range(k):
            c = body(lo + k * t + u, c)
        return c

    carry = lax.fori_loop(0, trips, several, carry)
    return _loop_k_per_trip(lo + k * trips, hi, body, carry, k // 2)


def _bit_transpose32(words):
    a = list(words)
    j, m = 16, 0x0000FFFF
    while j:
        mask = jnp.int32(m - (1 << 32) if m >= (1 << 31) else m)
        for k in range(32):
            if k & j == 0:
                t = (a[k] ^ lax.shift_right_logical(a[k + j], j)) & mask
                a[k] = a[k] ^ t
                a[k + j] = a[k + j] ^ lax.shift_left(t, j)
        j >>= 1
        m = (m ^ (m << j)) & 0xFFFFFFFF
    return a


def _rope_tile(t, tab, half):
    c, sa, sb = tab[:, 0:LANES], tab[:, LANES:2 * LANES], tab[:, 2 * LANES:3 * LANES]
    return t * c + pltpu.roll(t, half, 1) * sa + pltpu.roll(t, LANES - half, 1) * sb


def _rope_table(positions, head_dim, rot):
    half = rot // 2
    inv = ROPE_THETA ** (-jnp.arange(0, rot, 2, dtype=F32) / rot)
    ang = positions.astype(F32)[..., None] * inv
    cs = jnp.concatenate([jnp.cos(ang), jnp.sin(ang)], axis=-1)
    sel = np.zeros((rot, 3 * LANES), np.float32)
    one = np.zeros((3 * LANES,), np.float32)
    for l in range(LANES):
        j = l % head_dim
        if j < half:
            sel[j, l] = 1.0
            sel[half + j, 2 * LANES + l] = -1.0
        elif j < rot:
            sel[j - half, l] = 1.0
            sel[j, LANES + l] = 1.0
        else:
            one[l] = 1.0
    hi = cs.astype(BF16)
    mid = (cs - hi.astype(F32)).astype(BF16)
    lo = (cs - hi.astype(F32) - mid.astype(F32)).astype(BF16)
    cs3 = jnp.concatenate([hi, mid, lo], axis=-1)
    sel3 = jnp.asarray(np.tile(sel, (3, 1)), BF16)
    return jnp.dot(cs3, sel3, preferred_element_type=F32) + jnp.asarray(one)


def _staggered(hbs, w_ref, jobs):
    def mms(lo, hi):
        return [jnp.dot(hb, w_ref[:, lo:hi], preferred_element_type=F32) for hb in hbs]

    nxt = mms(jobs[0][0], jobs[0][1])
    for n, (_, _, epilogue) in enumerate(jobs):
        cur = nxt
        if n + 1 < len(jobs):
            nxt = mms(jobs[n + 1][0], jobs[n + 1][1])
        for part, y in enumerate(cur):
            epilogue(y, part)


def _proj_a_kernel(x_ref, g_ref, w_ref, kvn_ref, wuk_ref, t32_ref, t16_ref,
                   q_ref, kf_ref, ct_ref, qi_ref, ki_ref, wi_ref, qm_ref, *, tm):
    hm = KEY_CHUNK
    rows = [slice(r, r + hm) for r in range(0, tm, hm)]
    hbs = [_rms(x_ref[r, :], g_ref[...]).astype(BF16) for r in rows]

    def roped(ref, col, tab_ref, half, scale):
        def epilogue(y, part):
            tab = tab_ref[rows[part], :]
            for u in range(y.shape[1] // LANES):
                t = _rope_tile(y[:, u * LANES:(u + 1) * LANES], tab, half)
                if scale != 1.0:
                    t = t * scale
                ref[rows[part], col + u * LANES:col + (u + 1) * LANES] = t.astype(BF16)
        return epilogue

    latent_bf16 = {}

    def latent(y, part):
        c = _rms(y, kvn_ref[...])
        latent_bf16[part] = c.astype(BF16)
        ct_ref[part] = c.T.astype(BF16)

    def full_keys(y, part):
        k_rope = _rope_tile(y, t32_ref[rows[part], :], A_ROPE_DIM // 2)
        k_nope = jnp.dot(latent_bf16[part], wuk_ref[...], preferred_element_type=F32)
        for h in range(A_HEADS):
            kf_ref[rows[part], h * LANES:(h + 1) * LANES] = (
                k_nope[:, h * LANES:(h + 1) * LANES] + k_rope).astype(BF16)

    def index_keys(y, part):
        tab = t16_ref[rows[part], :]
        for u in range(2):
            ki_ref[part, u * hm:(u + 1) * hm, :] = _rope_tile(
                y[:, u * LANES:(u + 1) * LANES], tab, IDX_DIM // 8).astype(BF16)

    def index_weights(y, part):
        wi_ref[rows[part], :] = y * (IDX_HEADS * IDX_DIM) ** -0.5

    def mem_query(y, part):
        qm_ref[rows[part], :] = (y * (MEM_HEAD_DIM ** -0.5 * LOG2E)).astype(BF16)

    jobs, o = [], 0
    for j in range(0, A_HEADS * A_QK_DIM, 2 * LANES):
        jobs.append((o + j, o + j + 2 * LANES,
                     roped(q_ref, j, t32_ref, A_ROPE_DIM // 2, A_QK_DIM ** -0.5 * LOG2E)))
    o += A_HEADS * A_QK_DIM
    jobs.append((o, o + A_KV_RANK, latent))
    o += A_KV_RANK
    jobs.append((o, o + LANES, full_keys))
    o += LANES
    for j in range(0, IDX_HEADS * IDX_DIM, 2 * LANES):
        jobs.append((o + j, o + j + 2 * LANES, roped(qi_ref, j, t16_ref, IDX_DIM // 8, 1.0)))
    o += IDX_HEADS * IDX_DIM
    jobs.append((o, o + 2 * LANES, index_keys))
    o += 2 * LANES
    jobs.append((o, o + LANES, index_weights))
    o += LANES
    jobs.append((o, o + MEM_WIDTH, mem_query))
    _staggered(hbs, w_ref, jobs)


def _prep_a_w_in(w):
    d = w.shape[0]
    sizes = (A_HEADS * A_QK_DIM, A_KV_RANK, A_ROPE_DIM, IDX_HEADS * IDX_DIM, IDX_DIM, IDX_HEADS, MEM_WIDTH)
    offs = [0]
    for s in sizes:
        offs.append(offs[-1] + s)
    q, ckv, kr, qi, ki, wi, qm = [w[:, offs[i]:offs[i + 1]] for i in range(len(sizes))]
    z = lambda n: jnp.zeros((d, n), w.dtype)
    return jnp.concatenate([
        q, ckv, kr, z(LANES - A_ROPE_DIM), qi,
        ki, z(LANES - IDX_DIM), z(LANES - IDX_DIM), ki,
        wi, z(LANES - IDX_HEADS), qm], axis=1).astype(BF16)


def _proj_a(x, g, w, kvn, wuk, t32, t16, tm):
    B, S, D = x.shape
    n = w.shape[1]
    row = lambda width: pl.BlockSpec((None, tm, width), lambda b, i: (b, i, 0))
    const = lambda shape: pl.BlockSpec(shape, lambda b, i: (0,) * len(shape))
    out_shape = (
        jax.ShapeDtypeStruct((B, S, A_HEADS * A_QK_DIM), BF16),
        jax.ShapeDtypeStruct((B, S, A_HEADS * A_QK_DIM), BF16),
        jax.ShapeDtypeStruct((B, S // KEY_CHUNK, A_KV_RANK, KEY_CHUNK), BF16),
        jax.ShapeDtypeStruct((B, S, IDX_HEADS * IDX_DIM), BF16),
        jax.ShapeDtypeStruct((B, S // KEY_CHUNK, 2 * KEY_CHUNK, LANES), BF16),
        jax.ShapeDtypeStruct((B, S, LANES), F32),
        jax.ShapeDtypeStruct((B, S, MEM_WIDTH), BF16),
    )
    out_specs = (
        row(A_HEADS * A_QK_DIM), row(A_HEADS * A_QK_DIM),
        pl.BlockSpec((None, tm // KEY_CHUNK, A_KV_RANK, KEY_CHUNK), lambda b, i: (b, i, 0, 0)),
        row(IDX_HEADS * IDX_DIM),
        pl.BlockSpec((None, tm // KEY_CHUNK, 2 * KEY_CHUNK, LANES), lambda b, i: (b, i, 0, 0)),
        row(LANES), row(MEM_WIDTH),
    )
    return pl.pallas_call(
        functools.partial(_proj_a_kernel, tm=tm),
        grid=(B, S // tm),
        in_specs=[row(D), const((1, D)), const((D, n)), const((1, A_KV_RANK)), const(wuk.shape),
                  row(3 * LANES), row(3 * LANES)],
        out_specs=out_specs,
        out_shape=out_shape,
        compiler_params=_cparams(("parallel", "parallel")),
        name="proj_a",
    )(x, g, w, kvn, wuk, t32, t16)


def _proj_qkv_kernel(x_ref, g_ref, w_ref, tab_ref, q_ref, k_ref, v_ref, qm_ref, *,
                     head_dim, rot, nq, nk, nv, v_transposed):
    tm = x_ref.shape[0]
    hm = KEY_CHUNK
    rows = [slice(r, r + hm) for r in range(0, tm, hm)]
    hbs = [_rms(x_ref[r, :], g_ref[...]).astype(BF16) for r in rows]

    def roped(ref, col, scale):
        def epilogue(y, part):
            tab = tab_ref[rows[part], :]
            for u in range(2):
                t = _rope_tile(y[:, u * LANES:(u + 1) * LANES], tab, rot // 2)
                if scale != 1.0:
                    t = t * scale
                ref[rows[part], col + u * LANES:col + (u + 1) * LANES] = t.astype(BF16)
        return epilogue

    def value(col):
        def epilogue(y, part):
            if v_transposed:
                v_ref[part, col:col + 2 * LANES, :] = jnp.swapaxes(y[None], 1, 2)[0].astype(BF16)
            else:
                v_ref[rows[part], col:col + 2 * LANES] = y.astype(BF16)
        return epilogue

    def mem_query(y, part):
        qm_ref[rows[part], :] = (y * (MEM_HEAD_DIM ** -0.5 * LOG2E)).astype(BF16)

    jobs = [(j, j + 2 * LANES, roped(q_ref, j, head_dim ** -0.5 * LOG2E)) for j in range(0, nq, 2 * LANES)]
    jobs += [(nq + j, nq + j + 2 * LANES, roped(k_ref, j, 1.0)) for j in range(0, nk, 2 * LANES)]
    jobs += [(nq + nk + j, nq + nk + j + 2 * LANES, value(j)) for j in range(0, nv, 2 * LANES)]
    jobs.append((nq + nk + nv, nq + nk + nv + MEM_WIDTH, mem_query))
    _staggered(hbs, w_ref, jobs)


def _proj_qkv(x, g, w, tab, tm, head_dim, rot, nq, nk, nv, v_transposed):
    B, S, D = x.shape
    n = w.shape[1]
    row = lambda width: pl.BlockSpec((None, tm, width), lambda b, i: (b, i, 0))
    const = lambda shape: pl.BlockSpec(shape, lambda b, i: (0,) * len(shape))
    if v_transposed:
        v_spec = pl.BlockSpec((None, tm // KEY_CHUNK, nv, KEY_CHUNK), lambda b, i: (b, i, 0, 0))
        v_shape = jax.ShapeDtypeStruct((B, S // KEY_CHUNK, nv, KEY_CHUNK), BF16)
    else:
        v_spec, v_shape = row(nv), jax.ShapeDtypeStruct((B, S, nv), BF16)
    return pl.pallas_call(
        functools.partial(_proj_qkv_kernel, head_dim=head_dim, rot=rot, nq=nq, nk=nk, nv=nv,
                          v_transposed=v_transposed),
        grid=(B, S // tm),
        in_specs=[row(D), const((1, D)), const((D, n)), row(3 * LANES)],
        out_specs=(row(nq), row(nk), v_spec, row(MEM_WIDTH)),
        out_shape=(jax.ShapeDtypeStruct((B, S, nq), BF16), jax.ShapeDtypeStruct((B, S, nk), BF16), v_shape,
                   jax.ShapeDtypeStruct((B, S, MEM_WIDTH), BF16)),
        compiler_params=_cparams(("parallel", "parallel")),
        name="proj_qkv",
    )(x, g, w, tab)


def _prep_b_w_in(w):
    d = w.shape[0]
    nq, nkv = B_HEADS * B_HEAD_DIM, B_KV_HEADS * B_HEAD_DIM
    q, k, v, qm = w[:, :nq], w[:, nq:nq + nkv], w[:, nq + nkv:nq + 2 * nkv], w[:, nq + 2 * nkv:]
    z = jnp.zeros((d, B_HEAD_DIM), w.dtype)

    def spread(t):
        cols = []
        for h in range(B_KV_HEADS):
            th = t[:, h * B_HEAD_DIM:(h + 1) * B_HEAD_DIM]
            cols += [th, z, z, th]
        return jnp.concatenate(cols, axis=1)

    return jnp.concatenate([q, spread(k), spread(v), qm], axis=1).astype(BF16)


DSA_BLOCK = 256


def _dsa_kernel(q_ref, qi_ref, wi_ref, kf_ref, ct_ref, ki_ref, wuv_ref, o_ref,
                keys_ref, planes_ref, qbd_ref, qi2_ref, wrow_ref, acc_ref, sa_ref, sb_ref, *, topk):
    kc = KEY_CHUNK
    qb = DSA_BLOCK
    hw = A_HEADS * qb
    gw = 2 * qb
    i = pl.program_id(1)
    nch = (i * qb + qb + kc - 1) // kc

    @pl.when(i == 0)
    def _():
        qbd_ref[...] = jnp.zeros(qbd_ref.shape, BF16)

    for h in range(A_HEADS):
        qbd_ref[h // 2, (h % 2) * qb:(h % 2 + 1) * qb, (h % 2) * LANES:(h % 2 + 1) * LANES] = (
            q_ref[:, h * LANES:(h + 1) * LANES])
        qi2_ref[h * qb:(h + 1) * qb, :] = qi_ref[:, h * LANES:(h + 1) * LANES]
    w_t = wi_ref[...].T
    for j in range(IDX_HEADS // 2):
        wrow_ref[0:1, j * qb:(j + 1) * qb] = w_t[2 * j:2 * j + 1, :]
        wrow_ref[1:2, j * qb:(j + 1) * qb] = w_t[2 * j + 1:2 * j + 2, :]

    qpos = i * qb + lax.broadcasted_iota(I32, (kc, qb), 1)
    krow = lax.broadcasted_iota(I32, (kc, qb), 0)

    def score_chunk(c):
        off = pl.multiple_of(c * kc, kc)
        kk = ki_ref[c]
        acc = None
        for g in range(0, hw, qb):
            d = lax.dot_general(kk, qi2_ref[g:g + qb, :], _NT, preferred_element_type=F32)
            t = (jnp.maximum(d[0:kc], 0.0) * wrow_ref[0:1, g:g + qb]
                 + jnp.maximum(d[kc:2 * kc], 0.0) * wrow_ref[1:2, g:g + qb])
            acc = t if acc is None else acc + t
        bits = pltpu.bitcast(acc, I32)
        key = jnp.where(bits < 0, bits ^ 0x7FFFFFFF, bits)
        key = jnp.where(off + krow <= qpos, key, INT_MIN)
        keys_ref[pl.ds(off, kc), :] = key
        planes = _bit_transpose32([(key[8 * r:8 * r + 8, :] ^ INT_MIN) for r in range(kc // 8)])
        for p in range(32):
            planes_ref[c, p] = planes[p]

    @pl.when(i == 0)
    def _():
        planes_ref[...] = jnp.zeros(planes_ref.shape, I32)

    def score_body(c, carry):
        score_chunk(c)
        return carry

    _loop_k_per_trip(0, nch, score_body, 0, k=4)

    n_chunks = planes_ref.shape[0]

    def bit_body(p, carry):
        alive, need, res = carry
        ones = [a & planes_ref[c, p] for c, a in enumerate(alive)]
        cnt = ones[0] * 0
        for o in ones:
            cnt = cnt + lax.population_count(o)
        cnt = jnp.sum(cnt, axis=0, keepdims=True)
        take = cnt >= need
        alive = [jnp.where(take, o, a ^ o) for o, a in zip(ones, alive)]
        need = jnp.where(take, need, need - cnt)
        res = jnp.where(take, res | lax.shift_left(jnp.int32(1), 31 - p), res)
        return alive, need, res

    for g in range(0, hw, gw):
        sa_ref[0:kc, g:g + gw] = lax.dot_general(kf_ref[0:kc, (g // gw) * 2 * LANES:(g // gw + 1) * 2 * LANES],
                                                 qbd_ref[g // gw], _NT, preferred_element_type=F32)

    carry = ([jnp.where(c < nch, jnp.full((8, qb), -1, I32), 0) for c in range(n_chunks)],
             jnp.full((1, qb), topk, I32), jnp.zeros((1, qb), I32))
    for p in range(32):
        carry = bit_body(p, carry)
    alive, need, res = carry
    thr = jnp.maximum(res ^ INT_MIN, INT_MIN + 1)

    tied = lax.population_count(alive[0])
    for a in alive[1:]:
        tied = tied + lax.population_count(a)
    tied = jnp.sum(tied, axis=0, keepdims=True)
    surplus = jnp.where((tied > need) & (res != 0), 1, 0)

    @pl.when(jnp.max(surplus) > 0)
    def _():
        seq_len = keys_ref.shape[0]
        nbits = (seq_len - 1).bit_length()
        sub = lax.broadcasted_iota(I32, (8, qb), 0)
        pos_planes = [_bit_transpose32([(seq_len - 1 - c * kc - 8 * r) - sub for r in range(kc // 8)])[32 - nbits:]
                      for c in range(n_chunks)]
        state = (alive, need, jnp.zeros((1, qb), I32))
        for j in range(nbits):
            alive_j, need_j, kept = state
            ones = [a & pos_planes[c][j] for c, a in enumerate(alive_j)]
            cnt = lax.population_count(ones[0])
            for o in ones[1:]:
                cnt = cnt + lax.population_count(o)
            cnt = jnp.sum(cnt, axis=0, keepdims=True)
            take = cnt >= need_j
            state = ([jnp.where(take, o, a ^ o) for o, a in zip(ones, alive_j)],
                     jnp.where(take, need_j, need_j - cnt),
                     jnp.where(take, kept | (1 << (nbits - 1 - j)), kept))
        last_kept = state[2]

        def demote(c, carry_):
            off = pl.multiple_of(c * kc, kc)
            key = keys_ref[pl.ds(off, kc), :]
            inv_pos = (seq_len - 1 - off) - krow
            keys_ref[pl.ds(off, kc), :] = jnp.where((key == thr) & (inv_pos < last_kept), key - 1, key)
            return carry_

        lax.fori_loop(0, nch, demote, 0)

    acc_ref[...] = jnp.zeros(acc_ref.shape, F32)
    bias0 = jnp.where(keys_ref[0:kc, :] >= thr, 0.0, NEG_INF)
    bias0 = jnp.concatenate([bias0, bias0], axis=1)
    for g in range(0, hw, gw):
        t = sa_ref[0:kc, g:g + gw] + bias0
        sa_ref[0:kc, g:g + gw] = t
        sa_ref[kc:kc + 1, g:g + gw] = jnp.max(t, axis=0, keepdims=True)

    def logits(c, s_ref):
        off = pl.multiple_of(c * kc, kc)
        bias = jnp.where(keys_ref[pl.ds(off, kc), :] >= thr, 0.0, NEG_INF)
        bias = jnp.concatenate([bias, bias], axis=1)
        for g in range(0, hw, gw):
            t = lax.dot_general(kf_ref[pl.ds(off, kc), (g // gw) * 2 * LANES:(g // gw + 1) * 2 * LANES],
                                qbd_ref[g // gw], _NT, preferred_element_type=F32) + bias
            s_ref[0:kc, g:g + gw] = t
            s_ref[kc:kc + 1, g:g + gw] = jnp.max(t, axis=0, keepdims=True)

    def consume(s_ref, c, carry):
        m_prev, l_prev = carry
        ct = ct_ref[c]
        m_out, l_out = [], []
        for g in range(0, hw, gw):
            s = s_ref[0:kc, g:g + gw]
            m_new = jnp.maximum(m_prev[:, g:g + gw], s_ref[kc:kc + 1, g:g + gw])
            alpha = jnp.exp2(m_prev[:, g:g + gw] - m_new)
            p = jnp.exp2(s - m_new)
            l_out.append(alpha * l_prev[:, g:g + gw] + jnp.sum(p, axis=0, keepdims=True))
            m_out.append(m_new)
            acc_ref[:, g:g + gw] = alpha * acc_ref[:, g:g + gw] + jnp.dot(
                ct, p.astype(BF16), preferred_element_type=F32)
        return jnp.concatenate(m_out, axis=1), jnp.concatenate(l_out, axis=1)

    def pair_body(j, carry):
        c0 = 2 * j
        logits(c0 + 1, sb_ref)
        carry = consume(sa_ref, c0, carry)
        logits(jnp.minimum(c0 + 2, nch - 1), sa_ref)
        return consume(sb_ref, c0 + 1, carry)

    carry = _loop_k_per_trip(0, nch // 2, pair_body,
                             (jnp.full((1, hw), NEG_INF, F32), jnp.zeros((1, hw), F32)), k=2)
    _, l_fin = lax.fori_loop(0, nch % 2, lambda _, c: consume(sa_ref, nch - 1, c), carry)

    inv_l = 1.0 / l_fin
    for h in range(A_HEADS):
        cols = slice(h * qb, (h + 1) * qb)
        out_t = jnp.dot(wuv_ref[h], (acc_ref[:, cols] * inv_l[:, cols]).astype(BF16), preferred_element_type=F32)
        o_ref[:, h * LANES:(h + 1) * LANES] = jnp.swapaxes(out_t[None], 1, 2)[0].astype(BF16)


def _dsa(q, qi, wi, kf, ct, ki, wuv):
    B, S, _ = q.shape
    topk = min(TOPK_MAX, S // 4)
    qb = DSA_BLOCK
    blk = lambda width: pl.BlockSpec((None, qb, width), lambda b, i: (b, i, 0))
    seq = lambda width: pl.BlockSpec((None, S, width), lambda b, i: (b, 0, 0))
    const3 = lambda shape: pl.BlockSpec(shape, lambda b, i: (0, 0, 0))
    return pl.pallas_call(
        functools.partial(_dsa_kernel, topk=topk),
        grid=(B, S // qb),
        in_specs=[blk(A_HEADS * A_QK_DIM), blk(IDX_HEADS * IDX_DIM), blk(LANES),
                  seq(A_HEADS * A_QK_DIM),
                  pl.BlockSpec((None, S // KEY_CHUNK, A_KV_RANK, KEY_CHUNK), lambda b, i: (b, 0, 0, 0)),
                  pl.BlockSpec((None, S // KEY_CHUNK, 2 * KEY_CHUNK, LANES), lambda b, i: (b, 0, 0, 0)),
                  const3(wuv.shape)],
        out_specs=blk(A_HEADS * LANES),
        out_shape=jax.ShapeDtypeStruct((B, S, A_HEADS * LANES), BF16),
        scratch_shapes=[
            pltpu.VMEM((S, qb), I32),
            pltpu.VMEM((S // KEY_CHUNK, 32, 8, qb), I32),
            pltpu.VMEM((A_HEADS // 2, 2 * qb, 2 * LANES), BF16),
            pltpu.VMEM((A_HEADS * qb, LANES), BF16),
            pltpu.VMEM((8, A_HEADS * qb), F32),
            pltpu.VMEM((A_KV_RANK, A_HEADS * qb), F32),
            pltpu.VMEM((KEY_CHUNK + 8, A_HEADS * qb), F32),
            pltpu.VMEM((KEY_CHUNK + 8, A_HEADS * qb), F32),
        ],
        compiler_params=_cparams(("parallel", "arbitrary")),
        name="dsa",
    )(q, qi, wi, kf, ct, ki, wuv)


def _window_bias(branches, nrel):
    rel = jnp.arange(nrel, dtype=I32)[:, None, None]
    krow = jnp.arange(KEY_CHUNK, dtype=I32)[None, :, None]
    qcol = jnp.arange(KEY_CHUNK, dtype=I32)[None, None, :]
    dist = KEY_CHUNK * (nrel - 1 - rel) + qcol - krow
    mult = sum(((dist >= 0) & (dist <= window) & (dist % dil == 0)).astype(F32) for window, dil in branches)
    return jnp.where(mult > 0, jnp.log2(jnp.maximum(mult, 1.0)), NEG_INF)


def _wattn_kernel(q_ref, k_ref, vt_ref, bias_ref, o_ref, acc_ref, sa_ref, sb_ref, p_ref, *, nh, nrel):
    kc = qb = KEY_CHUNK
    top = pl.program_id(1)
    c_lo = jnp.maximum(top - (nrel - 1), 0)
    n = top + 1 - c_lo

    acc_ref[...] = jnp.zeros(acc_ref.shape, F32)

    def logits(c, s_ref):
        off = pl.multiple_of(c * kc, kc)
        bias = bias_ref[c - top + (nrel - 1)]
        for h in range(nh):
            cols = slice(h * LANES, (h + 1) * LANES)
            t = lax.dot_general(k_ref[pl.ds(off, kc), cols], q_ref[:, cols], _NT,
                                preferred_element_type=F32) + bias
            s_ref[0:kc, h * qb:(h + 1) * qb] = t
            s_ref[kc:kc + 1, h * qb:(h + 1) * qb] = jnp.max(t, axis=0, keepdims=True)

    def consume(s_ref, c, carry):
        m_prev, l_prev = carry
        m_new = jnp.maximum(m_prev, s_ref[kc:kc + 1, :])
        alpha = jnp.exp2(m_prev - m_new)
        p_ref[...] = jnp.exp2(s_ref[0:kc, :] - m_new).astype(BF16)
        l_new = alpha * l_prev + jnp.dot(jnp.ones((16, kc), BF16), p_ref[...], preferred_element_type=F32)[0:1, :]
        for h in range(nh):
            cols = slice(h * qb, (h + 1) * qb)
            acc_ref[:, cols] = alpha[:, cols] * acc_ref[:, cols] + jnp.dot(
                vt_ref[c, h * LANES:(h + 1) * LANES, :], p_ref[:, cols], preferred_element_type=F32)
        return m_new, l_new

    logits(c_lo, sa_ref)

    def pair_body(j, carry):
        c0 = c_lo + 2 * j
        logits(c0 + 1, sb_ref)
        carry = consume(sa_ref, c0, carry)
        logits(jnp.minimum(c0 + 2, top), sa_ref)
        return consume(sb_ref, c0 + 1, carry)

    carry = _loop_k_per_trip(0, n // 2, pair_body,
                             (jnp.full((1, nh * qb), NEG_INF, F32), jnp.zeros((1, nh * qb), F32)), k=4)
    _, l_fin = lax.fori_loop(0, n % 2, lambda _, c: consume(sa_ref, top, c), carry)

    inv_l = 1.0 / l_fin
    for h in range(nh):
        cols = slice(h * qb, (h + 1) * qb)
        o_ref[:, h * LANES:(h + 1) * LANES] = (acc_ref[:, cols] * inv_l[:, cols]).T.astype(BF16)


def _swa_kernel(sink_ref, q_ref, kp_ref, kc_ref, vp_ref, vc_ref, o_ref, *, subheads, max_dist):
    n = pl.program_id(1)
    qi = lax.broadcasted_iota(I32, (BLOCK, 2 * BLOCK), 0)
    kj = lax.broadcasted_iota(I32, (BLOCK, 2 * BLOCK), 1)
    dist = BLOCK + qi - kj
    mask = (dist >= 0) & (dist <= max_dist) & ((kj >= BLOCK) | (n > 0))
    flat = [(j, kt, vt, hidx) for j, subs in enumerate(subheads) for kt, vt, hidx in subs]

    def logits(j, kt):
        kk = jnp.concatenate([kp_ref[:, kt * LANES:(kt + 1) * LANES],
                              kc_ref[:, kt * LANES:(kt + 1) * LANES]], axis=0)
        return lax.dot_general(q_ref[:, j * LANES:(j + 1) * LANES], kk, _NT, preferred_element_type=F32)

    tiles = [None] * len(subheads)
    nxt = logits(flat[0][0], flat[0][1])
    for e, (j, kt, vt, hidx) in enumerate(flat):
        s = jnp.where(mask, nxt, NEG_INF)
        if e + 1 < len(flat):
            nxt = logits(flat[e + 1][0], flat[e + 1][1])
        vv = jnp.concatenate([vp_ref[:, vt * LANES:(vt + 1) * LANES],
                              vc_ref[:, vt * LANES:(vt + 1) * LANES]], axis=0)
        sk = sink_ref[hidx] * LOG2E
        m = jnp.maximum(jnp.max(s, axis=-1, keepdims=True), sk)
        p = jnp.exp2(s - m)
        l = jnp.sum(p, axis=-1, keepdims=True) + jnp.exp2(sk - m)
        o_s = jnp.dot(p.astype(BF16), vv, preferred_element_type=F32) / l
        tiles[j] = o_s if tiles[j] is None else tiles[j] + o_s
    for j, o_t in enumerate(tiles):
        o_ref[:, j * LANES:(j + 1) * LANES] = o_t.astype(BF16)


def _swa(q, k, v, sinks, *, subheads, max_dist):
    B, S, wq = q.shape
    wk, wv = k.shape[-1], v.shape[-1]
    cur = lambda width: pl.BlockSpec((None, BLOCK, width), lambda b, i: (b, i, 0))
    prv = lambda width: pl.BlockSpec((None, BLOCK, width), lambda b, i: (b, jnp.maximum(i - 1, 0), 0))
    return pl.pallas_call(
        functools.partial(_swa_kernel, subheads=subheads, max_dist=max_dist),
        grid=(B, S // BLOCK),
        in_specs=[pl.BlockSpec(memory_space=pltpu.SMEM), cur(wq), prv(wk), cur(wk), prv(wv), cur(wv)],
        out_specs=cur(wq),
        out_shape=jax.ShapeDtypeStruct((B, S, wq), BF16),
        compiler_params=_cparams(("parallel", "arbitrary")),
        name="swa",
    )(sinks, q, k, k, v, v)


def _wattn(q, k, vt, branches):
    B, S, wq = q.shape
    widest = max(w for w, _ in branches)
    nrel = min(-(-widest // KEY_CHUNK) + 1, S // KEY_CHUNK)
    bias = _window_bias(branches, nrel)
    nh = wq // LANES
    blk = lambda width: pl.BlockSpec((None, KEY_CHUNK, width), lambda b, i: (b, i, 0))
    return pl.pallas_call(
        functools.partial(_wattn_kernel, nh=nh, nrel=nrel),
        grid=(B, S // KEY_CHUNK),
        in_specs=[blk(wq),
                  pl.BlockSpec((None, S, wq), lambda b, i: (b, 0, 0)),
                  pl.BlockSpec((None,) + vt.shape[1:], lambda b, i: (b, 0, 0, 0)),
                  pl.BlockSpec(bias.shape, lambda b, i: (0, 0, 0))],
        out_specs=blk(wq),
        out_shape=jax.ShapeDtypeStruct((B, S, wq), BF16),
        scratch_shapes=[pltpu.VMEM((LANES, nh * KEY_CHUNK), F32),
                        pltpu.VMEM((KEY_CHUNK + 8, nh * KEY_CHUNK), F32),
                        pltpu.VMEM((KEY_CHUNK + 8, nh * KEY_CHUNK), F32),
                        pltpu.VMEM((KEY_CHUNK, nh * KEY_CHUNK), BF16)],
        compiler_params=_cparams(("parallel", "arbitrary")),
        name="wattn",
    )(q, k, vt, bias)


def _memkv_kernel(mem_ref, g_ref, w_ref, k_ref, v_ref):
    hb = _rms(mem_ref[...], g_ref[...]).astype(BF16)
    y = jnp.dot(hb, w_ref[...], preferred_element_type=F32)
    lane = lax.broadcasted_iota(I32, (y.shape[0], LANES), 1)
    for t in range(MEM_WIDTH // LANES):
        for out_ref, base in ((k_ref, 0), (v_ref, MEM_WIDTH)):
            tile = y[:, base + t * LANES:base + (t + 1) * LANES]
            out_ref[:, (2 * t) * LANES:(2 * t + 1) * LANES] = jnp.where(lane < MEM_HEAD_DIM, tile, 0.0).astype(BF16)
            out_ref[:, (2 * t + 1) * LANES:(2 * t + 2) * LANES] = jnp.where(lane >= MEM_HEAD_DIM, tile, 0.0).astype(BF16)


def _memkv(mem, g_mem, w):
    B, M, D = mem.shape
    L = w.shape[0]
    out = jax.ShapeDtypeStruct((L, B, M, 2 * MEM_WIDTH), BF16)
    ospec = pl.BlockSpec((None, None, M, 2 * MEM_WIDTH), lambda l, b: (l, b, 0, 0))
    return pl.pallas_call(
        _memkv_kernel,
        grid=(L, B),
        in_specs=[pl.BlockSpec((None, M, D), lambda l, b: (b, 0, 0)),
                  pl.BlockSpec((1, D), lambda l, b: (0, 0)),
                  pl.BlockSpec((None, D, 2 * MEM_WIDTH), lambda l, b: (l, 0, 0))],
        out_specs=(ospec, ospec),
        out_shape=(out, out),
        compiler_params=_cparams(("parallel", "parallel")),
        name="memkv",
    )(mem, g_mem, w)


def _mixer_residual(x_ref, mix_ref, qm_ref, mk_ref, mv_ref, w_ref):
    mw = mix_ref.shape[-1]
    n_sub = 2 * MEM_WIDTH // LANES
    logits = [lax.dot_general(qm_ref[:, (u // 2) * LANES:(u // 2 + 1) * LANES], mk_ref[:, u * LANES:(u + 1) * LANES],
                              _NT, preferred_element_type=F32) for u in range(n_sub)]
    y = x_ref[...] + jnp.dot(mix_ref[...], w_ref[0:mw, :], preferred_element_type=F32)
    tiles = []
    for t in range(n_sub // 2):
        o_t = None
        for u in (2 * t, 2 * t + 1):
            p = jnp.exp2(logits[u] - jnp.max(logits[u], axis=-1, keepdims=True))
            l = jnp.sum(p, axis=-1, keepdims=True)
            o_s = jnp.dot(p.astype(BF16), mv_ref[:, u * LANES:(u + 1) * LANES], preferred_element_type=F32) / l
            o_t = o_s if o_t is None else o_t + o_s
        tiles.append(o_t.astype(BF16))
    return y + jnp.dot(jnp.concatenate(tiles, axis=1), w_ref[mw:mw + MEM_WIDTH, :], preferred_element_type=F32)


def _out_kernel(x_ref, mix_ref, qm_ref, mk_ref, mv_ref, w_ref, o_ref):
    o_ref[...] = _mixer_residual(x_ref, mix_ref, qm_ref, mk_ref, mv_ref, w_ref)


def _out_proj(x, mix, qm, mk, mv, w, tm):
    B, S, D = x.shape
    row = lambda width: pl.BlockSpec((None, tm, width), lambda b, i: (b, i, 0))
    mem = pl.BlockSpec((None, mk.shape[1], 2 * MEM_WIDTH), lambda b, i: (b, 0, 0))
    return pl.pallas_call(
        _out_kernel,
        grid=(B, S // tm),
        in_specs=[row(D), row(mix.shape[-1]), row(MEM_WIDTH), mem, mem,
                  pl.BlockSpec(w.shape, lambda b, i: (0, 0))],
        out_specs=row(D),
        out_shape=jax.ShapeDtypeStruct((B, S, D), F32),
        compiler_params=_cparams(("parallel", "parallel")),
        name="out_proj",
    )(x, mix, qm, mk, mv, w)


def _ffn_kernel(x_ref, g_ref, wup_ref, cw_ref, cb_ref, wdn_ref, gf_ref, o_ref, carry_ref, *, tm, cw, final):
    dff = wdn_ref.shape[0]
    hm = min(tm, 2 * LANES)

    @pl.when(pl.program_id(1) == 0)
    def _():
        carry_ref[...] = jnp.zeros(carry_ref.shape, F32)

    row = lax.broadcasted_iota(I32, (hm, cw), 0)
    xs = [x_ref[r:r + hm, :] for r in range(0, tm, hm)]
    hbs = [_rms(x, g_ref[...]).astype(BF16) for x in xs]
    accs = list(xs)

    def up(hb, c0):
        return (jnp.dot(hb, wup_ref[:, c0:c0 + cw], preferred_element_type=F32),
                jnp.dot(hb, wup_ref[:, dff + c0:dff + c0 + cw], preferred_element_type=F32))

    def gate(a, b, prev, c0):
        p1, p2 = prev[7:8, :], prev[6:7, :]
        a1 = jnp.where(row == 0, p1, pltpu.roll(a, 1, 0))
        a2 = jnp.where(row == 0, p2, jnp.where(row == 1, p1, pltpu.roll(a, 2, 0)))
        w = cw_ref[:, c0:c0 + cw]
        conv = w[0:1, :] * a2 + w[1:2, :] * a1 + w[2:3, :] * a + cb_ref[:, c0:c0 + cw]
        return (conv / (1.0 + jnp.exp(-conv)) * b).astype(BF16)

    nxt = [up(hb, 0) for hb in hbs]
    for c0 in range(0, dff, cw):
        cur = nxt
        if c0 + cw < dff:
            nxt = [up(hb, c0 + cw) for hb in hbs]
        prev = carry_ref[:, c0:c0 + cw]
        for k, (a, b) in enumerate(cur):
            accs[k] = accs[k] + jnp.dot(gate(a, b, prev, c0), wdn_ref[c0:c0 + cw, :], preferred_element_type=F32)
            prev = a[hm - 8:hm, :]
        carry_ref[:, c0:c0 + cw] = prev
    for k, acc in enumerate(accs):
        o_ref[k * hm:(k + 1) * hm, :] = _rms(acc, gf_ref[...]) if final else acc


def _ffn(x, g, wup, cw, cb, wdn, gf, tm, final):
    B, S, D = x.shape
    dff = wdn.shape[0]
    row = pl.BlockSpec((None, tm, D), lambda b, i: (b, i, 0))
    const = lambda shape: pl.BlockSpec(shape, lambda b, i: (0, 0), pipeline_mode=pl.Buffered(1))
    return pl.pallas_call(
        functools.partial(_ffn_kernel, tm=tm, cw=2 * LANES, final=final),
        grid=(B, S // tm),
        in_specs=[row, const((1, D)), const(wup.shape), const(cw.shape), const((1, dff)), const(wdn.shape),
                  const((1, D))],
        out_specs=row,
        out_shape=jax.ShapeDtypeStruct((B, S, D), F32),
        scratch_shapes=[pltpu.VMEM((8, dff), F32)],
        compiler_params=_cparams(("arbitrary", "arbitrary")),
        name="ffn",
    )(x, g, wup, cw, cb, wdn, gf)


def _b_subheads():
    return tuple(((2 * (j // 2), 2 * (j // 2), 2 * j), (2 * (j // 2) + 1, 2 * (j // 2) + 1, 2 * j + 1))
                 for j in range(B_HEADS // 2))


def kernel(x, mem, positions, g_mix, g_ffn, g_mem, g_final, w_mem_kv, a_w_in, a_kv_norm, a_w_uk, a_w_uv, a_w_out,
           b_w_in, b_sinks, b_w_out, c_w_in, c_w_out, f_w_up, f_conv_w, f_conv_b, f_w_down):
    B, S, D = x.shape
    depth = g_mix.shape[0]
    tm = min(512, S)
    tab32 = _rope_table(positions, A_QK_DIM, A_ROPE_DIM)
    tab16 = _rope_table(positions, B_HEAD_DIM, B_HEAD_DIM // 4)
    mem_k, mem_v = _memkv(mem, g_mem.reshape(1, D), w_mem_kv.astype(BF16))
    conv_w = jnp.pad(f_conv_w, ((0, 0), (0, 8 - CONV_WIDTH), (0, 0)))
    for i in range(depth):
        kind, j = i % 3, i // 3
        g = g_mix[i].reshape(1, D)
        if kind == 0:
            wuk = jnp.pad(a_w_uk[j], ((0, 0), (0, 0), (A_ROPE_DIM, 0))).reshape(A_KV_RANK, -1).astype(BF16)
            wuv = jnp.transpose(a_w_uv[j], (1, 2, 0)).astype(BF16)
            q, kf, ct, qi, ki, wi, qm = _proj_a(x, g, _prep_a_w_in(a_w_in[j]), a_kv_norm[j].reshape(1, -1), wuk,
                                                tab32, tab16, tm)
            mix = _dsa(q, qi, wi, kf, ct, ki, wuv)
            w_out = a_w_out[j]
        elif kind == 1:
            nq, nkv = B_HEADS * B_HEAD_DIM, 4 * B_KV_HEADS * B_HEAD_DIM
            q, k, v, qm = _proj_qkv(x, g, _prep_b_w_in(b_w_in[j]), tab16, tm, B_HEAD_DIM, B_HEAD_DIM // 4,
                                    nq, nkv, nkv, False)
            mix = _swa(q, k, v, b_sinks[j], subheads=_b_subheads(), max_dist=B_WINDOW - 1)
            w_out = b_w_out[j]
        else:
            nq = C_HEADS * C_HEAD_DIM
            q, k, v, qm = _proj_qkv(x, g, c_w_in[j].astype(BF16), tab32, tm, C_HEAD_DIM, C_HEAD_DIM // 4,
                                    nq, nq, nq, True)
            mix = _wattn(q, k, v, C_BRANCHES)
            w_out = c_w_out[j]
        x = _out_proj(x, mix, qm, mem_k[i], mem_v[i], w_out.astype(BF16), tm)
        x = _ffn(x, g_ffn[i].reshape(1, D), f_w_up[i].astype(BF16), conv_w[i], f_conv_b[i].reshape(1, -1),
                 f_w_down[i].astype(BF16), g_final.reshape(1, D), tm, i == depth - 1)
    return x
```
